```python
import math
import jax, jax.numpy as jnp
from jax import lax
import numpy as np


D_MODEL = 1024
BATCH = 2
SEQ = 16384
DEPTH = 2

HEAD_DIM = 64
BLK = 128
NSA_HEADS = 4
NSA_CMP_LEN = 32
NSA_CMP_STRIDE = 16
NSA_CMP_HIDDEN = 256
NSA_SEL_BLK = 64
NSA_N_SEL = 16
NSA_N_LOCAL = 2
NSA_WINDOW = 512
SWA_HEADS = 8
SWA_KV_HEADS = 2
SWA_WINDOW = 128
SB_HEADS = 4
N_BUCKETS = 32
T5_MAX_DISTANCE = 4096
N_BIAS_HEADS = NSA_HEADS + SWA_HEADS
MIX_WIDTH = (NSA_HEADS + SWA_HEADS + SB_HEADS) * HEAD_DIM
D_FF = ((8 * D_MODEL // 3 + 255) // 256) * 256
IN_SPLITS = (NSA_HEADS * HEAD_DIM,
             HEAD_DIM, HEAD_DIM,
             HEAD_DIM, HEAD_DIM,
             HEAD_DIM, HEAD_DIM,
             3 * NSA_HEADS,
             SWA_HEADS * HEAD_DIM, SWA_KV_HEADS * HEAD_DIM, SWA_KV_HEADS * HEAD_DIM,
             SB_HEADS * HEAD_DIM, SB_HEADS * HEAD_DIM, SB_HEADS * HEAD_DIM)
IN_WIDTH = sum(IN_SPLITS)
RMS_EPS = 1e-6
FORCE_SCORE = 1e6

kernel_name = 'hybrid_nsa_swa_stickbreaking_block'


def rms_norm(x, g):
    xf = x.astype(jnp.float32)
    y = xf * lax.rsqrt(jnp.mean(xf * xf, axis=-1, keepdims=True) + RMS_EPS)
    return (y * g.astype(jnp.float32)).astype(x.dtype)


def split_heads(t, n):
    b, s, _ = t.shape
    return t.reshape(b, s, n, HEAD_DIM).transpose(0, 2, 1, 3)


def merge_heads(o):
    b, h, s, d = o.shape
    return o.transpose(0, 2, 1, 3).reshape(b, s, h * d)


def t5_bucket(dist):
    n = jnp.maximum(dist, 0)
    exact = N_BUCKETS // 2
    nf = jnp.maximum(n, 1).astype(jnp.float32)
    large = exact + (jnp.log(nf / exact) / math.log(T5_MAX_DISTANCE / exact)
                     * (N_BUCKETS - exact)).astype(jnp.int32)
    large = jnp.minimum(large, N_BUCKETS - 1)
    return jnp.where(n < exact, n, large)


def masked_softmax(logits, mask):
    m = jnp.max(jnp.where(mask, logits, -jnp.inf), axis=-1, keepdims=True)
    m = jnp.where(jnp.isfinite(m), m, 0.0)
    p = jnp.where(mask, jnp.exp(logits - m), 0.0)
    return p / jnp.maximum(jnp.sum(p, axis=-1, keepdims=True), 1e-30)


def band_blocks(k, n_prev):
    s, d = k.shape[-2], k.shape[-1]
    nb = s // BLK
    pad = [(0, 0)] * (k.ndim - 2) + [(n_prev * BLK, 0), (0, 0)]
    kp = jnp.pad(k, pad)
    parts = [kp[..., j * BLK: j * BLK + s, :].reshape(k.shape[:-2] + (nb, BLK, d))
             for j in range(n_prev + 1)]
    return jnp.concatenate(parts, axis=-2)


def banded_attention(q, k, v, window, bias_tab, sinks=None):
    b, g, r, s, d = q.shape
    nb = s // BLK
    n_prev = window // BLK
    kb = band_blocks(k, n_prev)
    vb = band_blocks(v, n_prev)
    nk = kb.shape[3]
    qb = q.reshape(b, g, r, nb, BLK, d)
    logits = jnp.einsum('bgrnqd,bgnkd->bgrnqk', qb, kb).astype(jnp.float32) / math.sqrt(d)
    qpos = jnp.arange(nb)[:, None] * BLK + jnp.arange(BLK)[None]
    kpos = jnp.arange(nb)[:, None] * BLK - n_prev * BLK + jnp.arange(nk)[None]
    dist = qpos[:, :, None] - kpos[:, None, :]
    mask = (dist >= 0) & (dist < window) & (kpos[:, None, :] >= 0)
    bias = bias_tab[t5_bucket(dist)].astype(jnp.float32).transpose(3, 0, 1, 2).reshape(g, r, nb, BLK, nk)
    logits = logits + bias
    m = jnp.max(jnp.where(mask, logits, -jnp.inf), axis=-1, keepdims=True)
    if sinks is not None:
        sk = sinks.astype(jnp.float32).reshape(1, g, r, 1, 1, 1)
        m = jnp.maximum(m, sk)
    p = jnp.where(mask, jnp.exp(logits - m), 0.0)
    denom = jnp.sum(p, axis=-1, keepdims=True)
    if sinks is not None:
        denom = denom + jnp.exp(sk - m)
    out = jnp.einsum('bgrnqk,bgnkd->bgrnqd', (p / denom).astype(v.dtype), vb)
    return out.reshape(b, g * r, s, d)


def nsa_compress(kv, pos, w1, b1, w2):
    b, s, d = kv.shape
    nc = (s - NSA_CMP_LEN) // NSA_CMP_STRIDE + 1
    idx = jnp.arange(nc)[:, None] * NSA_CMP_STRIDE + jnp.arange(NSA_CMP_LEN)[None]
    blocks = (kv[:, idx] + pos).reshape(b, nc, NSA_CMP_LEN * d)
    return jax.nn.silu(blocks @ w1 + b1) @ w2


def cmp_to_sel_weights(nc, ns):
    c0 = np.arange(nc)[:, None] * NSA_CMP_STRIDE
    s0 = np.arange(ns)[None, :] * NSA_SEL_BLK
    ov = np.minimum(c0 + NSA_CMP_LEN, s0 + NSA_SEL_BLK) - np.maximum(c0, s0)
    return jnp.asarray(np.clip(ov, 0, None) / NSA_CMP_LEN, dtype=jnp.float32)


def nsa_cmp_and_select(q, kc, vc, k_slc, v_slc, bias_tab):
    b, h, s, d = q.shape
    nb = s // BLK
    nc = kc.shape[1]
    ns = s // NSA_SEL_BLK
    n_sel = min(NSA_N_SEL, ns)
    scale = 1.0 / math.sqrt(d)
    wmap = cmp_to_sel_weights(nc, ns)
    cmp_end = jnp.arange(nc) * NSA_CMP_STRIDE + NSA_CMP_LEN - 1
    k_blocks = k_slc.reshape(b, ns, NSA_SEL_BLK, d)
    v_blocks = v_slc.reshape(b, ns, NSA_SEL_BLK, d)
    jsel = jnp.arange(ns)

    def block(n):
        qb = lax.dynamic_slice_in_dim(q, n * BLK, BLK, axis=2)
        t = n * BLK + jnp.arange(BLK)
        dist_c = t[:, None] - cmp_end[None]
        lc = jnp.einsum('bhqd,bcd->bhqc', qb, kc).astype(jnp.float32) * scale
        lc = lc + bias_tab[t5_bucket(dist_c)].astype(jnp.float32).transpose(2, 0, 1)
        pc = masked_softmax(lc, dist_c >= 0)
        o_c = jnp.einsum('bhqc,bcd->bhqd', pc.astype(vc.dtype), vc)
        imp = jnp.einsum('bhqc,cj->bqj', pc, wmap)
        cur = t // NSA_SEL_BLK
        valid_s = jsel[None] <= cur[:, None]
        forced = valid_s & ((jsel[None] == 0) | (jsel[None] > cur[:, None] - NSA_N_LOCAL))
        score = jnp.where(forced, FORCE_SCORE, jnp.where(valid_s, imp, -jnp.inf))
        _, idx = lax.top_k(score, n_sel)
        ks = jax.vmap(lambda kb_, i_: kb_[i_])(k_blocks, idx).reshape(b, BLK, n_sel * NSA_SEL_BLK, d)
        vs = jax.vmap(lambda vb_, i_: vb_[i_])(v_blocks, idx).reshape(b, BLK, n_sel * NSA_SEL_BLK, d)
        pos = (idx[..., None] * NSA_SEL_BLK + jnp.arange(NSA_SEL_BLK)).reshape(b, BLK, -1)
        dist_s = t[None, :, None] - pos
        ls = jnp.einsum('bhqd,bqkd->bhqk', qb, ks).astype(jnp.float32) * scale
        ls = ls + bias_tab[t5_bucket(dist_s)].astype(jnp.float32).transpose(0, 3, 1, 2)
        ps = masked_softmax(ls, (dist_s >= 0)[:, None])
        o_s = jnp.einsum('bhqk,bqkd->bhqd', ps.astype(vs.dtype), vs)
        return o_c, o_s

    o_c, o_s = lax.map(block, jnp.arange(nb))
    unblock = lambda o: o.transpose(1, 2, 0, 3, 4).reshape(b, h, s, d)
    return unblock(o_c), unblock(o_s)


def nsa_mixer(q, k_cmp, v_cmp, k_slc, v_slc, k_win, v_win, gate_logits, cmp_pos, w1, b1, w2, bias_tab):
    b, h, s, d = q.shape
    kc = nsa_compress(k_cmp, cmp_pos[0], w1[0], b1[0], w2[0])
    vc = nsa_compress(v_cmp, cmp_pos[1], w1[1], b1[1], w2[1])
    o_cmp, o_slc = nsa_cmp_and_select(q, kc, vc, k_slc, v_slc, bias_tab)
    o_win = banded_attention(q[:, None], k_win[:, None], v_win[:, None], NSA_WINDOW, bias_tab)
    g = jax.nn.sigmoid(gate_logits.astype(jnp.float32)).reshape(b, s, 3, h).transpose(2, 0, 3, 1)[..., None]
    g = g.astype(q.dtype)
    return g[0] * o_cmp + g[1] * o_slc + g[2] * o_win


def stick_breaking(q, k, v):
    b, h, s, d = q.shape
    nb = s // BLK
    spos = jnp.arange(s)

    def block(n):
        qb = lax.dynamic_slice_in_dim(q, n * BLK, BLK, axis=2)
        z = jnp.einsum('bhqd,bhkd->bhqk', qb, k).astype(jnp.float32) / math.sqrt(d)
        t = n * BLK + jnp.arange(BLK)
        before = spos[None] < t[:, None]
        log_not = jnp.where(before, jax.nn.log_sigmoid(-z), 0.0)
        later = lax.cumsum(log_not, axis=3, reverse=True) - log_not
        a = jnp.where(before, jnp.exp(jax.nn.log_sigmoid(z) + later), 0.0)
        return jnp.einsum('bhqk,bhkd->bhqd', a.astype(v.dtype), v)

    o = lax.map(block, jnp.arange(nb))
    return o.transpose(1, 2, 0, 3, 4).reshape(b, h, s, d)


def hybrid_mixer(h, w_in, cmp_pos, phi_w1, phi_b1, phi_w2, sinks, grp_g, rel_bias):
    b, s, _ = h.shape
    proj = h @ w_in
    split_points = [int(c) for c in np.cumsum(IN_SPLITS)[:-1]]
    (nq, kc, vc, ks, vs, kw, vw, ng, sq, sk, sv, bq, bk, bv) = jnp.split(proj, split_points, axis=-1)
    o_nsa = nsa_mixer(split_heads(nq, NSA_HEADS), kc, vc, ks, vs, kw, vw, ng,
                      cmp_pos, phi_w1, phi_b1, phi_w2, rel_bias[:, :NSA_HEADS])
    r = SWA_HEADS // SWA_KV_HEADS
    q_swa = split_heads(sq, SWA_HEADS).reshape(b, SWA_KV_HEADS, r, s, HEAD_DIM)
    o_swa = banded_attention(q_swa, split_heads(sk, SWA_KV_HEADS), split_heads(sv, SWA_KV_HEADS),
                             SWA_WINDOW, rel_bias[:, NSA_HEADS:], sinks)
    o_sb = stick_breaking(split_heads(bq, SB_HEADS), split_heads(bk, SB_HEADS), split_heads(bv, SB_HEADS))
    w_a = NSA_HEADS * HEAD_DIM
    w_b = w_a + SWA_HEADS * HEAD_DIM
    return jnp.concatenate([rms_norm(merge_heads(o_nsa), grp_g[:w_a]),
                            rms_norm(merge_heads(o_swa), grp_g[w_a:w_b]),
                            rms_norm(merge_heads(o_sb), grp_g[w_b:])], axis=-1)


def setup_inputs(seed: int = 0) -> dict:
    key = jax.random.key(seed)
    ks = jax.random.split(key, 18)
    nrm = lambda k, shape, sc: jax.random.normal(k, shape, jnp.float32) * sc
    gain = lambda k, shape: 1.0 + 0.05 * jax.random.normal(k, shape, jnp.float32)
    L = NSA_CMP_LEN
    return {
        'x': nrm(ks[0], (BATCH, SEQ, D_MODEL), 1.0),
        'rel_bias': nrm(ks[1], (N_BUCKETS, N_BIAS_HEADS), 0.2),
        'ln_attn_pre': gain(ks[2], (DEPTH, D_MODEL)),
        'w_in': nrm(ks[3], (DEPTH, D_MODEL, IN_WIDTH), D_MODEL ** -0.5),
        'nsa_cmp_pos': nrm(ks[4], (DEPTH, 2, L, HEAD_DIM), 0.2),
        'nsa_phi_w1': nrm(ks[5], (DEPTH, 2, L * HEAD_DIM, NSA_CMP_HIDDEN), (L * HEAD_DIM) ** -0.5),
        'nsa_phi_b1': nrm(ks[6], (DEPTH, 2, NSA_CMP_HIDDEN), 0.02),
        'nsa_phi_w2': nrm(ks[7], (DEPTH, 2, NSA_CMP_HIDDEN, HEAD_DIM), NSA_CMP_HIDDEN ** -0.5),
        'swa_sinks': nrm(ks[8], (DEPTH, SWA_HEADS), 0.5),
        'grp_norm_g': gain(ks[9], (DEPTH, MIX_WIDTH)),
        'w_out': nrm(ks[10], (DEPTH, MIX_WIDTH, D_MODEL), MIX_WIDTH ** -0.5),
        'ln_attn_post': gain(ks[11], (DEPTH, D_MODEL)),
        'ln_ffn_pre': gain(ks[12], (DEPTH, D_MODEL)),
        'ffn_w_gate': nrm(ks[13], (DEPTH, D_MODEL, D_FF), D_MODEL ** -0.5),
        'ffn_w_up': nrm(ks[14], (DEPTH, D_MODEL, D_FF), D_MODEL ** -0.5),
        'ffn_w_down': nrm(ks[15], (DEPTH, D_FF, D_MODEL), D_FF ** -0.5),
        'ln_ffn_post': gain(ks[16], (DEPTH, D_MODEL)),
    }


def reference(x, rel_bias, ln_attn_pre, w_in, nsa_cmp_pos, nsa_phi_w1, nsa_phi_b1, nsa_phi_w2,
              swa_sinks, grp_norm_g, w_out, ln_attn_post, ln_ffn_pre, ffn_w_gate, ffn_w_up,
              ffn_w_down, ln_ffn_post):
    for l in range(DEPTH):
        h = rms_norm(x, ln_attn_pre[l])
        mix = hybrid_mixer(h, w_in[l], nsa_cmp_pos[l], nsa_phi_w1[l], nsa_phi_b1[l], nsa_phi_w2[l],
                           swa_sinks[l], grp_norm_g[l], rel_bias)
        x = x + rms_norm(mix @ w_out[l], ln_attn_post[l])
        h = rms_norm(x, ln_ffn_pre[l])
        f = (jax.nn.silu(h @ ffn_w_gate[l]) * (h @ ffn_w_up[l])) @ ffn_w_down[l]
        x = x + rms_norm(f, ln_ffn_post[l])
    return x
```

```python
import functools
import math

import jax
import jax.numpy as jnp
from jax import lax
from jax.experimental import pallas as pl
from jax.experimental.pallas import tpu as pltpu

F32 = jnp.float32
BF16 = jnp.bfloat16

HEAD_DIM = 64
BLK = 128
NSA_HEADS = 4
CMP_LEN = 32
CMP_STRIDE = 16
CMP_HIDDEN = 256
SEL_BLK = 64
N_SEL = 16
N_LOCAL = 2
NSA_WINDOW = 512
SWA_HEADS = 8
SWA_KV_HEADS = 2
SWA_WINDOW = 128
SB_HEADS = 4
N_BUCKETS = 32
T5_MAX_DISTANCE = 4096
RMS_EPS = 1e-6
FORCE_SCORE = 1e6
NEG = -1e30
SEL_TILE = 256
SB_TILE = 128
SB_LOG_FLOOR = -104.0
N_BAND = 25
VMEM_LIMIT = 56 * 1024 * 1024

NT_DIMS = (((1,), (1,)), ((), ()))


def _cparams(sem):
    return pltpu.CompilerParams(dimension_semantics=sem, vmem_limit_bytes=VMEM_LIMIT)


def _rms(x, g):
    ms = jnp.mean(x * x, axis=-1, keepdims=True)
    return x * lax.rsqrt(ms + RMS_EPS) * g


def _iota(shape, axis):
    return lax.broadcasted_iota(jnp.int32, shape, axis)


def _bias_of_dist(dist, tab_ref, head):
    n = jnp.maximum(dist, 0)
    nf = jnp.maximum(n, 1).astype(F32)
    exact = N_BUCKETS // 2
    large = exact + (jnp.log(nf / exact) / math.log(T5_MAX_DISTANCE / exact)
                     * (N_BUCKETS - exact)).astype(jnp.int32)
    large = jnp.minimum(large, N_BUCKETS - 1)
    bucket = jnp.where(n < exact, n, large)
    val = jnp.full(dist.shape, tab_ref[0, head], F32)
    for k in range(1, N_BUCKETS):
        val = jnp.where(bucket == k, tab_ref[k, head], val)
    return val


def _band_table_kernel(tab_ref, o_ref, *, head_off):
    m = pl.program_id(0)
    h = pl.program_id(1)
    dist = BLK * m + _iota((BLK, BLK), 0) - _iota((BLK, BLK), 1)
    o_ref[0, 0] = _bias_of_dist(dist, tab_ref, h + head_off)


def _band_table(rel_bias, n_band, n_heads, head_off):
    return pl.pallas_call(
        functools.partial(_band_table_kernel, head_off=head_off),
        grid=(n_band, n_heads),
        in_specs=[pl.BlockSpec(memory_space=pltpu.SMEM)],
        out_specs=pl.BlockSpec((1, 1, BLK, BLK), lambda m, h: (m, h, 0, 0)),
        out_shape=jax.ShapeDtypeStruct((n_band, n_heads, BLK, BLK), F32),
        compiler_params=_cparams(("arbitrary", "arbitrary")),
        name="band_table",
    )(rel_bias)


def _cmp_table_kernel(tab_ref, o_ref, *, ncp):
    h = pl.program_id(0)
    u = _iota((BLK, ncp), 1)
    rel = jnp.where(u < 8, u, u - ncp)
    dist = _iota((BLK, ncp), 0) - CMP_STRIDE * rel - (CMP_LEN - 1)
    o_ref[0] = _bias_of_dist(dist, tab_ref, h)


def _cmp_table(rel_bias, ncp):
    return pl.pallas_call(
        functools.partial(_cmp_table_kernel, ncp=ncp),
        grid=(NSA_HEADS,),
        in_specs=[pl.BlockSpec(memory_space=pltpu.SMEM)],
        out_specs=pl.BlockSpec((1, BLK, ncp), lambda h: (h, 0, 0)),
        out_shape=jax.ShapeDtypeStruct((NSA_HEADS, BLK, ncp), F32),
        compiler_params=_cparams(("arbitrary",)),
        name="cmp_table",
    )(rel_bias)


def _proj_kernel(x_ref, g_ref, w_ref, qn_ref, ks_ref, vs_ref, kw_ref, vw_ref, sq_ref, sk_ref, sv_ref,
                 bq_ref, bk_ref, bv_ref, kc_ref, vc_ref, gt_ref):
    h = _rms(x_ref[0], g_ref[...]).astype(BF16)

    def seg(a, b):
        return jnp.dot(h, w_ref[:, a:b], preferred_element_type=F32)

    def heads(ref, a, n):
        y = seg(a, a + n * HEAD_DIM).astype(BF16)
        for i in range(n):
            ref[0, i] = y[:, i * HEAD_DIM:(i + 1) * HEAD_DIM]

    heads(qn_ref, 0, NSA_HEADS)
    y = seg(256, 512).astype(BF16)
    ks_ref[0] = y[:, 0:64]
    vs_ref[0] = y[:, 64:128]
    kw_ref[0] = y[:, 128:192]
    vw_ref[0] = y[:, 192:256]
    heads(sq_ref, 512, SWA_HEADS)
    heads(sk_ref, 1024, SWA_KV_HEADS)
    heads(sv_ref, 1152, SWA_KV_HEADS)
    heads(bq_ref, 1280, SB_HEADS)
    heads(bk_ref, 1536, SB_HEADS)
    heads(bv_ref, 1792, SB_HEADS)
    y = seg(2048, 2304)
    kc_ref[0] = y[:, 0:64]
    vc_ref[0] = y[:, 64:128]
    gt_ref[0] = y[:, 128:256]


def _proj(x, g, w, tm):
    b, s, d = x.shape
    nw = w.shape[1]
    hd = lambda n: pl.BlockSpec((1, n, tm, HEAD_DIM), lambda bi, i: (bi, 0, i, 0))
    flat = lambda n: pl.BlockSpec((1, tm, n), lambda bi, i: (bi, i, 0))
    hshape = lambda n: jax.ShapeDtypeStruct((b, n, s, HEAD_DIM), BF16)
    fshape = lambda n, dt: jax.ShapeDtypeStruct((b, s, n), dt)
    return pl.pallas_call(
        _proj_kernel,
        grid=(b, s // tm),
        in_specs=[pl.BlockSpec((1, tm, d), lambda bi, i: (bi, i, 0)),
                  pl.BlockSpec((1, d), lambda bi, i: (0, 0)),
                  pl.BlockSpec((d, nw), lambda bi, i: (0, 0))],
        out_specs=[hd(NSA_HEADS), flat(64), flat(64), flat(64), flat(64),
                   hd(SWA_HEADS), hd(SWA_KV_HEADS), hd(SWA_KV_HEADS),
                   hd(SB_HEADS), hd(SB_HEADS), hd(SB_HEADS),
                   flat(64), flat(64), flat(128)],
        out_shape=[hshape(NSA_HEADS), fshape(64, BF16), fshape(64, BF16), fshape(64, BF16), fshape(64, BF16),
                   hshape(SWA_HEADS), hshape(SWA_KV_HEADS), hshape(SWA_KV_HEADS),
                   hshape(SB_HEADS), hshape(SB_HEADS), hshape(SB_HEADS),
                   fshape(64, F32), fshape(64, F32), fshape(128, F32)],
        compiler_params=_cparams(("parallel", "parallel")),
        name="proj",
    )(x, g, w)


def _compress_kernel(c_ref, pos_ref, w1_ref, b1_ref, w2_ref, o_ref):
    c = c_ref[0, 0]
    nch = c.shape[0]
    xa = (c + pos_ref[0, 0:1]).astype(BF16)
    xb = (c + pos_ref[0, 1:2]).astype(BF16)
    p = jnp.dot(xa, w1_ref[0, 0], preferred_element_type=F32)
    q = jnp.dot(xb, w1_ref[0, 1], preferred_element_type=F32)
    hid = p + pltpu.roll(q, nch - 1, 0) + b1_ref[0]
    hid = hid * jax.nn.sigmoid(hid)
    out = jnp.dot(hid.astype(BF16), w2_ref[0], preferred_element_type=F32)
    row = _iota(out.shape, 0)
    o_ref[0, 0] = jnp.where(row < nch - 1, out, 0.0).astype(BF16)


def _compress(ckv, pos, w1, b1, w2):
    _, b, nch, cw = ckv.shape
    return pl.pallas_call(
        _compress_kernel,
        grid=(2, b),
        in_specs=[pl.BlockSpec((1, 1, nch, cw), lambda j, bi: (j, bi, 0, 0)),
                  pl.BlockSpec((1, 2, cw), lambda j, bi: (j, 0, 0)),
                  pl.BlockSpec((1, 2, cw, CMP_HIDDEN), lambda j, bi: (j, 0, 0, 0)),
                  pl.BlockSpec((1, 1, CMP_HIDDEN), lambda j, bi: (j, 0, 0)),
                  pl.BlockSpec((1, CMP_HIDDEN, HEAD_DIM), lambda j, bi: (j, 0, 0))],
        out_specs=pl.BlockSpec((1, 1, nch, HEAD_DIM), lambda j, bi: (j, bi, 0, 0)),
        out_shape=jax.ShapeDtypeStruct((2, b, nch, HEAD_DIM), BF16),
        compiler_params=_cparams(("parallel", "parallel")),
        name="compress",
    )(ckv, pos, w1, b1, w2)


def _merge_heads(o, n):
    return jnp.concatenate([o[i * BLK:(i + 1) * BLK] for i in range(n)], axis=1)


def _cmp_kernel(q_ref, kc_ref, vc_ref, hc_ref, wmap_ref, o_ref, sel_ref):
    n = pl.program_id(1)
    ncp = kc_ref.shape[2]
    ns = wmap_ref.shape[1]
    rows = NSA_HEADS * BLK
    q = q_ref[0].reshape(rows, HEAD_DIM)
    kc = kc_ref[0, 0]
    vc = vc_ref[0, 0]
    s = lax.dot_general(q, kc, NT_DIMS, preferred_element_type=F32)
    bias = pltpu.roll(hc_ref[...].reshape(rows, ncp), 8 * n, 1)
    t = BLK * n + _iota((BLK, ncp), 0)
    mask1 = (t - CMP_STRIDE * _iota((BLK, ncp), 1) - (CMP_LEN - 1)) >= 0
    mask = jnp.broadcast_to(mask1[None], (NSA_HEADS, BLK, ncp)).reshape(rows, ncp)
    s = jnp.where(mask, s + bias, NEG)
    mx = jnp.max(s, axis=1, keepdims=True)
    mx = jnp.where(mx <= 0.5 * NEG, 0.0, mx)
    p = jnp.where(mask, jnp.exp(s - mx), 0.0)
    pc = p / jnp.maximum(jnp.sum(p, axis=1, keepdims=True), 1e-30)
    o = jnp.dot(pc.astype(BF16), vc, preferred_element_type=F32)
    o_ref[0] = _merge_heads(o, NSA_HEADS)

    psum = jnp.sum(pc.reshape(NSA_HEADS, BLK, ncp), axis=0)
    hi = psum.astype(BF16)
    lo = (psum - hi.astype(F32)).astype(BF16)
    wmap = wmap_ref[...]
    imp = (jnp.dot(hi, wmap, preferred_element_type=F32)
           + jnp.dot(lo, wmap, preferred_element_type=F32))
    j = _iota((BLK, ns), 1)
    cur = (BLK * n + _iota((BLK, ns), 0)) // SEL_BLK
    valid = j <= cur
    forced = valid & ((j == 0) | (j > cur - N_LOCAL))
    score = jnp.where(forced, FORCE_SCORE, jnp.where(valid, imp, NEG))
    jf = j.astype(F32)
    sel = jnp.zeros((BLK, ns), F32)
    for _ in range(min(N_SEL, ns)):
        best = jnp.max(score, axis=1, keepdims=True)
        first = jnp.min(jnp.where(score == best, jf, float(ns)), axis=1, keepdims=True)
        pick = jf == first
        sel = jnp.where(pick, 1.0, sel)
        score = jnp.where(pick, -3e38, score)
    sel_ref[0] = jnp.where(valid, sel, 0.0).astype(BF16)


def _cmp_attn(qn, kcv, hc, wmap):
    b, _, s, _ = qn.shape
    ncp = kcv.shape[2]
    ns = wmap.shape[1]
    return pl.pallas_call(
        _cmp_kernel,
        grid=(b, s // BLK),
        in_specs=[pl.BlockSpec((1, NSA_HEADS, BLK, HEAD_DIM), lambda bi, n: (bi, 0, n, 0)),
                  pl.BlockSpec((1, 1, ncp, HEAD_DIM), lambda bi, n: (0, bi, 0, 0)),
                  pl.BlockSpec((1, 1, ncp, HEAD_DIM), lambda bi, n: (1, bi, 0, 0)),
                  pl.BlockSpec((NSA_HEADS, BLK, ncp), lambda bi, n: (0, 0, 0)),
                  pl.BlockSpec((ncp, ns), lambda bi, n: (0, 0))],
        out_specs=[pl.BlockSpec((1, BLK, NSA_HEADS * HEAD_DIM), lambda bi, n: (bi, n, 0)),
                   pl.BlockSpec((1, BLK, ns), lambda bi, n: (bi, n, 0))],
        out_shape=[jax.ShapeDtypeStruct((b, s, NSA_HEADS * HEAD_DIM), F32),
                   jax.ShapeDtypeStruct((b, s, ns), BF16)],
        compiler_params=_cparams(("parallel", "parallel")),
        name="cmp_attn",
    )(qn, kcv, kcv, hc, wmap)


def _sel_kernel(q_ref, k_ref, v_ref, sel_ref, band_ref, o_ref):
    n = pl.program_id(1)
    ns = sel_ref.shape[2]
    rows = NSA_HEADS * BLK
    q = q_ref[0].reshape(rows, HEAD_DIM)
    sel = sel_ref[0]
    t = BLK * n + _iota((BLK, SEL_TILE), 0)
    sub = SEL_TILE // BLK

    def body(kt, carry):
        m_i, l_i, acc = carry
        k0 = pl.multiple_of(kt * SEL_TILE, SEL_TILE)
        k = k_ref[0, pl.ds(k0, SEL_TILE), :]
        v = v_ref[0, pl.ds(k0, SEL_TILE), :]
        s = lax.dot_general(q, k, NT_DIMS, preferred_element_type=F32)
        bias = jnp.concatenate(
            [band_ref[jnp.clip(n - sub * kt - u, 0, N_BAND - 1)].reshape(rows, BLK) for u in range(sub)], axis=1)
        pos = k0 + _iota((BLK, SEL_TILE), 1)
        expand = (_iota((ns, SEL_TILE), 0) == (k0 + _iota((ns, SEL_TILE), 1)) // SEL_BLK)
        picked = jnp.dot(sel, jnp.where(expand, 1.0, 0.0).astype(BF16), preferred_element_type=F32)
        pen = jnp.where(pos <= t, (picked - 1.0) * (-NEG), NEG)
        s = ((s + bias).reshape(NSA_HEADS, BLK, SEL_TILE) + pen[None]).reshape(rows, SEL_TILE)
        m_new = jnp.maximum(m_i, jnp.max(s, axis=1, keepdims=True))
        alpha = jnp.exp(m_i - m_new)
        p = jnp.exp(s - m_new)
        l_new = alpha * l_i + jnp.sum(p, axis=1, keepdims=True)
        acc = alpha * acc + jnp.dot(p.astype(BF16), v, preferred_element_type=F32)
        return m_new, l_new, acc

    init = (jnp.full((rows, 1), NEG, F32), jnp.zeros((rows, 1), F32), jnp.zeros((rows, HEAD_DIM), F32))
    n_tiles = (BLK * n + BLK - 1) // SEL_TILE + 1
    _, l_i, acc = lax.fori_loop(0, n_tiles, body, init)
    o_ref[0] = _merge_heads(acc / l_i, NSA_HEADS)


def _sel_attn(qn, ks, vs, selmask, band):
    b, _, s, _ = qn.shape
    ns = selmask.shape[2]
    return pl.pallas_call(
        _sel_kernel,
        grid=(b, s // BLK),
        in_specs=[pl.BlockSpec((1, NSA_HEADS, BLK, HEAD_DIM), lambda bi, n: (bi, 0, n, 0)),
                  pl.BlockSpec((1, s, HEAD_DIM), lambda bi, n: (bi, 0, 0)),
                  pl.BlockSpec((1, s, HEAD_DIM), lambda bi, n: (bi, 0, 0)),
                  pl.BlockSpec((1, BLK, ns), lambda bi, n: (bi, n, 0)),
                  pl.BlockSpec((N_BAND, NSA_HEADS, BLK, BLK), lambda bi, n: (0, 0, 0, 0))],
        out_specs=pl.BlockSpec((1, BLK, NSA_HEADS * HEAD_DIM), lambda bi, n: (bi, n, 0)),
        out_shape=jax.ShapeDtypeStruct((b, s, NSA_HEADS * HEAD_DIM), F32),
        compiler_params=_cparams(("parallel", "arbitrary")),
        name="sel_attn",
    )(qn, ks, vs, selmask, band)


def _band_kernel(q_ref, k_ref, v_ref, band_ref, sink_ref, o_ref, *, window, use_sinks):
    n = pl.program_id(2)
    n_prev = window // BLK
    nk = (n_prev + 1) * BLK
    rows = 4 * BLK
    q = q_ref[0].reshape(rows, HEAD_DIM)
    first = jnp.maximum(n - n_prev, 0)
    k0 = pl.multiple_of(first * BLK, BLK)
    k = k_ref[0, 0, pl.ds(k0, nk), :]
    v = v_ref[0, 0, pl.ds(k0, nk), :]
    s = lax.dot_general(q, k, NT_DIMS, preferred_element_type=F32)
    bias = jnp.concatenate(
        [band_ref[jnp.clip(n - first - u, 0, n_prev)].reshape(rows, BLK) for u in range(n_prev + 1)], axis=1)
    dist1 = (BLK * n + _iota((BLK, nk), 0)) - (k0 + _iota((BLK, nk), 1))
    mask1 = (dist1 >= 0) & (dist1 < window)
    mask = jnp.broadcast_to(mask1[None], (4, BLK, nk)).reshape(rows, nk)
    s = jnp.where(mask, s + bias, NEG)
    mx = jnp.max(s, axis=1, keepdims=True)
    if use_sinks:
        sink = sink_ref[0]
        mx = jnp.maximum(mx, sink)
    p = jnp.where(mask, jnp.exp(s - mx), 0.0)
    den = jnp.sum(p, axis=1, keepdims=True)
    if use_sinks:
        den = den + jnp.exp(sink - mx)
    o = jnp.dot((p / den).astype(BF16), v, preferred_element_type=F32)
    o_ref[0] = _merge_heads(o, 4)


def _band_attn(q, k, v, band, sinks, window, use_sinks, name):
    b, hq, s, _ = q.shape
    g = hq // 4
    n_prev = window // BLK
    return pl.pallas_call(
        functools.partial(_band_kernel, window=window, use_sinks=use_sinks),
        grid=(b, g, s // BLK),
        in_specs=[pl.BlockSpec((1, 4, BLK, HEAD_DIM), lambda bi, gi, n: (bi, gi, n, 0)),
                  pl.BlockSpec((1, 1, s, HEAD_DIM), lambda bi, gi, n: (bi, gi, 0, 0)),
                  pl.BlockSpec((1, 1, s, HEAD_DIM), lambda bi, gi, n: (bi, gi, 0, 0)),
                  pl.BlockSpec((n_prev + 1, 4, BLK, BLK), lambda bi, gi, n: (0, gi, 0, 0)),
                  pl.BlockSpec((1, 4 * BLK, 1), lambda bi, gi, n: (gi, 0, 0))],
        out_specs=pl.BlockSpec((1, BLK, 4 * HEAD_DIM), lambda bi, gi, n: (bi, n, gi)),
        out_shape=jax.ShapeDtypeStruct((b, s, hq * HEAD_DIM), F32),
        compiler_params=_cparams(("parallel", "parallel", "arbitrary")),
        name=name,
    )(q, k, v, band, sinks)


def _sb_kernel(q_ref, k_ref, v_ref, o_ref):
    n = pl.program_id(2)
    q = q_ref[0, 0]
    t = BLK * n + _iota((BLK, SB_TILE), 0)
    later = jnp.where(_iota((SB_TILE, SB_TILE), 0) > _iota((SB_TILE, SB_TILE), 1), 1.0, 0.0).astype(BF16)

    def cond(c):
        kt, carry, _ = c
        return (kt >= 0) & (jnp.max(carry) > SB_LOG_FLOOR)

    def body(c):
        kt, carry, acc = c
        k0 = pl.multiple_of(kt * SB_TILE, SB_TILE)
        k = k_ref[0, 0, pl.ds(k0, SB_TILE), :]
        v = v_ref[0, 0, pl.ds(k0, SB_TILE), :]
        z = lax.dot_general(q, k, NT_DIMS, preferred_element_type=F32)
        before = (k0 + _iota((BLK, SB_TILE), 1)) < t
        softplus = jnp.maximum(z, 0.0) + jnp.log(1.0 + jnp.exp(-jnp.abs(z)))
        log_not = jnp.where(before, -softplus, 0.0)
        hi = log_not.astype(BF16)
        lo = (log_not - hi.astype(F32)).astype(BF16)
        tail = (jnp.dot(hi, later, preferred_element_type=F32)
                + jnp.dot(lo, later, preferred_element_type=F32))
        a = jnp.where(before, jnp.exp(z + log_not + tail + carry), 0.0)
        acc = acc + jnp.dot(a.astype(BF16), v, preferred_element_type=F32)
        carry = carry + jnp.sum(log_not, axis=1, keepdims=True)
        return kt - 1, carry, acc

    init = (n * (BLK // SB_TILE) + (BLK // SB_TILE - 1), jnp.zeros((BLK, 1), F32), jnp.zeros((BLK, HEAD_DIM), F32))
    _, _, acc = lax.while_loop(cond, body, init)
    o_ref[0, 0] = acc


def _sb_attn(q, k, v):
    b, h, s, _ = q.shape
    return pl.pallas_call(
        _sb_kernel,
        grid=(b, h, s // BLK),
        in_specs=[pl.BlockSpec((1, 1, BLK, HEAD_DIM), lambda bi, hi, n: (bi, hi, n, 0)),
                  pl.BlockSpec((1, 1, s, HEAD_DIM), lambda bi, hi, n: (bi, hi, 0, 0)),
                  pl.BlockSpec((1, 1, s, HEAD_DIM), lambda bi, hi, n: (bi, hi, 0, 0))],
        out_specs=pl.BlockSpec((1, 1, BLK, HEAD_DIM), lambda bi, hi, n: (bi, hi, n, 0)),
        out_shape=jax.ShapeDtypeStruct((b, h, s, HEAD_DIM), F32),
        compiler_params=_cparams(("parallel", "parallel", "arbitrary")),
        name="sb_attn",
    )(q, k, v)


def _out_kernel(x_ref, oc_ref, os_ref, ow_ref, gt_ref, swa_ref, sb_ref, gg_ref, w_ref, gp_ref, o_ref):
    tm = x_ref.shape[1]
    wa = NSA_HEADS * HEAD_DIM
    wb = wa + SWA_HEADS * HEAD_DIM
    gates = jax.nn.sigmoid(gt_ref[0])
    lane = _iota((tm, wa), 1)

    def spread(branch):
        cols = [jnp.broadcast_to(gates[:, branch * NSA_HEADS + i:branch * NSA_HEADS + i + 1], (tm, wa))
                for i in range(NSA_HEADS)]
        out = cols[NSA_HEADS - 1]
        for i in range(NSA_HEADS - 2, -1, -1):
            out = jnp.where(lane < (i + 1) * HEAD_DIM, cols[i], out)
        return out

    o_nsa = spread(0) * oc_ref[0] + spread(1) * os_ref[0] + spread(2) * ow_ref[0]
    sb = jnp.concatenate([sb_ref[0, i] for i in range(SB_HEADS)], axis=1)
    gg = gg_ref[...]
    mix = jnp.concatenate([_rms(o_nsa, gg[:, :wa]), _rms(swa_ref[0], gg[:, wa:wb]), _rms(sb, gg[:, wb:])],
                          axis=1).astype(BF16)
    y = jnp.dot(mix, w_ref[...], preferred_element_type=F32)
    o_ref[0] = x_ref[0] + _rms(y, gp_ref[...])


def _out_proj(x, o_cmp, o_sel, o_win, gates, o_swa, o_sb, gg, w, gp, tm):
    b, s, d = x.shape
    row = lambda n: pl.BlockSpec((1, tm, n), lambda bi, i: (bi, i, 0))
    const = lambda a, c: pl.BlockSpec((a, c), lambda bi, i: (0, 0))
    return pl.pallas_call(
        _out_kernel,
        grid=(b, s // tm),
        in_specs=[row(d), row(256), row(256), row(256), row(128), row(512),
                  pl.BlockSpec((1, SB_HEADS, tm, HEAD_DIM), lambda bi, i: (bi, 0, i, 0)),
                  const(1, d), const(d, d), const(1, d)],
        out_specs=row(d),
        out_shape=jax.ShapeDtypeStruct((b, s, d), F32),
        compiler_params=_cparams(("parallel", "parallel")),
        name="out_proj",
    )(x, o_cmp, o_sel, o_win, gates, o_swa, o_sb, gg, w, gp)


def _ffn_kernel(x_ref, g1_ref, wg_ref, wu_ref, wd_ref, g2_ref, o_ref, *, n_chunks):
    x = x_ref[...]
    h = _rms(x, g1_ref[...]).astype(BF16)
    dff = wg_ref.shape[1]
    cw = dff // n_chunks
    f = jnp.zeros(x.shape, F32)
    for c in range(n_chunks):
        gate = jnp.dot(h, wg_ref[:, c * cw:(c + 1) * cw], preferred_element_type=F32)
        up = jnp.dot(h, wu_ref[:, c * cw:(c + 1) * cw], preferred_element_type=F32)
        a = (gate * jax.nn.sigmoid(gate) * up).astype(BF16)
        f = f + jnp.dot(a, wd_ref[c * cw:(c + 1) * cw, :], preferred_element_type=F32)
    o_ref[...] = x + _rms(f, g2_ref[...])


def _ffn(x, g1, wg, wu, wd, g2, tm):
    t, d = x.shape
    dff = wg.shape[1]
    row = pl.BlockSpec((tm, d), lambda i: (i, 0))
    const = lambda a, c: pl.BlockSpec((a, c), lambda i: (0, 0))
    return pl.pallas_call(
        functools.partial(_ffn_kernel, n_chunks=4),
        grid=(t // tm,),
        in_specs=[row, const(1, d), const(d, dff), const(d, dff), const(dff, d), const(1, d)],
        out_specs=row,
        out_shape=jax.ShapeDtypeStruct((t, d), F32),
        compiler_params=_cparams(("parallel",)),
        name="ffn",
    )(x, g1, wg, wu, wd, g2)


def _permute_w_in(w):
    scale = 1.0 / math.sqrt(HEAD_DIM)
    nq, kc, vc, rest, ng = w[:, :256], w[:, 256:320], w[:, 320:384], w[:, 384:640], w[:, 640:652]
    sq, skv = w[:, 652:1164], w[:, 1164:1420]
    bq, bkv = w[:, 1420:1676], w[:, 1676:2188]
    pad = jnp.zeros((w.shape[0], 128 - ng.shape[1]), w.dtype)
    return jnp.concatenate([nq * scale, rest, sq * scale, skv, bq * scale, bkv, kc, vc, ng, pad], axis=1)


def _cmp_to_sel(nc_pad, nc, ns):
    c0 = jnp.arange(nc_pad)[:, None] * CMP_STRIDE
    s0 = jnp.arange(ns)[None, :] * SEL_BLK
    ov = jnp.minimum(c0 + CMP_LEN, s0 + SEL_BLK) - jnp.maximum(c0, s0)
    w = jnp.clip(ov, 0, None).astype(F32) / CMP_LEN
    return jnp.where(jnp.arange(nc_pad)[:, None] < nc, w, 0.0).astype(BF16)


def kernel(x, rel_bias, ln_attn_pre, w_in, nsa_cmp_pos, nsa_phi_w1, nsa_phi_b1, nsa_phi_w2, swa_sinks,
           grp_norm_g, w_out, ln_attn_post, ln_ffn_pre, ffn_w_gate, ffn_w_up, ffn_w_down, ln_ffn_post):
    b, s, d = x.shape
    depth = w_in.shape[0]
    nch = s // CMP_STRIDE
    ns = s // SEL_BLK
    t = b * s

    band_nsa = _band_table(rel_bias, N_BAND, NSA_HEADS, 0)
    band_swa = _band_table(rel_bias, SWA_WINDOW // BLK + 1, SWA_HEADS, NSA_HEADS)
    hc = _cmp_table(rel_bias, nch)
    wmap = _cmp_to_sel(nch, nch - 1, ns)
    no_sinks = jnp.zeros((1, 4 * BLK, 1), F32)

    for l in range(depth):
        outs = _proj(x, ln_attn_pre[l][None], _permute_w_in(w_in[l]).astype(BF16), 512)
        qn, ks, vs, kw, vw, sq, sk, sv, bq, bk, bv, kc, vc, gates = outs
        ckv = jnp.stack([kc.reshape(b, nch, CMP_STRIDE * HEAD_DIM), vc.reshape(b, nch, CMP_STRIDE * HEAD_DIM)])
        kcv = _compress(ckv, nsa_cmp_pos[l].reshape(2, 2, CMP_STRIDE * HEAD_DIM),
                        nsa_phi_w1[l].reshape(2, 2, CMP_STRIDE * HEAD_DIM, CMP_HIDDEN).astype(BF16),
                        nsa_phi_b1[l][:, None, :], nsa_phi_w2[l].astype(BF16))
        o_cmp, selmask = _cmp_attn(qn, kcv, hc, wmap)
        o_sel = _sel_attn(qn, ks, vs, selmask, band_nsa)
        o_win = _band_attn(qn, kw[:, None], vw[:, None], band_nsa, no_sinks, NSA_WINDOW, False, "win_attn")
        sinks = jnp.broadcast_to(swa_sinks[l].reshape(SWA_KV_HEADS, 4, 1, 1),
                                 (SWA_KV_HEADS, 4, BLK, 1)).reshape(SWA_KV_HEADS, 4 * BLK, 1)
        o_swa = _band_attn(sq, sk, sv, band_swa, sinks, SWA_WINDOW, True, "swa_attn")
        o_sb = _sb_attn(bq, bk, bv)
        x = _out_proj(x, o_cmp, o_sel, o_win, gates, o_swa, o_sb, grp_norm_g[l][None], w_out[l].astype(BF16),
                      ln_attn_post[l][None], 256)
        x = _ffn(x.reshape(t, d), ln_ffn_pre[l][None], ffn_w_gate[l].astype(BF16), ffn_w_up[l].astype(BF16),
                 ffn_w_down[l].astype(BF16), ln_ffn_post[l][None], 256).reshape(b, s, d)
    return x
```

```python
import functools
import math

import jax
import jax.numpy as jnp
from jax import lax
from jax.experimental import pallas as pl
from jax.experimental.pallas import tpu as pltpu

F32 = jnp.float32
BF16 = jnp.bfloat16

HEAD_DIM = 64
BLK = 128
NSA_HEADS = 4
CMP_LEN = 32
CMP_STRIDE = 16
CMP_HIDDEN = 256
SEL_BLK = 64
N_SEL = 16
N_LOCAL = 2
NSA_WINDOW = 512
SWA_HEADS = 8
SWA_KV_HEADS = 2
SWA_WINDOW = 128
SB_HEADS = 4
N_BUCKETS = 32
T5_MAX_DISTANCE = 4096
RMS_EPS = 1e-6
FORCE_SCORE = 1e6
NEG = -1e30
SEL_TILE = 512
SB_GROUP = 3
SB_LOG_FLOOR = -104.0
N_BAND = 25
VMEM_LIMIT = 56 * 1024 * 1024

NT_DIMS = (((1,), (1,)), ((), ()))


def _cparams(sem):
    return pltpu.CompilerParams(dimension_semantics=sem, vmem_limit_bytes=VMEM_LIMIT)


def _rms(x, g):
    ms = jnp.mean(x * x, axis=-1, keepdims=True)
    return x * lax.rsqrt(ms + RMS_EPS) * g


def _iota(shape, axis):
    return lax.broadcasted_iota(jnp.int32, shape, axis)


def _bias_of_dist(dist, tab_ref, head):
    n = jnp.maximum(dist, 0)
    nf = jnp.maximum(n, 1).astype(F32)
    exact = N_BUCKETS // 2
    large = exact + (jnp.log(nf / exact) / math.log(T5_MAX_DISTANCE / exact)
                     * (N_BUCKETS - exact)).astype(jnp.int32)
    large = jnp.minimum(large, N_BUCKETS - 1)
    bucket = jnp.where(n < exact, n, large)
    val = jnp.full(dist.shape, tab_ref[0, head], F32)
    for k in range(1, N_BUCKETS):
        val = jnp.where(bucket == k, tab_ref[k, head], val)
    return val


def _band_table_kernel(tab_ref, o_ref, *, head_off):
    m = pl.program_id(0)
    h = pl.program_id(1)
    dist = BLK * m + _iota((BLK, BLK), 0) - _iota((BLK, BLK), 1)
    o_ref[0, 0] = _bias_of_dist(dist, tab_ref, h + head_off)


def _band_table(rel_bias, n_band, n_heads, head_off):
    return pl.pallas_call(
        functools.partial(_band_table_kernel, head_off=head_off),
        grid=(n_band, n_heads),
        in_specs=[pl.BlockSpec(memory_space=pltpu.SMEM)],
        out_specs=pl.BlockSpec((1, 1, BLK, BLK), lambda m, h: (m, h, 0, 0)),
        out_shape=jax.ShapeDtypeStruct((n_band, n_heads, BLK, BLK), F32),
        compiler_params=_cparams(("arbitrary", "arbitrary")),
        name="band_table",
    )(rel_bias)


def _band_table_t_kernel(tab_ref, o_ref):
    m = pl.program_id(0)
    h = pl.program_id(1)
    dist = BLK * m + _iota((BLK, BLK), 1) - _iota((BLK, BLK), 0)
    o_ref[0] = _bias_of_dist(dist, tab_ref, h)


def _band_table_t(rel_bias, n_band):
    return pl.pallas_call(
        _band_table_t_kernel,
        grid=(n_band, NSA_HEADS),
        in_specs=[pl.BlockSpec(memory_space=pltpu.SMEM)],
        out_specs=pl.BlockSpec((1, BLK, BLK), lambda m, h: (m, 0, h)),
        out_shape=jax.ShapeDtypeStruct((n_band, BLK, NSA_HEADS * BLK), F32),
        compiler_params=_cparams(("arbitrary", "arbitrary")),
        name="band_table_t",
    )(rel_bias)


def _cmp_table_kernel(tab_ref, o_ref, *, ncp):
    h = pl.program_id(0)
    u = _iota((BLK, ncp), 1)
    rel = jnp.where(u < 8, u, u - ncp)
    dist = _iota((BLK, ncp), 0) - CMP_STRIDE * rel - (CMP_LEN - 1)
    o_ref[0] = _bias_of_dist(dist, tab_ref, h)


def _cmp_table(rel_bias, ncp):
    return pl.pallas_call(
        functools.partial(_cmp_table_kernel, ncp=ncp),
        grid=(NSA_HEADS,),
        in_specs=[pl.BlockSpec(memory_space=pltpu.SMEM)],
        out_specs=pl.BlockSpec((1, BLK, ncp), lambda h: (h, 0, 0)),
        out_shape=jax.ShapeDtypeStruct((NSA_HEADS, BLK, ncp), F32),
        compiler_params=_cparams(("arbitrary",)),
        name="cmp_table",
    )(rel_bias)


def _proj_kernel(x_ref, g_ref, w_ref, qn_ref, ks_ref, vs_ref, kw_ref, vw_ref, sq_ref, sk_ref, sv_ref,
                 bq_ref, bk_ref, bv_ref, kc_ref, vc_ref, gt_ref):
    h = _rms(x_ref[0], g_ref[...]).astype(BF16)

    def seg(a, b):
        return jnp.dot(h, w_ref[:, a:b], preferred_element_type=F32)

    def heads(ref, a, n):
        y = seg(a, a + n * HEAD_DIM).astype(BF16)
        for i in range(n):
            ref[0, i] = y[:, i * HEAD_DIM:(i + 1) * HEAD_DIM]

    heads(qn_ref, 0, NSA_HEADS)
    y = seg(256, 512)
    vs_ref[0] = y[:, 0:128].T[64:128].astype(BF16)
    y = y.astype(BF16)
    ks_ref[0] = y[:, 0:64]
    kw_ref[0] = y[:, 128:192]
    vw_ref[0] = y[:, 192:256]
    heads(sq_ref, 512, SWA_HEADS)
    heads(sk_ref, 1024, SWA_KV_HEADS)
    heads(sv_ref, 1152, SWA_KV_HEADS)
    heads(bq_ref, 1280, SB_HEADS)
    bk_ref[0] = seg(1536, 1792).astype(BF16)
    bv_ref[0] = seg(1792, 2048).astype(BF16)
    y = seg(2048, 2304)
    kc_ref[0] = y[:, 0:64]
    vc_ref[0] = y[:, 64:128]
    gt_ref[0] = y[:, 128:256]


def _proj(x, g, w, tm):
    b, s, d = x.shape
    nw = w.shape[1]
    hd = lambda n: pl.BlockSpec((1, n, tm, HEAD_DIM), lambda bi, i: (bi, 0, i, 0))
    flat = lambda n: pl.BlockSpec((1, tm, n), lambda bi, i: (bi, i, 0))
    hshape = lambda n: jax.ShapeDtypeStruct((b, n, s, HEAD_DIM), BF16)
    fshape = lambda n, dt: jax.ShapeDtypeStruct((b, s, n), dt)
    return pl.pallas_call(
        _proj_kernel,
        grid=(b, s // tm),
        in_specs=[pl.BlockSpec((1, tm, d), lambda bi, i: (bi, i, 0)),
                  pl.BlockSpec((1, d), lambda bi, i: (0, 0)),
                  pl.BlockSpec((d, nw), lambda bi, i: (0, 0))],
        out_specs=[hd(NSA_HEADS), flat(64), pl.BlockSpec((1, HEAD_DIM, tm), lambda bi, i: (bi, 0, i)),
                   flat(64), flat(64),
                   hd(SWA_HEADS), hd(SWA_KV_HEADS), hd(SWA_KV_HEADS),
                   hd(SB_HEADS), flat(256), flat(256),
                   flat(64), flat(64), flat(128)],
        out_shape=[hshape(NSA_HEADS), fshape(64, BF16), jax.ShapeDtypeStruct((b, HEAD_DIM, s), BF16),
                   fshape(64, BF16), fshape(64, BF16),
                   hshape(SWA_HEADS), hshape(SWA_KV_HEADS), hshape(SWA_KV_HEADS),
                   hshape(SB_HEADS), fshape(256, BF16), fshape(256, BF16),
                   fshape(64, F32), fshape(64, F32), fshape(128, F32)],
        compiler_params=_cparams(("parallel", "parallel")),
        name="proj",
    )(x, g, w)


def _compress_kernel(c_ref, pos_ref, w1_ref, b1_ref, w2_ref, o_ref):
    c = c_ref[0, 0]
    nch = c.shape[0]
    xa = (c + pos_ref[0, 0:1]).astype(BF16)
    xb = (c + pos_ref[0, 1:2]).astype(BF16)
    p = jnp.dot(xa, w1_ref[0, 0], preferred_element_type=F32)
    q = jnp.dot(xb, w1_ref[0, 1], preferred_element_type=F32)
    hid = p + pltpu.roll(q, nch - 1, 0) + b1_ref[0]
    hid = hid * jax.nn.sigmoid(hid)
    out = jnp.dot(hid.astype(BF16), w2_ref[0], preferred_element_type=F32)
    row = _iota(out.shape, 0)
    o_ref[0, 0] = jnp.where(row < nch - 1, out, 0.0).astype(BF16)


def _compress(ckv, pos, w1, b1, w2):
    _, b, nch, cw = ckv.shape
    return pl.pallas_call(
        _compress_kernel,
        grid=(2, b),
        in_specs=[pl.BlockSpec((1, 1, nch, cw), lambda j, bi: (j, bi, 0, 0)),
                  pl.BlockSpec((1, 2, cw), lambda j, bi: (j, 0, 0)),
                  pl.BlockSpec((1, 2, cw, CMP_HIDDEN), lambda j, bi: (j, 0, 0, 0)),
                  pl.BlockSpec((1, 1, CMP_HIDDEN), lambda j, bi: (j, 0, 0)),
                  pl.BlockSpec((1, CMP_HIDDEN, HEAD_DIM), lambda j, bi: (j, 0, 0))],
        out_specs=pl.BlockSpec((1, 1, nch, HEAD_DIM), lambda j, bi: (j, bi, 0, 0)),
        out_shape=jax.ShapeDtypeStruct((2, b, nch, HEAD_DIM), BF16),
        compiler_params=_cparams(("parallel", "parallel")),
        name="compress",
    )(ckv, pos, w1, b1, w2)


def _merge_heads(o, n):
    return jnp.concatenate([o[i * BLK:(i + 1) * BLK] for i in range(n)], axis=1)


def _cmp_kernel(q_ref, kc_ref, vc_ref, hc_ref, wmap_ref, o_ref, sel_ref):
    n = pl.program_id(1)
    ncp = kc_ref.shape[2]
    ns = wmap_ref.shape[1]
    rows = NSA_HEADS * BLK
    q = q_ref[0].reshape(rows, HEAD_DIM)
    kc = kc_ref[0, 0]
    vc = vc_ref[0, 0]
    s = lax.dot_general(q, kc, NT_DIMS, preferred_element_type=F32)
    bias = pltpu.roll(hc_ref[...].reshape(rows, ncp), 8 * n, 1)
    t = BLK * n + _iota((BLK, ncp), 0)
    mask1 = (t - CMP_STRIDE * _iota((BLK, ncp), 1) - (CMP_LEN - 1)) >= 0
    mask = jnp.broadcast_to(mask1[None], (NSA_HEADS, BLK, ncp)).reshape(rows, ncp)
    s = jnp.where(mask, s + bias, NEG)
    mx = jnp.max(s, axis=1, keepdims=True)
    mx = jnp.where(mx <= 0.5 * NEG, 0.0, mx)
    p = jnp.where(mask, jnp.exp(s - mx), 0.0)
    pc = p / jnp.maximum(jnp.sum(p, axis=1, keepdims=True), 1e-30)
    o = jnp.dot(pc.astype(BF16), vc, preferred_element_type=F32)
    o_ref[0] = _merge_heads(o, NSA_HEADS)

    psum = jnp.sum(pc.reshape(NSA_HEADS, BLK, ncp), axis=0)
    hi = psum.astype(BF16)
    lo = (psum - hi.astype(F32)).astype(BF16)
    wmap = wmap_ref[...]
    imp = (jnp.dot(hi, wmap, preferred_element_type=F32)
           + jnp.dot(lo, wmap, preferred_element_type=F32))
    j = _iota((BLK, ns), 1)
    cur = (BLK * n + _iota((BLK, ns), 0)) // SEL_BLK
    valid = j <= cur
    forced = valid & ((j == 0) | (j > cur - N_LOCAL))
    score = jnp.where(forced, FORCE_SCORE, jnp.where(valid, imp, NEG))
    jf = j.astype(F32)
    sel = jnp.zeros((BLK, ns), F32)
    for _ in range(min(N_SEL, ns)):
        best = jnp.max(score, axis=1, keepdims=True)
        first = jnp.min(jnp.where(score == best, jf, float(ns)), axis=1, keepdims=True)
        pick = jf == first
        sel = jnp.where(pick, 1.0, sel)
        score = jnp.where(pick, -3e38, score)
    sel_ref[0, 0] = jnp.where(valid & (sel > 0.5), 0.0, NEG).T


def _cmp_attn(qn, kcv, hc, wmap):
    b, _, s, _ = qn.shape
    ncp = kcv.shape[2]
    ns = wmap.shape[1]
    return pl.pallas_call(
        _cmp_kernel,
        grid=(b, s // BLK),
        in_specs=[pl.BlockSpec((1, NSA_HEADS, BLK, HEAD_DIM), lambda bi, n: (bi, 0, n, 0)),
                  pl.BlockSpec((1, 1, ncp, HEAD_DIM), lambda bi, n: (0, bi, 0, 0)),
                  pl.BlockSpec((1, 1, ncp, HEAD_DIM), lambda bi, n: (1, bi, 0, 0)),
                  pl.BlockSpec((NSA_HEADS, BLK, ncp), lambda bi, n: (0, 0, 0)),
                  pl.BlockSpec((ncp, ns), lambda bi, n: (0, 0))],
        out_specs=[pl.BlockSpec((1, BLK, NSA_HEADS * HEAD_DIM), lambda bi, n: (bi, n, 0)),
                   pl.BlockSpec((1, 1, ns, BLK), lambda bi, n: (bi, n, 0, 0))],
        out_shape=[jax.ShapeDtypeStruct((b, s, NSA_HEADS * HEAD_DIM), F32),
                   jax.ShapeDtypeStruct((b, s // BLK, ns, BLK), F32)],
        compiler_params=_cparams(("parallel", "parallel")),
        name="cmp_attn",
    )(qn, kcv, kcv, hc, wmap)


def _transpose_pad(x, axis):
    pad = jnp.zeros(x.shape, x.dtype)
    return jnp.concatenate([x, pad], axis=axis).T


def _sel_kernel(q_ref, k_ref, vt_ref, pen_ref, band_ref, o_ref, s_scr, p_scr, m_scr, l_scr, alpha_scr, acc_scr):
    n = pl.program_id(1)
    lanes = NSA_HEADS * BLK
    sub = SEL_TILE // BLK
    per = BLK // SEL_BLK
    qt = jnp.concatenate([_transpose_pad(q_ref[0, h].astype(F32), 1)[:HEAD_DIM] for h in range(NSA_HEADS)],
                         axis=1).astype(BF16)
    t = BLK * n + (_iota((BLK, lanes), 1) & (BLK - 1))

    def scores_into(kt, s_buf):
        for u in range(sub):
            ku = pl.multiple_of(kt * SEL_TILE + u * BLK, BLK)
            s = jnp.dot(k_ref[0, pl.ds(ku, BLK), :], qt, preferred_element_type=F32)
            s = s + band_ref[jnp.clip(n - sub * kt - u, 0, N_BAND - 1)]
            pen = jnp.concatenate(
                [jnp.broadcast_to(pen_ref[0, 0, pl.ds((sub * kt + u) * per + w, 1), :], (SEL_BLK, BLK))
                 for w in range(per)], axis=0)
            s_buf[u * BLK:(u + 1) * BLK, :] = s + jnp.concatenate([pen] * NSA_HEADS, axis=1)

    def add_values(kt, p_buf):
        k0 = pl.multiple_of(kt * SEL_TILE, SEL_TILE)
        pv = jnp.dot(vt_ref[0, :, pl.ds(k0, SEL_TILE)], p_buf[...], preferred_element_type=F32)
        acc_scr[...] = alpha_scr[...] * acc_scr[...] + pv

    def softmax_into(kt, s_buf, p_buf, causal):
        def piece(u):
            s = s_buf[u * BLK:(u + 1) * BLK, :]
            if causal:
                s = jnp.where(kt * SEL_TILE + u * BLK + _iota((BLK, lanes), 0) <= t, s, NEG)
            return s
        m_i = m_scr[...]
        mx = functools.reduce(jnp.maximum, [jnp.max(piece(u), axis=0, keepdims=True) for u in range(sub)])
        m_new = jnp.maximum(m_i, mx)
        alpha = jnp.exp(m_i - m_new)
        total = jnp.zeros((1, lanes), F32)
        for u in range(sub):
            p = jnp.exp(piece(u) - m_new)
            total = total + jnp.sum(p, axis=0, keepdims=True)
            p_buf[u * BLK:(u + 1) * BLK, :] = p.astype(BF16)
        m_scr[...] = m_new
        l_scr[...] = alpha * l_scr[...] + total
        alpha_scr[...] = alpha

    def stage(i, s_cur, s_nxt, p_cur, p_prev):
        add_values(jnp.maximum(i - 1, 0), p_prev)
        scores_into(i + 1, s_nxt)
        softmax_into(i, s_cur, p_cur, False)

    def finish(i, s_cur, p_cur, p_prev):
        add_values(jnp.maximum(i - 1, 0), p_prev)
        softmax_into(i, s_cur, p_cur, True)
        add_values(i, p_cur)
        o_t = acc_scr[...] / l_scr[...]
        o_ref[0] = jnp.concatenate(
            [_transpose_pad(o_t[:, h * BLK:(h + 1) * BLK], 0)[:, :HEAD_DIM] for h in range(NSA_HEADS)], axis=1)

    s0, s1, p0, p1 = s_scr.at[0], s_scr.at[1], p_scr.at[0], p_scr.at[1]
    m_scr[...] = jnp.full((1, lanes), NEG, F32)
    l_scr[...] = jnp.zeros((1, lanes), F32)
    alpha_scr[...] = jnp.ones((1, lanes), F32)
    acc_scr[...] = jnp.zeros((HEAD_DIM, lanes), F32)
    p1[...] = jnp.zeros((SEL_TILE, lanes), BF16)
    scores_into(0, s0)
    last = (BLK * n + BLK - 1) // SEL_TILE

    def pair(j, _):
        stage(2 * j, s0, s1, p0, p1)
        stage(2 * j + 1, s1, s0, p1, p0)
        return 0

    lax.fori_loop(0, last // 2, pair, 0)

    @pl.when(last % 2 == 1)
    def _():
        stage(last - 1, s0, s1, p0, p1)
        finish(last, s1, p1, p0)

    @pl.when(last % 2 == 0)
    def _():
        finish(last, s0, p0, p1)


def _sel_attn(qn, ks, vst, pen, band_t):
    b, _, s, _ = qn.shape
    ns = pen.shape[2]
    return pl.pallas_call(
        _sel_kernel,
        grid=(b, s // BLK),
        in_specs=[pl.BlockSpec((1, NSA_HEADS, BLK, HEAD_DIM), lambda bi, n: (bi, 0, n, 0)),
                  pl.BlockSpec((1, s, HEAD_DIM), lambda bi, n: (bi, 0, 0)),
                  pl.BlockSpec((1, HEAD_DIM, s), lambda bi, n: (bi, 0, 0)),
                  pl.BlockSpec((1, 1, ns, BLK), lambda bi, n: (bi, n, 0, 0)),
                  pl.BlockSpec((N_BAND, BLK, NSA_HEADS * BLK), lambda bi, n: (0, 0, 0))],
        out_specs=pl.BlockSpec((1, BLK, NSA_HEADS * HEAD_DIM), lambda bi, n: (bi, n, 0)),
        out_shape=jax.ShapeDtypeStruct((b, s, NSA_HEADS * HEAD_DIM), F32),
        scratch_shapes=[pltpu.VMEM((2, SEL_TILE, NSA_HEADS * BLK), F32),
                        pltpu.VMEM((2, SEL_TILE, NSA_HEADS * BLK), BF16),
                        pltpu.VMEM((1, NSA_HEADS * BLK), F32),
                        pltpu.VMEM((1, NSA_HEADS * BLK), F32),
                        pltpu.VMEM((1, NSA_HEADS * BLK), F32),
                        pltpu.VMEM((HEAD_DIM, NSA_HEADS * BLK), F32)],
        compiler_params=_cparams(("parallel", "arbitrary")),
        name="sel_attn",
    )(qn, ks, vst, pen, band_t)


def _band_kernel(q_ref, k_ref, v_ref, band_ref, sink_ref, o_ref, *, window, use_sinks):
    n = pl.program_id(2)
    n_prev = window // BLK
    nk = (n_prev + 1) * BLK
    rows = 4 * BLK
    q = q_ref[0].reshape(rows, HEAD_DIM)
    first = jnp.maximum(n - n_prev, 0)
    k0 = pl.multiple_of(first * BLK, BLK)
    k = k_ref[0, 0, pl.ds(k0, nk), :]
    v = v_ref[0, 0, pl.ds(k0, nk), :]
    s = lax.dot_general(q, k, NT_DIMS, preferred_element_type=F32)
    bias = jnp.concatenate(
        [band_ref[jnp.clip(n - first - u, 0, n_prev)].reshape(rows, BLK) for u in range(n_prev + 1)], axis=1)
    dist1 = (BLK * n + _iota((BLK, nk), 0)) - (k0 + _iota((BLK, nk), 1))
    mask1 = (dist1 >= 0) & (dist1 < window)
    mask = jnp.broadcast_to(mask1[None], (4, BLK, nk)).reshape(rows, nk)
    s = jnp.where(mask, s + bias, NEG)
    mx = jnp.max(s, axis=1, keepdims=True)
    if use_sinks:
        sink = sink_ref[0]
        mx = jnp.maximum(mx, sink)
    p = jnp.where(mask, jnp.exp(s - mx), 0.0)
    den = jnp.sum(p, axis=1, keepdims=True)
    if use_sinks:
        den = den + jnp.exp(sink - mx)
    o = jnp.dot((p / den).astype(BF16), v, preferred_element_type=F32)
    o_ref[0] = _merge_heads(o, 4)


def _band_attn(q, k, v, band, sinks, window, use_sinks, name):
    b, hq, s, _ = q.shape
    g = hq // 4
    n_prev = window // BLK
    return pl.pallas_call(
        functools.partial(_band_kernel, window=window, use_sinks=use_sinks),
        grid=(b, g, s // BLK),
        in_specs=[pl.BlockSpec((1, 4, BLK, HEAD_DIM), lambda bi, gi, n: (bi, gi, n, 0)),
                  pl.BlockSpec((1, 1, s, HEAD_DIM), lambda bi, gi, n: (bi, gi, 0, 0)),
                  pl.BlockSpec((1, 1, s, HEAD_DIM), lambda bi, gi, n: (bi, gi, 0, 0)),
                  pl.BlockSpec((n_prev + 1, 4, BLK, BLK), lambda bi, gi, n: (0, gi, 0, 0)),
                  pl.BlockSpec((1, 4 * BLK, 1), lambda bi, gi, n: (gi, 0, 0))],
        out_specs=pl.BlockSpec((1, BLK, 4 * HEAD_DIM), lambda bi, gi, n: (bi, n, gi)),
        out_shape=jax.ShapeDtypeStruct((b, s, hq * HEAD_DIM), F32),
        compiler_params=_cparams(("parallel", "parallel", "arbitrary")),
        name=name,
    )(q, k, v, band, sinks)


def _sb_kernel(q_ref, k_ref, v_ref, o_ref):
    n = pl.program_id(1)
    width = SB_HEADS * HEAD_DIM
    lane_head = _iota((BLK, width), 1) // HEAD_DIM
    t = BLK * n + _iota((BLK, BLK), 0)
    col = _iota((BLK, BLK), 1)
    later = jnp.where(_iota((BLK, BLK), 0) > col, 1.0, 0.0).astype(BF16)
    zero = jnp.zeros((BLK, HEAD_DIM), BF16)
    qs = [jnp.concatenate([zero] * h + [q_ref[0, h]] + [zero] * (SB_HEADS - 1 - h), axis=1)
          for h in range(SB_HEADS)]

    def cond(c):
        top, carry, _ = c
        return (top >= 0) & (jnp.max(functools.reduce(jnp.maximum, carry)) > SB_LOG_FLOOR)

    def body(c):
        top, carry, acc = c
        ks, vs, befores = [], [], []
        for j in range(SB_GROUP):
            kb = top - j
            k0 = pl.multiple_of(jnp.maximum(kb, 0) * BLK, BLK)
            ks.append(k_ref[0, pl.ds(k0, BLK), :])
            vs.append(v_ref[0, pl.ds(k0, BLK), :])
            befores.append((kb >= 0) & (k0 + col < t))
        v_all = jnp.concatenate(vs, axis=0)
        new_carry = []
        for h in range(SB_HEADS):
            offset = carry[h]
            a_parts = []
            for j in range(SB_GROUP):
                z = lax.dot_general(qs[h], ks[j], NT_DIMS, preferred_element_type=F32)
                softplus = jnp.maximum(z, 0.0) + jnp.log(1.0 + jnp.exp(-jnp.abs(z)))
                log_not = jnp.where(befores[j], -softplus, 0.0)
                hi = log_not.astype(BF16)
                lo = (log_not - hi.astype(F32)).astype(BF16)
                tail = (jnp.dot(hi, later, preferred_element_type=F32)
                        + jnp.dot(lo, later, preferred_element_type=F32))
                a_parts.append(jnp.where(befores[j], jnp.exp(z + log_not + tail + offset), 0.0).astype(BF16))
                offset = offset + jnp.sum(log_not, axis=1, keepdims=True)
            pv = jnp.dot(jnp.concatenate(a_parts, axis=1), v_all, preferred_element_type=F32)
            acc = acc + jnp.where(lane_head == h, pv, 0.0)
            new_carry.append(offset)
        return top - SB_GROUP, tuple(new_carry), acc

    init = (n, tuple(jnp.zeros((BLK, 1), F32) for _ in range(SB_HEADS)), jnp.zeros((BLK, width), F32))
    _, _, acc = lax.while_loop(cond, body, init)
    o_ref[0] = acc


def _sb_attn(q, k, v):
    b, h, s, _ = q.shape
    width = h * HEAD_DIM
    return pl.pallas_call(
        _sb_kernel,
        grid=(b, s // BLK),
        in_specs=[pl.BlockSpec((1, h, BLK, HEAD_DIM), lambda bi, n: (bi, 0, n, 0)),
                  pl.BlockSpec((1, s, width), lambda bi, n: (bi, 0, 0)),
                  pl.BlockSpec((1, s, width), lambda bi, n: (bi, 0, 0))],
        out_specs=pl.BlockSpec((1, BLK, width), lambda bi, n: (bi, n, 0)),
        out_shape=jax.ShapeDtypeStruct((b, s, width), F32),
        compiler_params=_cparams(("parallel", "arbitrary")),
        name="sb_attn",
    )(q, k, v)


def _out_kernel(x_ref, oc_ref, os_ref, ow_ref, gt_ref, swa_ref, sb_ref, gg_ref, w_ref, gp_ref, o_ref):
    tm = x_ref.shape[1]
    wa = NSA_HEADS * HEAD_DIM
    wb = wa + SWA_HEADS * HEAD_DIM
    gates = jax.nn.sigmoid(gt_ref[0])
    lane = _iota((tm, wa), 1)

    def spread(branch):
        cols = [jnp.broadcast_to(gates[:, branch * NSA_HEADS + i:branch * NSA_HEADS + i + 1], (tm, wa))
                for i in range(NSA_HEADS)]
        out = cols[NSA_HEADS - 1]
        for i in range(NSA_HEADS - 2, -1, -1):
            out = jnp.where(lane < (i + 1) * HEAD_DIM, cols[i], out)
        return out

    o_nsa = spread(0) * oc_ref[0] + spread(1) * os_ref[0] + spread(2) * ow_ref[0]
    gg = gg_ref[...]
    mix = jnp.concatenate([_rms(o_nsa, gg[:, :wa]), _rms(swa_ref[0], gg[:, wa:wb]), _rms(sb_ref[0], gg[:, wb:])],
                          axis=1).astype(BF16)
    y = jnp.dot(mix, w_ref[...], preferred_element_type=F32)
    o_ref[0] = x_ref[0] + _rms(y, gp_ref[...])


def _out_proj(x, o_cmp, o_sel, o_win, gates, o_swa, o_sb, gg, w, gp, tm):
    b, s, d = x.shape
    row = lambda n: pl.BlockSpec((1, tm, n), lambda bi, i: (bi, i, 0))
    const = lambda a, c: pl.BlockSpec((a, c), lambda bi, i: (0, 0))
    return pl.pallas_call(
        _out_kernel,
        grid=(b, s // tm),
        in_specs=[row(d), row(256), row(256), row(256), row(128), row(512), row(256),
                  const(1, d), const(d, d), const(1, d)],
        out_specs=row(d),
        out_shape=jax.ShapeDtypeStruct((b, s, d), F32),
        compiler_params=_cparams(("parallel", "parallel")),
        name="out_proj",
    )(x, o_cmp, o_sel, o_win, gates, o_swa, o_sb, gg, w, gp)


def _ffn_kernel(x_ref, g1_ref, wg_ref, wu_ref, wd_ref, g2_ref, o_ref, *, n_chunks):
    x = x_ref[...]
    h = _rms(x, g1_ref[...]).astype(BF16)
    dff = wg_ref.shape[1]
    cw = dff // n_chunks
    f = jnp.zeros(x.shape, F32)
    for c in range(n_chunks):
        gate = jnp.dot(h, wg_ref[:, c * cw:(c + 1) * cw], preferred_element_type=F32)
        up = jnp.dot(h, wu_ref[:, c * cw:(c + 1) * cw], preferred_element_type=F32)
        a = (gate * jax.nn.sigmoid(gate) * up).astype(BF16)
        f = f + jnp.dot(a, wd_ref[c * cw:(c + 1) * cw, :], preferred_element_type=F32)
    o_ref[...] = x + _rms(f, g2_ref[...])


def _ffn(x, g1, wg, wu, wd, g2, tm):
    t, d = x.shape
    dff = wg.shape[1]
    row = pl.BlockSpec((tm, d), lambda i: (i, 0))
    const = lambda a, c: pl.BlockSpec((a, c), lambda i: (0, 0))
    return pl.pallas_call(
        functools.partial(_ffn_kernel, n_chunks=4),
        grid=(t // tm,),
        in_specs=[row, const(1, d), const(d, dff), const(d, dff), const(dff, d), const(1, d)],
        out_specs=row,
        out_shape=jax.ShapeDtypeStruct((t, d), F32),
        compiler_params=_cparams(("parallel",)),
        name="ffn",
    )(x, g1, wg, wu, wd, g2)


def _permute_w_in(w):
    scale = 1.0 / math.sqrt(HEAD_DIM)
    nq, kc, vc, rest, ng = w[:, :256], w[:, 256:320], w[:, 320:384], w[:, 384:640], w[:, 640:652]
    sq, skv = w[:, 652:1164], w[:, 1164:1420]
    bq, bkv = w[:, 1420:1676], w[:, 1676:2188]
    pad = jnp.zeros((w.shape[0], 128 - ng.shape[1]), w.dtype)
    return jnp.concatenate([nq * scale, rest, sq * scale, skv, bq * scale, bkv, kc, vc, ng, pad], axis=1)


def _cmp_to_sel(nc_pad, nc, ns):
    c0 = jnp.arange(nc_pad)[:, None] * CMP_STRIDE
    s0 = jnp.arange(ns)[None, :] * SEL_BLK
    ov = jnp.minimum(c0 + CMP_LEN, s0 + SEL_BLK) - jnp.maximum(c0, s0)
    w = jnp.clip(ov, 0, None).astype(F32) / CMP_LEN
    return jnp.where(jnp.arange(nc_pad)[:, None] < nc, w, 0.0).astype(BF16)


def kernel(x, rel_bias, ln_attn_pre, w_in, nsa_cmp_pos, nsa_phi_w1, nsa_phi_b1, nsa_phi_w2, swa_sinks,
           grp_norm_g, w_out, ln_attn_post, ln_ffn_pre, ffn_w_gate, ffn_w_up, ffn_w_down, ln_ffn_post):
    b, s, d = x.shape
    depth = w_in.shape[0]
    nch = s // CMP_STRIDE
    ns = s // SEL_BLK
    t = b * s

    band_nsa = _band_table(rel_bias, N_BAND, NSA_HEADS, 0)
    band_swa = _band_table(rel_bias, SWA_WINDOW // BLK + 1, SWA_HEADS, NSA_HEADS)
    band_nsa_t = _band_table_t(rel_bias, N_BAND)
    hc = _cmp_table(rel_bias, nch)
    wmap = _cmp_to_sel(nch, nch - 1, ns)
    no_sinks = jnp.zeros((1, 4 * BLK, 1), F32)

    for l in range(depth):
        outs = _proj(x, ln_attn_pre[l][None], _permute_w_in(w_in[l]).astype(BF16), 512)
        qn, ks, vs, kw, vw, sq, sk, sv, bq, bk, bv, kc, vc, gates = outs
        ckv = jnp.stack([kc.reshape(b, nch, CMP_STRIDE * HEAD_DIM), vc.reshape(b, nch, CMP_STRIDE * HEAD_DIM)])
        kcv = _compress(ckv, nsa_cmp_pos[l].reshape(2, 2, CMP_STRIDE * HEAD_DIM),
                        nsa_phi_w1[l].reshape(2, 2, CMP_STRIDE * HEAD_DIM, CMP_HIDDEN).astype(BF16),
                        nsa_phi_b1[l][:, None, :], nsa_phi_w2[l].astype(BF16))
        o_cmp, selmask = _cmp_attn(qn, kcv, hc, wmap)
        o_sel = _sel_attn(qn, ks, vs, selmask, band_nsa_t)
        o_win = _band_attn(qn, kw[:, None], vw[:, None], band_nsa, no_sinks, NSA_WINDOW, False, "win_attn")
        sinks = jnp.broadcast_to(swa_sinks[l].reshape(SWA_KV_HEADS, 4, 1, 1),
                                 (SWA_KV_HEADS, 4, BLK, 1)).reshape(SWA_KV_HEADS, 4 * BLK, 1)
        o_swa = _band_attn(sq, sk, sv, band_swa, sinks, SWA_WINDOW, True, "swa_attn")
        o_sb = _sb_attn(bq, bk, bv)
        x = _out_proj(x, o_cmp, o_sel, o_win, gates, o_swa, o_sb, grp_norm_g[l][None], w_out[l].astype(BF16),
                      ln_attn_post[l][None], 256)
        x = _ffn(x.reshape(t, d), ln_ffn_pre[l][None], ffn_w_gate[l].astype(BF16), ffn_w_up[l].astype(BF16),
                 ffn_w_down[l].astype(BF16), ln_ffn_post[l][None], 256).reshape(b, s, d)
    return x
```

```python
import functools
import math

import jax
import jax.numpy as jnp
from jax import lax
from jax.experimental import pallas as pl
from jax.experimental.pallas import tpu as pltpu

F32 = jnp.float32
BF16 = jnp.bfloat16

HEAD_DIM = 64
BLK = 128
NSA_HEADS = 4
CMP_LEN = 32
CMP_STRIDE = 16
CMP_HIDDEN = 256
SEL_BLK = 64
N_SEL = 16
N_LOCAL = 2
NSA_WINDOW = 512
SWA_HEADS = 8
SWA_KV_HEADS = 2
SWA_WINDOW = 128
SB_HEADS = 4
N_BUCKETS = 32
T5_MAX_DISTANCE = 4096
RMS_EPS = 1e-6
FORCE_SCORE = 1e6
NEG = -1e30
SEL_TILE = 512
SB_GROUP = 3
SB_LOG_FLOOR = -104.0
PICKED = -3e38
CMP_QB = 4
BAND_QB = 2
CMP_CHUNK = 128
IMP_PAD = 16
IMP_ROWS = CMP_CHUNK // 4 + IMP_PAD
N_BAND = 25
VMEM_LIMIT = 56 * 1024 * 1024

NT_DIMS = (((1,), (1,)), ((), ()))


def _cparams(sem):
    return pltpu.CompilerParams(dimension_semantics=sem, vmem_limit_bytes=VMEM_LIMIT)


def _rms(x, g):
    ms = jnp.mean(x * x, axis=-1, keepdims=True)
    return x * lax.rsqrt(ms + RMS_EPS) * g


def _iota(shape, axis):
    return lax.broadcasted_iota(jnp.int32, shape, axis)


def _bias_of_dist(dist, tab_ref, head):
    n = jnp.maximum(dist, 0)
    nf = jnp.maximum(n, 1).astype(F32)
    exact = N_BUCKETS // 2
    large = exact + (jnp.log(nf / exact) / math.log(T5_MAX_DISTANCE / exact)
                     * (N_BUCKETS - exact)).astype(jnp.int32)
    large = jnp.minimum(large, N_BUCKETS - 1)
    bucket = jnp.where(n < exact, n, large)
    val = jnp.full(dist.shape, tab_ref[0, head], F32)
    for k in range(1, N_BUCKETS):
        val = jnp.where(bucket == k, tab_ref[k, head], val)
    return val


def _band_table_kernel(tab_ref, o_ref, *, head_off):
    m = pl.program_id(0)
    h = pl.program_id(1)
    dist = BLK * m + _iota((BLK, BLK), 0) - _iota((BLK, BLK), 1)
    o_ref[0, 0] = _bias_of_dist(dist, tab_ref, h + head_off)


def _band_table(rel_bias, n_band, n_heads, head_off):
    return pl.pallas_call(
        functools.partial(_band_table_kernel, head_off=head_off),
        grid=(n_band, n_heads),
        in_specs=[pl.BlockSpec(memory_space=pltpu.SMEM)],
        out_specs=pl.BlockSpec((1, 1, BLK, BLK), lambda m, h: (m, h, 0, 0)),
        out_shape=jax.ShapeDtypeStruct((n_band, n_heads, BLK, BLK), F32),
        compiler_params=_cparams(("arbitrary", "arbitrary")),
        name="band_table",
    )(rel_bias)


def _band_table_t_kernel(tab_ref, o_ref):
    m = pl.program_id(0)
    h = pl.program_id(1)
    dist = BLK * m + _iota((BLK, BLK), 1) - _iota((BLK, BLK), 0)
    o_ref[0] = _bias_of_dist(dist, tab_ref, h)


def _band_table_t(rel_bias, n_band):
    return pl.pallas_call(
        _band_table_t_kernel,
        grid=(n_band, NSA_HEADS),
        in_specs=[pl.BlockSpec(memory_space=pltpu.SMEM)],
        out_specs=pl.BlockSpec((1, BLK, BLK), lambda m, h: (m, 0, h)),
        out_shape=jax.ShapeDtypeStruct((n_band, BLK, NSA_HEADS * BLK), F32),
        compiler_params=_cparams(("arbitrary", "arbitrary")),
        name="band_table_t",
    )(rel_bias)


def _cmp_table_kernel(tab_ref, o_ref, *, ncp):
    h = pl.program_id(0)
    rel = _iota((2 * ncp, BLK), 0) - (ncp - 8)
    dist = _iota((2 * ncp, BLK), 1) - CMP_STRIDE * rel - (CMP_LEN - 1)
    o_ref[...] = _bias_of_dist(dist, tab_ref, h)


def _cmp_table(rel_bias, ncp):
    return pl.pallas_call(
        functools.partial(_cmp_table_kernel, ncp=ncp),
        grid=(NSA_HEADS,),
        in_specs=[pl.BlockSpec(memory_space=pltpu.SMEM)],
        out_specs=pl.BlockSpec((2 * ncp, BLK), lambda h: (0, h)),
        out_shape=jax.ShapeDtypeStruct((2 * ncp, NSA_HEADS * BLK), F32),
        compiler_params=_cparams(("arbitrary",)),
        name="cmp_table",
    )(rel_bias)


def _proj_kernel(x_ref, g_ref, w_ref, qn_ref, ks_ref, vs_ref, kw_ref, vw_ref, sq_ref, sk_ref, sv_ref,
                 bq_ref, bk_ref, bv_ref, kc_ref, vc_ref, gt_ref, qt_ref):
    h = _rms(x_ref[0], g_ref[...]).astype(BF16)

    def seg(a, b):
        return jnp.dot(h, w_ref[:, a:b], preferred_element_type=F32)

    def heads(ref, a, n):
        y = seg(a, a + n * HEAD_DIM).astype(BF16)
        for i in range(n):
            ref[0, i] = y[:, i * HEAD_DIM:(i + 1) * HEAD_DIM]

    y = seg(0, NSA_HEADS * HEAD_DIM)
    yb = y.astype(BF16)
    for i in range(NSA_HEADS):
        qn_ref[0, i] = yb[:, i * HEAD_DIM:(i + 1) * HEAD_DIM]
    yt = y.T
    for j in range(y.shape[0] // BLK):
        for i in range(NSA_HEADS):
            qt_ref[0, j, :, i * BLK:(i + 1) * BLK] = (
                yt[i * HEAD_DIM:(i + 1) * HEAD_DIM, j * BLK:(j + 1) * BLK].astype(BF16))
    y = seg(256, 512)
    vs_ref[0] = y[:, 0:128].T[64:128].astype(BF16)
    y = y.astype(BF16)
    ks_ref[0] = y[:, 0:64]
    kw_ref[0] = y[:, 128:192]
    vw_ref[0] = y[:, 192:256]
    heads(sq_ref, 512, SWA_HEADS)
    heads(sk_ref, 1024, SWA_KV_HEADS)
    heads(sv_ref, 1152, SWA_KV_HEADS)
    heads(bq_ref, 1280, SB_HEADS)
    bk_ref[0] = seg(1536, 1792).astype(BF16)
    bv_ref[0] = seg(1792, 2048).astype(BF16)
    y = seg(2048, 2304)
    kc_ref[0] = y[:, 0:64]
    vc_ref[0] = y[:, 64:128]
    gt_ref[0] = y[:, 128:256]


def _proj(x, g, w, tm):
    b, s, d = x.shape
    nw = w.shape[1]
    hd = lambda n: pl.BlockSpec((1, n, tm, HEAD_DIM), lambda bi, i: (bi, 0, i, 0))
    flat = lambda n: pl.BlockSpec((1, tm, n), lambda bi, i: (bi, i, 0))
    hshape = lambda n: jax.ShapeDtypeStruct((b, n, s, HEAD_DIM), BF16)
    fshape = lambda n, dt: jax.ShapeDtypeStruct((b, s, n), dt)
    return pl.pallas_call(
        _proj_kernel,
        grid=(b, s // tm),
        in_specs=[pl.BlockSpec((1, tm, d), lambda bi, i: (bi, i, 0)),
                  pl.BlockSpec((1, d), lambda bi, i: (0, 0)),
                  pl.BlockSpec((d, nw), lambda bi, i: (0, 0))],
        out_specs=[hd(NSA_HEADS), flat(64), pl.BlockSpec((1, HEAD_DIM, tm), lambda bi, i: (bi, 0, i)),
                   flat(64), flat(64),
                   hd(SWA_HEADS), hd(SWA_KV_HEADS), hd(SWA_KV_HEADS),
                   hd(SB_HEADS), flat(256), flat(256),
                   flat(64), flat(64), flat(128),
                   pl.BlockSpec((1, tm // BLK, HEAD_DIM, NSA_HEADS * BLK), lambda bi, i: (bi, i, 0, 0))],
        out_shape=[hshape(NSA_HEADS), fshape(64, BF16), jax.ShapeDtypeStruct((b, HEAD_DIM, s), BF16),
                   fshape(64, BF16), fshape(64, BF16),
                   hshape(SWA_HEADS), hshape(SWA_KV_HEADS), hshape(SWA_KV_HEADS),
                   hshape(SB_HEADS), fshape(256, BF16), fshape(256, BF16),
                   fshape(64, F32), fshape(64, F32), fshape(128, F32),
                   jax.ShapeDtypeStruct((b, s // BLK, HEAD_DIM, NSA_HEADS * BLK), BF16)],
        compiler_params=_cparams(("parallel", "parallel")),
        name="proj",
    )(x, g, w)


def _compress_kernel(c_ref, pos_ref, w1_ref, b1_ref, w2_ref, o_ref, ot_ref):
    c = c_ref[0, 0]
    nch = c.shape[0]
    xa = (c + pos_ref[0, 0:1]).astype(BF16)
    xb = (c + pos_ref[0, 1:2]).astype(BF16)
    p = jnp.dot(xa, w1_ref[0, 0], preferred_element_type=F32)
    q = jnp.dot(xb, w1_ref[0, 1], preferred_element_type=F32)
    hid = p + pltpu.roll(q, nch - 1, 0) + b1_ref[0]
    hid = hid * jax.nn.sigmoid(hid)
    out = jnp.dot(hid.astype(BF16), w2_ref[0], preferred_element_type=F32)
    row = _iota(out.shape, 0)
    out = jnp.where(row < nch - 1, out, 0.0)
    o_ref[0, 0] = out.astype(BF16)
    ot_ref[0, 0] = jnp.concatenate([out, jnp.zeros_like(out)], axis=1).T[:HEAD_DIM].astype(BF16)


def _compress(ckv, pos, w1, b1, w2):
    _, b, nch, cw = ckv.shape
    return pl.pallas_call(
        _compress_kernel,
        grid=(2, b),
        in_specs=[pl.BlockSpec((1, 1, nch, cw), lambda j, bi: (j, bi, 0, 0)),
                  pl.BlockSpec((1, 2, cw), lambda j, bi: (j, 0, 0)),
                  pl.BlockSpec((1, 2, cw, CMP_HIDDEN), lambda j, bi: (j, 0, 0, 0)),
                  pl.BlockSpec((1, 1, CMP_HIDDEN), lambda j, bi: (j, 0, 0)),
                  pl.BlockSpec((1, CMP_HIDDEN, HEAD_DIM), lambda j, bi: (j, 0, 0))],
        out_specs=[pl.BlockSpec((1, 1, nch, HEAD_DIM), lambda j, bi: (j, bi, 0, 0)),
                   pl.BlockSpec((1, 1, HEAD_DIM, nch), lambda j, bi: (j, bi, 0, 0))],
        out_shape=[jax.ShapeDtypeStruct((2, b, nch, HEAD_DIM), BF16),
                   jax.ShapeDtypeStruct((2, b, HEAD_DIM, nch), BF16)],
        compiler_params=_cparams(("parallel", "parallel")),
        name="compress",
    )(ckv, pos, w1, b1, w2)


def _merge_heads(o, n):
    return jnp.concatenate([o[i * BLK:(i + 1) * BLK] for i in range(n)], axis=1)


def _transpose_pad(x, axis):
    pad = jnp.zeros(x.shape, x.dtype)
    return jnp.concatenate([x, pad], axis=axis).T


def _cmp_kernel(qt_ref, kc_ref, vct_ref, hct_ref, wmapt_ref, o_ref, pen_ref, s_scr, oacc_scr, iacc_scr):
    step = pl.program_id(1)
    ncp = kc_ref.shape[2]
    ns = wmapt_ref.shape[0] - IMP_PAD
    lanes = NSA_HEADS * BLK
    row_c = _iota((CMP_CHUNK, lanes), 0)
    lane_q = _iota((CMP_CHUNK, lanes), 1) & (BLK - 1)
    n_chunks = (CMP_QB * step) // (CMP_CHUNK // 8) + 1

    def score_chunk(ch, mxs):
        c0 = pl.multiple_of(ch * CMP_CHUNK, CMP_CHUNK)
        kc = kc_ref[0, 0, pl.ds(c0, CMP_CHUNK), :]
        out = []
        for qb in range(CMP_QB):
            n = CMP_QB * step + qb
            s = jnp.dot(kc, qt_ref[0, qb], preferred_element_type=F32)
            bias = hct_ref[pl.ds(pl.multiple_of(ncp - 8 - 8 * n + c0, 8), CMP_CHUNK), :]
            visible = (BLK * n + lane_q - CMP_STRIDE * (c0 + row_c) - (CMP_LEN - 1)) >= 0
            s = jnp.where(visible, s + bias, NEG)
            s_scr[qb, ch] = s
            out.append(jnp.maximum(mxs[qb], jnp.max(s, axis=0, keepdims=True)))
        return tuple(out)

    mxs = lax.fori_loop(0, n_chunks, score_chunk, tuple(jnp.full((1, lanes), NEG, F32) for _ in range(CMP_QB)))
    ms = [jnp.where(mx <= 0.5 * NEG, 0.0, mx) for mx in mxs]
    oacc_scr[...] = jnp.zeros(oacc_scr.shape, F32)
    iacc_scr[...] = jnp.zeros(iacc_scr.shape, F32)

    def prob_chunk(ch, ls):
        c0 = pl.multiple_of(ch * CMP_CHUNK, CMP_CHUNK)
        vct = vct_ref[0, 0, :, pl.ds(c0, CMP_CHUNK)]
        r0 = pl.multiple_of(ch * (CMP_CHUNK // 4), 16)
        wmt = wmapt_ref[pl.ds(r0, IMP_ROWS), pl.ds(c0, CMP_CHUNK)]
        out = []
        for qb in range(CMP_QB):
            p = jnp.exp(s_scr[qb, ch] - ms[qb])
            pb = p.astype(BF16)
            oacc_scr[qb] += jnp.dot(vct, pb, preferred_element_type=F32)
            iacc_scr[qb, pl.ds(r0, IMP_ROWS), :] += jnp.dot(wmt, pb, preferred_element_type=F32)
            out.append(ls[qb] + jnp.sum(p, axis=0, keepdims=True))
        return tuple(out)

    ls = lax.fori_loop(0, n_chunks, prob_chunk, tuple(jnp.zeros((1, lanes), F32) for _ in range(CMP_QB)))
    imps = []
    for qb in range(CMP_QB):
        inv = 1.0 / jnp.maximum(ls[qb], 1e-30)
        o_t = oacc_scr[qb] * inv
        o_ref[0, qb * BLK:(qb + 1) * BLK, :] = jnp.concatenate(
            [_transpose_pad(o_t[:, h * BLK:(h + 1) * BLK], 0)[:, :HEAD_DIM] for h in range(NSA_HEADS)], axis=1)
        w = iacc_scr[qb, 0:ns, :] * inv
        imps.append(functools.reduce(jnp.add, [w[:, h * BLK:(h + 1) * BLK] for h in range(NSA_HEADS)]))

    imp = jnp.concatenate(imps, axis=1)
    shape = (ns, CMP_QB * BLK)
    j = _iota(shape, 0)
    cur = (CMP_QB * BLK * step + _iota(shape, 1)) // SEL_BLK
    valid = j <= cur
    forced = valid & ((j == 0) | (j > cur - N_LOCAL))
    score = jnp.where(forced, PICKED, jnp.where(valid, imp, NEG))
    jf = j.astype(F32)
    for _ in range(min(N_SEL, ns) - (N_LOCAL + 1)):
        best = jnp.max(score, axis=0, keepdims=True)
        first = jnp.min(jnp.where(score == best, jf, float(ns)), axis=0, keepdims=True)
        score = jnp.where(jf == first, PICKED, score)
    pen_ref[0, 0] = jnp.where(valid & (score == PICKED), 0.0, NEG)


def _cmp_attn(qt, kcv, kcv_t, hct, wmapt):
    b, nb, _, lanes = qt.shape
    ncp = kcv.shape[2]
    ns = wmapt.shape[0] - IMP_PAD
    s = nb * BLK
    return pl.pallas_call(
        _cmp_kernel,
        grid=(b, nb // CMP_QB),
        in_specs=[pl.BlockSpec((1, CMP_QB, HEAD_DIM, lanes), lambda bi, i: (bi, i, 0, 0)),
                  pl.BlockSpec((1, 1, ncp, HEAD_DIM), lambda bi, i: (0, bi, 0, 0)),
                  pl.BlockSpec((1, 1, HEAD_DIM, ncp), lambda bi, i: (1, bi, 0, 0)),
                  pl.BlockSpec((2 * ncp, lanes), lambda bi, i: (0, 0)),
                  pl.BlockSpec((ns + IMP_PAD, ncp), lambda bi, i: (0, 0))],
        out_specs=[pl.BlockSpec((1, CMP_QB * BLK, NSA_HEADS * HEAD_DIM), lambda bi, i: (bi, i, 0)),
                   pl.BlockSpec((1, 1, ns, CMP_QB * BLK), lambda bi, i: (bi, i, 0, 0))],
        out_shape=[jax.ShapeDtypeStruct((b, s, NSA_HEADS * HEAD_DIM), F32),
                   jax.ShapeDtypeStruct((b, nb // CMP_QB, ns, CMP_QB * BLK), F32)],
        scratch_shapes=[pltpu.VMEM((CMP_QB, ncp // CMP_CHUNK, CMP_CHUNK, lanes), F32),
                        pltpu.VMEM((CMP_QB, HEAD_DIM, lanes), F32),
                        pltpu.VMEM((CMP_QB, ns + IMP_PAD, lanes), F32)],
        compiler_params=_cparams(("parallel", "arbitrary")),
        name="cmp_attn",
    )(qt, kcv, kcv_t, hct, wmapt)


def _sel_kernel(qt_ref, k_ref, vt_ref, pen_ref, band_ref, o_ref, s_scr, p_scr, m_scr, l_scr, alpha_scr, acc_scr):
    n = pl.program_id(1)
    lanes = NSA_HEADS * BLK
    sub = SEL_TILE // BLK
    per = BLK // SEL_BLK
    qt = qt_ref[0, 0]
    t = BLK * n + (_iota((BLK, lanes), 1) & (BLK - 1))

    def scores_into(kt, s_buf):
        for u in range(sub):
            ku = pl.multiple_of(kt * SEL_TILE + u * BLK, BLK)
            s = jnp.dot(k_ref[0, pl.ds(ku, BLK), :], qt, preferred_element_type=F32)
            s = s + band_ref[jnp.clip(n - sub * kt - u, 0, N_BAND - 1)]
            pen = jnp.concatenate(
                [jnp.broadcast_to(pen_ref[0, 0, pl.ds((sub * kt + u) * per + w, 1), :], (SEL_BLK, BLK))
                 for w in range(per)], axis=0)
            s_buf[u * BLK:(u + 1) * BLK, :] = s + jnp.concatenate([pen] * NSA_HEADS, axis=1)

    def add_values(kt, p_buf):
        k0 = pl.multiple_of(kt * SEL_TILE, SEL_TILE)
        pv = jnp.dot(vt_ref[0, :, pl.ds(k0, SEL_TILE)], p_buf[...], preferred_element_type=F32)
        acc_scr[...] = alpha_scr[...] * acc_scr[...] + pv

    def softmax_into(kt, s_buf, p_buf, causal):
        def piece(u):
            s = s_buf[u * BLK:(u + 1) * BLK, :]
            if causal:
                s = jnp.where(kt * SEL_TILE + u * BLK + _iota((BLK, lanes), 0) <= t, s, NEG)
            return s
        m_i = m_scr[...]
        mx = functools.reduce(jnp.maximum, [jnp.max(piece(u), axis=0, keepdims=True) for u in range(sub)])
        m_new = jnp.maximum(m_i, mx)
        alpha = jnp.exp(m_i - m_new)
        total = jnp.zeros((1, lanes), F32)
        for u in range(sub):
            p = jnp.exp(piece(u) - m_new)
            total = total + jnp.sum(p, axis=0, keepdims=True)
            p_buf[u * BLK:(u + 1) * BLK, :] = p.astype(BF16)
        m_scr[...] = m_new
        l_scr[...] = alpha * l_scr[...] + total
        alpha_scr[...] = alpha

    def stage(i, s_cur, s_nxt, p_cur, p_prev):
        add_values(jnp.maximum(i - 1, 0), p_prev)
        scores_into(i + 1, s_nxt)
        softmax_into(i, s_cur, p_cur, False)

    def finish(i, s_cur, p_cur, p_prev):
        add_values(jnp.maximum(i - 1, 0), p_prev)
        softmax_into(i, s_cur, p_cur, True)
        add_values(i, p_cur)
        o_t = acc_scr[...] / l_scr[...]
        o_ref[0] = jnp.concatenate(
            [_transpose_pad(o_t[:, h * BLK:(h + 1) * BLK], 0)[:, :HEAD_DIM] for h in range(NSA_HEADS)], axis=1)

    s0, s1, p0, p1 = s_scr.at[0], s_scr.at[1], p_scr.at[0], p_scr.at[1]
    m_scr[...] = jnp.full((1, lanes), NEG, F32)
    l_scr[...] = jnp.zeros((1, lanes), F32)
    alpha_scr[...] = jnp.ones((1, lanes), F32)
    acc_scr[...] = jnp.zeros((HEAD_DIM, lanes), F32)
    p1[...] = jnp.zeros((SEL_TILE, lanes), BF16)
    scores_into(0, s0)
    last = (BLK * n + BLK - 1) // SEL_TILE

    def pair(j, _):
        stage(2 * j, s0, s1, p0, p1)
        stage(2 * j + 1, s1, s0, p1, p0)
        return 0

    lax.fori_loop(0, last // 2, pair, 0)

    @pl.when(last % 2 == 1)
    def _():
        stage(last - 1, s0, s1, p0, p1)
        finish(last, s1, p1, p0)

    @pl.when(last % 2 == 0)
    def _():
        finish(last, s0, p0, p1)


def _sel_attn(qt, ks, vst, pen, band_t):
    b, nb, _, lanes = qt.shape
    s = nb * BLK
    ns = pen.shape[2]
    return pl.pallas_call(
        _sel_kernel,
        grid=(b, nb),
        in_specs=[pl.BlockSpec((1, 1, HEAD_DIM, lanes), lambda bi, n: (bi, n, 0, 0)),
                  pl.BlockSpec((1, s, HEAD_DIM), lambda bi, n: (bi, 0, 0)),
                  pl.BlockSpec((1, HEAD_DIM, s), lambda bi, n: (bi, 0, 0)),
                  pl.BlockSpec((1, 1, ns, BLK), lambda bi, n: (bi, n // CMP_QB, 0, n % CMP_QB)),
                  pl.BlockSpec((N_BAND, BLK, NSA_HEADS * BLK), lambda bi, n: (0, 0, 0))],
        out_specs=pl.BlockSpec((1, BLK, NSA_HEADS * HEAD_DIM), lambda bi, n: (bi, n, 0)),
        out_shape=jax.ShapeDtypeStruct((b, s, NSA_HEADS * HEAD_DIM), F32),
        scratch_shapes=[pltpu.VMEM((2, SEL_TILE, NSA_HEADS * BLK), F32),
                        pltpu.VMEM((2, SEL_TILE, NSA_HEADS * BLK), BF16),
                        pltpu.VMEM((1, NSA_HEADS * BLK), F32),
                        pltpu.VMEM((1, NSA_HEADS * BLK), F32),
                        pltpu.VMEM((1, NSA_HEADS * BLK), F32),
                        pltpu.VMEM((HEAD_DIM, NSA_HEADS * BLK), F32)],
        compiler_params=_cparams(("parallel", "arbitrary")),
        name="sel_attn",
    )(qt, ks, vst, pen, band_t)


def _band_kernel(*refs, window, groups, use_sinks, n_parts):
    q_ref = refs[0]
    k_refs = refs[1:1 + n_parts]
    v_refs = refs[1 + n_parts:1 + 2 * n_parts]
    band_ref, sink_ref, o_ref = refs[1 + 2 * n_parts:]
    step = pl.program_id(1)
    n_prev = window // BLK
    nk = (n_prev + 1) * BLK
    rows = 4 * BLK
    dist = n_prev * BLK + _iota((BLK, nk), 0) - _iota((BLK, nk), 1)
    in_window = (dist >= 0) & (dist < window)
    for g in range(groups):
        kwin = jnp.concatenate([r[0, g] for r in k_refs], axis=0)
        vwin = jnp.concatenate([r[0, g] for r in v_refs], axis=0)
        for qb in range(BAND_QB):
            n = BAND_QB * step + qb
            q = q_ref[0, 4 * g:4 * g + 4, qb * BLK:(qb + 1) * BLK, :].reshape(rows, HEAD_DIM)
            k = kwin[qb * BLK:qb * BLK + nk]
            v = vwin[qb * BLK:qb * BLK + nk]
            s = lax.dot_general(q, k, NT_DIMS, preferred_element_type=F32)
            bias = jnp.concatenate(
                [band_ref[n_prev - u, 4 * g:4 * g + 4].reshape(rows, BLK) for u in range(n_prev + 1)], axis=1)
            mask1 = in_window & (_iota((BLK, nk), 1) >= (n_prev - n) * BLK)
            mask = jnp.broadcast_to(mask1[None], (4, BLK, nk)).reshape(rows, nk)
            s = jnp.where(mask, s + bias, NEG)
            mx = jnp.max(s, axis=1, keepdims=True)
            if use_sinks:
                sink = sink_ref[g]
                mx = jnp.maximum(mx, sink)
            p = jnp.where(mask, jnp.exp(s - mx), 0.0)
            den = jnp.sum(p, axis=1, keepdims=True)
            if use_sinks:
                den = den + jnp.exp(sink - mx)
            o = jnp.dot((p / den).astype(BF16), v, preferred_element_type=F32)
            o_ref[0, qb * BLK:(qb + 1) * BLK, 4 * g * HEAD_DIM:4 * (g + 1) * HEAD_DIM] = _merge_heads(o, 4)


def _band_attn(q, k, v, band, sinks, window, use_sinks, name):
    b, hq, s, _ = q.shape
    g = hq // 4
    n_prev = window // BLK
    tq = BAND_QB * BLK
    if n_prev * BLK % tq == 0:
        back = n_prev * BLK // tq
        prev = [pl.BlockSpec((1, g, tq, HEAD_DIM), lambda bi, i, d=d: (bi, 0, jnp.maximum(i - d, 0), 0))
                for d in range(back, 0, -1)]
    else:
        assert n_prev == 1
        prev = [pl.BlockSpec((1, g, BLK, HEAD_DIM), lambda bi, i: (bi, 0, jnp.maximum(BAND_QB * i - 1, 0), 0))]
    kv_specs = prev + [pl.BlockSpec((1, g, tq, HEAD_DIM), lambda bi, i: (bi, 0, i, 0))]
    n_parts = len(kv_specs)
    return pl.pallas_call(
        functools.partial(_band_kernel, window=window, groups=g, use_sinks=use_sinks, n_parts=n_parts),
        grid=(b, s // tq),
        in_specs=[pl.BlockSpec((1, hq, tq, HEAD_DIM), lambda bi, i: (bi, 0, i, 0))] + kv_specs + kv_specs
                 + [pl.BlockSpec((n_prev + 1, hq, BLK, BLK), lambda bi, i: (0, 0, 0, 0)),
                    pl.BlockSpec((g, 4 * BLK, 1), lambda bi, i: (0, 0, 0))],
        out_specs=pl.BlockSpec((1, tq, hq * HEAD_DIM), lambda bi, i: (bi, i, 0)),
        out_shape=jax.ShapeDtypeStruct((b, s, hq * HEAD_DIM), F32),
        compiler_params=_cparams(("parallel", "arbitrary")),
        name=name,
    )(q, *([k] * n_parts), *([v] * n_parts), band, sinks)


def _sb_kernel(q_ref, k_ref, v_ref, o_ref):
    n = pl.program_id(1)
    width = SB_HEADS * HEAD_DIM
    lane_head = _iota((BLK, width), 1) // HEAD_DIM
    t = BLK * n + _iota((BLK, BLK), 0)
    col = _iota((BLK, BLK), 1)
    later = jnp.where(_iota((BLK, BLK), 0) > col, 1.0, 0.0).astype(BF16)
    zero = jnp.zeros((BLK, HEAD_DIM), BF16)
    qs = [jnp.concatenate([zero] * h + [q_ref[0, h]] + [zero] * (SB_HEADS - 1 - h), axis=1)
          for h in range(SB_HEADS)]

    def cond(c):
        top, carry, _ = c
        return (top >= 0) & (jnp.max(functools.reduce(jnp.maximum, carry)) > SB_LOG_FLOOR)

    def body(c):
        top, carry, acc = c
        ks, vs, befores = [], [], []
        for j in range(SB_GROUP):
            kb = top - j
            k0 = pl.multiple_of(jnp.maximum(kb, 0) * BLK, BLK)
            ks.append(k_ref[0, pl.ds(k0, BLK), :])
            vs.append(v_ref[0, pl.ds(k0, BLK), :])
            befores.append((kb >= 0) & (k0 + col < t))
        v_all = jnp.concatenate(vs, axis=0)
        new_carry = []
        for h in range(SB_HEADS):
            offset = carry[h]
            a_parts = []
            for j in range(SB_GROUP):
                z = lax.dot_general(qs[h], ks[j], NT_DIMS, preferred_element_type=F32)
                softplus = jnp.maximum(z, 0.0) + jnp.log(1.0 + jnp.exp(-jnp.abs(z)))
                log_not = jnp.where(befores[j], -softplus, 0.0)
                hi = log_not.astype(BF16)
                lo = (log_not - hi.astype(F32)).astype(BF16)
                tail = (jnp.dot(hi, later, preferred_element_type=F32)
                        + jnp.dot(lo, later, preferred_element_type=F32))
                a_parts.append(jnp.where(befores[j], jnp.exp(z + log_not + tail + offset), 0.0).astype(BF16))
                offset = offset + jnp.sum(log_not, axis=1, keepdims=True)
            pv = jnp.dot(jnp.concatenate(a_parts, axis=1), v_all, preferred_element_type=F32)
            acc = acc + jnp.where(lane_head == h, pv, 0.0)
            new_carry.append(offset)
        return top - SB_GROUP, tuple(new_carry), acc

    init = (n, tuple(jnp.zeros((BLK, 1), F32) for _ in range(SB_HEADS)), jnp.zeros((BLK, width), F32))
    _, _, acc = lax.while_loop(cond, body, init)
    o_ref[0] = acc


def _sb_attn(q, k, v):
    b, h, s, _ = q.shape
    width = h * HEAD_DIM
    return pl.pallas_call(
        _sb_kernel,
        grid=(b, s // BLK),
        in_specs=[pl.BlockSpec((1, h, BLK, HEAD_DIM), lambda bi, n: (bi, 0, n, 0)),
                  pl.BlockSpec((1, s, width), lambda bi, n: (bi, 0, 0)),
                  pl.BlockSpec((1, s, width), lambda bi, n: (bi, 0, 0))],
        out_specs=pl.BlockSpec((1, BLK, width), lambda bi, n: (bi, n, 0)),
        out_shape=jax.ShapeDtypeStruct((b, s, width), F32),
        compiler_params=_cparams(("parallel", "arbitrary")),
        name="sb_attn",
    )(q, k, v)


def _out_kernel(x_ref, oc_ref, os_ref, ow_ref, gt_ref, swa_ref, sb_ref, gg_ref, w_ref, gp_ref, o_ref):
    tm = x_ref.shape[1]
    wa = NSA_HEADS * HEAD_DIM
    wb = wa + SWA_HEADS * HEAD_DIM
    gates = jax.nn.sigmoid(gt_ref[0])
    lane = _iota((tm, wa), 1)

    def spread(branch):
        cols = [jnp.broadcast_to(gates[:, branch * NSA_HEADS + i:branch * NSA_HEADS + i + 1], (tm, wa))
                for i in range(NSA_HEADS)]
        out = cols[NSA_HEADS - 1]
        for i in range(NSA_HEADS - 2, -1, -1):
            out = jnp.where(lane < (i + 1) * HEAD_DIM, cols[i], out)
        return out

    o_nsa = spread(0) * oc_ref[0] + spread(1) * os_ref[0] + spread(2) * ow_ref[0]
    gg = gg_ref[...]
    mix = jnp.concatenate([_rms(o_nsa, gg[:, :wa]), _rms(swa_ref[0], gg[:, wa:wb]), _rms(sb_ref[0], gg[:, wb:])],
                          axis=1).astype(BF16)
    y = jnp.dot(mix, w_ref[...], preferred_element_type=F32)
    o_ref[0] = x_ref[0] + _rms(y, gp_ref[...])


def _out_proj(x, o_cmp, o_sel, o_win, gates, o_swa, o_sb, gg, w, gp, tm):
    b, s, d = x.shape
    row = lambda n: pl.BlockSpec((1, tm, n), lambda bi, i: (bi, i, 0))
    const = lambda a, c: pl.BlockSpec((a, c), lambda bi, i: (0, 0))
    return pl.pallas_call(
        _out_kernel,
        grid=(b, s // tm),
        in_specs=[row(d), row(256), row(256), row(256), row(128), row(512), row(256),
                  const(1, d), const(d, d), const(1, d)],
        out_specs=row(d),
        out_shape=jax.ShapeDtypeStruct((b, s, d), F32),
        compiler_params=_cparams(("parallel", "parallel")),
        name="out_proj",
    )(x, o_cmp, o_sel, o_win, gates, o_swa, o_sb, gg, w, gp)


def _ffn_kernel(x_ref, g1_ref, wg_ref, wu_ref, wd_ref, g2_ref, o_ref, *, n_chunks):
    x = x_ref[...]
    h = _rms(x, g1_ref[...]).astype(BF16)
    dff = wg_ref.shape[1]
    cw = dff // n_chunks
    f = jnp.zeros(x.shape, F32)
    for c in range(n_chunks):
        gate = jnp.dot(h, wg_ref[:, c * cw:(c + 1) * cw], preferred_element_type=F32)
        up = jnp.dot(h, wu_ref[:, c * cw:(c + 1) * cw], preferred_element_type=F32)
        a = (gate * jax.nn.sigmoid(gate) * up).astype(BF16)
        f = f + jnp.dot(a, wd_ref[c * cw:(c + 1) * cw, :], preferred_element_type=F32)
    o_ref[...] = x + _rms(f, g2_ref[...])


def _ffn(x, g1, wg, wu, wd, g2, tm):
    t, d = x.shape
    dff = wg.shape[1]
    row = pl.BlockSpec((tm, d), lambda i: (i, 0))
    const = lambda a, c: pl.BlockSpec((a, c), lambda i: (0, 0))
    return pl.pallas_call(
        functools.partial(_ffn_kernel, n_chunks=4),
        grid=(t // tm,),
        in_specs=[row, const(1, d), const(d, dff), const(d, dff), const(dff, d), const(1, d)],
        out_specs=row,
        out_shape=jax.ShapeDtypeStruct((t, d), F32),
        compiler_params=_cparams(("parallel",)),
        name="ffn",
    )(x, g1, wg, wu, wd, g2)


def _permute_w_in(w):
    scale = 1.0 / math.sqrt(HEAD_DIM)
    nq, kc, vc, rest, ng = w[:, :256], w[:, 256:320], w[:, 320:384], w[:, 384:640], w[:, 640:652]
    sq, skv = w[:, 652:1164], w[:, 1164:1420]
    bq, bkv = w[:, 1420:1676], w[:, 1676:2188]
    pad = jnp.zeros((w.shape[0], 128 - ng.shape[1]), w.dtype)
    return jnp.concatenate([nq * scale, rest, sq * scale, skv, bq * scale, bkv, kc, vc, ng, pad], axis=1)


def _cmp_to_sel(nc_pad, nc, ns):
    c0 = jnp.arange(nc_pad)[:, None] * CMP_STRIDE
    s0 = jnp.arange(ns)[None, :] * SEL_BLK
    ov = jnp.minimum(c0 + CMP_LEN, s0 + SEL_BLK) - jnp.maximum(c0, s0)
    w = jnp.clip(ov, 0, None).astype(F32) / CMP_LEN
    return jnp.where(jnp.arange(nc_pad)[:, None] < nc, w, 0.0).astype(BF16)


def kernel(x, rel_bias, ln_attn_pre, w_in, nsa_cmp_pos, nsa_phi_w1, nsa_phi_b1, nsa_phi_w2, swa_sinks,
           grp_norm_g, w_out, ln_attn_post, ln_ffn_pre, ffn_w_gate, ffn_w_up, ffn_w_down, ln_ffn_post):
    b, s, d = x.shape
    depth = w_in.shape[0]
    nch = s // CMP_STRIDE
    ns = s // SEL_BLK
    t = b * s

    band_nsa = _band_table(rel_bias, N_BAND, NSA_HEADS, 0)
    band_swa = _band_table(rel_bias, SWA_WINDOW // BLK + 1, SWA_HEADS, NSA_HEADS)
    band_nsa_t = _band_table_t(rel_bias, N_BAND)
    hct = _cmp_table(rel_bias, nch)
    wmapt = jnp.pad(_cmp_to_sel(nch, nch - 1, ns).T, ((0, IMP_PAD), (0, 0)))
    no_sinks = jnp.zeros((1, 4 * BLK, 1), F32)

    for l in range(depth):
        outs = _proj(x, ln_attn_pre[l][None], _permute_w_in(w_in[l]).astype(BF16), 512)
        qn, ks, vs, kw, vw, sq, sk, sv, bq, bk, bv, kc, vc, gates, qt = outs
        ckv = jnp.stack([kc.reshape(b, nch, CMP_STRIDE * HEAD_DIM), vc.reshape(b, nch, CMP_STRIDE * HEAD_DIM)])
        kcv, kcv_t = _compress(ckv, nsa_cmp_pos[l].reshape(2, 2, CMP_STRIDE * HEAD_DIM),
                               nsa_phi_w1[l].reshape(2, 2, CMP_STRIDE * HEAD_DIM, CMP_HIDDEN).astype(BF16),
                               nsa_phi_b1[l][:, None, :], nsa_phi_w2[l].astype(BF16))
        o_cmp, selmask = _cmp_attn(qt, kcv, kcv_t, hct, wmapt)
        o_sel = _sel_attn(qt, ks, vs, selmask, band_nsa_t)
        o_win = _band_attn(qn, kw[:, None], vw[:, None], band_nsa, no_sinks, NSA_WINDOW, False, "win_attn")
        sinks = jnp.broadcast_to(swa_sinks[l].reshape(SWA_KV_HEADS, 4, 1, 1),
                                 (SWA_KV_HEADS, 4, BLK, 1)).reshape(SWA_KV_HEADS, 4 * BLK, 1)
        o_swa = _band_attn(sq, sk, sv, band_swa, sinks, SWA_WINDOW, True, "swa_attn")
        o_sb = _sb_attn(bq, bk, bv)
        x = _out_proj(x, o_cmp, o_sel, o_win, gates, o_swa, o_sb, grp_norm_g[l][None], w_out[l].astype(BF16),
                      ln_attn_post[l][None], 256)
        x = _ffn(x.reshape(t, d), ln_ffn_pre[l][None], ffn_w_gate[l].astype(BF16), ffn_w_up[l].astype(BF16),
                 ffn_w_down[l].astype(BF16), ln_ffn_post[l][None], 256).reshape(b, s, d)
    return x
```

```python
import functools
import math

import jax
import jax.numpy as jnp
from jax import lax
from jax.experimental import pallas as pl
from jax.experimental.pallas import tpu as pltpu

F32 = jnp.float32
BF16 = jnp.bfloat16

HEAD_DIM = 64
BLK = 128
NSA_HEADS = 4
CMP_LEN = 32
CMP_STRIDE = 16
CMP_HIDDEN = 256
SEL_BLK = 64
N_SEL = 16
N_LOCAL = 2
NSA_WINDOW = 512
SWA_HEADS = 8
SWA_KV_HEADS = 2
SWA_WINDOW = 128
SB_HEADS = 4
N_BUCKETS = 32
T5_MAX_DISTANCE = 4096
RMS_EPS = 1e-6
FORCE_SCORE = 1e6
NEG = -1e30
SEL_TILE = 512
SB_GROUP = 3
SB_LOG_FLOOR = -104.0
PICKED = -3e38
CMP_QB = 4
VT_ROWS = HEAD_DIM + 16
BAND_QB = 2
CMP_CHUNK = 128
IMP_PAD = 16
IMP_ROWS = CMP_CHUNK // 4 + IMP_PAD
N_BAND = 25
VMEM_LIMIT = 56 * 1024 * 1024

NT_DIMS = (((1,), (1,)), ((), ()))


def _cparams(sem):
    return pltpu.CompilerParams(dimension_semantics=sem, vmem_limit_bytes=VMEM_LIMIT)


def _rms(x, g):
    ms = jnp.mean(x * x, axis=-1, keepdims=True)
    return x * lax.rsqrt(ms + RMS_EPS) * g


def _iota(shape, axis):
    return lax.broadcasted_iota(jnp.int32, shape, axis)


def _bias_of_dist(dist, tab_ref, head):
    n = jnp.maximum(dist, 0)
    nf = jnp.maximum(n, 1).astype(F32)
    exact = N_BUCKETS // 2
    large = exact + (jnp.log(nf / exact) / math.log(T5_MAX_DISTANCE / exact)
                     * (N_BUCKETS - exact)).astype(jnp.int32)
    large = jnp.minimum(large, N_BUCKETS - 1)
    bucket = jnp.where(n < exact, n, large)
    val = jnp.full(dist.shape, tab_ref[0, head], F32)
    for k in range(1, N_BUCKETS):
        val = jnp.where(bucket == k, tab_ref[k, head], val)
    return val


def _band_table_kernel(tab_ref, o_ref, *, head_off):
    m = pl.program_id(0)
    h = pl.program_id(1)
    dist = BLK * m + _iota((BLK, BLK), 0) - _iota((BLK, BLK), 1)
    o_ref[0, 0] = _bias_of_dist(dist, tab_ref, h + head_off)


def _band_table(rel_bias, n_band, n_heads, head_off):
    return pl.pallas_call(
        functools.partial(_band_table_kernel, head_off=head_off),
        grid=(n_band, n_heads),
        in_specs=[pl.BlockSpec(memory_space=pltpu.SMEM)],
        out_specs=pl.BlockSpec((1, 1, BLK, BLK), lambda m, h: (m, h, 0, 0)),
        out_shape=jax.ShapeDtypeStruct((n_band, n_heads, BLK, BLK), F32),
        compiler_params=_cparams(("arbitrary", "arbitrary")),
        name="band_table",
    )(rel_bias)


def _band_table_t_kernel(tab_ref, o_ref, *, head_off, shift):
    m = pl.program_id(0)
    h = pl.program_id(1) + head_off
    dist = BLK * m + _iota((BLK, BLK), 1) - _iota((BLK, BLK), 0)
    bias = _bias_of_dist(dist, tab_ref, h)
    o_ref[0] = bias - tab_ref[N_BUCKETS - 1, h] if shift else bias


def _band_table_t(rel_bias, n_band, n_heads, head_off, shift):
    return pl.pallas_call(
        functools.partial(_band_table_t_kernel, head_off=head_off, shift=shift),
        grid=(n_band, n_heads),
        in_specs=[pl.BlockSpec(memory_space=pltpu.SMEM)],
        out_specs=pl.BlockSpec((1, BLK, BLK), lambda m, h: (m, 0, h)),
        out_shape=jax.ShapeDtypeStruct((n_band, BLK, n_heads * BLK), F32),
        compiler_params=_cparams(("arbitrary", "arbitrary")),
        name="band_table_t",
    )(rel_bias)


def _cmp_table_kernel(tab_ref, o_ref, *, ncp):
    h = pl.program_id(0)
    rel = _iota((2 * ncp, BLK), 0) - (ncp - 8)
    dist = _iota((2 * ncp, BLK), 1) - CMP_STRIDE * rel - (CMP_LEN - 1)
    o_ref[...] = _bias_of_dist(dist, tab_ref, h)


def _cmp_table(rel_bias, ncp):
    return pl.pallas_call(
        functools.partial(_cmp_table_kernel, ncp=ncp),
        grid=(NSA_HEADS,),
        in_specs=[pl.BlockSpec(memory_space=pltpu.SMEM)],
        out_specs=pl.BlockSpec((2 * ncp, BLK), lambda h: (0, h)),
        out_shape=jax.ShapeDtypeStruct((2 * ncp, NSA_HEADS * BLK), F32),
        compiler_params=_cparams(("arbitrary",)),
        name="cmp_table",
    )(rel_bias)


def _proj_kernel(x_ref, g_ref, w_ref, qt_ref, ks_ref, vs_ref, kw_ref, vw_ref, sqt_ref, sk_ref, sv_ref,
                 bq_ref, bk_ref, bv_ref, kc_ref, vc_ref, gt_ref):
    h = _rms(x_ref[0], g_ref[...]).astype(BF16)
    tm = x_ref.shape[1]

    def seg(a, b):
        return jnp.dot(h, w_ref[:, a:b], preferred_element_type=F32)

    def heads(ref, a, n):
        y = seg(a, a + n * HEAD_DIM).astype(BF16)
        for i in range(n):
            ref[0, i] = y[:, i * HEAD_DIM:(i + 1) * HEAD_DIM]

    def queries_t(ref, a, n):
        yt = seg(a, a + n * HEAD_DIM).T
        for j in range(tm // BLK):
            for i in range(n):
                ref[0, j, :, i * BLK:(i + 1) * BLK] = (
                    yt[i * HEAD_DIM:(i + 1) * HEAD_DIM, j * BLK:(j + 1) * BLK].astype(BF16))

    ones_rows = jnp.where(_iota((VT_ROWS - HEAD_DIM, tm), 0) == 0, 1.0, 0.0)

    def values_t(yt):
        return jnp.concatenate([yt, ones_rows], axis=0).astype(BF16)

    queries_t(qt_ref, 0, NSA_HEADS)
    y = seg(256, 512)
    vs_ref[0] = values_t(y[:, 0:128].T[64:128])
    vw_ref[0] = values_t(y[:, 128:256].T[64:128])
    y = y.astype(BF16)
    ks_ref[0] = y[:, 0:64]
    kw_ref[0] = y[:, 128:192]
    queries_t(sqt_ref, 512, SWA_HEADS)
    heads(sk_ref, 1024, SWA_KV_HEADS)
    yt = seg(1152, 1280).T
    for i in range(SWA_KV_HEADS):
        sv_ref[0, i] = values_t(yt[i * HEAD_DIM:(i + 1) * HEAD_DIM])
    heads(bq_ref, 1280, SB_HEADS)
    bk_ref[0] = seg(1536, 1792).astype(BF16)
    bv_ref[0] = seg(1792, 2048).astype(BF16)
    y = seg(2048, 2304)
    kc_ref[0] = y[:, 0:64]
    vc_ref[0] = y[:, 64:128]
    gt_ref[0] = y[:, 128:256]


def _proj(x, g, w, tm):
    b, s, d = x.shape
    nw = w.shape[1]
    hd = lambda n: pl.BlockSpec((1, n, tm, HEAD_DIM), lambda bi, i: (bi, 0, i, 0))
    flat = lambda n: pl.BlockSpec((1, tm, n), lambda bi, i: (bi, i, 0))
    hshape = lambda n: jax.ShapeDtypeStruct((b, n, s, HEAD_DIM), BF16)
    fshape = lambda n, dt: jax.ShapeDtypeStruct((b, s, n), dt)
    qt_spec = lambda n: pl.BlockSpec((1, tm // BLK, HEAD_DIM, n * BLK), lambda bi, i: (bi, i, 0, 0))
    qt_shape = lambda n: jax.ShapeDtypeStruct((b, s // BLK, HEAD_DIM, n * BLK), BF16)
    vt_spec = pl.BlockSpec((1, VT_ROWS, tm), lambda bi, i: (bi, 0, i))
    vt_shape = jax.ShapeDtypeStruct((b, VT_ROWS, s), BF16)
    return pl.pallas_call(
        _proj_kernel,
        grid=(b, s // tm),
        in_specs=[pl.BlockSpec((1, tm, d), lambda bi, i: (bi, i, 0)),
                  pl.BlockSpec((1, d), lambda bi, i: (0, 0)),
                  pl.BlockSpec((d, nw), lambda bi, i: (0, 0))],
        out_specs=[qt_spec(NSA_HEADS), flat(64), vt_spec, flat(64), vt_spec,
                   qt_spec(SWA_HEADS), hd(SWA_KV_HEADS),
                   pl.BlockSpec((1, SWA_KV_HEADS, VT_ROWS, tm), lambda bi, i: (bi, 0, 0, i)),
                   hd(SB_HEADS), flat(256), flat(256),
                   flat(64), flat(64), flat(128)],
        out_shape=[qt_shape(NSA_HEADS), fshape(64, BF16), vt_shape, fshape(64, BF16), vt_shape,
                   qt_shape(SWA_HEADS), hshape(SWA_KV_HEADS),
                   jax.ShapeDtypeStruct((b, SWA_KV_HEADS, VT_ROWS, s), BF16),
                   hshape(SB_HEADS), fshape(256, BF16), fshape(256, BF16),
                   fshape(64, F32), fshape(64, F32), fshape(128, F32)],
        compiler_params=_cparams(("parallel", "parallel")),
        name="proj",
    )(x, g, w)


def _compress_kernel(c_ref, pos_ref, w1_ref, b1_ref, w2_ref, o_ref, ot_ref):
    c = c_ref[0, 0]
    nch = c.shape[0]
    xa = (c + pos_ref[0, 0:1]).astype(BF16)
    xb = (c + pos_ref[0, 1:2]).astype(BF16)
    p = jnp.dot(xa, w1_ref[0, 0], preferred_element_type=F32)
    q = jnp.dot(xb, w1_ref[0, 1], preferred_element_type=F32)
    hid = p + pltpu.roll(q, nch - 1, 0) + b1_ref[0]
    hid = hid * jax.nn.sigmoid(hid)
    out = jnp.dot(hid.astype(BF16), w2_ref[0], preferred_element_type=F32)
    row = _iota(out.shape, 0)
    out = jnp.where(row < nch - 1, out, 0.0)
    o_ref[0, 0] = out.astype(BF16)
    ot_ref[0, 0] = jnp.concatenate([out, jnp.zeros_like(out)], axis=1).T[:HEAD_DIM].astype(BF16)


def _compress(ckv, pos, w1, b1, w2):
    _, b, nch, cw = ckv.shape
    return pl.pallas_call(
        _compress_kernel,
        grid=(2, b),
        in_specs=[pl.BlockSpec((1, 1, nch, cw), lambda j, bi: (j, bi, 0, 0)),
                  pl.BlockSpec((1, 2, cw), lambda j, bi: (j, 0, 0)),
                  pl.BlockSpec((1, 2, cw, CMP_HIDDEN), lambda j, bi: (j, 0, 0, 0)),
                  pl.BlockSpec((1, 1, CMP_HIDDEN), lambda j, bi: (j, 0, 0)),
                  pl.BlockSpec((1, CMP_HIDDEN, HEAD_DIM), lambda j, bi: (j, 0, 0))],
        out_specs=[pl.BlockSpec((1, 1, nch, HEAD_DIM), lambda j, bi: (j, bi, 0, 0)),
                   pl.BlockSpec((1, 1, HEAD_DIM, nch), lambda j, bi: (j, bi, 0, 0))],
        out_shape=[jax.ShapeDtypeStruct((2, b, nch, HEAD_DIM), BF16),
                   jax.ShapeDtypeStruct((2, b, HEAD_DIM, nch), BF16)],
        compiler_params=_cparams(("parallel", "parallel")),
        name="compress",
    )(ckv, pos, w1, b1, w2)


def _merge_heads(o, n):
    return jnp.concatenate([o[i * BLK:(i + 1) * BLK] for i in range(n)], axis=1)


def _transpose_pad(x, axis):
    pad = jnp.zeros(x.shape, x.dtype)
    return jnp.concatenate([x, pad], axis=axis).T


def _cmp_kernel(qt_ref, kc_ref, vct_ref, hct_ref, wmapt_ref, o_ref, pen_ref, s_scr, oacc_scr, iacc_scr):
    step = pl.program_id(1)
    ncp = kc_ref.shape[2]
    ns = wmapt_ref.shape[0] - IMP_PAD
    lanes = NSA_HEADS * BLK
    row_c = _iota((CMP_CHUNK, lanes), 0)
    lane_q = _iota((CMP_CHUNK, lanes), 1) & (BLK - 1)
    n_chunks = (CMP_QB * step) // (CMP_CHUNK // 8) + 1

    def score_chunk(ch, mxs):
        c0 = pl.multiple_of(ch * CMP_CHUNK, CMP_CHUNK)
        kc = kc_ref[0, 0, pl.ds(c0, CMP_CHUNK), :]
        out = []
        for qb in range(CMP_QB):
            n = CMP_QB * step + qb
            s = jnp.dot(kc, qt_ref[0, qb], preferred_element_type=F32)
            bias = hct_ref[pl.ds(pl.multiple_of(ncp - 8 - 8 * n + c0, 8), CMP_CHUNK), :]
            visible = (BLK * n + lane_q - CMP_STRIDE * (c0 + row_c) - (CMP_LEN - 1)) >= 0
            s = jnp.where(visible, s + bias, NEG)
            s_scr[qb, ch] = s
            out.append(jnp.maximum(mxs[qb], jnp.max(s, axis=0, keepdims=True)))
        return tuple(out)

    mxs = lax.fori_loop(0, n_chunks, score_chunk, tuple(jnp.full((1, lanes), NEG, F32) for _ in range(CMP_QB)))
    ms = [jnp.where(mx <= 0.5 * NEG, 0.0, mx) for mx in mxs]
    oacc_scr[...] = jnp.zeros(oacc_scr.shape, F32)
    iacc_scr[...] = jnp.zeros(iacc_scr.shape, F32)

    def prob_chunk(ch, ls):
        c0 = pl.multiple_of(ch * CMP_CHUNK, CMP_CHUNK)
        vct = vct_ref[0, 0, :, pl.ds(c0, CMP_CHUNK)]
        r0 = pl.multiple_of(ch * (CMP_CHUNK // 4), 16)
        wmt = wmapt_ref[pl.ds(r0, IMP_ROWS), pl.ds(c0, CMP_CHUNK)]
        out = []
        for qb in range(CMP_QB):
            p = jnp.exp(s_scr[qb, ch] - ms[qb])
            pb = p.astype(BF16)
            oacc_scr[qb] += jnp.dot(vct, pb, preferred_element_type=F32)
            iacc_scr[qb, pl.ds(r0, IMP_ROWS), :] += jnp.dot(wmt, pb, preferred_element_type=F32)
            out.append(ls[qb] + jnp.sum(p, axis=0, keepdims=True))
        return tuple(out)

    ls = lax.fori_loop(0, n_chunks, prob_chunk, tuple(jnp.zeros((1, lanes), F32) for _ in range(CMP_QB)))
    imps = []
    for qb in range(CMP_QB):
        inv = 1.0 / jnp.maximum(ls[qb], 1e-30)
        o_t = oacc_scr[qb] * inv
        o_ref[0, qb * BLK:(qb + 1) * BLK, :] = jnp.concatenate(
            [_transpose_pad(o_t[:, h * BLK:(h + 1) * BLK], 0)[:, :HEAD_DIM] for h in range(NSA_HEADS)], axis=1)
        w = iacc_scr[qb, 0:ns, :] * inv
        imps.append(functools.reduce(jnp.add, [w[:, h * BLK:(h + 1) * BLK] for h in range(NSA_HEADS)]))

    imp = jnp.concatenate(imps, axis=1)
    shape = (ns, CMP_QB * BLK)
    j = _iota(shape, 0)
    cur = (CMP_QB * BLK * step + _iota(shape, 1)) // SEL_BLK
    valid = j <= cur
    forced = valid & ((j == 0) | (j > cur - N_LOCAL))
    score = jnp.where(forced, PICKED, jnp.where(valid, imp, NEG))
    jf = j.astype(F32)
    for _ in range(min(N_SEL, ns) - (N_LOCAL + 1)):
        best = jnp.max(score, axis=0, keepdims=True)
        first = jnp.min(jnp.where(score == best, jf, float(ns)), axis=0, keepdims=True)
        score = jnp.where(jf == first, PICKED, score)
    pen_ref[0, 0] = jnp.where(valid & (score == PICKED), 0.0, NEG)


def _cmp_attn(qt, kcv, kcv_t, hct, wmapt):
    b, nb, _, lanes = qt.shape
    ncp = kcv.shape[2]
    ns = wmapt.shape[0] - IMP_PAD
    s = nb * BLK
    return pl.pallas_call(
        _cmp_kernel,
        grid=(b, nb // CMP_QB),
        in_specs=[pl.BlockSpec((1, CMP_QB, HEAD_DIM, lanes), lambda bi, i: (bi, i, 0, 0)),
                  pl.BlockSpec((1, 1, ncp, HEAD_DIM), lambda bi, i: (0, bi, 0, 0)),
                  pl.BlockSpec((1, 1, HEAD_DIM, ncp), lambda bi, i: (1, bi, 0, 0)),
                  pl.BlockSpec((2 * ncp, lanes), lambda bi, i: (0, 0)),
                  pl.BlockSpec((ns + IMP_PAD, ncp), lambda bi, i: (0, 0))],
        out_specs=[pl.BlockSpec((1, CMP_QB * BLK, NSA_HEADS * HEAD_DIM), lambda bi, i: (bi, i, 0)),
                   pl.BlockSpec((1, 1, ns, CMP_QB * BLK), lambda bi, i: (bi, i, 0, 0))],
        out_shape=[jax.ShapeDtypeStruct((b, s, NSA_HEADS * HEAD_DIM), F32),
                   jax.ShapeDtypeStruct((b, nb // CMP_QB, ns, CMP_QB * BLK), F32)],
        scratch_shapes=[pltpu.VMEM((CMP_QB, ncp // CMP_CHUNK, CMP_CHUNK, lanes), F32),
                        pltpu.VMEM((CMP_QB, HEAD_DIM, lanes), F32),
                        pltpu.VMEM((CMP_QB, ns + IMP_PAD, lanes), F32)],
        compiler_params=_cparams(("parallel", "arbitrary")),
        name="cmp_attn",
    )(qt, kcv, kcv_t, hct, wmapt)


def _sel_kernel(qt_ref, k_ref, vt_ref, pen_ref, band_ref, o_ref, s_scr, p_scr, m_scr, alpha_scr, acc_scr):
    n = pl.program_id(1)
    lanes = NSA_HEADS * BLK
    sub = SEL_TILE // BLK
    per = BLK // SEL_BLK
    qt = qt_ref[0, 0]
    t = BLK * n + (_iota((BLK, lanes), 1) & (BLK - 1))

    def scores_into(kt, s_buf, far):
        for u in range(sub):
            ku = pl.multiple_of(kt * SEL_TILE + u * BLK, BLK)
            s = jnp.dot(k_ref[0, pl.ds(ku, BLK), :], qt, preferred_element_type=F32)
            if not far:
                s = s + band_ref[jnp.clip(n - sub * kt - u, 0, N_BAND - 1)]
            pen = jnp.concatenate(
                [jnp.broadcast_to(pen_ref[0, 0, pl.ds((sub * kt + u) * per + w, 1), :], (SEL_BLK, BLK))
                 for w in range(per)], axis=0)
            s_buf[u * BLK:(u + 1) * BLK, :] = s + jnp.concatenate([pen] * NSA_HEADS, axis=1)

    def add_values(kt, p_buf):
        k0 = pl.multiple_of(kt * SEL_TILE, SEL_TILE)
        pv = jnp.dot(vt_ref[0, :, pl.ds(k0, SEL_TILE)], p_buf[...], preferred_element_type=F32)
        acc_scr[...] = alpha_scr[...] * acc_scr[...] + pv

    def softmax_into(kt, s_buf, p_buf, causal):
        def piece(u):
            s = s_buf[u * BLK:(u + 1) * BLK, :]
            if causal:
                s = jnp.where(kt * SEL_TILE + u * BLK + _iota((BLK, lanes), 0) <= t, s, NEG)
            return s
        m_i = m_scr[...]
        mx = functools.reduce(jnp.maximum, [jnp.max(piece(u), axis=0, keepdims=True) for u in range(sub)])
        m_new = jnp.maximum(m_i, mx)
        for u in range(sub):
            p_buf[u * BLK:(u + 1) * BLK, :] = jnp.exp(piece(u) - m_new).astype(BF16)
        m_scr[...] = m_new
        alpha_scr[...] = jnp.exp(m_i - m_new)

    def stage(i, s_cur, s_nxt, p_cur, p_prev, far):
        add_values(jnp.maximum(i - 1, 0), p_prev)
        scores_into(i + 1, s_nxt, far)
        softmax_into(i, s_cur, p_cur, False)

    def finish(i, s_cur, p_cur, p_prev):
        add_values(jnp.maximum(i - 1, 0), p_prev)
        softmax_into(i, s_cur, p_cur, True)
        add_values(i, p_cur)
        acc = acc_scr[...]
        o_t = acc[:HEAD_DIM] / acc[HEAD_DIM:HEAD_DIM + 1]
        o_ref[0] = jnp.concatenate(
            [_transpose_pad(o_t[:, h * BLK:(h + 1) * BLK], 0)[:, :HEAD_DIM] for h in range(NSA_HEADS)], axis=1)

    s0, s1, p0, p1 = s_scr.at[0], s_scr.at[1], p_scr.at[0], p_scr.at[1]
    m_scr[...] = jnp.full((1, lanes), NEG, F32)
    alpha_scr[...] = jnp.ones((1, lanes), F32)
    acc_scr[...] = jnp.zeros(acc_scr.shape, F32)
    p1[...] = jnp.zeros((SEL_TILE, lanes), BF16)
    scores_into(0, s0, False)
    last = (BLK * n + BLK - 1) // SEL_TILE
    n_far = jnp.maximum((n - (N_BAND - 2)) // sub, 0)
    far_pairs = jnp.minimum(jnp.maximum((n_far - 1) // 2, 0), last // 2)

    def pair(j, _, far):
        stage(2 * j, s0, s1, p0, p1, far)
        stage(2 * j + 1, s1, s0, p1, p0, far)
        return 0

    lax.fori_loop(0, far_pairs, functools.partial(pair, far=True), 0)
    lax.fori_loop(far_pairs, last // 2, functools.partial(pair, far=False), 0)

    @pl.when(last % 2 == 1)
    def _():
        stage(last - 1, s0, s1, p0, p1, False)
        finish(last, s1, p1, p0)

    @pl.when(last % 2 == 0)
    def _():
        finish(last, s0, p0, p1)


def _sel_attn(qt, ks, vst, pen, band_t):
    b, nb, _, lanes = qt.shape
    s = nb * BLK
    ns = pen.shape[2]
    return pl.pallas_call(
        _sel_kernel,
        grid=(b, nb),
        in_specs=[pl.BlockSpec((1, 1, HEAD_DIM, lanes), lambda bi, n: (bi, n, 0, 0)),
                  pl.BlockSpec((1, s, HEAD_DIM), lambda bi, n: (bi, 0, 0)),
                  pl.BlockSpec((1, VT_ROWS, s), lambda bi, n: (bi, 0, 0)),
                  pl.BlockSpec((1, 1, ns, BLK), lambda bi, n: (bi, n // CMP_QB, 0, n % CMP_QB)),
                  pl.BlockSpec((N_BAND, BLK, NSA_HEADS * BLK), lambda bi, n: (0, 0, 0))],
        out_specs=pl.BlockSpec((1, BLK, NSA_HEADS * HEAD_DIM), lambda bi, n: (bi, n, 0)),
        out_shape=jax.ShapeDtypeStruct((b, s, NSA_HEADS * HEAD_DIM), F32),
        scratch_shapes=[pltpu.VMEM((2, SEL_TILE, NSA_HEADS * BLK), F32),
                        pltpu.VMEM((2, SEL_TILE, NSA_HEADS * BLK), BF16),
                        pltpu.VMEM((1, NSA_HEADS * BLK), F32),
                        pltpu.VMEM((1, NSA_HEADS * BLK), F32),
                        pltpu.VMEM((VT_ROWS, NSA_HEADS * BLK), F32)],
        compiler_params=_cparams(("parallel", "arbitrary")),
        name="sel_attn",
    )(qt, ks, vst, pen, band_t)


def _band_kernel(*refs, window, groups, use_sinks, n_parts):
    qt_ref = refs[0]
    k_refs = refs[1:1 + n_parts]
    vt_refs = refs[1 + n_parts:1 + 2 * n_parts]
    band_ref, sink_ref, o_ref = refs[1 + 2 * n_parts:]
    step = pl.program_id(1)
    n_prev = window // BLK
    nk = (n_prev + 1) * BLK
    lanes = 4 * BLK
    row = _iota((nk, lanes), 0)
    dist = n_prev * BLK + (_iota((nk, lanes), 1) & (BLK - 1)) - row
    in_window = (dist >= 0) & (dist < window)
    for g in range(groups):
        kwin = jnp.concatenate([r[0, g] for r in k_refs], axis=0)
        vtwin = jnp.concatenate([r[0, g] for r in vt_refs], axis=1)
        bias = jnp.concatenate([band_ref[n_prev - u, :, g * lanes:(g + 1) * lanes] for u in range(n_prev + 1)],
                               axis=0)
        for qb in range(BAND_QB):
            n = BAND_QB * step + qb
            qt = qt_ref[0, qb, :, g * lanes:(g + 1) * lanes]
            s = jnp.dot(kwin[qb * BLK:qb * BLK + nk], qt, preferred_element_type=F32)
            s = jnp.where(in_window & (row >= (n_prev - n) * BLK), s + bias, NEG)
            mx = jnp.max(s, axis=0, keepdims=True)
            if use_sinks:
                sink = sink_ref[g]
                mx = jnp.maximum(mx, sink)
            p = jnp.exp(s - mx).astype(BF16)
            acc = jnp.dot(vtwin[:, qb * BLK:qb * BLK + nk], p, preferred_element_type=F32)
            den = acc[HEAD_DIM:HEAD_DIM + 1]
            if use_sinks:
                den = den + jnp.exp(sink - mx)
            o_t = acc[:HEAD_DIM] / den
            o_ref[0, qb * BLK:(qb + 1) * BLK, 4 * g * HEAD_DIM:4 * (g + 1) * HEAD_DIM] = jnp.concatenate(
                [_transpose_pad(o_t[:, h * BLK:(h + 1) * BLK], 0)[:, :HEAD_DIM] for h in range(4)], axis=1)


def _band_attn(qt, k, vt, band_t, sinks, window, use_sinks, name):
    b, nb, _, width = qt.shape
    g = width // (4 * BLK)
    s = nb * BLK
    n_prev = window // BLK
    tq = BAND_QB * BLK
    if n_prev * BLK % tq == 0:
        piece = tq
        starts = [lambda i, d=d: jnp.maximum(i - d, 0) for d in range(n_prev * BLK // tq, 0, -1)]
    else:
        assert n_prev == 1
        piece = BLK
        starts = [lambda i: jnp.maximum(BAND_QB * i - 1, 0)]
    k_specs = [pl.BlockSpec((1, g, piece, HEAD_DIM), lambda bi, i, f=f: (bi, 0, f(i), 0)) for f in starts]
    k_specs.append(pl.BlockSpec((1, g, tq, HEAD_DIM), lambda bi, i: (bi, 0, i, 0)))
    vt_specs = [pl.BlockSpec((1, g, VT_ROWS, piece), lambda bi, i, f=f: (bi, 0, 0, f(i))) for f in starts]
    vt_specs.append(pl.BlockSpec((1, g, VT_ROWS, tq), lambda bi, i: (bi, 0, 0, i)))
    n_parts = len(k_specs)
    return pl.pallas_call(
        functools.partial(_band_kernel, window=window, groups=g, use_sinks=use_sinks, n_parts=n_parts),
        grid=(b, s // tq),
        in_specs=[pl.BlockSpec((1, BAND_QB, HEAD_DIM, width), lambda bi, i: (bi, i, 0, 0))] + k_specs + vt_specs
                 + [pl.BlockSpec((n_prev + 1, BLK, width), lambda bi, i: (0, 0, 0)),
                    pl.BlockSpec((g, 1, 4 * BLK), lambda bi, i: (0, 0, 0))],
        out_specs=pl.BlockSpec((1, tq, g * 4 * HEAD_DIM), lambda bi, i: (bi, i, 0)),
        out_shape=jax.ShapeDtypeStruct((b, s, g * 4 * HEAD_DIM), F32),
        compiler_params=_cparams(("parallel", "arbitrary")),
        name=name,
    )(qt, *([k] * n_parts), *([vt] * n_parts), band_t, sinks)


def _sb_kernel(q_ref, k_ref, v_ref, o_ref):
    n = pl.program_id(1)
    width = SB_HEADS * HEAD_DIM
    lane_head = _iota((BLK, width), 1) // HEAD_DIM
    t = BLK * n + _iota((BLK, BLK), 0)
    col = _iota((BLK, BLK), 1)
    later = jnp.where(_iota((BLK, BLK), 0) > col, 1.0, 0.0).astype(BF16)
    zero = jnp.zeros((BLK, HEAD_DIM), BF16)
    qs = [jnp.concatenate([zero] * h + [q_ref[0, h]] + [zero] * (SB_HEADS - 1 - h), axis=1)
          for h in range(SB_HEADS)]

    def cond(c):
        top, carry, _ = c
        return (top >= 0) & (jnp.max(functools.reduce(jnp.maximum, carry)) > SB_LOG_FLOOR)

    def body(c):
        top, carry, acc = c
        ks, vs, befores = [], [], []
        for j in range(SB_GROUP):
            kb = top - j
            k0 = pl.multiple_of(jnp.maximum(kb, 0) * BLK, BLK)
            ks.append(k_ref[0, pl.ds(k0, BLK), :])
            vs.append(v_ref[0, pl.ds(k0, BLK), :])
            befores.append((kb >= 0) & (k0 + col < t))
        v_all = jnp.concatenate(vs, axis=0)
        new_carry = []
        for h in range(SB_HEADS):
            offset = carry[h]
            a_parts = []
            for j in range(SB_GROUP):
                z = lax.dot_general(qs[h], ks[j], NT_DIMS, preferred_element_type=F32)
                softplus = jnp.maximum(z, 0.0) + jnp.log(1.0 + jnp.exp(-jnp.abs(z)))
                log_not = jnp.where(befores[j], -softplus, 0.0)
                hi = log_not.astype(BF16)
                lo = (log_not - hi.astype(F32)).astype(BF16)
                tail = (jnp.dot(hi, later, preferred_element_type=F32)
                        + jnp.dot(lo, later, preferred_element_type=F32))
                a_parts.append(jnp.where(befores[j], jnp.exp(z + log_not + tail + offset), 0.0).astype(BF16))
                offset = offset + jnp.sum(log_not, axis=1, keepdims=True)
            pv = jnp.dot(jnp.concatenate(a_parts, axis=1), v_all, preferred_element_type=F32)
            acc = acc + jnp.where(lane_head == h, pv, 0.0)
            new_carry.append(offset)
        return top - SB_GROUP, tuple(new_carry), acc

    init = (n, tuple(jnp.zeros((BLK, 1), F32) for _ in range(SB_HEADS)), jnp.zeros((BLK, width), F32))
    _, _, acc = lax.while_loop(cond, body, init)
    o_ref[0] = acc


def _sb_attn(q, k, v):
    b, h, s, _ = q.shape
    width = h * HEAD_DIM
    return pl.pallas_call(
        _sb_kernel,
        grid=(b, s // BLK),
        in_specs=[pl.BlockSpec((1, h, BLK, HEAD_DIM), lambda bi, n: (bi, 0, n, 0)),
                  pl.BlockSpec((1, s, width), lambda bi, n: (bi, 0, 0)),
                  pl.BlockSpec((1, s, width), lambda bi, n: (bi, 0, 0))],
        out_specs=pl.BlockSpec((1, BLK, width), lambda bi, n: (bi, n, 0)),
        out_shape=jax.ShapeDtypeStruct((b, s, width), F32),
        compiler_params=_cparams(("parallel", "arbitrary")),
        name="sb_attn",
    )(q, k, v)


def _out_kernel(x_ref, oc_ref, os_ref, ow_ref, gt_ref, swa_ref, sb_ref, gg_ref, w_ref, gp_ref, o_ref):
    tm = x_ref.shape[1]
    wa = NSA_HEADS * HEAD_DIM
    wb = wa + SWA_HEADS * HEAD_DIM
    gates = jax.nn.sigmoid(gt_ref[0])
    lane = _iota((tm, wa), 1)

    def spread(branch):
        cols = [jnp.broadcast_to(gates[:, branch * NSA_HEADS + i:branch * NSA_HEADS + i + 1], (tm, wa))
                for i in range(NSA_HEADS)]
        out = cols[NSA_HEADS - 1]
        for i in range(NSA_HEADS - 2, -1, -1):
            out = jnp.where(lane < (i + 1) * HEAD_DIM, cols[i], out)
        return out

    o_nsa = spread(0) * oc_ref[0] + spread(1) * os_ref[0] + spread(2) * ow_ref[0]
    gg = gg_ref[...]
    mix = jnp.concatenate([_rms(o_nsa, gg[:, :wa]), _rms(swa_ref[0], gg[:, wa:wb]), _rms(sb_ref[0], gg[:, wb:])],
                          axis=1).astype(BF16)
    y = jnp.dot(mix, w_ref[...], preferred_element_type=F32)
    o_ref[0] = x_ref[0] + _rms(y, gp_ref[...])


def _out_proj(x, o_cmp, o_sel, o_win, gates, o_swa, o_sb, gg, w, gp, tm):
    b, s, d = x.shape
    row = lambda n: pl.BlockSpec((1, tm, n), lambda bi, i: (bi, i, 0))
    const = lambda a, c: pl.BlockSpec((a, c), lambda bi, i: (0, 0))
    return pl.pallas_call(
        _out_kernel,
        grid=(b, s // tm),
        in_specs=[row(d), row(256), row(256), row(256), row(128), row(512), row(256),
                  const(1, d), const(d, d), const(1, d)],
        out_specs=row(d),
        out_shape=jax.ShapeDtypeStruct((b, s, d), F32),
        compiler_params=_cparams(("parallel", "parallel")),
        name="out_proj",
    )(x, o_cmp, o_sel, o_win, gates, o_swa, o_sb, gg, w, gp)


def _ffn_kernel(x_ref, g1_ref, wg_ref, wu_ref, wd_ref, g2_ref, o_ref, *, n_chunks):
    x = x_ref[...]
    h = _rms(x, g1_ref[...]).astype(BF16)
    dff = wg_ref.shape[1]
    cw = dff // n_chunks
    f = jnp.zeros(x.shape, F32)
    for c in range(n_chunks):
        gate = jnp.dot(h, wg_ref[:, c * cw:(c + 1) * cw], preferred_element_type=F32)
        up = jnp.dot(h, wu_ref[:, c * cw:(c + 1) * cw], preferred_element_type=F32)
        a = (gate * jax.nn.sigmoid(gate) * up).astype(BF16)
        f = f + jnp.dot(a, wd_ref[c * cw:(c + 1) * cw, :], preferred_element_type=F32)
    o_ref[...] = x + _rms(f, g2_ref[...])


def _ffn(x, g1, wg, wu, wd, g2, tm):
    t, d = x.shape
    dff = wg.shape[1]
    row = pl.BlockSpec((tm, d), lambda i: (i, 0))
    const = lambda a, c: pl.BlockSpec((a, c), lambda i: (0, 0))
    return pl.pallas_call(
        functools.partial(_ffn_kernel, n_chunks=4),
        grid=(t // tm,),
        in_specs=[row, const(1, d), const(d, dff), const(d, dff), const(dff, d), const(1, d)],
        out_specs=row,
        out_shape=jax.ShapeDtypeStruct((t, d), F32),
        compiler_params=_cparams(("parallel",)),
        name="ffn",
    )(x, g1, wg, wu, wd, g2)


def _permute_w_in(w):
    scale = 1.0 / math.sqrt(HEAD_DIM)
    nq, kc, vc, rest, ng = w[:, :256], w[:, 256:320], w[:, 320:384], w[:, 384:640], w[:, 640:652]
    sq, skv = w[:, 652:1164], w[:, 1164:1420]
    bq, bkv = w[:, 1420:1676], w[:, 1676:2188]
    pad = jnp.zeros((w.shape[0], 128 - ng.shape[1]), w.dtype)
    return jnp.concatenate([nq * scale, rest, sq * scale, skv, bq * scale, bkv, kc, vc, ng, pad], axis=1)


def _cmp_to_sel(nc_pad, nc, ns):
    c0 = jnp.arange(nc_pad)[:, None] * CMP_STRIDE
    s0 = jnp.arange(ns)[None, :] * SEL_BLK
    ov = jnp.minimum(c0 + CMP_LEN, s0 + SEL_BLK) - jnp.maximum(c0, s0)
    w = jnp.clip(ov, 0, None).astype(F32) / CMP_LEN
    return jnp.where(jnp.arange(nc_pad)[:, None] < nc, w, 0.0).astype(BF16)


def kernel(x, rel_bias, ln_attn_pre, w_in, nsa_cmp_pos, nsa_phi_w1, nsa_phi_b1, nsa_phi_w2, swa_sinks,
           grp_norm_g, w_out, ln_attn_post, ln_ffn_pre, ffn_w_gate, ffn_w_up, ffn_w_down, ln_ffn_post):
    b, s, d = x.shape
    depth = w_in.shape[0]
    nch = s // CMP_STRIDE
    ns = s // SEL_BLK
    t = b * s

    band_nsa_t = _band_table_t(rel_bias, N_BAND, NSA_HEADS, 0, True)
    band_swa_t = _band_table_t(rel_bias, SWA_WINDOW // BLK + 1, SWA_HEADS, NSA_HEADS, False)
    hct = _cmp_table(rel_bias, nch)
    wmapt = jnp.pad(_cmp_to_sel(nch, nch - 1, ns).T, ((0, IMP_PAD), (0, 0)))
    no_sinks = jnp.zeros((1, 1, 4 * BLK), F32)

    for l in range(depth):
        outs = _proj(x, ln_attn_pre[l][None], _permute_w_in(w_in[l]).astype(BF16), 512)
        qt, ks, vs, kw, vw, sqt, sk, sv, bq, bk, bv, kc, vc, gates = outs
        ckv = jnp.stack([kc.reshape(b, nch, CMP_STRIDE * HEAD_DIM), vc.reshape(b, nch, CMP_STRIDE * HEAD_DIM)])
        kcv, kcv_t = _compress(ckv, nsa_cmp_pos[l].reshape(2, 2, CMP_STRIDE * HEAD_DIM),
                               nsa_phi_w1[l].reshape(2, 2, CMP_STRIDE * HEAD_DIM, CMP_HIDDEN).astype(BF16),
                               nsa_phi_b1[l][:, None, :], nsa_phi_w2[l].astype(BF16))
        o_cmp, selmask = _cmp_attn(qt, kcv, kcv_t, hct, wmapt)
        o_sel = _sel_attn(qt, ks, vs, selmask, band_nsa_t)
        o_win = _band_attn(qt, kw[:, None], vw[:, None], band_nsa_t, no_sinks, NSA_WINDOW, False, "win_attn")
        sinks = jnp.broadcast_to(swa_sinks[l].reshape(SWA_KV_HEADS, 1, 4, 1),
                                 (SWA_KV_HEADS, 1, 4, BLK)).reshape(SWA_KV_HEADS, 1, 4 * BLK)
        o_swa = _band_attn(sqt, sk, sv, band_swa_t, sinks, SWA_WINDOW, True, "swa_attn")
        o_sb = _sb_attn(bq, bk, bv)
        x = _out_proj(x, o_cmp, o_sel, o_win, gates, o_swa, o_sb, grp_norm_g[l][None], w_out[l].astype(BF16),
                      ln_attn_post[l][None], 256)
        x = _ffn(x.reshape(t, d), ln_ffn_pre[l][None], ffn_w_gate[l].astype(BF16), ffn_w_up[l].astype(BF16),
                 ffn_w_down[l].astype(BF16), ln_ffn_post[l][None], 256).reshape(b, s, d)
    return x
```

```python
import functools
import math

import jax
import jax.numpy as jnp
from jax import lax
from jax.experimental import pallas as pl
from jax.experimental.pallas import tpu as pltpu

F32 = jnp.float32
BF16 = jnp.bfloat16

HEAD_DIM = 64
BLK = 128
NSA_HEADS = 4
CMP_LEN = 32
CMP_STRIDE = 16
CMP_HIDDEN = 256
SEL_BLK = 64
N_SEL = 16
N_LOCAL = 2
NSA_WINDOW = 512
SWA_HEADS = 8
SWA_KV_HEADS = 2
SWA_WINDOW = 128
SB_HEADS = 4
N_BUCKETS = 32
T5_MAX_DISTANCE = 4096
RMS_EPS = 1e-6
FORCE_SCORE = 1e6
NEG = -1e30
LOG2E = 1.0 / math.log(2.0)
SEL_TILE = 512
SB_GROUP = 3
SB_LOG_FLOOR = -104.0
PICKED = -3e38
CMP_QB = 4
VT_ROWS = HEAD_DIM + 16
BAND_QB = 2
CMP_CHUNK = 128
IMP_PAD = 16
IMP_ROWS = CMP_CHUNK // 4 + IMP_PAD
N_BAND = 25
VMEM_LIMIT = 56 * 1024 * 1024

NT_DIMS = (((1,), (1,)), ((), ()))


def _cparams(sem):
    return pltpu.CompilerParams(dimension_semantics=sem, vmem_limit_bytes=VMEM_LIMIT)


def _rms(x, g):
    ms = jnp.mean(x * x, axis=-1, keepdims=True)
    return x * lax.rsqrt(ms + RMS_EPS) * g


def _iota(shape, axis):
    return lax.broadcasted_iota(jnp.int32, shape, axis)


def _bias_of_dist(dist, tab_ref, head):
    n = jnp.maximum(dist, 0)
    nf = jnp.maximum(n, 1).astype(F32)
    exact = N_BUCKETS // 2
    large = exact + (jnp.log(nf / exact) / math.log(T5_MAX_DISTANCE / exact)
                     * (N_BUCKETS - exact)).astype(jnp.int32)
    large = jnp.minimum(large, N_BUCKETS - 1)
    bucket = jnp.where(n < exact, n, large)
    val = jnp.full(dist.shape, tab_ref[0, head], F32)
    for k in range(1, N_BUCKETS):
        val = jnp.where(bucket == k, tab_ref[k, head], val)
    return val


def _band_table_kernel(tab_ref, o_ref, *, head_off):
    m = pl.program_id(0)
    h = pl.program_id(1)
    dist = BLK * m + _iota((BLK, BLK), 0) - _iota((BLK, BLK), 1)
    o_ref[0, 0] = _bias_of_dist(dist, tab_ref, h + head_off)


def _band_table(rel_bias, n_band, n_heads, head_off):
    return pl.pallas_call(
        functools.partial(_band_table_kernel, head_off=head_off),
        grid=(n_band, n_heads),
        in_specs=[pl.BlockSpec(memory_space=pltpu.SMEM)],
        out_specs=pl.BlockSpec((1, 1, BLK, BLK), lambda m, h: (m, h, 0, 0)),
        out_shape=jax.ShapeDtypeStruct((n_band, n_heads, BLK, BLK), F32),
        compiler_params=_cparams(("arbitrary", "arbitrary")),
        name="band_table",
    )(rel_bias)


def _band_table_t_kernel(tab_ref, o_ref, *, head_off, shift):
    m = pl.program_id(0)
    h = pl.program_id(1) + head_off
    dist = BLK * m + _iota((BLK, BLK), 1) - _iota((BLK, BLK), 0)
    bias = _bias_of_dist(dist, tab_ref, h)
    o_ref[0] = (bias - tab_ref[N_BUCKETS - 1, h]) * LOG2E if shift else bias


def _band_table_t(rel_bias, n_band, n_heads, head_off, shift):
    return pl.pallas_call(
        functools.partial(_band_table_t_kernel, head_off=head_off, shift=shift),
        grid=(n_band, n_heads),
        in_specs=[pl.BlockSpec(memory_space=pltpu.SMEM)],
        out_specs=pl.BlockSpec((1, BLK, BLK), lambda m, h: (m, 0, h)),
        out_shape=jax.ShapeDtypeStruct((n_band, BLK, n_heads * BLK), F32),
        compiler_params=_cparams(("arbitrary", "arbitrary")),
        name="band_table_t",
    )(rel_bias)


def _cmp_table_kernel(tab_ref, o_ref, *, ncp):
    h = pl.program_id(0)
    rel = _iota((2 * ncp, BLK), 0) - (ncp - 8)
    dist = _iota((2 * ncp, BLK), 1) - CMP_STRIDE * rel - (CMP_LEN - 1)
    o_ref[...] = _bias_of_dist(dist, tab_ref, h) * LOG2E


def _cmp_table(rel_bias, ncp):
    return pl.pallas_call(
        functools.partial(_cmp_table_kernel, ncp=ncp),
        grid=(NSA_HEADS,),
        in_specs=[pl.BlockSpec(memory_space=pltpu.SMEM)],
        out_specs=pl.BlockSpec((2 * ncp, BLK), lambda h: (0, h)),
        out_shape=jax.ShapeDtypeStruct((2 * ncp, NSA_HEADS * BLK), F32),
        compiler_params=_cparams(("arbitrary",)),
        name="cmp_table",
    )(rel_bias)


def _proj_kernel(x_ref, g_ref, w_ref, qt_ref, ks_ref, vs_ref, kw_ref, vw_ref, sqt_ref, sk_ref, sv_ref,
                 bq_ref, bk_ref, bv_ref, kc_ref, vc_ref, gt_ref):
    h = _rms(x_ref[0], g_ref[...]).astype(BF16)
    tm = x_ref.shape[1]

    def seg(a, b):
        return jnp.dot(h, w_ref[:, a:b], preferred_element_type=F32)

    def heads(ref, a, n):
        y = seg(a, a + n * HEAD_DIM).astype(BF16)
        for i in range(n):
            ref[0, i] = y[:, i * HEAD_DIM:(i + 1) * HEAD_DIM]

    def queries_t(ref, a, n, scale=None):
        yt = seg(a, a + n * HEAD_DIM).T
        if scale is not None:
            yt = yt * scale
        for j in range(tm // BLK):
            for i in range(n):
                ref[0, j, :, i * BLK:(i + 1) * BLK] = (
                    yt[i * HEAD_DIM:(i + 1) * HEAD_DIM, j * BLK:(j + 1) * BLK].astype(BF16))

    ones_rows = jnp.where(_iota((VT_ROWS - HEAD_DIM, tm), 0) == 0, 1.0, 0.0)

    def values_t(yt):
        return jnp.concatenate([yt, ones_rows], axis=0).astype(BF16)

    queries_t(qt_ref, 0, NSA_HEADS, LOG2E)
    y = seg(256, 512)
    vs_ref[0] = values_t(y[:, 0:128].T[64:128])
    vw_ref[0] = values_t(y[:, 128:256].T[64:128])
    y = y.astype(BF16)
    blk_in_tile = ((pl.program_id(1) * tm + _iota((tm, HEAD_DIM), 0)) & (SEL_TILE - 1)) // SEL_BLK
    onehot = jnp.where(_iota((tm, HEAD_DIM), 1) == blk_in_tile, 1.0, 0.0).astype(BF16)
    ks_ref[0] = jnp.concatenate([y[:, 0:64], onehot], axis=1)
    kw_ref[0] = y[:, 128:192]
    queries_t(sqt_ref, 512, SWA_HEADS)
    heads(sk_ref, 1024, SWA_KV_HEADS)
    yt = seg(1152, 1280).T
    for i in range(SWA_KV_HEADS):
        sv_ref[0, i] = values_t(yt[i * HEAD_DIM:(i + 1) * HEAD_DIM])
    queries_t(bq_ref, 1280, SB_HEADS)
    bk_ref[0] = seg(1536, 1792).astype(BF16)
    bv_ref[0] = seg(1792, 2048).T.astype(BF16)
    y = seg(2048, 2304)
    kc_ref[0] = y[:, 0:64]
    vc_ref[0] = y[:, 64:128]
    gt_ref[0] = y[:, 128:256]


def _proj(x, g, w, tm):
    b, s, d = x.shape
    nw = w.shape[1]
    hd = lambda n: pl.BlockSpec((1, n, tm, HEAD_DIM), lambda bi, i: (bi, 0, i, 0))
    flat = lambda n: pl.BlockSpec((1, tm, n), lambda bi, i: (bi, i, 0))
    hshape = lambda n: jax.ShapeDtypeStruct((b, n, s, HEAD_DIM), BF16)
    fshape = lambda n, dt: jax.ShapeDtypeStruct((b, s, n), dt)
    qt_spec = lambda n: pl.BlockSpec((1, tm // BLK, HEAD_DIM, n * BLK), lambda bi, i: (bi, i, 0, 0))
    qt_shape = lambda n: jax.ShapeDtypeStruct((b, s // BLK, HEAD_DIM, n * BLK), BF16)
    vt_spec = pl.BlockSpec((1, VT_ROWS, tm), lambda bi, i: (bi, 0, i))
    vt_shape = jax.ShapeDtypeStruct((b, VT_ROWS, s), BF16)
    return pl.pallas_call(
        _proj_kernel,
        grid=(b, s // tm),
        in_specs=[pl.BlockSpec((1, tm, d), lambda bi, i: (bi, i, 0)),
                  pl.BlockSpec((1, d), lambda bi, i: (0, 0)),
                  pl.BlockSpec((d, nw), lambda bi, i: (0, 0))],
        out_specs=[qt_spec(NSA_HEADS), flat(128), vt_spec, flat(64), vt_spec,
                   qt_spec(SWA_HEADS), hd(SWA_KV_HEADS),
                   pl.BlockSpec((1, SWA_KV_HEADS, VT_ROWS, tm), lambda bi, i: (bi, 0, 0, i)),
                   qt_spec(SB_HEADS), flat(256),
                   pl.BlockSpec((1, SB_HEADS * HEAD_DIM, tm), lambda bi, i: (bi, 0, i)),
                   flat(64), flat(64), flat(128)],
        out_shape=[qt_shape(NSA_HEADS), fshape(128, BF16), vt_shape, fshape(64, BF16), vt_shape,
                   qt_shape(SWA_HEADS), hshape(SWA_KV_HEADS),
                   jax.ShapeDtypeStruct((b, SWA_KV_HEADS, VT_ROWS, s), BF16),
                   qt_shape(SB_HEADS), fshape(256, BF16),
                   jax.ShapeDtypeStruct((b, SB_HEADS * HEAD_DIM, s), BF16),
                   fshape(64, F32), fshape(64, F32), fshape(128, F32)],
        compiler_params=_cparams(("parallel", "parallel")),
        name="proj",
    )(x, g, w)


def _compress_kernel(c_ref, pos_ref, w1_ref, b1_ref, w2_ref, o_ref, ot_ref):
    c = c_ref[0, 0]
    nch = c.shape[0]
    xa = (c + pos_ref[0, 0:1]).astype(BF16)
    xb = (c + pos_ref[0, 1:2]).astype(BF16)
    p = jnp.dot(xa, w1_ref[0, 0], preferred_element_type=F32)
    q = jnp.dot(xb, w1_ref[0, 1], preferred_element_type=F32)
    hid = p + pltpu.roll(q, nch - 1, 0) + b1_ref[0]
    hid = hid * jax.nn.sigmoid(hid)
    out = jnp.dot(hid.astype(BF16), w2_ref[0], preferred_element_type=F32)
    row = _iota(out.shape, 0)
    out = jnp.where(row < nch - 1, out, 0.0)
    o_ref[0, 0] = out.astype(BF16)
    ot_ref[0, 0] = jnp.concatenate([out, jnp.zeros_like(out)], axis=1).T[:HEAD_DIM].astype(BF16)


def _compress(ckv, pos, w1, b1, w2):
    _, b, nch, cw = ckv.shape
    return pl.pallas_call(
        _compress_kernel,
        grid=(2, b),
        in_specs=[pl.BlockSpec((1, 1, nch, cw), lambda j, bi: (j, bi, 0, 0)),
                  pl.BlockSpec((1, 2, cw), lambda j, bi: (j, 0, 0)),
                  pl.BlockSpec((1, 2, cw, CMP_HIDDEN), lambda j, bi: (j, 0, 0, 0)),
                  pl.BlockSpec((1, 1, CMP_HIDDEN), lambda j, bi: (j, 0, 0)),
                  pl.BlockSpec((1, CMP_HIDDEN, HEAD_DIM), lambda j, bi: (j, 0, 0))],
        out_specs=[pl.BlockSpec((1, 1, nch, HEAD_DIM), lambda j, bi: (j, bi, 0, 0)),
                   pl.BlockSpec((1, 1, HEAD_DIM, nch), lambda j, bi: (j, bi, 0, 0))],
        out_shape=[jax.ShapeDtypeStruct((2, b, nch, HEAD_DIM), BF16),
                   jax.ShapeDtypeStruct((2, b, HEAD_DIM, nch), BF16)],
        compiler_params=_cparams(("parallel", "parallel")),
        name="compress",
    )(ckv, pos, w1, b1, w2)


def _merge_heads(o, n):
    return jnp.concatenate([o[i * BLK:(i + 1) * BLK] for i in range(n)], axis=1)


def _transpose_pad(x, axis):
    pad = jnp.zeros(x.shape, x.dtype)
    return jnp.concatenate([x, pad], axis=axis).T


def _cmp_kernel(qt_ref, kc_ref, vct_ref, hct_ref, wmapt_ref, o_ref, pen_ref, s_scr, oacc_scr, iacc_scr):
    step = pl.program_id(1)
    ncp = kc_ref.shape[2]
    ns = wmapt_ref.shape[0] - IMP_PAD
    lanes = NSA_HEADS * BLK
    row_c = _iota((CMP_CHUNK, lanes), 0)
    lane_q = _iota((CMP_CHUNK, lanes), 1) & (BLK - 1)
    n_chunks = (CMP_QB * step) // (CMP_CHUNK // 8) + 1

    def score_chunk(ch, mxs):
        c0 = pl.multiple_of(ch * CMP_CHUNK, CMP_CHUNK)
        kc = kc_ref[0, 0, pl.ds(c0, CMP_CHUNK), :]
        out = []
        for qb in range(CMP_QB):
            n = CMP_QB * step + qb
            s = jnp.dot(kc, qt_ref[0, qb], preferred_element_type=F32)
            bias = hct_ref[pl.ds(pl.multiple_of(ncp - 8 - 8 * n + c0, 8), CMP_CHUNK), :]
            visible = (BLK * n + lane_q - CMP_STRIDE * (c0 + row_c) - (CMP_LEN - 1)) >= 0
            s = jnp.where(visible, s + bias, NEG)
            s_scr[qb, ch] = s
            out.append(jnp.maximum(mxs[qb], jnp.max(s, axis=0, keepdims=True)))
        return tuple(out)

    mxs = lax.fori_loop(0, n_chunks, score_chunk, tuple(jnp.full((1, lanes), NEG, F32) for _ in range(CMP_QB)))
    ms = [jnp.where(mx <= 0.5 * NEG, 0.0, mx) for mx in mxs]
    oacc_scr[...] = jnp.zeros(oacc_scr.shape, F32)
    iacc_scr[...] = jnp.zeros(iacc_scr.shape, F32)

    def prob_chunk(ch, ls):
        c0 = pl.multiple_of(ch * CMP_CHUNK, CMP_CHUNK)
        vct = vct_ref[0, 0, :, pl.ds(c0, CMP_CHUNK)]
        r0 = pl.multiple_of(ch * (CMP_CHUNK // 4), 16)
        wmt = wmapt_ref[pl.ds(r0, IMP_ROWS), pl.ds(c0, CMP_CHUNK)]
        out = []
        for qb in range(CMP_QB):
            p = jnp.exp2(s_scr[qb, ch] - ms[qb])
            pb = p.astype(BF16)
            oacc_scr[qb] += jnp.dot(vct, pb, preferred_element_type=F32)
            iacc_scr[qb, pl.ds(r0, IMP_ROWS), :] += jnp.dot(wmt, pb, preferred_element_type=F32)
            out.append(ls[qb] + jnp.sum(p, axis=0, keepdims=True))
        return tuple(out)

    ls = lax.fori_loop(0, n_chunks, prob_chunk, tuple(jnp.zeros((1, lanes), F32) for _ in range(CMP_QB)))
    imps = []
    for qb in range(CMP_QB):
        inv = 1.0 / jnp.maximum(ls[qb], 1e-30)
        o_t = oacc_scr[qb] * inv
        o_ref[0, qb * BLK:(qb + 1) * BLK, :] = jnp.concatenate(
            [_transpose_pad(o_t[:, h * BLK:(h + 1) * BLK], 0)[:, :HEAD_DIM] for h in range(NSA_HEADS)], axis=1)
        w = iacc_scr[qb, 0:ns, :] * inv
        imps.append(functools.reduce(jnp.add, [w[:, h * BLK:(h + 1) * BLK] for h in range(NSA_HEADS)]))

    imp = jnp.concatenate(imps, axis=1)
    shape = (ns, CMP_QB * BLK)
    j = _iota(shape, 0)
    cur = (CMP_QB * BLK * step + _iota(shape, 1)) // SEL_BLK
    valid = j <= cur
    forced = valid & ((j == 0) | (j > cur - N_LOCAL))
    score = jnp.where(forced, PICKED, jnp.where(valid, imp, NEG))
    jf = j.astype(F32)
    for _ in range(min(N_SEL, ns) - (N_LOCAL + 1)):
        best = jnp.max(score, axis=0, keepdims=True)
        first = jnp.min(jnp.where(score == best, jf, float(ns)), axis=0, keepdims=True)
        score = jnp.where(jf == first, PICKED, score)
    pen_ref[0, 0] = jnp.where(valid & (score == PICKED), 0.0, NEG)


def _cmp_attn(qt, kcv, kcv_t, hct, wmapt):
    b, nb, _, lanes = qt.shape
    ncp = kcv.shape[2]
    ns = wmapt.shape[0] - IMP_PAD
    s = nb * BLK
    return pl.pallas_call(
        _cmp_kernel,
        grid=(b, nb // CMP_QB),
        in_specs=[pl.BlockSpec((1, CMP_QB, HEAD_DIM, lanes), lambda bi, i: (bi, i, 0, 0)),
                  pl.BlockSpec((1, 1, ncp, HEAD_DIM), lambda bi, i: (0, bi, 0, 0)),
                  pl.BlockSpec((1, 1, HEAD_DIM, ncp), lambda bi, i: (1, bi, 0, 0)),
                  pl.BlockSpec((2 * ncp, lanes), lambda bi, i: (0, 0)),
                  pl.BlockSpec((ns + IMP_PAD, ncp), lambda bi, i: (0, 0))],
        out_specs=[pl.BlockSpec((1, CMP_QB * BLK, NSA_HEADS * HEAD_DIM), lambda bi, i: (bi, i, 0)),
                   pl.BlockSpec((1, 1, ns, CMP_QB * BLK), lambda bi, i: (bi, i, 0, 0))],
        out_shape=[jax.ShapeDtypeStruct((b, s, NSA_HEADS * HEAD_DIM), F32),
                   jax.ShapeDtypeStruct((b, nb // CMP_QB, ns, CMP_QB * BLK), F32)],
        scratch_shapes=[pltpu.VMEM((CMP_QB, ncp // CMP_CHUNK, CMP_CHUNK, lanes), F32),
                        pltpu.VMEM((CMP_QB, HEAD_DIM, lanes), F32),
                        pltpu.VMEM((CMP_QB, ns + IMP_PAD, lanes), F32)],
        compiler_params=_cparams(("parallel", "arbitrary")),
        name="cmp_attn",
    )(qt, kcv, kcv_t, hct, wmapt)


def _sel_kernel(qt_ref, k_ref, vt_ref, pen_ref, band_ref, o_ref, s_scr, p_scr, m_scr, alpha_scr, acc_scr):
    n = pl.program_id(1)
    lanes = NSA_HEADS * BLK
    sub = SEL_TILE // BLK
    per = BLK // SEL_BLK
    qt = qt_ref[0, 0]
    t = BLK * n + (_iota((BLK, lanes), 1) & (BLK - 1))

    def scores_into(kt, s_buf, far):
        rows = sub * per
        pen = pen_ref[0, 0, pl.ds(pl.multiple_of(kt * rows, rows), rows), :]
        pen = jnp.concatenate([pen] * NSA_HEADS, axis=1)
        tail = jnp.concatenate([pen, jnp.zeros((HEAD_DIM - rows, lanes), F32)], axis=0).astype(BF16)
        q_aug = jnp.concatenate([qt, tail], axis=0)
        for u in range(sub):
            ku = pl.multiple_of(kt * SEL_TILE + u * BLK, BLK)
            s = jnp.dot(k_ref[0, pl.ds(ku, BLK), :], q_aug, preferred_element_type=F32)
            if not far:
                s = s + band_ref[jnp.clip(n - sub * kt - u, 0, N_BAND - 1)]
            s_buf[u * BLK:(u + 1) * BLK, :] = s

    def add_values(kt, p_buf):
        k0 = pl.multiple_of(kt * SEL_TILE, SEL_TILE)
        pv = jnp.dot(vt_ref[0, :, pl.ds(k0, SEL_TILE)], p_buf[...], preferred_element_type=F32)
        acc_scr[...] = alpha_scr[...] * acc_scr[...] + pv

    def softmax_into(kt, s_buf, p_buf, causal):
        def piece(u):
            s = s_buf[u * BLK:(u + 1) * BLK, :]
            if causal:
                s = jnp.where(kt * SEL_TILE + u * BLK + _iota((BLK, lanes), 0) <= t, s, NEG)
            return s
        m_i = m_scr[...]
        mx = functools.reduce(jnp.maximum, [jnp.max(piece(u), axis=0, keepdims=True) for u in range(sub)])
        m_new = jnp.maximum(m_i, mx)
        for u in range(sub):
            p_buf[u * BLK:(u + 1) * BLK, :] = jnp.exp2(piece(u) - m_new).astype(BF16)
        m_scr[...] = m_new
        alpha_scr[...] = jnp.exp2(m_i - m_new)

    def stage(i, s_cur, s_nxt, p_cur, p_prev, far):
        add_values(jnp.maximum(i - 1, 0), p_prev)
        scores_into(i + 1, s_nxt, far)
        softmax_into(i, s_cur, p_cur, False)

    def finish(i, s_cur, p_cur, p_prev):
        add_values(jnp.maximum(i - 1, 0), p_prev)
        softmax_into(i, s_cur, p_cur, True)
        add_values(i, p_cur)
        acc = acc_scr[...]
        o_t = acc[:HEAD_DIM] / acc[HEAD_DIM:HEAD_DIM + 1]
        o_ref[0] = jnp.concatenate(
            [_transpose_pad(o_t[:, h * BLK:(h + 1) * BLK], 0)[:, :HEAD_DIM] for h in range(NSA_HEADS)], axis=1)

    s0, s1, p0, p1 = s_scr.at[0], s_scr.at[1], p_scr.at[0], p_scr.at[1]
    m_scr[...] = jnp.full((1, lanes), NEG, F32)
    alpha_scr[...] = jnp.ones((1, lanes), F32)
    acc_scr[...] = jnp.zeros(acc_scr.shape, F32)
    p1[...] = jnp.zeros((SEL_TILE, lanes), BF16)
    scores_into(0, s0, False)
    last = (BLK * n + BLK - 1) // SEL_TILE
    n_far = jnp.maximum((n - (N_BAND - 2)) // sub, 0)
    far_pairs = jnp.minimum(jnp.maximum((n_far - 1) // 2, 0), last // 2)

    def pair(j, _, far):
        stage(2 * j, s0, s1, p0, p1, far)
        stage(2 * j + 1, s1, s0, p1, p0, far)
        return 0

    lax.fori_loop(0, far_pairs, functools.partial(pair, far=True), 0)
    lax.fori_loop(far_pairs, last // 2, functools.partial(pair, far=False), 0)

    @pl.when(last % 2 == 1)
    def _():
        stage(last - 1, s0, s1, p0, p1, False)
        finish(last, s1, p1, p0)

    @pl.when(last % 2 == 0)
    def _():
        finish(last, s0, p0, p1)


def _sel_attn(qt, ks, vst, pen, band_t):
    b, nb, _, lanes = qt.shape
    s = nb * BLK
    ns = pen.shape[2]
    return pl.pallas_call(
        _sel_kernel,
        grid=(b, nb),
        in_specs=[pl.BlockSpec((1, 1, HEAD_DIM, lanes), lambda bi, n: (bi, n, 0, 0)),
                  pl.BlockSpec((1, s, 2 * HEAD_DIM), lambda bi, n: (bi, 0, 0)),
                  pl.BlockSpec((1, VT_ROWS, s), lambda bi, n: (bi, 0, 0)),
                  pl.BlockSpec((1, 1, ns, BLK), lambda bi, n: (bi, n // CMP_QB, 0, n % CMP_QB)),
                  pl.BlockSpec((N_BAND, BLK, NSA_HEADS * BLK), lambda bi, n: (0, 0, 0))],
        out_specs=pl.BlockSpec((1, BLK, NSA_HEADS * HEAD_DIM), lambda bi, n: (bi, n, 0)),
        out_shape=jax.ShapeDtypeStruct((b, s, NSA_HEADS * HEAD_DIM), F32),
        scratch_shapes=[pltpu.VMEM((2, SEL_TILE, NSA_HEADS * BLK), F32),
                        pltpu.VMEM((2, SEL_TILE, NSA_HEADS * BLK), BF16),
                        pltpu.VMEM((1, NSA_HEADS * BLK), F32),
                        pltpu.VMEM((1, NSA_HEADS * BLK), F32),
                        pltpu.VMEM((VT_ROWS, NSA_HEADS * BLK), F32)],
        compiler_params=_cparams(("parallel", "arbitrary")),
        name="sel_attn",
    )(qt, ks, vst, pen, band_t)


def _band_kernel(*refs, window, groups, use_sinks, base2, n_parts):
    qt_ref = refs[0]
    k_refs = refs[1:1 + n_parts]
    vt_refs = refs[1 + n_parts:1 + 2 * n_parts]
    band_ref, sink_ref, o_ref = refs[1 + 2 * n_parts:]
    step = pl.program_id(1)
    n_prev = window // BLK
    nk = (n_prev + 1) * BLK
    lanes = 4 * BLK
    row = _iota((nk, lanes), 0)
    dist = n_prev * BLK + (_iota((nk, lanes), 1) & (BLK - 1)) - row
    in_window = (dist >= 0) & (dist < window)
    for g in range(groups):
        kwin = jnp.concatenate([r[0, g] for r in k_refs], axis=0)
        vtwin = jnp.concatenate([r[0, g] for r in vt_refs], axis=1)
        bias = jnp.concatenate([band_ref[n_prev - u, :, g * lanes:(g + 1) * lanes] for u in range(n_prev + 1)],
                               axis=0)
        for qb in range(BAND_QB):
            n = BAND_QB * step + qb
            qt = qt_ref[0, qb, :, g * lanes:(g + 1) * lanes]
            s = jnp.dot(kwin[qb * BLK:qb * BLK + nk], qt, preferred_element_type=F32)
            s = jnp.where(in_window & (row >= (n_prev - n) * BLK), s + bias, NEG)
            mx = jnp.max(s, axis=0, keepdims=True)
            if use_sinks:
                sink = sink_ref[g]
                mx = jnp.maximum(mx, sink)
            p = (jnp.exp2(s - mx) if base2 else jnp.exp(s - mx)).astype(BF16)
            acc = jnp.dot(vtwin[:, qb * BLK:qb * BLK + nk], p, preferred_element_type=F32)
            den = acc[HEAD_DIM:HEAD_DIM + 1]
            if use_sinks:
                den = den + jnp.exp(sink - mx)
            o_t = acc[:HEAD_DIM] / den
            o_ref[0, qb * BLK:(qb + 1) * BLK, 4 * g * HEAD_DIM:4 * (g + 1) * HEAD_DIM] = jnp.concatenate(
                [_transpose_pad(o_t[:, h * BLK:(h + 1) * BLK], 0)[:, :HEAD_DIM] for h in range(4)], axis=1)


def _band_attn(qt, k, vt, band_t, sinks, window, use_sinks, base2, name):
    b, nb, _, width = qt.shape
    g = width // (4 * BLK)
    s = nb * BLK
    n_prev = window // BLK
    tq = BAND_QB * BLK
    if n_prev * BLK % tq == 0:
        piece = tq
        starts = [lambda i, d=d: jnp.maximum(i - d, 0) for d in range(n_prev * BLK // tq, 0, -1)]
    else:
        assert n_prev == 1
        piece = BLK
        starts = [lambda i: jnp.maximum(BAND_QB * i - 1, 0)]
    k_specs = [pl.BlockSpec((1, g, piece, HEAD_DIM), lambda bi, i, f=f: (bi, 0, f(i), 0)) for f in starts]
    k_specs.append(pl.BlockSpec((1, g, tq, HEAD_DIM), lambda bi, i: (bi, 0, i, 0)))
    vt_specs = [pl.BlockSpec((1, g, VT_ROWS, piece), lambda bi, i, f=f: (bi, 0, 0, f(i))) for f in starts]
    vt_specs.append(pl.BlockSpec((1, g, VT_ROWS, tq), lambda bi, i: (bi, 0, 0, i)))
    n_parts = len(k_specs)
    return pl.pallas_call(
        functools.partial(_band_kernel, window=window, groups=g, use_sinks=use_sinks, base2=base2,
                          n_parts=n_parts),
        grid=(b, s // tq),
        in_specs=[pl.BlockSpec((1, BAND_QB, HEAD_DIM, width), lambda bi, i: (bi, i, 0, 0))] + k_specs + vt_specs
                 + [pl.BlockSpec((n_prev + 1, BLK, width), lambda bi, i: (0, 0, 0)),
                    pl.BlockSpec((g, 1, 4 * BLK), lambda bi, i: (0, 0, 0))],
        out_specs=pl.BlockSpec((1, tq, g * 4 * HEAD_DIM), lambda bi, i: (bi, i, 0)),
        out_shape=jax.ShapeDtypeStruct((b, s, g * 4 * HEAD_DIM), F32),
        compiler_params=_cparams(("parallel", "arbitrary")),
        name=name,
    )(qt, *([k] * n_parts), *([vt] * n_parts), band_t, sinks)


def _sb_kernel(qt_ref, k_ref, vt_ref, o_ref):
    n = pl.program_id(1)
    lanes = SB_HEADS * BLK
    gk = SB_GROUP * BLK
    t = BLK * n + (_iota((BLK, lanes), 1) & (BLK - 1))
    row = _iota((BLK, lanes), 0)
    later = jnp.where(_iota((gk, 2 * gk), 1) % gk > _iota((gk, 2 * gk), 0), 1.0, 0.0).astype(BF16)
    zero = jnp.zeros((HEAD_DIM, BLK), BF16)
    q_bd = jnp.concatenate(
        [jnp.concatenate([zero] * h + [qt_ref[0, 0, :, h * BLK:(h + 1) * BLK]] + [zero] * (SB_HEADS - 1 - h), axis=0)
         for h in range(SB_HEADS)], axis=1)

    def cond(c):
        top, carry, _ = c
        return (top >= 0) & (jnp.max(carry) > SB_LOG_FLOOR)

    def body(c):
        top, carry, acc = c
        ks, vts, befores = [], [], []
        for j in range(SB_GROUP - 1, -1, -1):
            kb = top - j
            k0 = pl.multiple_of(jnp.maximum(kb, 0) * BLK, BLK)
            ks.append(k_ref[0, pl.ds(k0, BLK), :])
            vts.append(vt_ref[0, :, pl.ds(k0, BLK)])
            befores.append((kb >= 0) & (k0 + row < t))
        before = jnp.concatenate(befores, axis=0)
        z = jnp.dot(jnp.concatenate(ks, axis=0), q_bd, preferred_element_type=F32)
        softplus = jnp.maximum(z, 0.0) + jnp.log(1.0 + jnp.exp(-jnp.abs(z)))
        log_not = jnp.where(before, -softplus, 0.0)
        hi = log_not.astype(BF16)
        lo = (log_not - hi.astype(F32)).astype(BF16)
        tail = jnp.dot(later, jnp.concatenate([hi, lo], axis=0), preferred_element_type=F32)
        a = jnp.where(before, jnp.exp(z + log_not + tail + carry), 0.0).astype(BF16)
        pv = jnp.dot(jnp.concatenate(vts, axis=1), a, preferred_element_type=F32)
        acc = tuple(acc[h] + pv[h * HEAD_DIM:(h + 1) * HEAD_DIM, h * BLK:(h + 1) * BLK] for h in range(SB_HEADS))
        return top - SB_GROUP, carry + jnp.sum(log_not, axis=0, keepdims=True), acc

    init = (n, jnp.zeros((1, lanes), F32), tuple(jnp.zeros((HEAD_DIM, BLK), F32) for _ in range(SB_HEADS)))
    _, _, acc = lax.while_loop(cond, body, init)
    o_ref[0] = jnp.concatenate([_transpose_pad(a, 0)[:, :HEAD_DIM] for a in acc], axis=1)


def _sb_attn(qt, k, vt):
    b, nb, _, lanes = qt.shape
    s = nb * BLK
    width = SB_HEADS * HEAD_DIM
    return pl.pallas_call(
        _sb_kernel,
        grid=(b, nb),
        in_specs=[pl.BlockSpec((1, 1, HEAD_DIM, lanes), lambda bi, n: (bi, n, 0, 0)),
                  pl.BlockSpec((1, s, width), lambda bi, n: (bi, 0, 0)),
                  pl.BlockSpec((1, width, s), lambda bi, n: (bi, 0, 0))],
        out_specs=pl.BlockSpec((1, BLK, width), lambda bi, n: (bi, n, 0)),
        out_shape=jax.ShapeDtypeStruct((b, s, width), F32),
        compiler_params=_cparams(("parallel", "arbitrary")),
        name="sb_attn",
    )(qt, k, vt)


def _out_kernel(x_ref, oc_ref, os_ref, ow_ref, gt_ref, swa_ref, sb_ref, gg_ref, w_ref, gp_ref, o_ref):
    tm = x_ref.shape[1]
    wa = NSA_HEADS * HEAD_DIM
    wb = wa + SWA_HEADS * HEAD_DIM
    gates = jax.nn.sigmoid(gt_ref[0])
    lane = _iota((tm, wa), 1)

    def spread(branch):
        cols = [jnp.broadcast_to(gates[:, branch * NSA_HEADS + i:branch * NSA_HEADS + i + 1], (tm, wa))
                for i in range(NSA_HEADS)]
        out = cols[NSA_HEADS - 1]
        for i in range(NSA_HEADS - 2, -1, -1):
            out = jnp.where(lane < (i + 1) * HEAD_DIM, cols[i], out)
        return out

    o_nsa = spread(0) * oc_ref[0] + spread(1) * os_ref[0] + spread(2) * ow_ref[0]
    gg = gg_ref[...]
    mix = jnp.concatenate([_rms(o_nsa, gg[:, :wa]), _rms(swa_ref[0], gg[:, wa:wb]), _rms(sb_ref[0], gg[:, wb:])],
                          axis=1).astype(BF16)
    y = jnp.dot(mix, w_ref[...], preferred_element_type=F32)
    o_ref[0] = x_ref[0] + _rms(y, gp_ref[...])


def _out_proj(x, o_cmp, o_sel, o_win, gates, o_swa, o_sb, gg, w, gp, tm):
    b, s, d = x.shape
    row = lambda n: pl.BlockSpec((1, tm, n), lambda bi, i: (bi, i, 0))
    const = lambda a, c: pl.BlockSpec((a, c), lambda bi, i: (0, 0))
    return pl.pallas_call(
        _out_kernel,
        grid=(b, s // tm),
        in_specs=[row(d), row(256), row(256), row(256), row(128), row(512), row(256),
                  const(1, d), const(d, d), const(1, d)],
        out_specs=row(d),
        out_shape=jax.ShapeDtypeStruct((b, s, d), F32),
        compiler_params=_cparams(("parallel", "parallel")),
        name="out_proj",
    )(x, o_cmp, o_sel, o_win, gates, o_swa, o_sb, gg, w, gp)


def _ffn_kernel(x_ref, g1_ref, wg_ref, wu_ref, wd_ref, g2_ref, o_ref, *, n_chunks):
    x = x_ref[...]
    h = _rms(x, g1_ref[...]).astype(BF16)
    dff = wg_ref.shape[1]
    cw = dff // n_chunks
    f = jnp.zeros(x.shape, F32)
    for c in range(n_chunks):
        gate = jnp.dot(h, wg_ref[:, c * cw:(c + 1) * cw], preferred_element_type=F32)
        up = jnp.dot(h, wu_ref[:, c * cw:(c + 1) * cw], preferred_element_type=F32)
        a = (gate * jax.nn.sigmoid(gate) * up).astype(BF16)
        f = f + jnp.dot(a, wd_ref[c * cw:(c + 1) * cw, :], preferred_element_type=F32)
    o_ref[...] = x + _rms(f, g2_ref[...])


def _ffn(x, g1, wg, wu, wd, g2, tm):
    t, d = x.shape
    dff = wg.shape[1]
    row = pl.BlockSpec((tm, d), lambda i: (i, 0))
    const = lambda a, c: pl.BlockSpec((a, c), lambda i: (0, 0))
    return pl.pallas_call(
        functools.partial(_ffn_kernel, n_chunks=4),
        grid=(t // tm,),
        in_specs=[row, const(1, d), const(d, dff), const(d, dff), const(dff, d), const(1, d)],
        out_specs=row,
        out_shape=jax.ShapeDtypeStruct((t, d), F32),
        compiler_params=_cparams(("parallel",)),
        name="ffn",
    )(x, g1, wg, wu, wd, g2)


def _permute_w_in(w):
    scale = 1.0 / math.sqrt(HEAD_DIM)
    nq, kc, vc, rest, ng = w[:, :256], w[:, 256:320], w[:, 320:384], w[:, 384:640], w[:, 640:652]
    sq, skv = w[:, 652:1164], w[:, 1164:1420]
    bq, bkv = w[:, 1420:1676], w[:, 1676:2188]
    pad = jnp.zeros((w.shape[0], 128 - ng.shape[1]), w.dtype)
    return jnp.concatenate([nq * scale, rest, sq * scale, skv, bq * scale, bkv, kc, vc, ng, pad], axis=1)


def _cmp_to_sel(nc_pad, nc, ns):
    c0 = jnp.arange(nc_pad)[:, None] * CMP_STRIDE
    s0 = jnp.arange(ns)[None, :] * SEL_BLK
    ov = jnp.minimum(c0 + CMP_LEN, s0 + SEL_BLK) - jnp.maximum(c0, s0)
    w = jnp.clip(ov, 0, None).astype(F32) / CMP_LEN
    return jnp.where(jnp.arange(nc_pad)[:, None] < nc, w, 0.0).astype(BF16)


def kernel(x, rel_bias, ln_attn_pre, w_in, nsa_cmp_pos, nsa_phi_w1, nsa_phi_b1, nsa_phi_w2, swa_sinks,
           grp_norm_g, w_out, ln_attn_post, ln_ffn_pre, ffn_w_gate, ffn_w_up, ffn_w_down, ln_ffn_post):
    b, s, d = x.shape
    depth = w_in.shape[0]
    nch = s // CMP_STRIDE
    ns = s // SEL_BLK
    t = b * s

    band_nsa_t = _band_table_t(rel_bias, N_BAND, NSA_HEADS, 0, True)
    band_swa_t = _band_table_t(rel_bias, SWA_WINDOW // BLK + 1, SWA_HEADS, NSA_HEADS, False)
    hct = _cmp_table(rel_bias, nch)
    wmapt = jnp.pad(_cmp_to_sel(nch, nch - 1, ns).T, ((0, IMP_PAD), (0, 0)))
    no_sinks = jnp.zeros((1, 1, 4 * BLK), F32)

    for l in range(depth):
        outs = _proj(x, ln_attn_pre[l][None], _permute_w_in(w_in[l]).astype(BF16), 512)
        qt, ks, vs, kw, vw, sqt, sk, sv, bq, bk, bv, kc, vc, gates = outs
        ckv = jnp.stack([kc.reshape(b, nch, CMP_STRIDE * HEAD_DIM), vc.reshape(b, nch, CMP_STRIDE * HEAD_DIM)])
        kcv, kcv_t = _compress(ckv, nsa_cmp_pos[l].reshape(2, 2, CMP_STRIDE * HEAD_DIM),
                               nsa_phi_w1[l].reshape(2, 2, CMP_STRIDE * HEAD_DIM, CMP_HIDDEN).astype(BF16),
                               nsa_phi_b1[l][:, None, :], nsa_phi_w2[l].astype(BF16))
        o_cmp, selmask = _cmp_attn(qt, kcv, kcv_t, hct, wmapt)
        o_sel = _sel_attn(qt, ks, vs, selmask, band_nsa_t)
        o_win = _band_attn(qt, kw[:, None], vw[:, None], band_nsa_t, no_sinks, NSA_WINDOW, False, True,
                           "win_attn")
        sinks = jnp.broadcast_to(swa_sinks[l].reshape(SWA_KV_HEADS, 1, 4, 1),
                                 (SWA_KV_HEADS, 1, 4, BLK)).reshape(SWA_KV_HEADS, 1, 4 * BLK)
        o_swa = _band_attn(sqt, sk, sv, band_swa_t, sinks, SWA_WINDOW, True, False, "swa_attn")
        o_sb = _sb_attn(bq, bk, bv)
        x = _out_proj(x, o_cmp, o_sel, o_win, gates, o_swa, o_sb, grp_norm_g[l][None], w_out[l].astype(BF16),
                      ln_attn_post[l][None], 256)
        x = _ffn(x.reshape(t, d), ln_ffn_pre[l][None], ffn_w_gate[l].astype(BF16), ffn_w_up[l].astype(BF16),
                 ffn_w_down[l].astype(BF16), ln_ffn_post[l][None], 256).reshape(b, s, d)
    return x
```

```python
import functools
import math

import jax
import jax.numpy as jnp
from jax import lax
from jax.experimental import pallas as pl
from jax.experimental.pallas import tpu as pltpu

F32 = jnp.float32
BF16 = jnp.bfloat16

HEAD_DIM = 64
BLK = 128
NSA_HEADS = 4
CMP_LEN = 32
CMP_STRIDE = 16
CMP_HIDDEN = 256
SEL_BLK = 64
N_SEL = 16
N_LOCAL = 2
NSA_WINDOW = 512
SWA_HEADS = 8
SWA_KV_HEADS = 2
SWA_WINDOW = 128
SB_HEADS = 4
N_BUCKETS = 32
T5_MAX_DISTANCE = 4096
RMS_EPS = 1e-6
FORCE_SCORE = 1e6
NEG = -1e30
LOG2E = 1.0 / math.log(2.0)
SEL_TILE = 512
SB_GROUP = 3
SB_LOG_FLOOR = -104.0
PICKED = -3e38
CMP_QB = 4
VT_ROWS = HEAD_DIM + 16
SEL_QB = 2
FFN_CHUNK = 1024
BAND_QB = 2
CMP_CHUNK = 128
IMP_PAD = 16
IMP_ROWS = CMP_CHUNK // 4 + IMP_PAD
N_BAND = 25
VMEM_LIMIT = 56 * 1024 * 1024

NT_DIMS = (((1,), (1,)), ((), ()))


def _cparams(sem):
    return pltpu.CompilerParams(dimension_semantics=sem, vmem_limit_bytes=VMEM_LIMIT)


def _rms(x, g):
    ms = jnp.mean(x * x, axis=-1, keepdims=True)
    return x * lax.rsqrt(ms + RMS_EPS) * g


def _iota(shape, axis):
    return lax.broadcasted_iota(jnp.int32, shape, axis)


def _bias_of_dist(dist, tab_ref, head):
    n = jnp.maximum(dist, 0)
    nf = jnp.maximum(n, 1).astype(F32)
    exact = N_BUCKETS // 2
    large = exact + (jnp.log(nf / exact) / math.log(T5_MAX_DISTANCE / exact)
                     * (N_BUCKETS - exact)).astype(jnp.int32)
    large = jnp.minimum(large, N_BUCKETS - 1)
    bucket = jnp.where(n < exact, n, large)
    val = jnp.full(dist.shape, tab_ref[0, head], F32)
    for k in range(1, N_BUCKETS):
        val = jnp.where(bucket == k, tab_ref[k, head], val)
    return val


def _band_table_kernel(tab_ref, o_ref, *, head_off):
    m = pl.program_id(0)
    h = pl.program_id(1)
    dist = BLK * m + _iota((BLK, BLK), 0) - _iota((BLK, BLK), 1)
    o_ref[0, 0] = _bias_of_dist(dist, tab_ref, h + head_off)


def _band_table(rel_bias, n_band, n_heads, head_off):
    return pl.pallas_call(
        functools.partial(_band_table_kernel, head_off=head_off),
        grid=(n_band, n_heads),
        in_specs=[pl.BlockSpec(memory_space=pltpu.SMEM)],
        out_specs=pl.BlockSpec((1, 1, BLK, BLK), lambda m, h: (m, h, 0, 0)),
        out_shape=jax.ShapeDtypeStruct((n_band, n_heads, BLK, BLK), F32),
        compiler_params=_cparams(("arbitrary", "arbitrary")),
        name="band_table",
    )(rel_bias)


def _band_table_t_kernel(tab_ref, o_ref, *, head_off, shift):
    m = pl.program_id(0)
    h = pl.program_id(1) + head_off
    dist = BLK * m + _iota((BLK, BLK), 1) - _iota((BLK, BLK), 0)
    bias = _bias_of_dist(dist, tab_ref, h)
    o_ref[0] = (bias - tab_ref[N_BUCKETS - 1, h]) * LOG2E if shift else bias


def _band_table_t(rel_bias, n_band, n_heads, head_off, shift):
    return pl.pallas_call(
        functools.partial(_band_table_t_kernel, head_off=head_off, shift=shift),
        grid=(n_band, n_heads),
        in_specs=[pl.BlockSpec(memory_space=pltpu.SMEM)],
        out_specs=pl.BlockSpec((1, BLK, BLK), lambda m, h: (m, 0, h)),
        out_shape=jax.ShapeDtypeStruct((n_band, BLK, n_heads * BLK), F32),
        compiler_params=_cparams(("arbitrary", "arbitrary")),
        name="band_table_t",
    )(rel_bias)


def _cmp_table_kernel(tab_ref, o_ref, *, ncp):
    h = pl.program_id(0)
    rel = _iota((2 * ncp, BLK), 0) - (ncp - 8)
    dist = _iota((2 * ncp, BLK), 1) - CMP_STRIDE * rel - (CMP_LEN - 1)
    o_ref[...] = _bias_of_dist(dist, tab_ref, h) * LOG2E


def _cmp_table(rel_bias, ncp):
    return pl.pallas_call(
        functools.partial(_cmp_table_kernel, ncp=ncp),
        grid=(NSA_HEADS,),
        in_specs=[pl.BlockSpec(memory_space=pltpu.SMEM)],
        out_specs=pl.BlockSpec((2 * ncp, BLK), lambda h: (0, h)),
        out_shape=jax.ShapeDtypeStruct((2 * ncp, NSA_HEADS * BLK), F32),
        compiler_params=_cparams(("arbitrary",)),
        name="cmp_table",
    )(rel_bias)


def _proj_kernel(x_ref, g_ref, w_ref, qt_ref, ks_ref, vs_ref, kw_ref, vw_ref, sqt_ref, sk_ref, sv_ref,
                 bq_ref, bk_ref, bv_ref, kc_ref, vc_ref, gt_ref):
    h = _rms(x_ref[0], g_ref[...]).astype(BF16)
    tm = x_ref.shape[1]

    def seg(a, b):
        return jnp.dot(h, w_ref[:, a:b], preferred_element_type=F32)

    def heads(ref, a, n):
        y = seg(a, a + n * HEAD_DIM).astype(BF16)
        for i in range(n):
            ref[0, i] = y[:, i * HEAD_DIM:(i + 1) * HEAD_DIM]

    def queries_t(ref, a, n, scale=None):
        yt = seg(a, a + n * HEAD_DIM).T
        if scale is not None:
            yt = yt * scale
        for j in range(tm // BLK):
            for i in range(n):
                ref[0, j, :, i * BLK:(i + 1) * BLK] = (
                    yt[i * HEAD_DIM:(i + 1) * HEAD_DIM, j * BLK:(j + 1) * BLK].astype(BF16))

    ones_rows = jnp.where(_iota((VT_ROWS - HEAD_DIM, tm), 0) == 0, 1.0, 0.0)

    def values_t(yt):
        return jnp.concatenate([yt, ones_rows], axis=0).astype(BF16)

    queries_t(qt_ref, 0, NSA_HEADS, LOG2E)
    y = seg(256, 512)
    vs_ref[0] = values_t(y[:, 0:128].T[64:128])
    vw_ref[0] = values_t(y[:, 128:256].T[64:128])
    y = y.astype(BF16)
    blk_in_tile = ((pl.program_id(1) * tm + _iota((tm, HEAD_DIM), 0)) & (SEL_TILE - 1)) // SEL_BLK
    onehot = jnp.where(_iota((tm, HEAD_DIM), 1) == blk_in_tile, 1.0, 0.0).astype(BF16)
    ks_ref[0] = jnp.concatenate([y[:, 0:64], onehot], axis=1)
    kw_ref[0] = y[:, 128:192]
    queries_t(sqt_ref, 512, SWA_HEADS)
    heads(sk_ref, 1024, SWA_KV_HEADS)
    yt = seg(1152, 1280).T
    for i in range(SWA_KV_HEADS):
        sv_ref[0, i] = values_t(yt[i * HEAD_DIM:(i + 1) * HEAD_DIM])
    queries_t(bq_ref, 1280, SB_HEADS)
    bk_ref[0] = seg(1536, 1792).astype(BF16)
    bv_ref[0] = seg(1792, 2048).T.astype(BF16)
    y = seg(2048, 2304)
    kc_ref[0] = y[:, 0:64]
    vc_ref[0] = y[:, 64:128]
    gt_ref[0] = y[:, 128:256]


def _proj(x, g, w, tm):
    b, s, d = x.shape
    nw = w.shape[1]
    hd = lambda n: pl.BlockSpec((1, n, tm, HEAD_DIM), lambda bi, i: (bi, 0, i, 0))
    flat = lambda n: pl.BlockSpec((1, tm, n), lambda bi, i: (bi, i, 0))
    hshape = lambda n: jax.ShapeDtypeStruct((b, n, s, HEAD_DIM), BF16)
    fshape = lambda n, dt: jax.ShapeDtypeStruct((b, s, n), dt)
    qt_spec = lambda n: pl.BlockSpec((1, tm // BLK, HEAD_DIM, n * BLK), lambda bi, i: (bi, i, 0, 0))
    qt_shape = lambda n: jax.ShapeDtypeStruct((b, s // BLK, HEAD_DIM, n * BLK), BF16)
    vt_spec = pl.BlockSpec((1, VT_ROWS, tm), lambda bi, i: (bi, 0, i))
    vt_shape = jax.ShapeDtypeStruct((b, VT_ROWS, s), BF16)
    return pl.pallas_call(
        _proj_kernel,
        grid=(b, s // tm),
        in_specs=[pl.BlockSpec((1, tm, d), lambda bi, i: (bi, i, 0)),
                  pl.BlockSpec((1, d), lambda bi, i: (0, 0)),
                  pl.BlockSpec((d, nw), lambda bi, i: (0, 0))],
        out_specs=[qt_spec(NSA_HEADS), flat(128), vt_spec, flat(64), vt_spec,
                   qt_spec(SWA_HEADS), hd(SWA_KV_HEADS),
                   pl.BlockSpec((1, SWA_KV_HEADS, VT_ROWS, tm), lambda bi, i: (bi, 0, 0, i)),
                   qt_spec(SB_HEADS), flat(256),
                   pl.BlockSpec((1, SB_HEADS * HEAD_DIM, tm), lambda bi, i: (bi, 0, i)),
                   flat(64), flat(64), flat(128)],
        out_shape=[qt_shape(NSA_HEADS), fshape(128, BF16), vt_shape, fshape(64, BF16), vt_shape,
                   qt_shape(SWA_HEADS), hshape(SWA_KV_HEADS),
                   jax.ShapeDtypeStruct((b, SWA_KV_HEADS, VT_ROWS, s), BF16),
                   qt_shape(SB_HEADS), fshape(256, BF16),
                   jax.ShapeDtypeStruct((b, SB_HEADS * HEAD_DIM, s), BF16),
                   fshape(64, F32), fshape(64, F32), fshape(128, F32)],
        compiler_params=_cparams(("parallel", "parallel")),
        name="proj",
    )(x, g, w)


def _compress_kernel(c_ref, pos_ref, w1_ref, b1_ref, w2_ref, o_ref, ot_ref):
    c = c_ref[0, 0]
    nch = c.shape[0]
    xa = (c + pos_ref[0, 0:1]).astype(BF16)
    xb = (c + pos_ref[0, 1:2]).astype(BF16)
    p = jnp.dot(xa, w1_ref[0, 0], preferred_element_type=F32)
    q = jnp.dot(xb, w1_ref[0, 1], preferred_element_type=F32)
    hid = p + pltpu.roll(q, nch - 1, 0) + b1_ref[0]
    hid = hid * jax.nn.sigmoid(hid)
    out = jnp.dot(hid.astype(BF16), w2_ref[0], preferred_element_type=F32)
    row = _iota(out.shape, 0)
    out = jnp.where(row < nch - 1, out, 0.0)
    o_ref[0, 0] = out.astype(BF16)
    ot_ref[0, 0] = jnp.concatenate([out, jnp.zeros_like(out)], axis=1).T[:HEAD_DIM].astype(BF16)


def _compress(ckv, pos, w1, b1, w2):
    _, b, nch, cw = ckv.shape
    return pl.pallas_call(
        _compress_kernel,
        grid=(2, b),
        in_specs=[pl.BlockSpec((1, 1, nch, cw), lambda j, bi: (j, bi, 0, 0)),
                  pl.BlockSpec((1, 2, cw), lambda j, bi: (j, 0, 0)),
                  pl.BlockSpec((1, 2, cw, CMP_HIDDEN), lambda j, bi: (j, 0, 0, 0)),
                  pl.BlockSpec((1, 1, CMP_HIDDEN), lambda j, bi: (j, 0, 0)),
                  pl.BlockSpec((1, CMP_HIDDEN, HEAD_DIM), lambda j, bi: (j, 0, 0))],
        out_specs=[pl.BlockSpec((1, 1, nch, HEAD_DIM), lambda j, bi: (j, bi, 0, 0)),
                   pl.BlockSpec((1, 1, HEAD_DIM, nch), lambda j, bi: (j, bi, 0, 0))],
        out_shape=[jax.ShapeDtypeStruct((2, b, nch, HEAD_DIM), BF16),
                   jax.ShapeDtypeStruct((2, b, HEAD_DIM, nch), BF16)],
        compiler_params=_cparams(("parallel", "parallel")),
        name="compress",
    )(ckv, pos, w1, b1, w2)


def _merge_heads(o, n):
    return jnp.concatenate([o[i * BLK:(i + 1) * BLK] for i in range(n)], axis=1)


def _transpose_pad(x, axis):
    pad = jnp.zeros(x.shape, x.dtype)
    return jnp.concatenate([x, pad], axis=axis).T


def _cmp_kernel(qt_ref, kc_ref, vct_ref, hct_ref, wmapt_ref, o_ref, pen_ref, s_scr, oacc_scr, iacc_scr):
    step = pl.program_id(1)
    ncp = kc_ref.shape[2]
    ns = wmapt_ref.shape[0] - IMP_PAD
    lanes = NSA_HEADS * BLK
    row_c = _iota((CMP_CHUNK, lanes), 0)
    lane_q = _iota((CMP_CHUNK, lanes), 1) & (BLK - 1)
    n_chunks = (CMP_QB * step) // (CMP_CHUNK // 8) + 1

    def score_chunk(ch, mxs):
        c0 = pl.multiple_of(ch * CMP_CHUNK, CMP_CHUNK)
        kc = kc_ref[0, 0, pl.ds(c0, CMP_CHUNK), :]
        out = []
        for qb in range(CMP_QB):
            n = CMP_QB * step + qb
            s = jnp.dot(kc, qt_ref[0, qb], preferred_element_type=F32)
            bias = hct_ref[pl.ds(pl.multiple_of(ncp - 8 - 8 * n + c0, 8), CMP_CHUNK), :]
            visible = (BLK * n + lane_q - CMP_STRIDE * (c0 + row_c) - (CMP_LEN - 1)) >= 0
            s = jnp.where(visible, s + bias, NEG)
            s_scr[qb, ch] = s
            out.append(jnp.maximum(mxs[qb], jnp.max(s, axis=0, keepdims=True)))
        return tuple(out)

    mxs = lax.fori_loop(0, n_chunks, score_chunk, tuple(jnp.full((1, lanes), NEG, F32) for _ in range(CMP_QB)))
    ms = [jnp.where(mx <= 0.5 * NEG, 0.0, mx) for mx in mxs]
    oacc_scr[...] = jnp.zeros(oacc_scr.shape, F32)
    iacc_scr[...] = jnp.zeros(iacc_scr.shape, F32)

    def prob_chunk(ch, ls):
        c0 = pl.multiple_of(ch * CMP_CHUNK, CMP_CHUNK)
        vct = vct_ref[0, 0, :, pl.ds(c0, CMP_CHUNK)]
        r0 = pl.multiple_of(ch * (CMP_CHUNK // 4), 16)
        wmt = wmapt_ref[pl.ds(r0, IMP_ROWS), pl.ds(c0, CMP_CHUNK)]
        out = []
        for qb in range(CMP_QB):
            p = jnp.exp2(s_scr[qb, ch] - ms[qb])
            pb = p.astype(BF16)
            oacc_scr[qb] += jnp.dot(vct, pb, preferred_element_type=F32)
            iacc_scr[qb, pl.ds(r0, IMP_ROWS), :] += jnp.dot(wmt, pb, preferred_element_type=F32)
            out.append(ls[qb] + jnp.sum(p, axis=0, keepdims=True))
        return tuple(out)

    ls = lax.fori_loop(0, n_chunks, prob_chunk, tuple(jnp.zeros((1, lanes), F32) for _ in range(CMP_QB)))
    imps = []
    for qb in range(CMP_QB):
        inv = 1.0 / jnp.maximum(ls[qb], 1e-30)
        o_t = oacc_scr[qb] * inv
        o_ref[0, qb * BLK:(qb + 1) * BLK, :] = jnp.concatenate(
            [_transpose_pad(o_t[:, h * BLK:(h + 1) * BLK], 0)[:, :HEAD_DIM] for h in range(NSA_HEADS)], axis=1)
        w = iacc_scr[qb, 0:ns, :] * inv
        imps.append(functools.reduce(jnp.add, [w[:, h * BLK:(h + 1) * BLK] for h in range(NSA_HEADS)]))

    imp = jnp.concatenate(imps, axis=1)
    shape = (ns, CMP_QB * BLK)
    j = _iota(shape, 0)
    cur = (CMP_QB * BLK * step + _iota(shape, 1)) // SEL_BLK
    valid = j <= cur
    forced = valid & ((j == 0) | (j > cur - N_LOCAL))
    score = jnp.where(forced, PICKED, jnp.where(valid, imp, NEG))
    jf = j.astype(F32)
    for _ in range(min(N_SEL, ns) - (N_LOCAL + 1)):
        best = jnp.max(score, axis=0, keepdims=True)
        first = jnp.min(jnp.where(score == best, jf, float(ns)), axis=0, keepdims=True)
        score = jnp.where(jf == first, PICKED, score)
    pen_ref[0, 0] = jnp.where(valid & (score == PICKED), 0.0, NEG)


def _cmp_attn(qt, kcv, kcv_t, hct, wmapt):
    b, nb, _, lanes = qt.shape
    ncp = kcv.shape[2]
    ns = wmapt.shape[0] - IMP_PAD
    s = nb * BLK
    return pl.pallas_call(
        _cmp_kernel,
        grid=(b, nb // CMP_QB),
        in_specs=[pl.BlockSpec((1, CMP_QB, HEAD_DIM, lanes), lambda bi, i: (bi, i, 0, 0)),
                  pl.BlockSpec((1, 1, ncp, HEAD_DIM), lambda bi, i: (0, bi, 0, 0)),
                  pl.BlockSpec((1, 1, HEAD_DIM, ncp), lambda bi, i: (1, bi, 0, 0)),
                  pl.BlockSpec((2 * ncp, lanes), lambda bi, i: (0, 0)),
                  pl.BlockSpec((ns + IMP_PAD, ncp), lambda bi, i: (0, 0))],
        out_specs=[pl.BlockSpec((1, CMP_QB * BLK, NSA_HEADS * HEAD_DIM), lambda bi, i: (bi, i, 0)),
                   pl.BlockSpec((1, 1, ns, CMP_QB * BLK), lambda bi, i: (bi, i, 0, 0))],
        out_shape=[jax.ShapeDtypeStruct((b, s, NSA_HEADS * HEAD_DIM), F32),
                   jax.ShapeDtypeStruct((b, nb // CMP_QB, ns, CMP_QB * BLK), F32)],
        scratch_shapes=[pltpu.VMEM((CMP_QB, ncp // CMP_CHUNK, CMP_CHUNK, lanes), F32),
                        pltpu.VMEM((CMP_QB, HEAD_DIM, lanes), F32),
                        pltpu.VMEM((CMP_QB, ns + IMP_PAD, lanes), F32)],
        compiler_params=_cparams(("parallel", "arbitrary")),
        name="cmp_attn",
    )(qt, kcv, kcv_t, hct, wmapt)


def _sel_kernel(qt_ref, k_ref, vt_ref, pen_ref, band_ref, o_ref, s_scr, p_scr, m_scr, alpha_scr, acc_scr):
    n = SEL_QB * pl.program_id(1)
    qlanes = NSA_HEADS * BLK
    lanes = SEL_QB * qlanes
    sub = SEL_TILE // BLK
    per = BLK // SEL_BLK
    qt = jnp.concatenate([qt_ref[0, qb] for qb in range(SEL_QB)], axis=1)
    lane = _iota((BLK, lanes), 1)
    t = BLK * (n + lane // qlanes) + (lane & (BLK - 1))

    def scores_into(kt, s_buf, far):
        rows = sub * per
        pen = pen_ref[0, 0, pl.ds(pl.multiple_of(kt * rows, rows), rows), :]
        pen = jnp.concatenate([pen[:, qb * BLK:(qb + 1) * BLK] for qb in range(SEL_QB) for _ in range(NSA_HEADS)],
                              axis=1)
        tail = jnp.concatenate([pen, jnp.zeros((HEAD_DIM - rows, lanes), F32)], axis=0).astype(BF16)
        q_aug = jnp.concatenate([qt, tail], axis=0)
        k0 = pl.multiple_of(kt * SEL_TILE, SEL_TILE)
        s = jnp.dot(k_ref[0, pl.ds(k0, SEL_TILE), :], q_aug, preferred_element_type=F32)
        if far:
            s_buf[...] = s
        else:
            for u in range(sub):
                for qb in range(SEL_QB):
                    cols = slice(qb * qlanes, (qb + 1) * qlanes)
                    s_buf[u * BLK:(u + 1) * BLK, cols] = (
                        s[u * BLK:(u + 1) * BLK, cols] + band_ref[jnp.clip(n + qb - sub * kt - u, 0, N_BAND - 1)])

    def add_values(kt, p_buf):
        k0 = pl.multiple_of(kt * SEL_TILE, SEL_TILE)
        pv = jnp.dot(vt_ref[0, :, pl.ds(k0, SEL_TILE)], p_buf[...], preferred_element_type=F32)
        acc_scr[...] = alpha_scr[...] * acc_scr[...] + pv

    def softmax_into(kt, s_buf, p_buf, causal):
        def piece(u):
            s = s_buf[u * BLK:(u + 1) * BLK, :]
            if causal:
                s = jnp.where(kt * SEL_TILE + u * BLK + _iota((BLK, lanes), 0) <= t, s, NEG)
            return s
        m_i = m_scr[...]
        mx = functools.reduce(jnp.maximum, [jnp.max(piece(u), axis=0, keepdims=True) for u in range(sub)])
        m_new = jnp.maximum(m_i, mx)
        for u in range(sub):
            p_buf[u * BLK:(u + 1) * BLK, :] = jnp.exp2(piece(u) - m_new).astype(BF16)
        m_scr[...] = m_new
        alpha_scr[...] = jnp.exp2(m_i - m_new)

    def stage(i, s_cur, s_nxt, p_cur, p_prev, far):
        add_values(jnp.maximum(i - 1, 0), p_prev)
        scores_into(i + 1, s_nxt, far)
        softmax_into(i, s_cur, p_cur, False)

    def finish(i, s_cur, p_cur, p_prev):
        add_values(jnp.maximum(i - 1, 0), p_prev)
        softmax_into(i, s_cur, p_cur, True)
        add_values(i, p_cur)
        acc = acc_scr[...]
        o_t = acc[:HEAD_DIM] / acc[HEAD_DIM:HEAD_DIM + 1]
        for qb in range(SEL_QB):
            o_ref[0, qb * BLK:(qb + 1) * BLK, :] = jnp.concatenate(
                [_transpose_pad(o_t[:, qb * qlanes + h * BLK:qb * qlanes + (h + 1) * BLK], 0)[:, :HEAD_DIM]
                 for h in range(NSA_HEADS)], axis=1)

    s0, s1, p0, p1 = s_scr.at[0], s_scr.at[1], p_scr.at[0], p_scr.at[1]
    m_scr[...] = jnp.full((1, lanes), NEG, F32)
    alpha_scr[...] = jnp.ones((1, lanes), F32)
    acc_scr[...] = jnp.zeros(acc_scr.shape, F32)
    p1[...] = jnp.zeros((SEL_TILE, lanes), BF16)
    scores_into(0, s0, False)
    last = (BLK * n + BLK - 1) // SEL_TILE
    n_far = jnp.maximum((n - (N_BAND - 2)) // sub, 0)
    far_pairs = jnp.minimum(jnp.maximum((n_far - 1) // 2, 0), last // 2)

    def pair(j, _, far):
        stage(2 * j, s0, s1, p0, p1, far)
        stage(2 * j + 1, s1, s0, p1, p0, far)
        return 0

    lax.fori_loop(0, far_pairs, functools.partial(pair, far=True), 0)
    lax.fori_loop(far_pairs, last // 2, functools.partial(pair, far=False), 0)

    @pl.when(last % 2 == 1)
    def _():
        stage(last - 1, s0, s1, p0, p1, False)
        finish(last, s1, p1, p0)

    @pl.when(last % 2 == 0)
    def _():
        finish(last, s0, p0, p1)


def _sel_attn(qt, ks, vst, pen, band_t):
    b, nb, _, qlanes = qt.shape
    s = nb * BLK
    ns = pen.shape[2]
    lanes = SEL_QB * qlanes
    per_cmp = CMP_QB // SEL_QB
    return pl.pallas_call(
        _sel_kernel,
        grid=(b, nb // SEL_QB),
        in_specs=[pl.BlockSpec((1, SEL_QB, HEAD_DIM, qlanes), lambda bi, i: (bi, i, 0, 0)),
                  pl.BlockSpec((1, s, 2 * HEAD_DIM), lambda bi, i: (bi, 0, 0)),
                  pl.BlockSpec((1, VT_ROWS, s), lambda bi, i: (bi, 0, 0)),
                  pl.BlockSpec((1, 1, ns, SEL_QB * BLK), lambda bi, i: (bi, i // per_cmp, 0, i % per_cmp)),
                  pl.BlockSpec((N_BAND, BLK, qlanes), lambda bi, i: (0, 0, 0))],
        out_specs=pl.BlockSpec((1, SEL_QB * BLK, NSA_HEADS * HEAD_DIM), lambda bi, i: (bi, i, 0)),
        out_shape=jax.ShapeDtypeStruct((b, s, NSA_HEADS * HEAD_DIM), F32),
        scratch_shapes=[pltpu.VMEM((2, SEL_TILE, lanes), F32),
                        pltpu.VMEM((2, SEL_TILE, lanes), BF16),
                        pltpu.VMEM((1, lanes), F32),
                        pltpu.VMEM((1, lanes), F32),
                        pltpu.VMEM((VT_ROWS, lanes), F32)],
        compiler_params=_cparams(("parallel", "arbitrary")),
        name="sel_attn",
    )(qt, ks, vst, pen, band_t)


def _band_kernel(*refs, window, groups, use_sinks, base2, n_parts):
    qt_ref = refs[0]
    k_refs = refs[1:1 + n_parts]
    vt_refs = refs[1 + n_parts:1 + 2 * n_parts]
    band_ref, sink_ref, o_ref = refs[1 + 2 * n_parts:]
    step = pl.program_id(1)
    n_prev = window // BLK
    nk = (n_prev + 1) * BLK
    lanes = 4 * BLK
    row = _iota((nk, lanes), 0)
    dist = n_prev * BLK + (_iota((nk, lanes), 1) & (BLK - 1)) - row
    in_window = (dist >= 0) & (dist < window)
    for g in range(groups):
        kwin = jnp.concatenate([r[0, g] for r in k_refs], axis=0)
        vtwin = jnp.concatenate([r[0, g] for r in vt_refs], axis=1)
        bias = jnp.concatenate([band_ref[n_prev - u, :, g * lanes:(g + 1) * lanes] for u in range(n_prev + 1)],
                               axis=0)
        for qb in range(BAND_QB):
            n = BAND_QB * step + qb
            qt = qt_ref[0, qb, :, g * lanes:(g + 1) * lanes]
            s = jnp.dot(kwin[qb * BLK:qb * BLK + nk], qt, preferred_element_type=F32)
            s = jnp.where(in_window & (row >= (n_prev - n) * BLK), s + bias, NEG)
            mx = jnp.max(s, axis=0, keepdims=True)
            if use_sinks:
                sink = sink_ref[g]
                mx = jnp.maximum(mx, sink)
            p = (jnp.exp2(s - mx) if base2 else jnp.exp(s - mx)).astype(BF16)
            acc = jnp.dot(vtwin[:, qb * BLK:qb * BLK + nk], p, preferred_element_type=F32)
            den = acc[HEAD_DIM:HEAD_DIM + 1]
            if use_sinks:
                den = den + jnp.exp(sink - mx)
            o_t = acc[:HEAD_DIM] / den
            o_ref[0, qb * BLK:(qb + 1) * BLK, 4 * g * HEAD_DIM:4 * (g + 1) * HEAD_DIM] = jnp.concatenate(
                [_transpose_pad(o_t[:, h * BLK:(h + 1) * BLK], 0)[:, :HEAD_DIM] for h in range(4)], axis=1)


def _band_attn(qt, k, vt, band_t, sinks, window, use_sinks, base2, name):
    b, nb, _, width = qt.shape
    g = width // (4 * BLK)
    s = nb * BLK
    n_prev = window // BLK
    tq = BAND_QB * BLK
    if n_prev * BLK % tq == 0:
        piece = tq
        starts = [lambda i, d=d: jnp.maximum(i - d, 0) for d in range(n_prev * BLK // tq, 0, -1)]
    else:
        assert n_prev == 1
        piece = BLK
        starts = [lambda i: jnp.maximum(BAND_QB * i - 1, 0)]
    k_specs = [pl.BlockSpec((1, g, piece, HEAD_DIM), lambda bi, i, f=f: (bi, 0, f(i), 0)) for f in starts]
    k_specs.append(pl.BlockSpec((1, g, tq, HEAD_DIM), lambda bi, i: (bi, 0, i, 0)))
    vt_specs = [pl.BlockSpec((1, g, VT_ROWS, piece), lambda bi, i, f=f: (bi, 0, 0, f(i))) for f in starts]
    vt_specs.append(pl.BlockSpec((1, g, VT_ROWS, tq), lambda bi, i: (bi, 0, 0, i)))
    n_parts = len(k_specs)
    return pl.pallas_call(
        functools.partial(_band_kernel, window=window, groups=g, use_sinks=use_sinks, base2=base2,
                          n_parts=n_parts),
        grid=(b, s // tq),
        in_specs=[pl.BlockSpec((1, BAND_QB, HEAD_DIM, width), lambda bi, i: (bi, i, 0, 0))] + k_specs + vt_specs
                 + [pl.BlockSpec((n_prev + 1, BLK, width), lambda bi, i: (0, 0, 0)),
                    pl.BlockSpec((g, 1, 4 * BLK), lambda bi, i: (0, 0, 0))],
        out_specs=pl.BlockSpec((1, tq, g * 4 * HEAD_DIM), lambda bi, i: (bi, i, 0)),
        out_shape=jax.ShapeDtypeStruct((b, s, g * 4 * HEAD_DIM), F32),
        compiler_params=_cparams(("parallel", "arbitrary")),
        name=name,
    )(qt, *([k] * n_parts), *([vt] * n_parts), band_t, sinks)


def _sb_kernel(qt_ref, k_ref, vt_ref, o_ref):
    n = pl.program_id(1)
    lanes = SB_HEADS * BLK
    gk = SB_GROUP * BLK
    t = BLK * n + (_iota((BLK, lanes), 1) & (BLK - 1))
    row = _iota((BLK, lanes), 0)
    later = jnp.where(_iota((gk, 2 * gk), 1) % gk > _iota((gk, 2 * gk), 0), 1.0, 0.0).astype(BF16)
    zero = jnp.zeros((HEAD_DIM, BLK), BF16)
    q_bd = jnp.concatenate(
        [jnp.concatenate([zero] * h + [qt_ref[0, 0, :, h * BLK:(h + 1) * BLK]] + [zero] * (SB_HEADS - 1 - h), axis=0)
         for h in range(SB_HEADS)], axis=1)

    def cond(c):
        top, carry, _ = c
        return (top >= 0) & (jnp.max(carry) > SB_LOG_FLOOR)

    def body(c):
        top, carry, acc = c
        ks, vts, befores = [], [], []
        for j in range(SB_GROUP - 1, -1, -1):
            kb = top - j
            k0 = pl.multiple_of(jnp.maximum(kb, 0) * BLK, BLK)
            ks.append(k_ref[0, pl.ds(k0, BLK), :])
            vts.append(vt_ref[0, :, pl.ds(k0, BLK)])
            befores.append((kb >= 0) & (k0 + row < t))
        before = jnp.concatenate(befores, axis=0)
        z = jnp.dot(jnp.concatenate(ks, axis=0), q_bd, preferred_element_type=F32)
        softplus = jnp.maximum(z, 0.0) + jnp.log(1.0 + jnp.exp(-jnp.abs(z)))
        log_not = jnp.where(before, -softplus, 0.0)
        hi = log_not.astype(BF16)
        lo = (log_not - hi.astype(F32)).astype(BF16)
        tail = jnp.dot(later, jnp.concatenate([hi, lo], axis=0), preferred_element_type=F32)
        a = jnp.where(before, jnp.exp(z + log_not + tail + carry), 0.0).astype(BF16)
        pv = jnp.dot(jnp.concatenate(vts, axis=1), a, preferred_element_type=F32)
        acc = tuple(acc[h] + pv[h * HEAD_DIM:(h + 1) * HEAD_DIM, h * BLK:(h + 1) * BLK] for h in range(SB_HEADS))
        return top - SB_GROUP, carry + jnp.sum(log_not, axis=0, keepdims=True), acc

    init = (n, jnp.zeros((1, lanes), F32), tuple(jnp.zeros((HEAD_DIM, BLK), F32) for _ in range(SB_HEADS)))
    _, _, acc = lax.while_loop(cond, body, init)
    o_ref[0] = jnp.concatenate([_transpose_pad(a, 0)[:, :HEAD_DIM] for a in acc], axis=1)


def _sb_attn(qt, k, vt):
    b, nb, _, lanes = qt.shape
    s = nb * BLK
    width = SB_HEADS * HEAD_DIM
    return pl.pallas_call(
        _sb_kernel,
        grid=(b, nb),
        in_specs=[pl.BlockSpec((1, 1, HEAD_DIM, lanes), lambda bi, n: (bi, n, 0, 0)),
                  pl.BlockSpec((1, s, width), lambda bi, n: (bi, 0, 0)),
                  pl.BlockSpec((1, width, s), lambda bi, n: (bi, 0, 0))],
        out_specs=pl.BlockSpec((1, BLK, width), lambda bi, n: (bi, n, 0)),
        out_shape=jax.ShapeDtypeStruct((b, s, width), F32),
        compiler_params=_cparams(("parallel", "arbitrary")),
        name="sb_attn",
    )(qt, k, vt)


def _out_kernel(x_ref, oc_ref, os_ref, ow_ref, gt_ref, swa_ref, sb_ref, gg_ref, w_ref, gp_ref, o_ref):
    tm = x_ref.shape[1]
    wa = NSA_HEADS * HEAD_DIM
    wb = wa + SWA_HEADS * HEAD_DIM
    gates = jax.nn.sigmoid(gt_ref[0])
    lane = _iota((tm, wa), 1)

    def spread(branch):
        cols = [jnp.broadcast_to(gates[:, branch * NSA_HEADS + i:branch * NSA_HEADS + i + 1], (tm, wa))
                for i in range(NSA_HEADS)]
        out = cols[NSA_HEADS - 1]
        for i in range(NSA_HEADS - 2, -1, -1):
            out = jnp.where(lane < (i + 1) * HEAD_DIM, cols[i], out)
        return out

    o_nsa = spread(0) * oc_ref[0] + spread(1) * os_ref[0] + spread(2) * ow_ref[0]
    gg = gg_ref[...]
    mix = jnp.concatenate([_rms(o_nsa, gg[:, :wa]), _rms(swa_ref[0], gg[:, wa:wb]), _rms(sb_ref[0], gg[:, wb:])],
                          axis=1).astype(BF16)
    y = jnp.dot(mix, w_ref[...], preferred_element_type=F32)
    o_ref[0] = x_ref[0] + _rms(y, gp_ref[...])


def _out_proj(x, o_cmp, o_sel, o_win, gates, o_swa, o_sb, gg, w, gp, tm):
    b, s, d = x.shape
    row = lambda n: pl.BlockSpec((1, tm, n), lambda bi, i: (bi, i, 0))
    const = lambda a, c: pl.BlockSpec((a, c), lambda bi, i: (0, 0))
    return pl.pallas_call(
        _out_kernel,
        grid=(b, s // tm),
        in_specs=[row(d), row(256), row(256), row(256), row(128), row(512), row(256),
                  const(1, d), const(d, d), const(1, d)],
        out_specs=row(d),
        out_shape=jax.ShapeDtypeStruct((b, s, d), F32),
        compiler_params=_cparams(("parallel", "parallel")),
        name="out_proj",
    )(x, o_cmp, o_sel, o_win, gates, o_swa, o_sb, gg, w, gp)


def _ffn_kernel(x_ref, g1_ref, wg_ref, wu_ref, wd_ref, g2_ref, o_ref, *, chunk):
    x = x_ref[...]
    h = _rms(x, g1_ref[...]).astype(BF16)
    dff = wg_ref.shape[1]
    f = jnp.zeros(x.shape, F32)
    for a0 in range(0, dff, chunk):
        a1 = min(a0 + chunk, dff)
        gate = jnp.dot(h, wg_ref[:, a0:a1], preferred_element_type=F32)
        up = jnp.dot(h, wu_ref[:, a0:a1], preferred_element_type=F32)
        a = (gate * jax.nn.sigmoid(gate) * up).astype(BF16)
        f = f + jnp.dot(a, wd_ref[a0:a1, :], preferred_element_type=F32)
    o_ref[...] = x + _rms(f, g2_ref[...])


def _ffn(x, g1, wg, wu, wd, g2, tm):
    t, d = x.shape
    dff = wg.shape[1]
    row = pl.BlockSpec((tm, d), lambda i: (i, 0))
    const = lambda a, c: pl.BlockSpec((a, c), lambda i: (0, 0))
    return pl.pallas_call(
        functools.partial(_ffn_kernel, chunk=FFN_CHUNK),
        grid=(t // tm,),
        in_specs=[row, const(1, d), const(d, dff), const(d, dff), const(dff, d), const(1, d)],
        out_specs=row,
        out_shape=jax.ShapeDtypeStruct((t, d), F32),
        compiler_params=_cparams(("parallel",)),
        name="ffn",
    )(x, g1, wg, wu, wd, g2)


def _permute_w_in(w):
    scale = 1.0 / math.sqrt(HEAD_DIM)
    nq, kc, vc, rest, ng = w[:, :256], w[:, 256:320], w[:, 320:384], w[:, 384:640], w[:, 640:652]
    sq, skv = w[:, 652:1164], w[:, 1164:1420]
    bq, bkv = w[:, 1420:1676], w[:, 1676:2188]
    pad = jnp.zeros((w.shape[0], 128 - ng.shape[1]), w.dtype)
    return jnp.concatenate([nq * scale, rest, sq * scale, skv, bq * scale, bkv, kc, vc, ng, pad], axis=1)


def _cmp_to_sel(nc_pad, nc, ns):
    c0 = jnp.arange(nc_pad)[:, None] * CMP_STRIDE
    s0 = jnp.arange(ns)[None, :] * SEL_BLK
    ov = jnp.minimum(c0 + CMP_LEN, s0 + SEL_BLK) - jnp.maximum(c0, s0)
    w = jnp.clip(ov, 0, None).astype(F32) / CMP_LEN
    return jnp.where(jnp.arange(nc_pad)[:, None] < nc, w, 0.0).astype(BF16)


def kernel(x, rel_bias, ln_attn_pre, w_in, nsa_cmp_pos, nsa_phi_w1, nsa_phi_b1, nsa_phi_w2, swa_sinks,
           grp_norm_g, w_out, ln_attn_post, ln_ffn_pre, ffn_w_gate, ffn_w_up, ffn_w_down, ln_ffn_post):
    b, s, d = x.shape
    depth = w_in.shape[0]
    nch = s // CMP_STRIDE
    ns = s // SEL_BLK
    t = b * s

    band_nsa_t = _band_table_t(rel_bias, N_BAND, NSA_HEADS, 0, True)
    band_swa_t = _band_table_t(rel_bias, SWA_WINDOW // BLK + 1, SWA_HEADS, NSA_HEADS, False)
    hct = _cmp_table(rel_bias, nch)
    wmapt = jnp.pad(_cmp_to_sel(nch, nch - 1, ns).T, ((0, IMP_PAD), (0, 0)))
    no_sinks = jnp.zeros((1, 1, 4 * BLK), F32)

    for l in range(depth):
        outs = _proj(x, ln_attn_pre[l][None], _permute_w_in(w_in[l]).astype(BF16), 512)
        qt, ks, vs, kw, vw, sqt, sk, sv, bq, bk, bv, kc, vc, gates = outs
        ckv = jnp.stack([kc.reshape(b, nch, CMP_STRIDE * HEAD_DIM), vc.reshape(b, nch, CMP_STRIDE * HEAD_DIM)])
        kcv, kcv_t = _compress(ckv, nsa_cmp_pos[l].reshape(2, 2, CMP_STRIDE * HEAD_DIM),
                               nsa_phi_w1[l].reshape(2, 2, CMP_STRIDE * HEAD_DIM, CMP_HIDDEN).astype(BF16),
                               nsa_phi_b1[l][:, None, :], nsa_phi_w2[l].astype(BF16))
        o_cmp, selmask = _cmp_attn(qt, kcv, kcv_t, hct, wmapt)
        o_sel = _sel_attn(qt, ks, vs, selmask, band_nsa_t)
        o_win = _band_attn(qt, kw[:, None], vw[:, None], band_nsa_t, no_sinks, NSA_WINDOW, False, True,
                           "win_attn")
        sinks = jnp.broadcast_to(swa_sinks[l].reshape(SWA_KV_HEADS, 1, 4, 1),
                                 (SWA_KV_HEADS, 1, 4, BLK)).reshape(SWA_KV_HEADS, 1, 4 * BLK)
        o_swa = _band_attn(sqt, sk, sv, band_swa_t, sinks, SWA_WINDOW, True, False, "swa_attn")
        o_sb = _sb_attn(bq, bk, bv)
        x = _out_proj(x, o_cmp, o_sel, o_win, gates, o_swa, o_sb, grp_norm_g[l][None], w_out[l].astype(BF16),
                      ln_attn_post[l][None], 256)
        x = _ffn(x.reshape(t, d), ln_ffn_pre[l][None], ffn_w_gate[l].astype(BF16), ffn_w_up[l].astype(BF16),
                 ffn_w_down[l].astype(BF16), ln_ffn_post[l][None], 256).reshape(b, s, d)
    return x
```

```python
import functools
import math

import jax
import jax.numpy as jnp
from jax import lax
from jax.experimental import pallas as pl
from jax.experimental.pallas import tpu as pltpu

F32 = jnp.float32
BF16 = jnp.bfloat16

HEAD_DIM = 64
BLK = 128
NSA_HEADS = 4
CMP_LEN = 32
CMP_STRIDE = 16
CMP_HIDDEN = 256
SEL_BLK = 64
N_SEL = 16
N_LOCAL = 2
NSA_WINDOW = 512
SWA_HEADS = 8
SWA_KV_HEADS = 2
SWA_WINDOW = 128
SB_HEADS = 4
N_BUCKETS = 32
T5_MAX_DISTANCE = 4096
RMS_EPS = 1e-6
FORCE_SCORE = 1e6
NEG = -1e30
LOG2E = 1.0 / math.log(2.0)
SEL_TILE = 512
SB_GROUP = 3
SB_LOG2_FLOOR = -104.0 * LOG2E
PICKED = -3e38
CMP_QB = 4
VT_ROWS = HEAD_DIM + 16
OUT_ROWS = 128
SEL_QB = 2
FFN_CHUNK = 1024
BAND_QB = 4
CMP_CHUNK = 128
IMP_PAD = 16
IMP_ROWS = CMP_CHUNK // 4 + IMP_PAD
N_BAND = 25
VMEM_LIMIT = 56 * 1024 * 1024

NT_DIMS = (((1,), (1,)), ((), ()))


def _cparams(sem):
    return pltpu.CompilerParams(dimension_semantics=sem, vmem_limit_bytes=VMEM_LIMIT)


def _rms(x, g):
    ms = jnp.mean(x * x, axis=-1, keepdims=True)
    return x * lax.rsqrt(ms + RMS_EPS) * g


def _iota(shape, axis):
    return lax.broadcasted_iota(jnp.int32, shape, axis)


def _bias_of_dist(dist, tab_ref, head):
    n = jnp.maximum(dist, 0)
    nf = jnp.maximum(n, 1).astype(F32)
    exact = N_BUCKETS // 2
    large = exact + (jnp.log(nf / exact) / math.log(T5_MAX_DISTANCE / exact)
                     * (N_BUCKETS - exact)).astype(jnp.int32)
    large = jnp.minimum(large, N_BUCKETS - 1)
    bucket = jnp.where(n < exact, n, large)
    val = jnp.full(dist.shape, tab_ref[0, head], F32)
    for k in range(1, N_BUCKETS):
        val = jnp.where(bucket == k, tab_ref[k, head], val)
    return val


def _band_table_kernel(tab_ref, o_ref, *, head_off):
    m = pl.program_id(0)
    h = pl.program_id(1)
    dist = BLK * m + _iota((BLK, BLK), 0) - _iota((BLK, BLK), 1)
    o_ref[0, 0] = _bias_of_dist(dist, tab_ref, h + head_off)


def _band_table(rel_bias, n_band, n_heads, head_off):
    return pl.pallas_call(
        functools.partial(_band_table_kernel, head_off=head_off),
        grid=(n_band, n_heads),
        in_specs=[pl.BlockSpec(memory_space=pltpu.SMEM)],
        out_specs=pl.BlockSpec((1, 1, BLK, BLK), lambda m, h: (m, h, 0, 0)),
        out_shape=jax.ShapeDtypeStruct((n_band, n_heads, BLK, BLK), F32),
        compiler_params=_cparams(("arbitrary", "arbitrary")),
        name="band_table",
    )(rel_bias)


def _band_table_t_kernel(tab_ref, o_ref, *, head_off, shift):
    m = pl.program_id(0)
    h = pl.program_id(1) + head_off
    dist = BLK * m + _iota((BLK, BLK), 1) - _iota((BLK, BLK), 0)
    bias = _bias_of_dist(dist, tab_ref, h)
    o_ref[0] = (bias - tab_ref[N_BUCKETS - 1, h]) * LOG2E if shift else bias


def _band_table_t(rel_bias, n_band, n_heads, head_off, shift):
    return pl.pallas_call(
        functools.partial(_band_table_t_kernel, head_off=head_off, shift=shift),
        grid=(n_band, n_heads),
        in_specs=[pl.BlockSpec(memory_space=pltpu.SMEM)],
        out_specs=pl.BlockSpec((1, BLK, BLK), lambda m, h: (m, 0, h)),
        out_shape=jax.ShapeDtypeStruct((n_band, BLK, n_heads * BLK), F32),
        compiler_params=_cparams(("arbitrary", "arbitrary")),
        name="band_table_t",
    )(rel_bias)


def _cmp_table_kernel(tab_ref, o_ref, *, ncp):
    h = pl.program_id(0)
    rel = _iota((2 * ncp, BLK), 0) - (ncp - 8)
    dist = _iota((2 * ncp, BLK), 1) - CMP_STRIDE * rel - (CMP_LEN - 1)
    o_ref[...] = _bias_of_dist(dist, tab_ref, h) * LOG2E


def _cmp_table(rel_bias, ncp):
    return pl.pallas_call(
        functools.partial(_cmp_table_kernel, ncp=ncp),
        grid=(NSA_HEADS,),
        in_specs=[pl.BlockSpec(memory_space=pltpu.SMEM)],
        out_specs=pl.BlockSpec((2 * ncp, BLK), lambda h: (0, h)),
        out_shape=jax.ShapeDtypeStruct((2 * ncp, NSA_HEADS * BLK), F32),
        compiler_params=_cparams(("arbitrary",)),
        name="cmp_table",
    )(rel_bias)


def _proj_kernel(x_ref, g_ref, w_ref, qt_ref, ks_ref, vs_ref, kw_ref, vw_ref, sqt_ref, sk_ref, sv_ref,
                 bq_ref, bk_ref, bv_ref, kc_ref, vc_ref, gt_ref):
    h = _rms(x_ref[0], g_ref[...]).astype(BF16)
    tm = x_ref.shape[1]

    def seg(a, b):
        return jnp.dot(h, w_ref[:, a:b], preferred_element_type=F32)

    def heads(ref, a, n):
        y = seg(a, a + n * HEAD_DIM).astype(BF16)
        for i in range(n):
            ref[0, i] = y[:, i * HEAD_DIM:(i + 1) * HEAD_DIM]

    def queries_t(ref, a, n, scale=None):
        yt = seg(a, a + n * HEAD_DIM).T
        if scale is not None:
            yt = yt * scale
        for j in range(tm // BLK):
            for i in range(n):
                ref[0, j, :, i * BLK:(i + 1) * BLK] = (
                    yt[i * HEAD_DIM:(i + 1) * HEAD_DIM, j * BLK:(j + 1) * BLK].astype(BF16))

    ones_rows = jnp.where(_iota((VT_ROWS - HEAD_DIM, tm), 0) == 0, 1.0, 0.0)

    def values_t(yt):
        return jnp.concatenate([yt, ones_rows], axis=0).astype(BF16)

    queries_t(qt_ref, 0, NSA_HEADS, LOG2E)
    y = seg(256, 512)
    vs_ref[0] = values_t(y[:, 0:128].T[64:128])
    vw_ref[0] = values_t(y[:, 128:256].T[64:128])
    y = y.astype(BF16)
    blk_in_tile = ((pl.program_id(1) * tm + _iota((tm, HEAD_DIM), 0)) & (SEL_TILE - 1)) // SEL_BLK
    onehot = jnp.where(_iota((tm, HEAD_DIM), 1) == blk_in_tile, 1.0, 0.0).astype(BF16)
    ks_ref[0] = jnp.concatenate([y[:, 0:64], onehot], axis=1)
    kw_ref[0] = y[:, 128:192]
    queries_t(sqt_ref, 512, SWA_HEADS)
    heads(sk_ref, 1024, SWA_KV_HEADS)
    yt = seg(1152, 1280).T
    for i in range(SWA_KV_HEADS):
        sv_ref[0, i] = values_t(yt[i * HEAD_DIM:(i + 1) * HEAD_DIM])
    queries_t(bq_ref, 1280, SB_HEADS, LOG2E)
    bk_ref[0] = seg(1536, 1792).astype(BF16)
    bv_ref[0] = seg(1792, 2048).T.astype(BF16)
    y = seg(2048, 2304)
    kc_ref[0] = y[:, 0:64]
    vc_ref[0] = y[:, 64:128]
    gt_ref[0] = y[:, 128:256]


def _proj(x, g, w, tm):
    b, s, d = x.shape
    nw = w.shape[1]
    hd = lambda n: pl.BlockSpec((1, n, tm, HEAD_DIM), lambda bi, i: (bi, 0, i, 0))
    flat = lambda n: pl.BlockSpec((1, tm, n), lambda bi, i: (bi, i, 0))
    hshape = lambda n: jax.ShapeDtypeStruct((b, n, s, HEAD_DIM), BF16)
    fshape = lambda n, dt: jax.ShapeDtypeStruct((b, s, n), dt)
    qt_spec = lambda n: pl.BlockSpec((1, tm // BLK, HEAD_DIM, n * BLK), lambda bi, i: (bi, i, 0, 0))
    qt_shape = lambda n: jax.ShapeDtypeStruct((b, s // BLK, HEAD_DIM, n * BLK), BF16)
    vt_spec = pl.BlockSpec((1, VT_ROWS, tm), lambda bi, i: (bi, 0, i))
    vt_shape = jax.ShapeDtypeStruct((b, VT_ROWS, s), BF16)
    return pl.pallas_call(
        _proj_kernel,
        grid=(b, s // tm),
        in_specs=[pl.BlockSpec((1, tm, d), lambda bi, i: (bi, i, 0)),
                  pl.BlockSpec((1, d), lambda bi, i: (0, 0)),
                  pl.BlockSpec((d, nw), lambda bi, i: (0, 0))],
        out_specs=[qt_spec(NSA_HEADS), flat(128), vt_spec, flat(64), vt_spec,
                   qt_spec(SWA_HEADS), hd(SWA_KV_HEADS),
                   pl.BlockSpec((1, SWA_KV_HEADS, VT_ROWS, tm), lambda bi, i: (bi, 0, 0, i)),
                   qt_spec(SB_HEADS), flat(256),
                   pl.BlockSpec((1, SB_HEADS * HEAD_DIM, tm), lambda bi, i: (bi, 0, i)),
                   flat(64), flat(64), flat(128)],
        out_shape=[qt_shape(NSA_HEADS), fshape(128, BF16), vt_shape, fshape(64, BF16), vt_shape,
                   qt_shape(SWA_HEADS), hshape(SWA_KV_HEADS),
                   jax.ShapeDtypeStruct((b, SWA_KV_HEADS, VT_ROWS, s), BF16),
                   qt_shape(SB_HEADS), fshape(256, BF16),
                   jax.ShapeDtypeStruct((b, SB_HEADS * HEAD_DIM, s), BF16),
                   fshape(64, F32), fshape(64, F32), fshape(128, F32)],
        compiler_params=_cparams(("parallel", "parallel")),
        name="proj",
    )(x, g, w)


def _compress_kernel(c_ref, pos_ref, w1_ref, b1_ref, w2_ref, o_ref, ot_ref):
    c = c_ref[0, 0]
    nch = c.shape[0]
    xa = (c + pos_ref[0, 0:1]).astype(BF16)
    xb = (c + pos_ref[0, 1:2]).astype(BF16)
    p = jnp.dot(xa, w1_ref[0, 0], preferred_element_type=F32)
    q = jnp.dot(xb, w1_ref[0, 1], preferred_element_type=F32)
    hid = p + pltpu.roll(q, nch - 1, 0) + b1_ref[0]
    hid = hid * jax.nn.sigmoid(hid)
    out = jnp.dot(hid.astype(BF16), w2_ref[0], preferred_element_type=F32)
    row = _iota(out.shape, 0)
    out = jnp.where(row < nch - 1, out, 0.0)
    o_ref[0, 0] = out.astype(BF16)
    ot_ref[0, 0] = jnp.concatenate([out, jnp.zeros_like(out)], axis=1).T[:HEAD_DIM].astype(BF16)


def _compress(ckv, pos, w1, b1, w2):
    _, b, nch, cw = ckv.shape
    return pl.pallas_call(
        _compress_kernel,
        grid=(2, b),
        in_specs=[pl.BlockSpec((1, 1, nch, cw), lambda j, bi: (j, bi, 0, 0)),
                  pl.BlockSpec((1, 2, cw), lambda j, bi: (j, 0, 0)),
                  pl.BlockSpec((1, 2, cw, CMP_HIDDEN), lambda j, bi: (j, 0, 0, 0)),
                  pl.BlockSpec((1, 1, CMP_HIDDEN), lambda j, bi: (j, 0, 0)),
                  pl.BlockSpec((1, CMP_HIDDEN, HEAD_DIM), lambda j, bi: (j, 0, 0))],
        out_specs=[pl.BlockSpec((1, 1, nch, HEAD_DIM), lambda j, bi: (j, bi, 0, 0)),
                   pl.BlockSpec((1, 1, HEAD_DIM, nch), lambda j, bi: (j, bi, 0, 0))],
        out_shape=[jax.ShapeDtypeStruct((2, b, nch, HEAD_DIM), BF16),
                   jax.ShapeDtypeStruct((2, b, HEAD_DIM, nch), BF16)],
        compiler_params=_cparams(("parallel", "parallel")),
        name="compress",
    )(ckv, pos, w1, b1, w2)


def _merge_heads(o, n):
    return jnp.concatenate([o[i * BLK:(i + 1) * BLK] for i in range(n)], axis=1)


def _transpose_pad(x, axis):
    pad = jnp.zeros(x.shape, x.dtype)
    return jnp.concatenate([x, pad], axis=axis).T


def _cmp_kernel(qt_ref, kc_ref, vct_ref, hct_ref, wmapt_ref, o_ref, pen_ref, s_scr, oacc_scr, iacc_scr):
    step = pl.program_id(1)
    ncp = kc_ref.shape[2]
    ns = wmapt_ref.shape[0] - IMP_PAD
    lanes = NSA_HEADS * BLK
    row_c = _iota((CMP_CHUNK, lanes), 0)
    lane_q = _iota((CMP_CHUNK, lanes), 1) & (BLK - 1)
    n_chunks = (CMP_QB * step) // (CMP_CHUNK // 8) + 1

    def score_chunk(ch, mxs):
        c0 = pl.multiple_of(ch * CMP_CHUNK, CMP_CHUNK)
        kc = kc_ref[0, 0, pl.ds(c0, CMP_CHUNK), :]
        out = []
        for qb in range(CMP_QB):
            n = CMP_QB * step + qb
            s = jnp.dot(kc, qt_ref[0, qb], preferred_element_type=F32)
            bias = hct_ref[pl.ds(pl.multiple_of(ncp - 8 - 8 * n + c0, 8), CMP_CHUNK), :]
            visible = (BLK * n + lane_q - CMP_STRIDE * (c0 + row_c) - (CMP_LEN - 1)) >= 0
            s = jnp.where(visible, s + bias, NEG)
            s_scr[qb, ch] = s
            out.append(jnp.maximum(mxs[qb], jnp.max(s, axis=0, keepdims=True)))
        return tuple(out)

    mxs = lax.fori_loop(0, n_chunks, score_chunk, tuple(jnp.full((1, lanes), NEG, F32) for _ in range(CMP_QB)))
    ms = [jnp.where(mx <= 0.5 * NEG, 0.0, mx) for mx in mxs]
    oacc_scr[...] = jnp.zeros(oacc_scr.shape, F32)
    iacc_scr[...] = jnp.zeros(iacc_scr.shape, F32)

    def prob_chunk(ch, ls):
        c0 = pl.multiple_of(ch * CMP_CHUNK, CMP_CHUNK)
        vct = vct_ref[0, 0, :, pl.ds(c0, CMP_CHUNK)]
        r0 = pl.multiple_of(ch * (CMP_CHUNK // 4), 16)
        wmt = wmapt_ref[pl.ds(r0, IMP_ROWS), pl.ds(c0, CMP_CHUNK)]
        out = []
        for qb in range(CMP_QB):
            p = jnp.exp2(s_scr[qb, ch] - ms[qb])
            pb = p.astype(BF16)
            oacc_scr[qb] += jnp.dot(vct, pb, preferred_element_type=F32)
            iacc_scr[qb, pl.ds(r0, IMP_ROWS), :] += jnp.dot(wmt, pb, preferred_element_type=F32)
            out.append(ls[qb] + jnp.sum(p, axis=0, keepdims=True))
        return tuple(out)

    ls = lax.fori_loop(0, n_chunks, prob_chunk, tuple(jnp.zeros((1, lanes), F32) for _ in range(CMP_QB)))
    imps = []
    for qb in range(CMP_QB):
        inv = 1.0 / jnp.maximum(ls[qb], 1e-30)
        o_t = oacc_scr[qb] * inv
        o_ref[0, qb * BLK:(qb + 1) * BLK, :] = jnp.concatenate(
            [_transpose_pad(o_t[:, h * BLK:(h + 1) * BLK], 0)[:, :HEAD_DIM] for h in range(NSA_HEADS)], axis=1)
        w = iacc_scr[qb, 0:ns, :] * inv
        imps.append(functools.reduce(jnp.add, [w[:, h * BLK:(h + 1) * BLK] for h in range(NSA_HEADS)]))

    imp = jnp.concatenate(imps, axis=1)
    shape = (ns, CMP_QB * BLK)
    j = _iota(shape, 0)
    cur = (CMP_QB * BLK * step + _iota(shape, 1)) // SEL_BLK
    valid = j <= cur
    forced = valid & ((j == 0) | (j > cur - N_LOCAL))
    score = jnp.where(forced, PICKED, jnp.where(valid, imp, NEG))
    jf = j.astype(F32)
    for _ in range(min(N_SEL, ns) - (N_LOCAL + 1)):
        best = jnp.max(score, axis=0, keepdims=True)
        first = jnp.min(jnp.where(score == best, jf, float(ns)), axis=0, keepdims=True)
        score = jnp.where(jf == first, PICKED, score)
    pen_ref[0, 0] = jnp.where(valid & (score == PICKED), 0.0, NEG)


def _cmp_attn(qt, kcv, kcv_t, hct, wmapt):
    b, nb, _, lanes = qt.shape
    ncp = kcv.shape[2]
    ns = wmapt.shape[0] - IMP_PAD
    s = nb * BLK
    return pl.pallas_call(
        _cmp_kernel,
        grid=(b, nb // CMP_QB),
        in_specs=[pl.BlockSpec((1, CMP_QB, HEAD_DIM, lanes), lambda bi, i: (bi, i, 0, 0)),
                  pl.BlockSpec((1, 1, ncp, HEAD_DIM), lambda bi, i: (0, bi, 0, 0)),
                  pl.BlockSpec((1, 1, HEAD_DIM, ncp), lambda bi, i: (1, bi, 0, 0)),
                  pl.BlockSpec((2 * ncp, lanes), lambda bi, i: (0, 0)),
                  pl.BlockSpec((ns + IMP_PAD, ncp), lambda bi, i: (0, 0))],
        out_specs=[pl.BlockSpec((1, CMP_QB * BLK, NSA_HEADS * HEAD_DIM), lambda bi, i: (bi, i, 0)),
                   pl.BlockSpec((1, 1, ns, CMP_QB * BLK), lambda bi, i: (bi, i, 0, 0))],
        out_shape=[jax.ShapeDtypeStruct((b, s, NSA_HEADS * HEAD_DIM), F32),
                   jax.ShapeDtypeStruct((b, nb // CMP_QB, ns, CMP_QB * BLK), F32)],
        scratch_shapes=[pltpu.VMEM((CMP_QB, ncp // CMP_CHUNK, CMP_CHUNK, lanes), F32),
                        pltpu.VMEM((CMP_QB, HEAD_DIM, lanes), F32),
                        pltpu.VMEM((CMP_QB, ns + IMP_PAD, lanes), F32)],
        compiler_params=_cparams(("parallel", "arbitrary")),
        name="cmp_attn",
    )(qt, kcv, kcv_t, hct, wmapt)


def _sel_kernel(qt_ref, k_ref, vt_ref, pen_ref, band_ref, o_ref, s_scr, p_scr, m_scr, alpha_scr, acc_scr):
    n = SEL_QB * pl.program_id(1)
    qlanes = NSA_HEADS * BLK
    lanes = SEL_QB * qlanes
    sub = SEL_TILE // BLK
    per = BLK // SEL_BLK
    qt = jnp.concatenate([qt_ref[0, qb] for qb in range(SEL_QB)], axis=1)
    lane = _iota((BLK, lanes), 1)
    t = BLK * (n + lane // qlanes) + (lane & (BLK - 1))

    def scores_into(kt, s_buf, far):
        rows = sub * per
        pen = pen_ref[0, 0, pl.ds(pl.multiple_of(kt * rows, rows), rows), :]
        pen = jnp.concatenate([pen[:, qb * BLK:(qb + 1) * BLK] for qb in range(SEL_QB) for _ in range(NSA_HEADS)],
                              axis=1)
        tail = jnp.concatenate([pen, jnp.zeros((HEAD_DIM - rows, lanes), F32)], axis=0).astype(BF16)
        q_aug = jnp.concatenate([qt, tail], axis=0)
        k0 = pl.multiple_of(kt * SEL_TILE, SEL_TILE)
        s = jnp.dot(k_ref[0, pl.ds(k0, SEL_TILE), :], q_aug, preferred_element_type=F32)
        if far:
            s_buf[...] = s
        else:
            for u in range(sub):
                for qb in range(SEL_QB):
                    cols = slice(qb * qlanes, (qb + 1) * qlanes)
                    s_buf[u * BLK:(u + 1) * BLK, cols] = (
                        s[u * BLK:(u + 1) * BLK, cols] + band_ref[jnp.clip(n + qb - sub * kt - u, 0, N_BAND - 1)])

    def add_values(kt, p_buf):
        k0 = pl.multiple_of(kt * SEL_TILE, SEL_TILE)
        pv = jnp.dot(vt_ref[0, :, pl.ds(k0, SEL_TILE)], p_buf[...], preferred_element_type=F32)
        acc_scr[...] = alpha_scr[...] * acc_scr[...] + pv

    def softmax_into(kt, s_buf, p_buf, causal):
        def piece(u):
            s = s_buf[u * BLK:(u + 1) * BLK, :]
            if causal:
                s = jnp.where(kt * SEL_TILE + u * BLK + _iota((BLK, lanes), 0) <= t, s, NEG)
            return s
        m_i = m_scr[...]
        mx = functools.reduce(jnp.maximum, [jnp.max(piece(u), axis=0, keepdims=True) for u in range(sub)])
        m_new = jnp.maximum(m_i, mx)
        for u in range(sub):
            p_buf[u * BLK:(u + 1) * BLK, :] = jnp.exp2(piece(u) - m_new).astype(BF16)
        m_scr[...] = m_new
        alpha_scr[...] = jnp.exp2(m_i - m_new)

    def stage(i, s_cur, s_nxt, p_cur, p_prev, far):
        add_values(jnp.maximum(i - 1, 0), p_prev)
        scores_into(i + 1, s_nxt, far)
        softmax_into(i, s_cur, p_cur, False)

    def finish(i, s_cur, p_cur, p_prev):
        add_values(jnp.maximum(i - 1, 0), p_prev)
        softmax_into(i, s_cur, p_cur, True)
        add_values(i, p_cur)
        acc = acc_scr[...]
        o_t = acc[:HEAD_DIM] / acc[HEAD_DIM:HEAD_DIM + 1]
        for qb in range(SEL_QB):
            o_ref[0, qb * BLK:(qb + 1) * BLK, :] = jnp.concatenate(
                [_transpose_pad(o_t[:, qb * qlanes + h * BLK:qb * qlanes + (h + 1) * BLK], 0)[:, :HEAD_DIM]
                 for h in range(NSA_HEADS)], axis=1)

    s0, s1, p0, p1 = s_scr.at[0], s_scr.at[1], p_scr.at[0], p_scr.at[1]
    m_scr[...] = jnp.full((1, lanes), NEG, F32)
    alpha_scr[...] = jnp.ones((1, lanes), F32)
    acc_scr[...] = jnp.zeros(acc_scr.shape, F32)
    p1[...] = jnp.zeros((SEL_TILE, lanes), BF16)
    scores_into(0, s0, False)
    last = (BLK * n + BLK - 1) // SEL_TILE
    n_far = jnp.maximum((n - (N_BAND - 2)) // sub, 0)
    far_pairs = jnp.minimum(jnp.maximum((n_far - 1) // 2, 0), last // 2)

    def pair(j, _, far):
        stage(2 * j, s0, s1, p0, p1, far)
        stage(2 * j + 1, s1, s0, p1, p0, far)
        return 0

    lax.fori_loop(0, far_pairs, functools.partial(pair, far=True), 0)
    lax.fori_loop(far_pairs, last // 2, functools.partial(pair, far=False), 0)

    @pl.when(last % 2 == 1)
    def _():
        stage(last - 1, s0, s1, p0, p1, False)
        finish(last, s1, p1, p0)

    @pl.when(last % 2 == 0)
    def _():
        finish(last, s0, p0, p1)


def _sel_attn(qt, ks, vst, pen, band_t):
    b, nb, _, qlanes = qt.shape
    s = nb * BLK
    ns = pen.shape[2]
    lanes = SEL_QB * qlanes
    per_cmp = CMP_QB // SEL_QB
    return pl.pallas_call(
        _sel_kernel,
        grid=(b, nb // SEL_QB),
        in_specs=[pl.BlockSpec((1, SEL_QB, HEAD_DIM, qlanes), lambda bi, i: (bi, i, 0, 0)),
                  pl.BlockSpec((1, s, 2 * HEAD_DIM), lambda bi, i: (bi, 0, 0)),
                  pl.BlockSpec((1, VT_ROWS, s), lambda bi, i: (bi, 0, 0)),
                  pl.BlockSpec((1, 1, ns, SEL_QB * BLK), lambda bi, i: (bi, i // per_cmp, 0, i % per_cmp)),
                  pl.BlockSpec((N_BAND, BLK, qlanes), lambda bi, i: (0, 0, 0))],
        out_specs=pl.BlockSpec((1, SEL_QB * BLK, NSA_HEADS * HEAD_DIM), lambda bi, i: (bi, i, 0)),
        out_shape=jax.ShapeDtypeStruct((b, s, NSA_HEADS * HEAD_DIM), F32),
        scratch_shapes=[pltpu.VMEM((2, SEL_TILE, lanes), F32),
                        pltpu.VMEM((2, SEL_TILE, lanes), BF16),
                        pltpu.VMEM((1, lanes), F32),
                        pltpu.VMEM((1, lanes), F32),
                        pltpu.VMEM((VT_ROWS, lanes), F32)],
        compiler_params=_cparams(("parallel", "arbitrary")),
        name="sel_attn",
    )(qt, ks, vst, pen, band_t)


def _band_kernel(*refs, window, groups, use_sinks, base2, n_parts):
    qt_ref = refs[0]
    k_refs = refs[1:1 + n_parts]
    vt_refs = refs[1 + n_parts:1 + 2 * n_parts]
    band_ref, sink_ref, o_ref = refs[1 + 2 * n_parts:]
    step = pl.program_id(1)
    n_prev = window // BLK
    nk = (n_prev + 1) * BLK
    lanes = 4 * BLK
    row = _iota((nk, lanes), 0)
    dist = n_prev * BLK + (_iota((nk, lanes), 1) & (BLK - 1)) - row
    in_window = (dist >= 0) & (dist < window)
    for g in range(groups):
        kwin = jnp.concatenate([r[0, g] for r in k_refs], axis=0)
        vtwin = jnp.concatenate([r[0, g] for r in vt_refs], axis=1)
        bias = jnp.concatenate([band_ref[n_prev - u, :, g * lanes:(g + 1) * lanes] for u in range(n_prev + 1)],
                               axis=0)
        for qb in range(BAND_QB):
            n = BAND_QB * step + qb
            qt = qt_ref[0, qb, :, g * lanes:(g + 1) * lanes]
            s = jnp.dot(kwin[qb * BLK:qb * BLK + nk], qt, preferred_element_type=F32)
            s = jnp.where(in_window & (row >= (n_prev - n) * BLK), s + bias, NEG)
            mx = jnp.max(s, axis=0, keepdims=True)
            if use_sinks:
                sink = sink_ref[g]
                mx = jnp.maximum(mx, sink)
            p = (jnp.exp2(s - mx) if base2 else jnp.exp(s - mx)).astype(BF16)
            acc = jnp.dot(vtwin[:, qb * BLK:qb * BLK + nk], p, preferred_element_type=F32)
            den = acc[HEAD_DIM:HEAD_DIM + 1]
            if use_sinks:
                den = den + jnp.exp(sink - mx)
            o_t = acc[:HEAD_DIM] / den
            o_ref[0, qb * BLK:(qb + 1) * BLK, 4 * g * HEAD_DIM:4 * (g + 1) * HEAD_DIM] = jnp.concatenate(
                [_transpose_pad(o_t[:, h * BLK:(h + 1) * BLK], 0)[:, :HEAD_DIM] for h in range(4)], axis=1)


def _band_attn(qt, k, vt, band_t, sinks, window, use_sinks, base2, name):
    b, nb, _, width = qt.shape
    g = width // (4 * BLK)
    s = nb * BLK
    n_prev = window // BLK
    tq = BAND_QB * BLK
    if n_prev * BLK % tq == 0:
        piece = tq
        starts = [lambda i, d=d: jnp.maximum(i - d, 0) for d in range(n_prev * BLK // tq, 0, -1)]
    else:
        assert n_prev == 1
        piece = BLK
        starts = [lambda i: jnp.maximum(BAND_QB * i - 1, 0)]
    k_specs = [pl.BlockSpec((1, g, piece, HEAD_DIM), lambda bi, i, f=f: (bi, 0, f(i), 0)) for f in starts]
    k_specs.append(pl.BlockSpec((1, g, tq, HEAD_DIM), lambda bi, i: (bi, 0, i, 0)))
    vt_specs = [pl.BlockSpec((1, g, VT_ROWS, piece), lambda bi, i, f=f: (bi, 0, 0, f(i))) for f in starts]
    vt_specs.append(pl.BlockSpec((1, g, VT_ROWS, tq), lambda bi, i: (bi, 0, 0, i)))
    n_parts = len(k_specs)
    return pl.pallas_call(
        functools.partial(_band_kernel, window=window, groups=g, use_sinks=use_sinks, base2=base2,
                          n_parts=n_parts),
        grid=(b, s // tq),
        in_specs=[pl.BlockSpec((1, BAND_QB, HEAD_DIM, width), lambda bi, i: (bi, i, 0, 0))] + k_specs + vt_specs
                 + [pl.BlockSpec((n_prev + 1, BLK, width), lambda bi, i: (0, 0, 0)),
                    pl.BlockSpec((g, 1, 4 * BLK), lambda bi, i: (0, 0, 0))],
        out_specs=pl.BlockSpec((1, tq, g * 4 * HEAD_DIM), lambda bi, i: (bi, i, 0)),
        out_shape=jax.ShapeDtypeStruct((b, s, g * 4 * HEAD_DIM), F32),
        compiler_params=_cparams(("parallel", "arbitrary")),
        name=name,
    )(qt, *([k] * n_parts), *([vt] * n_parts), band_t, sinks)


def _sb_kernel(qt_ref, k_ref, vt_ref, later_ref, o_ref):
    n = pl.program_id(1)
    lanes = SB_HEADS * BLK
    gk = SB_GROUP * BLK
    t = BLK * n + (_iota((BLK, lanes), 1) & (BLK - 1))
    row = _iota((BLK, lanes), 0)
    later = later_ref[...]
    zero = jnp.zeros((HEAD_DIM, BLK), BF16)
    q_bd = jnp.concatenate(
        [jnp.concatenate([zero] * h + [qt_ref[0, 0, :, h * BLK:(h + 1) * BLK]] + [zero] * (SB_HEADS - 1 - h), axis=0)
         for h in range(SB_HEADS)], axis=1)

    def cond(c):
        top, carry, _ = c
        return (top >= 0) & (jnp.max(carry) > SB_LOG2_FLOOR)

    def body(c):
        top, carry, acc = c
        ks, vts, befores = [], [], []
        for j in range(SB_GROUP - 1, -1, -1):
            kb = top - j
            k0 = pl.multiple_of(jnp.maximum(kb, 0) * BLK, BLK)
            ks.append(k_ref[0, pl.ds(k0, BLK), :])
            vts.append(vt_ref[0, :, pl.ds(k0, BLK)])
            befores.append((kb >= 0) & (k0 + row < t))
        before = jnp.concatenate(befores, axis=0)
        z = jnp.dot(jnp.concatenate(ks, axis=0), q_bd, preferred_element_type=F32)
        softplus = jnp.maximum(z, 0.0) + jnp.log2(1.0 + jnp.exp2(-jnp.abs(z)))
        log_not = jnp.where(before, -softplus, 0.0)
        hi = log_not.astype(BF16)
        lo = (log_not - hi.astype(F32)).astype(BF16)
        tail = jnp.dot(later, jnp.concatenate([hi, lo], axis=0), preferred_element_type=F32)
        a = jnp.where(before, jnp.exp2(z + log_not + tail + carry), 0.0).astype(BF16)
        pv = jnp.dot(jnp.concatenate(vts, axis=1), a, preferred_element_type=F32)
        acc = tuple(acc[h] + pv[h * HEAD_DIM:(h + 1) * HEAD_DIM, h * BLK:(h + 1) * BLK] for h in range(SB_HEADS))
        return top - SB_GROUP, carry + jnp.sum(log_not, axis=0, keepdims=True), acc

    init = (n, jnp.zeros((1, lanes), F32), tuple(jnp.zeros((HEAD_DIM, BLK), F32) for _ in range(SB_HEADS)))
    _, _, acc = lax.while_loop(cond, body, init)
    o_ref[0] = jnp.concatenate([_transpose_pad(a, 0)[:, :HEAD_DIM] for a in acc], axis=1)


def _sb_attn(qt, k, vt):
    b, nb, _, lanes = qt.shape
    s = nb * BLK
    width = SB_HEADS * HEAD_DIM
    gk = SB_GROUP * BLK
    later = (jnp.arange(2 * gk)[None, :] % gk > jnp.arange(gk)[:, None]).astype(BF16)
    return pl.pallas_call(
        _sb_kernel,
        grid=(b, nb),
        in_specs=[pl.BlockSpec((1, 1, HEAD_DIM, lanes), lambda bi, n: (bi, n, 0, 0)),
                  pl.BlockSpec((1, s, width), lambda bi, n: (bi, 0, 0)),
                  pl.BlockSpec((1, width, s), lambda bi, n: (bi, 0, 0)),
                  pl.BlockSpec((gk, 2 * gk), lambda bi, n: (0, 0))],
        out_specs=pl.BlockSpec((1, BLK, width), lambda bi, n: (bi, n, 0)),
        out_shape=jax.ShapeDtypeStruct((b, s, width), F32),
        compiler_params=_cparams(("parallel", "arbitrary")),
        name="sb_attn",
    )(qt, k, vt, later)


def _out_kernel(x_ref, oc_ref, os_ref, ow_ref, gt_ref, swa_ref, sb_ref, gg_ref, w_ref, gp_ref, o_ref):
    wa = NSA_HEADS * HEAD_DIM
    wb = wa + SWA_HEADS * HEAD_DIM
    gg = gg_ref[...]
    lane = _iota((OUT_ROWS, wa), 1)
    for r0 in range(0, x_ref.shape[1], OUT_ROWS):
        rows = slice(r0, r0 + OUT_ROWS)
        gates = jax.nn.sigmoid(gt_ref[0, rows])

        def spread(branch, gates=gates):
            cols = [jnp.broadcast_to(gates[:, branch * NSA_HEADS + i:branch * NSA_HEADS + i + 1], (OUT_ROWS, wa))
                    for i in range(NSA_HEADS)]
            out = cols[NSA_HEADS - 1]
            for i in range(NSA_HEADS - 2, -1, -1):
                out = jnp.where(lane < (i + 1) * HEAD_DIM, cols[i], out)
            return out

        o_nsa = spread(0) * oc_ref[0, rows] + spread(1) * os_ref[0, rows] + spread(2) * ow_ref[0, rows]
        mix = jnp.concatenate([_rms(o_nsa, gg[:, :wa]), _rms(swa_ref[0, rows], gg[:, wa:wb]),
                               _rms(sb_ref[0, rows], gg[:, wb:])], axis=1).astype(BF16)
        y = jnp.dot(mix, w_ref[...], preferred_element_type=F32)
        o_ref[0, rows] = x_ref[0, rows] + _rms(y, gp_ref[...])


def _out_proj(x, o_cmp, o_sel, o_win, gates, o_swa, o_sb, gg, w, gp, tm):
    b, s, d = x.shape
    row = lambda n: pl.BlockSpec((1, tm, n), lambda bi, i: (bi, i, 0))
    const = lambda a, c: pl.BlockSpec((a, c), lambda bi, i: (0, 0))
    return pl.pallas_call(
        _out_kernel,
        grid=(b, s // tm),
        in_specs=[row(d), row(256), row(256), row(256), row(128), row(512), row(256),
                  const(1, d), const(d, d), const(1, d)],
        out_specs=row(d),
        out_shape=jax.ShapeDtypeStruct((b, s, d), F32),
        compiler_params=_cparams(("parallel", "parallel")),
        name="out_proj",
    )(x, o_cmp, o_sel, o_win, gates, o_swa, o_sb, gg, w, gp)


def _ffn_kernel(x_ref, g1_ref, wg_ref, wu_ref, wd_ref, g2_ref, o_ref, *, chunk):
    x = x_ref[...]
    h = _rms(x, g1_ref[...]).astype(BF16)
    dff = wg_ref.shape[1]
    f = jnp.zeros(x.shape, F32)
    for a0 in range(0, dff, chunk):
        a1 = min(a0 + chunk, dff)
        gate = jnp.dot(h, wg_ref[:, a0:a1], preferred_element_type=F32)
        up = jnp.dot(h, wu_ref[:, a0:a1], preferred_element_type=F32)
        a = (gate * jax.nn.sigmoid(gate) * up).astype(BF16)
        f = f + jnp.dot(a, wd_ref[a0:a1, :], preferred_element_type=F32)
    o_ref[...] = x + _rms(f, g2_ref[...])


def _ffn(x, g1, wg, wu, wd, g2, tm):
    t, d = x.shape
    dff = wg.shape[1]
    row = pl.BlockSpec((tm, d), lambda i: (i, 0))
    const = lambda a, c: pl.BlockSpec((a, c), lambda i: (0, 0))
    return pl.pallas_call(
        functools.partial(_ffn_kernel, chunk=FFN_CHUNK),
        grid=(t // tm,),
        in_specs=[row, const(1, d), const(d, dff), const(d, dff), const(dff, d), const(1, d)],
        out_specs=row,
        out_shape=jax.ShapeDtypeStruct((t, d), F32),
        compiler_params=_cparams(("parallel",)),
        name="ffn",
    )(x, g1, wg, wu, wd, g2)


def _permute_w_in(w):
    scale = 1.0 / math.sqrt(HEAD_DIM)
    nq, kc, vc, rest, ng = w[:, :256], w[:, 256:320], w[:, 320:384], w[:, 384:640], w[:, 640:652]
    sq, skv = w[:, 652:1164], w[:, 1164:1420]
    bq, bkv = w[:, 1420:1676], w[:, 1676:2188]
    pad = jnp.zeros((w.shape[0], 128 - ng.shape[1]), w.dtype)
    return jnp.concatenate([nq * scale, rest, sq * scale, skv, bq * scale, bkv, kc, vc, ng, pad], axis=1)


def _cmp_to_sel(nc_pad, nc, ns):
    c0 = jnp.arange(nc_pad)[:, None] * CMP_STRIDE
    s0 = jnp.arange(ns)[None, :] * SEL_BLK
    ov = jnp.minimum(c0 + CMP_LEN, s0 + SEL_BLK) - jnp.maximum(c0, s0)
    w = jnp.clip(ov, 0, None).astype(F32) / CMP_LEN
    return jnp.where(jnp.arange(nc_pad)[:, None] < nc, w, 0.0).astype(BF16)


def kernel(x, rel_bias, ln_attn_pre, w_in, nsa_cmp_pos, nsa_phi_w1, nsa_phi_b1, nsa_phi_w2, swa_sinks,
           grp_norm_g, w_out, ln_attn_post, ln_ffn_pre, ffn_w_gate, ffn_w_up, ffn_w_down, ln_ffn_post):
    b, s, d = x.shape
    depth = w_in.shape[0]
    nch = s // CMP_STRIDE
    ns = s // SEL_BLK
    t = b * s

    band_nsa_t = _band_table_t(rel_bias, N_BAND, NSA_HEADS, 0, True)
    band_swa_t = _band_table_t(rel_bias, SWA_WINDOW // BLK + 1, SWA_HEADS, NSA_HEADS, False)
    hct = _cmp_table(rel_bias, nch)
    wmapt = jnp.pad(_cmp_to_sel(nch, nch - 1, ns).T, ((0, IMP_PAD), (0, 0)))
    no_sinks = jnp.zeros((1, 1, 4 * BLK), F32)

    for l in range(depth):
        outs = _proj(x, ln_attn_pre[l][None], _permute_w_in(w_in[l]).astype(BF16), 512)
        qt, ks, vs, kw, vw, sqt, sk, sv, bq, bk, bv, kc, vc, gates = outs
        ckv = jnp.stack([kc.reshape(b, nch, CMP_STRIDE * HEAD_DIM), vc.reshape(b, nch, CMP_STRIDE * HEAD_DIM)])
        kcv, kcv_t = _compress(ckv, nsa_cmp_pos[l].reshape(2, 2, CMP_STRIDE * HEAD_DIM),
                               nsa_phi_w1[l].reshape(2, 2, CMP_STRIDE * HEAD_DIM, CMP_HIDDEN).astype(BF16),
                               nsa_phi_b1[l][:, None, :], nsa_phi_w2[l].astype(BF16))
        o_cmp, selmask = _cmp_attn(qt, kcv, kcv_t, hct, wmapt)
        o_sel = _sel_attn(qt, ks, vs, selmask, band_nsa_t)
        o_win = _band_attn(qt, kw[:, None], vw[:, None], band_nsa_t, no_sinks, NSA_WINDOW, False, True,
                           "win_attn")
        sinks = jnp.broadcast_to(swa_sinks[l].reshape(SWA_KV_HEADS, 1, 4, 1),
                                 (SWA_KV_HEADS, 1, 4, BLK)).reshape(SWA_KV_HEADS, 1, 4 * BLK)
        o_swa = _band_attn(sqt, sk, sv, band_swa_t, sinks, SWA_WINDOW, True, False, "swa_attn")
        o_sb = _sb_attn(bq, bk, bv)
        x = _out_proj(x, o_cmp, o_sel, o_win, gates, o_swa, o_sb, grp_norm_g[l][None], w_out[l].astype(BF16),
                      ln_attn_post[l][None], 512)
        x = _ffn(x.reshape(t, d), ln_ffn_pre[l][None], ffn_w_gate[l].astype(BF16), ffn_w_up[l].astype(BF16),
                 ffn_w_down[l].astype(BF16), ln_ffn_post[l][None], 256).reshape(b, s, d)
    return x
```

```python
import functools
import math

import jax
import jax.numpy as jnp
from jax import lax
from jax.experimental import pallas as pl
from jax.experimental.pallas import tpu as pltpu

F32 = jnp.float32
BF16 = jnp.bfloat16

HEAD_DIM = 64
BLK = 128
NSA_HEADS = 4
CMP_LEN = 32
CMP_STRIDE = 16
CMP_HIDDEN = 256
SEL_BLK = 64
N_SEL = 16
N_LOCAL = 2
NSA_WINDOW = 512
SWA_HEADS = 8
SWA_KV_HEADS = 2
SWA_WINDOW = 128
SB_HEADS = 4
N_BUCKETS = 32
T5_MAX_DISTANCE = 4096
RMS_EPS = 1e-6
FORCE_SCORE = 1e6
NEG = -1e30
LOG2E = 1.0 / math.log(2.0)
SEL_TILE = 512
SB_GROUP = 3
SB_LOG2_FLOOR = -104.0 * LOG2E
PICKED = -3e38
CMP_QB = 4
VT_ROWS = HEAD_DIM + 16
OUT_ROWS = 128
SEL_QB = 2
FFN_CHUNK = 1024
BAND_QB = 4
CMP_CHUNK = 128
IMP_PAD = 16
IMP_ROWS = CMP_CHUNK // 4 + IMP_PAD
N_BAND = 25
VMEM_LIMIT = 56 * 1024 * 1024


def _cparams(sem):
    return pltpu.CompilerParams(dimension_semantics=sem, vmem_limit_bytes=VMEM_LIMIT)


def _rms(x, g):
    ms = jnp.mean(x * x, axis=-1, keepdims=True)
    return x * lax.rsqrt(ms + RMS_EPS) * g


def _iota(shape, axis):
    return lax.broadcasted_iota(jnp.int32, shape, axis)


def _bias_of_dist(dist, tab_ref, head):
    n = jnp.maximum(dist, 0)
    nf = jnp.maximum(n, 1).astype(F32)
    exact = N_BUCKETS // 2
    large = exact + (jnp.log(nf / exact) / math.log(T5_MAX_DISTANCE / exact)
                     * (N_BUCKETS - exact)).astype(jnp.int32)
    large = jnp.minimum(large, N_BUCKETS - 1)
    bucket = jnp.where(n < exact, n, large)
    val = jnp.full(dist.shape, tab_ref[0, head], F32)
    for k in range(1, N_BUCKETS):
        val = jnp.where(bucket == k, tab_ref[k, head], val)
    return val


def _band_table_t_kernel(tab_ref, o_ref, *, head_off, shift):
    h = pl.program_id(0) + head_off

    def tile(m, _):
        dist = BLK * m + _iota((BLK, BLK), 1) - _iota((BLK, BLK), 0)
        bias = _bias_of_dist(dist, tab_ref, h)
        o_ref[m] = (bias - tab_ref[N_BUCKETS - 1, h]) * LOG2E if shift else bias
        return 0

    lax.fori_loop(0, o_ref.shape[0], tile, 0)


def _band_table_t(rel_bias, n_band, n_heads, head_off, shift):
    return pl.pallas_call(
        functools.partial(_band_table_t_kernel, head_off=head_off, shift=shift),
        grid=(n_heads,),
        in_specs=[pl.BlockSpec(memory_space=pltpu.SMEM)],
        out_specs=pl.BlockSpec((n_band, BLK, BLK), lambda h: (0, 0, h)),
        out_shape=jax.ShapeDtypeStruct((n_band, BLK, n_heads * BLK), F32),
        compiler_params=_cparams(("arbitrary",)),
        name="band_table_t",
    )(rel_bias)


def _cmp_table_kernel(tab_ref, o_ref, *, ncp):
    h = pl.program_id(0)
    rel = _iota((2 * ncp, BLK), 0) - (ncp - 8)
    dist = _iota((2 * ncp, BLK), 1) - CMP_STRIDE * rel - (CMP_LEN - 1)
    o_ref[...] = _bias_of_dist(dist, tab_ref, h) * LOG2E


def _cmp_table(rel_bias, ncp):
    return pl.pallas_call(
        functools.partial(_cmp_table_kernel, ncp=ncp),
        grid=(NSA_HEADS,),
        in_specs=[pl.BlockSpec(memory_space=pltpu.SMEM)],
        out_specs=pl.BlockSpec((2 * ncp, BLK), lambda h: (0, h)),
        out_shape=jax.ShapeDtypeStruct((2 * ncp, NSA_HEADS * BLK), F32),
        compiler_params=_cparams(("arbitrary",)),
        name="cmp_table",
    )(rel_bias)


def _proj_kernel(x_ref, g_ref, w_ref, qt_ref, ks_ref, vs_ref, kw_ref, vw_ref, sqt_ref, sk_ref, sv_ref,
                 bq_ref, bk_ref, bv_ref, kc_ref, vc_ref, gt_ref):
    h = _rms(x_ref[0], g_ref[...]).astype(BF16)
    tm = x_ref.shape[1]

    def seg(a, b):
        return jnp.dot(h, w_ref[:, a:b], preferred_element_type=F32)

    def heads(ref, a, n):
        y = seg(a, a + n * HEAD_DIM).astype(BF16)
        for i in range(n):
            ref[0, i] = y[:, i * HEAD_DIM:(i + 1) * HEAD_DIM]

    def queries_t(ref, a, n, scale=None):
        yt = seg(a, a + n * HEAD_DIM).T
        if scale is not None:
            yt = yt * scale
        for j in range(tm // BLK):
            for i in range(n):
                ref[0, j, :, i * BLK:(i + 1) * BLK] = (
                    yt[i * HEAD_DIM:(i + 1) * HEAD_DIM, j * BLK:(j + 1) * BLK].astype(BF16))

    ones_rows = jnp.where(_iota((VT_ROWS - HEAD_DIM, tm), 0) == 0, 1.0, 0.0)

    def values_t(yt):
        return jnp.concatenate([yt, ones_rows], axis=0).astype(BF16)

    queries_t(qt_ref, 0, NSA_HEADS, LOG2E)
    y = seg(256, 512)
    vs_ref[0] = values_t(y[:, 0:128].T[64:128])
    vw_ref[0] = values_t(y[:, 128:256].T[64:128])
    y = y.astype(BF16)
    blk_in_tile = ((pl.program_id(1) * tm + _iota((tm, HEAD_DIM), 0)) & (SEL_TILE - 1)) // SEL_BLK
    onehot = jnp.where(_iota((tm, HEAD_DIM), 1) == blk_in_tile, 1.0, 0.0).astype(BF16)
    ks_ref[0] = jnp.concatenate([y[:, 0:64], onehot], axis=1)
    kw_ref[0] = y[:, 128:192]
    queries_t(sqt_ref, 512, SWA_HEADS)
    heads(sk_ref, 1024, SWA_KV_HEADS)
    yt = seg(1152, 1280).T
    for i in range(SWA_KV_HEADS):
        sv_ref[0, i] = values_t(yt[i * HEAD_DIM:(i + 1) * HEAD_DIM])
    queries_t(bq_ref, 1280, SB_HEADS, LOG2E)
    bk_ref[0] = seg(1536, 1792).astype(BF16)
    bv_ref[0] = seg(1792, 2048).T.astype(BF16)
    y = seg(2048, 2304)
    kc_ref[0] = y[:, 0:64]
    vc_ref[0] = y[:, 64:128]
    gt_ref[0] = y[:, 128:256]


def _proj(x, g, w, tm):
    b, s, d = x.shape
    nw = w.shape[1]
    hd = lambda n: pl.BlockSpec((1, n, tm, HEAD_DIM), lambda bi, i: (bi, 0, i, 0))
    flat = lambda n: pl.BlockSpec((1, tm, n), lambda bi, i: (bi, i, 0))
    hshape = lambda n: jax.ShapeDtypeStruct((b, n, s, HEAD_DIM), BF16)
    fshape = lambda n, dt: jax.ShapeDtypeStruct((b, s, n), dt)
    qt_spec = lambda n: pl.BlockSpec((1, tm // BLK, HEAD_DIM, n * BLK), lambda bi, i: (bi, i, 0, 0))
    qt_shape = lambda n: jax.ShapeDtypeStruct((b, s // BLK, HEAD_DIM, n * BLK), BF16)
    vt_spec = pl.BlockSpec((1, VT_ROWS, tm), lambda bi, i: (bi, 0, i))
    vt_shape = jax.ShapeDtypeStruct((b, VT_ROWS, s), BF16)
    return pl.pallas_call(
        _proj_kernel,
        grid=(b, s // tm),
        in_specs=[pl.BlockSpec((1, tm, d), lambda bi, i: (bi, i, 0)),
                  pl.BlockSpec((1, d), lambda bi, i: (0, 0)),
                  pl.BlockSpec((d, nw), lambda bi, i: (0, 0))],
        out_specs=[qt_spec(NSA_HEADS), flat(128), vt_spec, flat(64), vt_spec,
                   qt_spec(SWA_HEADS), hd(SWA_KV_HEADS),
                   pl.BlockSpec((1, SWA_KV_HEADS, VT_ROWS, tm), lambda bi, i: (bi, 0, 0, i)),
                   qt_spec(SB_HEADS), flat(256),
                   pl.BlockSpec((1, SB_HEADS * HEAD_DIM, tm), lambda bi, i: (bi, 0, i)),
                   flat(64), flat(64), flat(128)],
        out_shape=[qt_shape(NSA_HEADS), fshape(128, BF16), vt_shape, fshape(64, BF16), vt_shape,
                   qt_shape(SWA_HEADS), hshape(SWA_KV_HEADS),
                   jax.ShapeDtypeStruct((b, SWA_KV_HEADS, VT_ROWS, s), BF16),
                   qt_shape(SB_HEADS), fshape(256, BF16),
                   jax.ShapeDtypeStruct((b, SB_HEADS * HEAD_DIM, s), BF16),
                   fshape(64, F32), fshape(64, F32), fshape(128, F32)],
        compiler_params=_cparams(("parallel", "parallel")),
        name="proj",
    )(x, g, w)


def _compress_kernel(c_ref, pos_ref, w1_ref, b1_ref, w2_ref, o_ref, ot_ref):
    c = c_ref[0, 0]
    nch = c.shape[0]
    xa = (c + pos_ref[0, 0:1]).astype(BF16)
    xb = (c + pos_ref[0, 1:2]).astype(BF16)
    p = jnp.dot(xa, w1_ref[0, 0], preferred_element_type=F32)
    q = jnp.dot(xb, w1_ref[0, 1], preferred_element_type=F32)
    hid = p + pltpu.roll(q, nch - 1, 0) + b1_ref[0]
    hid = hid * jax.nn.sigmoid(hid)
    out = jnp.dot(hid.astype(BF16), w2_ref[0], preferred_element_type=F32)
    row = _iota(out.shape, 0)
    out = jnp.where(row < nch - 1, out, 0.0)
    o_ref[0, 0] = out.astype(BF16)
    ot_ref[0, 0] = jnp.concatenate([out, jnp.zeros_like(out)], axis=1).T[:HEAD_DIM].astype(BF16)


def _compress(ckv, pos, w1, b1, w2):
    _, b, nch, cw = ckv.shape
    return pl.pallas_call(
        _compress_kernel,
        grid=(2, b),
        in_specs=[pl.BlockSpec((1, 1, nch, cw), lambda j, bi: (j, bi, 0, 0)),
                  pl.BlockSpec((1, 2, cw), lambda j, bi: (j, 0, 0)),
                  pl.BlockSpec((1, 2, cw, CMP_HIDDEN), lambda j, bi: (j, 0, 0, 0)),
                  pl.BlockSpec((1, 1, CMP_HIDDEN), lambda j, bi: (j, 0, 0)),
                  pl.BlockSpec((1, CMP_HIDDEN, HEAD_DIM), lambda j, bi: (j, 0, 0))],
        out_specs=[pl.BlockSpec((1, 1, nch, HEAD_DIM), lambda j, bi: (j, bi, 0, 0)),
                   pl.BlockSpec((1, 1, HEAD_DIM, nch), lambda j, bi: (j, bi, 0, 0))],
        out_shape=[jax.ShapeDtypeStruct((2, b, nch, HEAD_DIM), BF16),
                   jax.ShapeDtypeStruct((2, b, HEAD_DIM, nch), BF16)],
        compiler_params=_cparams(("parallel", "parallel")),
        name="compress",
    )(ckv, pos, w1, b1, w2)


def _transpose_pad(x, axis):
    pad = jnp.zeros(x.shape, x.dtype)
    return jnp.concatenate([x, pad], axis=axis).T


def _cmp_kernel(qt_ref, kc_ref, vct_ref, hct_ref, wmapt_ref, o_ref, pen_ref, s_scr, oacc_scr, iacc_scr):
    step = pl.program_id(1)
    ncp = kc_ref.shape[2]
    ns = wmapt_ref.shape[0] - IMP_PAD
    lanes = NSA_HEADS * BLK
    row_c = _iota((CMP_CHUNK, lanes), 0)
    lane_q = _iota((CMP_CHUNK, lanes), 1) & (BLK - 1)
    n_chunks = (CMP_QB * step) // (CMP_CHUNK // 8) + 1

    def score_chunk(ch, mxs):
        c0 = pl.multiple_of(ch * CMP_CHUNK, CMP_CHUNK)
        kc = kc_ref[0, 0, pl.ds(c0, CMP_CHUNK), :]
        out = []
        for qb in range(CMP_QB):
            n = CMP_QB * step + qb
            s = jnp.dot(kc, qt_ref[0, qb], preferred_element_type=F32)
            bias = hct_ref[pl.ds(pl.multiple_of(ncp - 8 - 8 * n + c0, 8), CMP_CHUNK), :]
            visible = (BLK * n + lane_q - CMP_STRIDE * (c0 + row_c) - (CMP_LEN - 1)) >= 0
            s = jnp.where(visible, s + bias, NEG)
            s_scr[qb, ch] = s
            out.append(jnp.maximum(mxs[qb], jnp.max(s, axis=0, keepdims=True)))
        return tuple(out)

    mxs = lax.fori_loop(0, n_chunks, score_chunk, tuple(jnp.full((1, lanes), NEG, F32) for _ in range(CMP_QB)))
    ms = [jnp.where(mx <= 0.5 * NEG, 0.0, mx) for mx in mxs]
    oacc_scr[...] = jnp.zeros(oacc_scr.shape, F32)
    iacc_scr[...] = jnp.zeros(iacc_scr.shape, F32)

    def prob_chunk(ch, ls):
        c0 = pl.multiple_of(ch * CMP_CHUNK, CMP_CHUNK)
        vct = vct_ref[0, 0, :, pl.ds(c0, CMP_CHUNK)]
        r0 = pl.multiple_of(ch * (CMP_CHUNK // 4), 16)
        wmt = wmapt_ref[pl.ds(r0, IMP_ROWS), pl.ds(c0, CMP_CHUNK)]
        both = jnp.concatenate([vct, wmt], axis=0)
        out = []
        for qb in range(CMP_QB):
            p = jnp.exp2(s_scr[qb, ch] - ms[qb])
            prod = jnp.dot(both, p.astype(BF16), preferred_element_type=F32)
            oacc_scr[qb] += prod[:HEAD_DIM]
            iacc_scr[qb, pl.ds(r0, IMP_ROWS), :] += prod[HEAD_DIM:]
            out.append(ls[qb] + jnp.sum(p, axis=0, keepdims=True))
        return tuple(out)

    ls = lax.fori_loop(0, n_chunks, prob_chunk, tuple(jnp.zeros((1, lanes), F32) for _ in range(CMP_QB)))
    imps = []
    for qb in range(CMP_QB):
        inv = 1.0 / jnp.maximum(ls[qb], 1e-30)
        o_t = oacc_scr[qb] * inv
        o_ref[0, qb * BLK:(qb + 1) * BLK, :] = jnp.concatenate(
            [_transpose_pad(o_t[:, h * BLK:(h + 1) * BLK], 0)[:, :HEAD_DIM] for h in range(NSA_HEADS)], axis=1)
        w = iacc_scr[qb, 0:ns, :] * inv
        imps.append(functools.reduce(jnp.add, [w[:, h * BLK:(h + 1) * BLK] for h in range(NSA_HEADS)]))

    imp = jnp.concatenate(imps, axis=1)
    shape = (ns, CMP_QB * BLK)
    j = _iota(shape, 0)
    cur = (CMP_QB * BLK * step + _iota(shape, 1)) // SEL_BLK
    valid = j <= cur
    forced = valid & ((j == 0) | (j > cur - N_LOCAL))
    score = jnp.where(forced, PICKED, jnp.where(valid, imp, NEG))
    jf = j.astype(F32)
    for _ in range(min(N_SEL, ns) - (N_LOCAL + 1)):
        best = jnp.max(score, axis=0, keepdims=True)
        first = jnp.min(jnp.where(score == best, jf, float(ns)), axis=0, keepdims=True)
        score = jnp.where(jf == first, PICKED, score)
    pen_ref[0, 0] = jnp.where(valid & (score == PICKED), 0.0, NEG)


def _cmp_attn(qt, kcv, kcv_t, hct, wmapt):
    b, nb, _, lanes = qt.shape
    ncp = kcv.shape[2]
    ns = wmapt.shape[0] - IMP_PAD
    s = nb * BLK
    return pl.pallas_call(
        _cmp_kernel,
        grid=(b, nb // CMP_QB),
        in_specs=[pl.BlockSpec((1, CMP_QB, HEAD_DIM, lanes), lambda bi, i: (bi, i, 0, 0)),
                  pl.BlockSpec((1, 1, ncp, HEAD_DIM), lambda bi, i: (0, bi, 0, 0)),
                  pl.BlockSpec((1, 1, HEAD_DIM, ncp), lambda bi, i: (1, bi, 0, 0)),
                  pl.BlockSpec((2 * ncp, lanes), lambda bi, i: (0, 0)),
                  pl.BlockSpec((ns + IMP_PAD, ncp), lambda bi, i: (0, 0))],
        out_specs=[pl.BlockSpec((1, CMP_QB * BLK, NSA_HEADS * HEAD_DIM), lambda bi, i: (bi, i, 0)),
                   pl.BlockSpec((1, 1, ns, CMP_QB * BLK), lambda bi, i: (bi, i, 0, 0))],
        out_shape=[jax.ShapeDtypeStruct((b, s, NSA_HEADS * HEAD_DIM), F32),
                   jax.ShapeDtypeStruct((b, nb // CMP_QB, ns, CMP_QB * BLK), F32)],
        scratch_shapes=[pltpu.VMEM((CMP_QB, ncp // CMP_CHUNK, CMP_CHUNK, lanes), F32),
                        pltpu.VMEM((CMP_QB, HEAD_DIM, lanes), F32),
                        pltpu.VMEM((CMP_QB, ns + IMP_PAD, lanes), F32)],
        compiler_params=_cparams(("parallel", "arbitrary")),
        name="cmp_attn",
    )(qt, kcv, kcv_t, hct, wmapt)


def _sel_kernel(qt_ref, k_ref, vt_ref, pen_ref, band_ref, o_ref, s_scr, p_scr, mx_scr, m_scr, alpha_scr, acc_scr):
    n = SEL_QB * pl.program_id(1)
    qlanes = NSA_HEADS * BLK
    lanes = SEL_QB * qlanes
    sub = SEL_TILE // BLK
    per = BLK // SEL_BLK
    qt = jnp.concatenate([qt_ref[0, qb] for qb in range(SEL_QB)], axis=1)
    lane = _iota((BLK, lanes), 1)
    t = BLK * (n + lane // qlanes) + (lane & (BLK - 1))

    def scores_into(kt, s_buf, mx_buf, far):
        rows = sub * per
        pen = pen_ref[0, 0, pl.ds(pl.multiple_of(kt * rows, rows), rows), :]
        pen = jnp.concatenate([pen[:, qb * BLK:(qb + 1) * BLK] for qb in range(SEL_QB) for _ in range(NSA_HEADS)],
                              axis=1)
        tail = jnp.concatenate([pen, jnp.zeros((HEAD_DIM - rows, lanes), F32)], axis=0).astype(BF16)
        q_aug = jnp.concatenate([qt, tail], axis=0)
        k0 = pl.multiple_of(kt * SEL_TILE, SEL_TILE)
        s = jnp.dot(k_ref[0, pl.ds(k0, SEL_TILE), :], q_aug, preferred_element_type=F32)
        if far:
            s_buf[...] = s
            mx_buf[...] = jnp.max(s, axis=0, keepdims=True)
        else:
            mxs = []
            for qb in range(SEL_QB):
                cols = slice(qb * qlanes, (qb + 1) * qlanes)
                parts = [s[u * BLK:(u + 1) * BLK, cols] + band_ref[jnp.clip(n + qb - sub * kt - u, 0, N_BAND - 1)]
                         for u in range(sub)]
                for u in range(sub):
                    s_buf[u * BLK:(u + 1) * BLK, cols] = parts[u]
                mxs.append(functools.reduce(jnp.maximum, [jnp.max(v, axis=0, keepdims=True) for v in parts]))
            mx_buf[...] = jnp.concatenate(mxs, axis=1)

    def add_values(kt, p_buf):
        k0 = pl.multiple_of(kt * SEL_TILE, SEL_TILE)
        pv = jnp.dot(vt_ref[0, :, pl.ds(k0, SEL_TILE)], p_buf[...], preferred_element_type=F32)
        acc_scr[...] = alpha_scr[...] * acc_scr[...] + pv

    def softmax_into(kt, s_buf, mx_buf, p_buf, causal):
        def piece(u):
            s = s_buf[u * BLK:(u + 1) * BLK, :]
            if causal:
                s = jnp.where(kt * SEL_TILE + u * BLK + _iota((BLK, lanes), 0) <= t, s, NEG)
            return s
        m_i = m_scr[...]
        if causal:
            mx = functools.reduce(jnp.maximum, [jnp.max(piece(u), axis=0, keepdims=True) for u in range(sub)])
        else:
            mx = mx_buf[...]
        m_new = jnp.maximum(m_i, mx)
        for u in range(sub):
            p_buf[u * BLK:(u + 1) * BLK, :] = jnp.exp2(piece(u) - m_new).astype(BF16)
        m_scr[...] = m_new
        alpha_scr[...] = jnp.exp2(m_i - m_new)

    def stage(i, cur, nxt, p_cur, p_prev, far):
        add_values(jnp.maximum(i - 1, 0), p_prev)
        scores_into(i + 1, *nxt, far)
        softmax_into(i, *cur, p_cur, False)

    def finish(i, cur, p_cur, p_prev):
        add_values(jnp.maximum(i - 1, 0), p_prev)
        softmax_into(i, *cur, p_cur, True)
        add_values(i, p_cur)
        acc = acc_scr[...]
        o_t = acc[:HEAD_DIM] / acc[HEAD_DIM:HEAD_DIM + 1]
        for qb in range(SEL_QB):
            o_ref[0, qb * BLK:(qb + 1) * BLK, :] = jnp.concatenate(
                [_transpose_pad(o_t[:, qb * qlanes + h * BLK:qb * qlanes + (h + 1) * BLK], 0)[:, :HEAD_DIM]
                 for h in range(NSA_HEADS)], axis=1)

    s0, s1 = (s_scr.at[0], mx_scr.at[0]), (s_scr.at[1], mx_scr.at[1])
    p0, p1 = p_scr.at[0], p_scr.at[1]
    m_scr[...] = jnp.full((1, lanes), NEG, F32)
    alpha_scr[...] = jnp.ones((1, lanes), F32)
    acc_scr[...] = jnp.zeros(acc_scr.shape, F32)
    p1[...] = jnp.zeros((SEL_TILE, lanes), BF16)
    scores_into(0, *s0, False)
    last = (BLK * n + BLK - 1) // SEL_TILE
    n_far = jnp.maximum((n - (N_BAND - 2)) // sub, 0)
    far_pairs = jnp.minimum(jnp.maximum((n_far - 1) // 2, 0), last // 2)

    def pair(j, _, far):
        stage(2 * j, s0, s1, p0, p1, far)
        stage(2 * j + 1, s1, s0, p1, p0, far)
        return 0

    lax.fori_loop(0, far_pairs, functools.partial(pair, far=True), 0)
    lax.fori_loop(far_pairs, last // 2, functools.partial(pair, far=False), 0)

    @pl.when(last % 2 == 1)
    def _():
        stage(last - 1, s0, s1, p0, p1, False)
        finish(last, s1, p1, p0)

    @pl.when(last % 2 == 0)
    def _():
        finish(last, s0, p0, p1)


def _sel_attn(qt, ks, vst, pen, band_t):
    b, nb, _, qlanes = qt.shape
    s = nb * BLK
    ns = pen.shape[2]
    lanes = SEL_QB * qlanes
    per_cmp = CMP_QB // SEL_QB
    return pl.pallas_call(
        _sel_kernel,
        grid=(b, nb // SEL_QB),
        in_specs=[pl.BlockSpec((1, SEL_QB, HEAD_DIM, qlanes), lambda bi, i: (bi, i, 0, 0)),
                  pl.BlockSpec((1, s, 2 * HEAD_DIM), lambda bi, i: (bi, 0, 0)),
                  pl.BlockSpec((1, VT_ROWS, s), lambda bi, i: (bi, 0, 0)),
                  pl.BlockSpec((1, 1, ns, SEL_QB * BLK), lambda bi, i: (bi, i // per_cmp, 0, i % per_cmp)),
                  pl.BlockSpec((N_BAND, BLK, qlanes), lambda bi, i: (0, 0, 0))],
        out_specs=pl.BlockSpec((1, SEL_QB * BLK, NSA_HEADS * HEAD_DIM), lambda bi, i: (bi, i, 0)),
        out_shape=jax.ShapeDtypeStruct((b, s, NSA_HEADS * HEAD_DIM), F32),
        scratch_shapes=[pltpu.VMEM((2, SEL_TILE, lanes), F32),
                        pltpu.VMEM((2, SEL_TILE, lanes), BF16),
                        pltpu.VMEM((2, 1, lanes), F32),
                        pltpu.VMEM((1, lanes), F32),
                        pltpu.VMEM((1, lanes), F32),
                        pltpu.VMEM((VT_ROWS, lanes), F32)],
        compiler_params=_cparams(("parallel", "arbitrary")),
        name="sel_attn",
    )(qt, ks, vst, pen, band_t)


def _band_kernel(*refs, window, groups, use_sinks, base2, n_parts):
    qt_ref = refs[0]
    k_refs = refs[1:1 + n_parts]
    vt_refs = refs[1 + n_parts:1 + 2 * n_parts]
    band_ref, sink_ref, o_ref = refs[1 + 2 * n_parts:]
    step = pl.program_id(1)
    n_prev = window // BLK
    nk = (n_prev + 1) * BLK
    lanes = 4 * BLK
    row = _iota((nk, lanes), 0)
    dist = n_prev * BLK + (_iota((nk, lanes), 1) & (BLK - 1)) - row
    in_window = (dist >= 0) & (dist < window)
    for g in range(groups):
        kwin = jnp.concatenate([r[0, g] for r in k_refs], axis=0)
        vtwin = jnp.concatenate([r[0, g] for r in vt_refs], axis=1)
        bias = jnp.concatenate([band_ref[n_prev - u, :, g * lanes:(g + 1) * lanes] for u in range(n_prev + 1)],
                               axis=0)
        for qb in range(BAND_QB):
            n = BAND_QB * step + qb
            qt = qt_ref[0, qb, :, g * lanes:(g + 1) * lanes]
            s = jnp.dot(kwin[qb * BLK:qb * BLK + nk], qt, preferred_element_type=F32)
            s = jnp.where(in_window & (row >= (n_prev - n) * BLK), s + bias, NEG)
            mx = jnp.max(s, axis=0, keepdims=True)
            if use_sinks:
                sink = sink_ref[g]
                mx = jnp.maximum(mx, sink)
            p = (jnp.exp2(s - mx) if base2 else jnp.exp(s - mx)).astype(BF16)
            acc = jnp.dot(vtwin[:, qb * BLK:qb * BLK + nk], p, preferred_element_type=F32)
            den = acc[HEAD_DIM:HEAD_DIM + 1]
            if use_sinks:
                den = den + jnp.exp(sink - mx)
            o_t = acc[:HEAD_DIM] / den
            o_ref[0, qb * BLK:(qb + 1) * BLK, 4 * g * HEAD_DIM:4 * (g + 1) * HEAD_DIM] = jnp.concatenate(
                [_transpose_pad(o_t[:, h * BLK:(h + 1) * BLK], 0)[:, :HEAD_DIM] for h in range(4)], axis=1)


def _band_attn(qt, k, vt, band_t, sinks, window, use_sinks, base2, name):
    b, nb, _, width = qt.shape
    g = width // (4 * BLK)
    s = nb * BLK
    n_prev = window // BLK
    tq = BAND_QB * BLK
    if n_prev * BLK % tq == 0:
        piece = tq
        starts = [lambda i, d=d: jnp.maximum(i - d, 0) for d in range(n_prev * BLK // tq, 0, -1)]
    else:
        assert n_prev == 1
        piece = BLK
        starts = [lambda i: jnp.maximum(BAND_QB * i - 1, 0)]
    k_specs = [pl.BlockSpec((1, g, piece, HEAD_DIM), lambda bi, i, f=f: (bi, 0, f(i), 0)) for f in starts]
    k_specs.append(pl.BlockSpec((1, g, tq, HEAD_DIM), lambda bi, i: (bi, 0, i, 0)))
    vt_specs = [pl.BlockSpec((1, g, VT_ROWS, piece), lambda bi, i, f=f: (bi, 0, 0, f(i))) for f in starts]
    vt_specs.append(pl.BlockSpec((1, g, VT_ROWS, tq), lambda bi, i: (bi, 0, 0, i)))
    n_parts = len(k_specs)
    return pl.pallas_call(
        functools.partial(_band_kernel, window=window, groups=g, use_sinks=use_sinks, base2=base2,
                          n_parts=n_parts),
        grid=(b, s // tq),
        in_specs=[pl.BlockSpec((1, BAND_QB, HEAD_DIM, width), lambda bi, i: (bi, i, 0, 0))] + k_specs + vt_specs
                 + [pl.BlockSpec((n_prev + 1, BLK, width), lambda bi, i: (0, 0, 0)),
                    pl.BlockSpec((g, 1, 4 * BLK), lambda bi, i: (0, 0, 0))],
        out_specs=pl.BlockSpec((1, tq, g * 4 * HEAD_DIM), lambda bi, i: (bi, i, 0)),
        out_shape=jax.ShapeDtypeStruct((b, s, g * 4 * HEAD_DIM), F32),
        compiler_params=_cparams(("parallel", "arbitrary")),
        name=name,
    )(qt, *([k] * n_parts), *([vt] * n_parts), band_t, sinks)


def _sb_kernel(qt_ref, k_ref, vt_ref, later_ref, o_ref):
    n = pl.program_id(1)
    lanes = SB_HEADS * BLK
    t = BLK * n + (_iota((BLK, lanes), 1) & (BLK - 1))
    row = _iota((BLK, lanes), 0)
    later = later_ref[...]
    zero = jnp.zeros((HEAD_DIM, BLK), BF16)
    q_bd = jnp.concatenate(
        [jnp.concatenate([zero] * h + [qt_ref[0, 0, :, h * BLK:(h + 1) * BLK]] + [zero] * (SB_HEADS - 1 - h), axis=0)
         for h in range(SB_HEADS)], axis=1)

    def cond(c):
        top, carry, _ = c
        return (top >= 0) & (jnp.max(carry) > SB_LOG2_FLOOR)

    def body(c):
        top, carry, acc = c
        ks, vts, befores = [], [], []
        for j in range(SB_GROUP - 1, -1, -1):
            kb = top - j
            k0 = pl.multiple_of(jnp.maximum(kb, 0) * BLK, BLK)
            ks.append(k_ref[0, pl.ds(k0, BLK), :])
            vts.append(vt_ref[0, :, pl.ds(k0, BLK)])
            befores.append((kb >= 0) & (k0 + row < t))
        before = jnp.concatenate(befores, axis=0)
        z = jnp.dot(jnp.concatenate(ks, axis=0), q_bd, preferred_element_type=F32)
        softplus = jnp.maximum(z, 0.0) + jnp.log2(1.0 + jnp.exp2(-jnp.abs(z)))
        log_not = jnp.where(before, -softplus, 0.0)
        hi = log_not.astype(BF16)
        lo = (log_not - hi.astype(F32)).astype(BF16)
        tail = jnp.dot(later, jnp.concatenate([hi, lo], axis=0), preferred_element_type=F32)
        a = jnp.where(before, jnp.exp2(z + log_not + tail + carry), 0.0).astype(BF16)
        pv = jnp.dot(jnp.concatenate(vts, axis=1), a, preferred_element_type=F32)
        acc = tuple(acc[h] + pv[h * HEAD_DIM:(h + 1) * HEAD_DIM, h * BLK:(h + 1) * BLK] for h in range(SB_HEADS))
        return top - SB_GROUP, carry + jnp.sum(log_not, axis=0, keepdims=True), acc

    init = (n, jnp.zeros((1, lanes), F32), tuple(jnp.zeros((HEAD_DIM, BLK), F32) for _ in range(SB_HEADS)))
    _, _, acc = lax.while_loop(cond, body, init)
    o_ref[0] = jnp.concatenate([_transpose_pad(a, 0)[:, :HEAD_DIM] for a in acc], axis=1)


def _sb_attn(qt, k, vt):
    b, nb, _, lanes = qt.shape
    s = nb * BLK
    width = SB_HEADS * HEAD_DIM
    gk = SB_GROUP * BLK
    later = (jnp.arange(2 * gk)[None, :] % gk > jnp.arange(gk)[:, None]).astype(BF16)
    return pl.pallas_call(
        _sb_kernel,
        grid=(b, nb),
        in_specs=[pl.BlockSpec((1, 1, HEAD_DIM, lanes), lambda bi, n: (bi, n, 0, 0)),
                  pl.BlockSpec((1, s, width), lambda bi, n: (bi, 0, 0)),
                  pl.BlockSpec((1, width, s), lambda bi, n: (bi, 0, 0)),
                  pl.BlockSpec((gk, 2 * gk), lambda bi, n: (0, 0))],
        out_specs=pl.BlockSpec((1, BLK, width), lambda bi, n: (bi, n, 0)),
        out_shape=jax.ShapeDtypeStruct((b, s, width), F32),
        compiler_params=_cparams(("parallel", "arbitrary")),
        name="sb_attn",
    )(qt, k, vt, later)


def _out_kernel(x_ref, oc_ref, os_ref, ow_ref, gt_ref, swa_ref, sb_ref, gg_ref, w_ref, gp_ref, o_ref):
    wa = NSA_HEADS * HEAD_DIM
    wb = wa + SWA_HEADS * HEAD_DIM
    gg = gg_ref[...]
    lane = _iota((OUT_ROWS, wa), 1)
    for r0 in range(0, x_ref.shape[1], OUT_ROWS):
        rows = slice(r0, r0 + OUT_ROWS)
        gates = jax.nn.sigmoid(gt_ref[0, rows])

        def spread(branch, gates=gates):
            cols = [jnp.broadcast_to(gates[:, branch * NSA_HEADS + i:branch * NSA_HEADS + i + 1], (OUT_ROWS, wa))
                    for i in range(NSA_HEADS)]
            out = cols[NSA_HEADS - 1]
            for i in range(NSA_HEADS - 2, -1, -1):
                out = jnp.where(lane < (i + 1) * HEAD_DIM, cols[i], out)
            return out

        o_nsa = spread(0) * oc_ref[0, rows] + spread(1) * os_ref[0, rows] + spread(2) * ow_ref[0, rows]
        mix = jnp.concatenate([_rms(o_nsa, gg[:, :wa]), _rms(swa_ref[0, rows], gg[:, wa:wb]),
                               _rms(sb_ref[0, rows], gg[:, wb:])], axis=1).astype(BF16)
        y = jnp.dot(mix, w_ref[...], preferred_element_type=F32)
        o_ref[0, rows] = x_ref[0, rows] + _rms(y, gp_ref[...])


def _out_proj(x, o_cmp, o_sel, o_win, gates, o_swa, o_sb, gg, w, gp, tm):
    b, s, d = x.shape
    row = lambda n: pl.BlockSpec((1, tm, n), lambda bi, i: (bi, i, 0))
    const = lambda a, c: pl.BlockSpec((a, c), lambda bi, i: (0, 0))
    return pl.pallas_call(
        _out_kernel,
        grid=(b, s // tm),
        in_specs=[row(d), row(256), row(256), row(256), row(128), row(512), row(256),
                  const(1, d), const(d, d), const(1, d)],
        out_specs=row(d),
        out_shape=jax.ShapeDtypeStruct((b, s, d), F32),
        compiler_params=_cparams(("parallel", "parallel")),
        name="out_proj",
    )(x, o_cmp, o_sel, o_win, gates, o_swa, o_sb, gg, w, gp)


def _ffn_kernel(x_ref, g1_ref, wg_ref, wu_ref, wd_ref, g2_ref, o_ref, *, chunk):
    x = x_ref[...]
    h = _rms(x, g1_ref[...]).astype(BF16)
    dff = wg_ref.shape[1]
    f = jnp.zeros(x.shape, F32)
    for a0 in range(0, dff, chunk):
        a1 = min(a0 + chunk, dff)
        gate = jnp.dot(h, wg_ref[:, a0:a1], preferred_element_type=F32)
        up = jnp.dot(h, wu_ref[:, a0:a1], preferred_element_type=F32)
        a = (gate * jax.nn.sigmoid(gate) * up).astype(BF16)
        f = f + jnp.dot(a, wd_ref[a0:a1, :], preferred_element_type=F32)
    o_ref[...] = x + _rms(f, g2_ref[...])


def _ffn(x, g1, wg, wu, wd, g2, tm):
    t, d = x.shape
    dff = wg.shape[1]
    row = pl.BlockSpec((tm, d), lambda i: (i, 0))
    const = lambda a, c: pl.BlockSpec((a, c), lambda i: (0, 0))
    weight = lambda a, c: pl.BlockSpec((a, c), lambda i: (0, 0), pipeline_mode=pl.Buffered(1))
    return pl.pallas_call(
        functools.partial(_ffn_kernel, chunk=FFN_CHUNK),
        grid=(t // tm,),
        in_specs=[row, const(1, d), weight(d, dff), weight(d, dff), weight(dff, d), const(1, d)],
        out_specs=row,
        out_shape=jax.ShapeDtypeStruct((t, d), F32),
        compiler_params=_cparams(("parallel",)),
        name="ffn",
    )(x, g1, wg, wu, wd, g2)


def _permute_w_in(w):
    scale = 1.0 / math.sqrt(HEAD_DIM)
    nq, kc, vc, rest, ng = w[:, :256], w[:, 256:320], w[:, 320:384], w[:, 384:640], w[:, 640:652]
    sq, skv = w[:, 652:1164], w[:, 1164:1420]
    bq, bkv = w[:, 1420:1676], w[:, 1676:2188]
    pad = jnp.zeros((w.shape[0], 128 - ng.shape[1]), w.dtype)
    return jnp.concatenate([nq * scale, rest, sq * scale, skv, bq * scale, bkv, kc, vc, ng, pad], axis=1)


def _cmp_to_sel(nc_pad, nc, ns):
    c0 = jnp.arange(nc_pad)[:, None] * CMP_STRIDE
    s0 = jnp.arange(ns)[None, :] * SEL_BLK
    ov = jnp.minimum(c0 + CMP_LEN, s0 + SEL_BLK) - jnp.maximum(c0, s0)
    w = jnp.clip(ov, 0, None).astype(F32) / CMP_LEN
    return jnp.where(jnp.arange(nc_pad)[:, None] < nc, w, 0.0).astype(BF16)


def kernel(x, rel_bias, ln_attn_pre, w_in, nsa_cmp_pos, nsa_phi_w1, nsa_phi_b1, nsa_phi_w2, swa_sinks,
           grp_norm_g, w_out, ln_attn_post, ln_ffn_pre, ffn_w_gate, ffn_w_up, ffn_w_down, ln_ffn_post):
    b, s, d = x.shape
    depth = w_in.shape[0]
    nch = s // CMP_STRIDE
    ns = s // SEL_BLK
    t = b * s

    band_nsa_t = _band_table_t(rel_bias, N_BAND, NSA_HEADS, 0, True)
    band_swa_t = _band_table_t(rel_bias, SWA_WINDOW // BLK + 1, SWA_HEADS, NSA_HEADS, False)
    hct = _cmp_table(rel_bias, nch)
    wmapt = jnp.pad(_cmp_to_sel(nch, nch - 1, ns).T, ((0, IMP_PAD), (0, 0)))
    no_sinks = jnp.zeros((1, 1, 4 * BLK), F32)

    for l in range(depth):
        outs = _proj(x, ln_attn_pre[l][None], _permute_w_in(w_in[l]).astype(BF16), 512)
        qt, ks, vs, kw, vw, sqt, sk, sv, bq, bk, bv, kc, vc, gates = outs
        ckv = jnp.stack([kc.reshape(b, nch, CMP_STRIDE * HEAD_DIM), vc.reshape(b, nch, CMP_STRIDE * HEAD_DIM)])
        kcv, kcv_t = _compress(ckv, nsa_cmp_pos[l].reshape(2, 2, CMP_STRIDE * HEAD_DIM),
                               nsa_phi_w1[l].reshape(2, 2, CMP_STRIDE * HEAD_DIM, CMP_HIDDEN).astype(BF16),
                               nsa_phi_b1[l][:, None, :], nsa_phi_w2[l].astype(BF16))
        o_cmp, selmask = _cmp_attn(qt, kcv, kcv_t, hct, wmapt)
        o_sel = _sel_attn(qt, ks, vs, selmask, band_nsa_t)
        o_win = _band_attn(qt, kw[:, None], vw[:, None], band_nsa_t, no_sinks, NSA_WINDOW, False, True,
                           "win_attn")
        sinks = jnp.broadcast_to(swa_sinks[l].reshape(SWA_KV_HEADS, 1, 4, 1),
                                 (SWA_KV_HEADS, 1, 4, BLK)).reshape(SWA_KV_HEADS, 1, 4 * BLK)
        o_swa = _band_attn(sqt, sk, sv, band_swa_t, sinks, SWA_WINDOW, True, False, "swa_attn")
        o_sb = _sb_attn(bq, bk, bv)
        x = _out_proj(x, o_cmp, o_sel, o_win, gates, o_swa, o_sb, grp_norm_g[l][None], w_out[l].astype(BF16),
                      ln_attn_post[l][None], 512)
        x = _ffn(x.reshape(t, d), ln_ffn_pre[l][None], ffn_w_gate[l].astype(BF16), ffn_w_up[l].astype(BF16),
                 ffn_w_down[l].astype(BF16), ln_ffn_post[l][None], 512).reshape(b, s, d)
    return x
```

```python
import functools
import math

import jax
import jax.numpy as jnp
from jax import lax
from jax.experimental import pallas as pl
from jax.experimental.pallas import tpu as pltpu

F32 = jnp.float32
BF16 = jnp.bfloat16

HEAD_DIM = 64
BLK = 128
NSA_HEADS = 4
CMP_LEN = 32
CMP_STRIDE = 16
CMP_HIDDEN = 256
SEL_BLK = 64
N_SEL = 16
N_LOCAL = 2
NSA_WINDOW = 512
SWA_HEADS = 8
SWA_KV_HEADS = 2
SWA_WINDOW = 128
SB_HEADS = 4
N_BUCKETS = 32
T5_MAX_DISTANCE = 4096
RMS_EPS = 1e-6
FORCE_SCORE = 1e6
NEG = -1e30
LOG2E = 1.0 / math.log(2.0)
SEL_TILE = 512
SB_GROUP = 3
SB_QB = 4
SB_LOG2_FLOOR = -104.0 * LOG2E
PICKED = -3e38
CMP_QB = 4
VT_ROWS = HEAD_DIM + 16
OUT_ROWS = 128
SEL_QB = 2
FFN_CHUNK = 1024
BAND_QB = 4
CMP_CHUNK = 128
IMP_PAD = 16
IMP_ROWS = CMP_CHUNK // 4 + IMP_PAD
N_BAND = 25
VMEM_LIMIT = 56 * 1024 * 1024


def _cparams(sem):
    return pltpu.CompilerParams(dimension_semantics=sem, vmem_limit_bytes=VMEM_LIMIT)


def _rms(x, g):
    ms = jnp.mean(x * x, axis=-1, keepdims=True)
    return x * lax.rsqrt(ms + RMS_EPS) * g


def _iota(shape, axis):
    return lax.broadcasted_iota(jnp.int32, shape, axis)


def _bias_of_dist(dist, tab_ref, head):
    n = jnp.maximum(dist, 0)
    nf = jnp.maximum(n, 1).astype(F32)
    exact = N_BUCKETS // 2
    large = exact + (jnp.log(nf / exact) / math.log(T5_MAX_DISTANCE / exact)
                     * (N_BUCKETS - exact)).astype(jnp.int32)
    large = jnp.minimum(large, N_BUCKETS - 1)
    bucket = jnp.where(n < exact, n, large)
    val = jnp.full(dist.shape, tab_ref[0, head], F32)
    for k in range(1, N_BUCKETS):
        val = jnp.where(bucket == k, tab_ref[k, head], val)
    return val


def _band_table_t_kernel(tab_ref, o_ref, *, head_off, shift):
    h = pl.program_id(0) + head_off

    def tile(m, _):
        dist = BLK * m + _iota((BLK, BLK), 1) - _iota((BLK, BLK), 0)
        bias = _bias_of_dist(dist, tab_ref, h)
        o_ref[m] = (bias - tab_ref[N_BUCKETS - 1, h]) * LOG2E if shift else bias
        return 0

    lax.fori_loop(0, o_ref.shape[0], tile, 0)


def _band_table_t(rel_bias, n_band, n_heads, head_off, shift):
    return pl.pallas_call(
        functools.partial(_band_table_t_kernel, head_off=head_off, shift=shift),
        grid=(n_heads,),
        in_specs=[pl.BlockSpec(memory_space=pltpu.SMEM)],
        out_specs=pl.BlockSpec((n_band, BLK, BLK), lambda h: (0, 0, h)),
        out_shape=jax.ShapeDtypeStruct((n_band, BLK, n_heads * BLK), F32),
        compiler_params=_cparams(("arbitrary",)),
        name="band_table_t",
    )(rel_bias)


def _cmp_table_kernel(tab_ref, o_ref, *, ncp):
    h = pl.program_id(0)
    rel = _iota((2 * ncp, BLK), 0) - (ncp - 8)
    dist = _iota((2 * ncp, BLK), 1) - CMP_STRIDE * rel - (CMP_LEN - 1)
    o_ref[...] = _bias_of_dist(dist, tab_ref, h) * LOG2E


def _cmp_table(rel_bias, ncp):
    return pl.pallas_call(
        functools.partial(_cmp_table_kernel, ncp=ncp),
        grid=(NSA_HEADS,),
        in_specs=[pl.BlockSpec(memory_space=pltpu.SMEM)],
        out_specs=pl.BlockSpec((2 * ncp, BLK), lambda h: (0, h)),
        out_shape=jax.ShapeDtypeStruct((2 * ncp, NSA_HEADS * BLK), F32),
        compiler_params=_cparams(("arbitrary",)),
        name="cmp_table",
    )(rel_bias)


def _proj_kernel(x_ref, g_ref, w_ref, qt_ref, ks_ref, vs_ref, kw_ref, vw_ref, sqt_ref, sk_ref, sv_ref,
                 bq_ref, bk_ref, bv_ref, kc_ref, vc_ref, gt_ref):
    h = _rms(x_ref[0], g_ref[...]).astype(BF16)
    tm = x_ref.shape[1]

    def seg(a, b):
        return jnp.dot(h, w_ref[:, a:b], preferred_element_type=F32)

    def heads(ref, a, n):
        y = seg(a, a + n * HEAD_DIM).astype(BF16)
        for i in range(n):
            ref[0, i] = y[:, i * HEAD_DIM:(i + 1) * HEAD_DIM]

    def queries_t(ref, a, n, scale=None):
        yt = seg(a, a + n * HEAD_DIM).T
        if scale is not None:
            yt = yt * scale
        for j in range(tm // BLK):
            for i in range(n):
                ref[0, j, :, i * BLK:(i + 1) * BLK] = (
                    yt[i * HEAD_DIM:(i + 1) * HEAD_DIM, j * BLK:(j + 1) * BLK].astype(BF16))

    ones_rows = jnp.where(_iota((VT_ROWS - HEAD_DIM, tm), 0) == 0, 1.0, 0.0)

    def values_t(yt):
        return jnp.concatenate([yt, ones_rows], axis=0).astype(BF16)

    queries_t(qt_ref, 0, NSA_HEADS, LOG2E)
    y = seg(256, 512)
    vs_ref[0] = values_t(y[:, 0:128].T[64:128])
    vw_ref[0] = values_t(y[:, 128:256].T[64:128])
    y = y.astype(BF16)
    blk_in_tile = ((pl.program_id(1) * tm + _iota((tm, HEAD_DIM), 0)) & (SEL_TILE - 1)) // SEL_BLK
    onehot = jnp.where(_iota((tm, HEAD_DIM), 1) == blk_in_tile, 1.0, 0.0).astype(BF16)
    ks_ref[0] = jnp.concatenate([y[:, 0:64], onehot], axis=1)
    kw_ref[0] = y[:, 128:192]
    queries_t(sqt_ref, 512, SWA_HEADS)
    heads(sk_ref, 1024, SWA_KV_HEADS)
    yt = seg(1152, 1280).T
    for i in range(SWA_KV_HEADS):
        sv_ref[0, i] = values_t(yt[i * HEAD_DIM:(i + 1) * HEAD_DIM])
    queries_t(bq_ref, 1280, SB_HEADS, LOG2E)
    bk_ref[0] = seg(1536, 1792).astype(BF16)
    bv_ref[0] = seg(1792, 2048).T.astype(BF16)
    y = seg(2048, 2304)
    kc_ref[0] = y[:, 0:64]
    vc_ref[0] = y[:, 64:128]
    gt_ref[0] = y[:, 128:256]


def _proj(x, g, w, tm):
    b, s, d = x.shape
    nw = w.shape[1]
    hd = lambda n: pl.BlockSpec((1, n, tm, HEAD_DIM), lambda bi, i: (bi, 0, i, 0))
    flat = lambda n: pl.BlockSpec((1, tm, n), lambda bi, i: (bi, i, 0))
    hshape = lambda n: jax.ShapeDtypeStruct((b, n, s, HEAD_DIM), BF16)
    fshape = lambda n, dt: jax.ShapeDtypeStruct((b, s, n), dt)
    qt_spec = lambda n: pl.BlockSpec((1, tm // BLK, HEAD_DIM, n * BLK), lambda bi, i: (bi, i, 0, 0))
    qt_shape = lambda n: jax.ShapeDtypeStruct((b, s // BLK, HEAD_DIM, n * BLK), BF16)
    vt_spec = pl.BlockSpec((1, VT_ROWS, tm), lambda bi, i: (bi, 0, i))
    vt_shape = jax.ShapeDtypeStruct((b, VT_ROWS, s), BF16)
    return pl.pallas_call(
        _proj_kernel,
        grid=(b, s // tm),
        in_specs=[pl.BlockSpec((1, tm, d), lambda bi, i: (bi, i, 0)),
                  pl.BlockSpec((1, d), lambda bi, i: (0, 0)),
                  pl.BlockSpec((d, nw), lambda bi, i: (0, 0))],
        out_specs=[qt_spec(NSA_HEADS), flat(128), vt_spec, flat(64), vt_spec,
                   qt_spec(SWA_HEADS), hd(SWA_KV_HEADS),
                   pl.BlockSpec((1, SWA_KV_HEADS, VT_ROWS, tm), lambda bi, i: (bi, 0, 0, i)),
                   qt_spec(SB_HEADS), flat(256),
                   pl.BlockSpec((1, SB_HEADS * HEAD_DIM, tm), lambda bi, i: (bi, 0, i)),
                   flat(64), flat(64), flat(128)],
        out_shape=[qt_shape(NSA_HEADS), fshape(128, BF16), vt_shape, fshape(64, BF16), vt_shape,
                   qt_shape(SWA_HEADS), hshape(SWA_KV_HEADS),
                   jax.ShapeDtypeStruct((b, SWA_KV_HEADS, VT_ROWS, s), BF16),
                   qt_shape(SB_HEADS), fshape(256, BF16),
                   jax.ShapeDtypeStruct((b, SB_HEADS * HEAD_DIM, s), BF16),
                   fshape(64, F32), fshape(64, F32), fshape(128, F32)],
        compiler_params=_cparams(("parallel", "parallel")),
        name="proj",
    )(x, g, w)


def _compress_kernel(c_ref, pos_ref, w1_ref, b1_ref, w2_ref, o_ref, ot_ref):
    c = c_ref[0, 0]
    nch = c.shape[0]
    xa = (c + pos_ref[0, 0:1]).astype(BF16)
    xb = (c + pos_ref[0, 1:2]).astype(BF16)
    p = jnp.dot(xa, w1_ref[0, 0], preferred_element_type=F32)
    q = jnp.dot(xb, w1_ref[0, 1], preferred_element_type=F32)
    hid = p + pltpu.roll(q, nch - 1, 0) + b1_ref[0]
    hid = hid * jax.nn.sigmoid(hid)
    out = jnp.dot(hid.astype(BF16), w2_ref[0], preferred_element_type=F32)
    row = _iota(out.shape, 0)
    out = jnp.where(row < nch - 1, out, 0.0)
    o_ref[0, 0] = out.astype(BF16)
    ot_ref[0, 0] = jnp.concatenate([out, jnp.zeros_like(out)], axis=1).T[:HEAD_DIM].astype(BF16)


def _compress(ckv, pos, w1, b1, w2):
    _, b, nch, cw = ckv.shape
    return pl.pallas_call(
        _compress_kernel,
        grid=(2, b),
        in_specs=[pl.BlockSpec((1, 1, nch, cw), lambda j, bi: (j, bi, 0, 0)),
                  pl.BlockSpec((1, 2, cw), lambda j, bi: (j, 0, 0)),
                  pl.BlockSpec((1, 2, cw, CMP_HIDDEN), lambda j, bi: (j, 0, 0, 0)),
                  pl.BlockSpec((1, 1, CMP_HIDDEN), lambda j, bi: (j, 0, 0)),
                  pl.BlockSpec((1, CMP_HIDDEN, HEAD_DIM), lambda j, bi: (j, 0, 0))],
        out_specs=[pl.BlockSpec((1, 1, nch, HEAD_DIM), lambda j, bi: (j, bi, 0, 0)),
                   pl.BlockSpec((1, 1, HEAD_DIM, nch), lambda j, bi: (j, bi, 0, 0))],
        out_shape=[jax.ShapeDtypeStruct((2, b, nch, HEAD_DIM), BF16),
                   jax.ShapeDtypeStruct((2, b, HEAD_DIM, nch), BF16)],
        compiler_params=_cparams(("parallel", "parallel")),
        name="compress",
    )(ckv, pos, w1, b1, w2)


def _transpose_pad(x, axis):
    pad = jnp.zeros(x.shape, x.dtype)
    return jnp.concatenate([x, pad], axis=axis).T


def _cmp_kernel(qt_ref, kc_ref, vct_ref, hct_ref, wmapt_ref, o_ref, pen_ref, s_scr, oacc_scr, iacc_scr):
    step = pl.program_id(1)
    ncp = kc_ref.shape[2]
    ns = wmapt_ref.shape[0] - IMP_PAD
    lanes = NSA_HEADS * BLK
    row_c = _iota((CMP_CHUNK, lanes), 0)
    lane_q = _iota((CMP_CHUNK, lanes), 1) & (BLK - 1)
    n_chunks = (CMP_QB * step) // (CMP_CHUNK // 8) + 1

    def score_chunk(ch, mxs):
        c0 = pl.multiple_of(ch * CMP_CHUNK, CMP_CHUNK)
        kc = kc_ref[0, 0, pl.ds(c0, CMP_CHUNK), :]
        raw = [jnp.dot(kc, qt_ref[0, qb], preferred_element_type=F32) for qb in range(CMP_QB)]
        out = []
        for qb in range(CMP_QB):
            n = CMP_QB * step + qb
            bias = hct_ref[pl.ds(pl.multiple_of(ncp - 8 - 8 * n + c0, 8), CMP_CHUNK), :]
            visible = (BLK * n + lane_q - CMP_STRIDE * (c0 + row_c) - (CMP_LEN - 1)) >= 0
            s = jnp.where(visible, raw[qb] + bias, NEG)
            s_scr[qb, ch] = s
            out.append(jnp.maximum(mxs[qb], jnp.max(s, axis=0, keepdims=True)))
        return tuple(out)

    mxs = lax.fori_loop(0, n_chunks, score_chunk, tuple(jnp.full((1, lanes), NEG, F32) for _ in range(CMP_QB)))
    ms = [jnp.where(mx <= 0.5 * NEG, 0.0, mx) for mx in mxs]
    oacc_scr[...] = jnp.zeros(oacc_scr.shape, F32)
    iacc_scr[...] = jnp.zeros(iacc_scr.shape, F32)

    def prob_chunk(ch, ls):
        c0 = pl.multiple_of(ch * CMP_CHUNK, CMP_CHUNK)
        vct = vct_ref[0, 0, :, pl.ds(c0, CMP_CHUNK)]
        r0 = pl.multiple_of(ch * (CMP_CHUNK // 4), 16)
        wmt = wmapt_ref[pl.ds(r0, IMP_ROWS), pl.ds(c0, CMP_CHUNK)]
        both = jnp.concatenate([vct, wmt], axis=0)
        ps = [jnp.exp2(s_scr[qb, ch] - ms[qb]) for qb in range(CMP_QB)]
        prods = [jnp.dot(both, p.astype(BF16), preferred_element_type=F32) for p in ps]
        for qb in range(CMP_QB):
            oacc_scr[qb] += prods[qb][:HEAD_DIM]
            iacc_scr[qb, pl.ds(r0, IMP_ROWS), :] += prods[qb][HEAD_DIM:]
        return tuple(ls[qb] + jnp.sum(ps[qb], axis=0, keepdims=True) for qb in range(CMP_QB))

    ls = lax.fori_loop(0, n_chunks, prob_chunk, tuple(jnp.zeros((1, lanes), F32) for _ in range(CMP_QB)))
    imps = []
    for qb in range(CMP_QB):
        inv = 1.0 / jnp.maximum(ls[qb], 1e-30)
        o_t = oacc_scr[qb] * inv
        o_ref[0, qb * BLK:(qb + 1) * BLK, :] = jnp.concatenate(
            [_transpose_pad(o_t[:, h * BLK:(h + 1) * BLK], 0)[:, :HEAD_DIM] for h in range(NSA_HEADS)], axis=1)
        w = iacc_scr[qb, 0:ns, :] * inv
        imps.append(functools.reduce(jnp.add, [w[:, h * BLK:(h + 1) * BLK] for h in range(NSA_HEADS)]))

    imp = jnp.concatenate(imps, axis=1)
    shape = (ns, CMP_QB * BLK)
    j = _iota(shape, 0)
    cur = (CMP_QB * BLK * step + _iota(shape, 1)) // SEL_BLK
    valid = j <= cur
    forced = valid & ((j == 0) | (j > cur - N_LOCAL))
    score = jnp.where(forced, PICKED, jnp.where(valid, imp, NEG))
    jf = j.astype(F32)
    for _ in range(min(N_SEL, ns) - (N_LOCAL + 1)):
        best = jnp.max(score, axis=0, keepdims=True)
        first = jnp.min(jnp.where(score == best, jf, float(ns)), axis=0, keepdims=True)
        score = jnp.where(jf == first, PICKED, score)
    pen_ref[0, 0] = jnp.where(valid & (score == PICKED), 0.0, NEG)


def _cmp_attn(qt, kcv, kcv_t, hct, wmapt):
    b, nb, _, lanes = qt.shape
    ncp = kcv.shape[2]
    ns = wmapt.shape[0] - IMP_PAD
    s = nb * BLK
    return pl.pallas_call(
        _cmp_kernel,
        grid=(b, nb // CMP_QB),
        in_specs=[pl.BlockSpec((1, CMP_QB, HEAD_DIM, lanes), lambda bi, i: (bi, i, 0, 0)),
                  pl.BlockSpec((1, 1, ncp, HEAD_DIM), lambda bi, i: (0, bi, 0, 0)),
                  pl.BlockSpec((1, 1, HEAD_DIM, ncp), lambda bi, i: (1, bi, 0, 0)),
                  pl.BlockSpec((2 * ncp, lanes), lambda bi, i: (0, 0)),
                  pl.BlockSpec((ns + IMP_PAD, ncp), lambda bi, i: (0, 0))],
        out_specs=[pl.BlockSpec((1, CMP_QB * BLK, NSA_HEADS * HEAD_DIM), lambda bi, i: (bi, i, 0)),
                   pl.BlockSpec((1, 1, ns, CMP_QB * BLK), lambda bi, i: (bi, i, 0, 0))],
        out_shape=[jax.ShapeDtypeStruct((b, s, NSA_HEADS * HEAD_DIM), F32),
                   jax.ShapeDtypeStruct((b, nb // CMP_QB, ns, CMP_QB * BLK), F32)],
        scratch_shapes=[pltpu.VMEM((CMP_QB, ncp // CMP_CHUNK, CMP_CHUNK, lanes), F32),
                        pltpu.VMEM((CMP_QB, HEAD_DIM, lanes), F32),
                        pltpu.VMEM((CMP_QB, ns + IMP_PAD, lanes), F32)],
        compiler_params=_cparams(("parallel", "arbitrary")),
        name="cmp_attn",
    )(qt, kcv, kcv_t, hct, wmapt)


def _sel_kernel(qt_ref, k_ref, vt_ref, pen_ref, band_ref, o_ref, s_scr, p_scr, mx_scr, m_scr, alpha_scr, acc_scr):
    n = SEL_QB * pl.program_id(1)
    qlanes = NSA_HEADS * BLK
    lanes = SEL_QB * qlanes
    sub = SEL_TILE // BLK
    per = BLK // SEL_BLK
    qt = jnp.concatenate([qt_ref[0, qb] for qb in range(SEL_QB)], axis=1)
    lane = _iota((BLK, lanes), 1)
    t = BLK * (n + lane // qlanes) + (lane & (BLK - 1))

    def scores_into(kt, s_buf, mx_buf, far):
        rows = sub * per
        pen = pen_ref[0, 0, pl.ds(pl.multiple_of(kt * rows, rows), rows), :]
        pen = jnp.concatenate([pen[:, qb * BLK:(qb + 1) * BLK] for qb in range(SEL_QB) for _ in range(NSA_HEADS)],
                              axis=1)
        tail = jnp.concatenate([pen, jnp.zeros((HEAD_DIM - rows, lanes), F32)], axis=0).astype(BF16)
        q_aug = jnp.concatenate([qt, tail], axis=0)
        k0 = pl.multiple_of(kt * SEL_TILE, SEL_TILE)
        s = jnp.dot(k_ref[0, pl.ds(k0, SEL_TILE), :], q_aug, preferred_element_type=F32)
        if far:
            s_buf[...] = s
            mx_buf[...] = jnp.max(s, axis=0, keepdims=True)
        else:
            mxs = []
            for qb in range(SEL_QB):
                cols = slice(qb * qlanes, (qb + 1) * qlanes)
                parts = [s[u * BLK:(u + 1) * BLK, cols] + band_ref[jnp.clip(n + qb - sub * kt - u, 0, N_BAND - 1)]
                         for u in range(sub)]
                for u in range(sub):
                    s_buf[u * BLK:(u + 1) * BLK, cols] = parts[u]
                mxs.append(functools.reduce(jnp.maximum, [jnp.max(v, axis=0, keepdims=True) for v in parts]))
            mx_buf[...] = jnp.concatenate(mxs, axis=1)

    def add_values(kt, p_buf):
        k0 = pl.multiple_of(kt * SEL_TILE, SEL_TILE)
        pv = jnp.dot(vt_ref[0, :, pl.ds(k0, SEL_TILE)], p_buf[...], preferred_element_type=F32)
        acc_scr[...] = alpha_scr[...] * acc_scr[...] + pv

    def softmax_into(kt, s_buf, mx_buf, p_buf, causal):
        def piece(u):
            s = s_buf[u * BLK:(u + 1) * BLK, :]
            if causal:
                s = jnp.where(kt * SEL_TILE + u * BLK + _iota((BLK, lanes), 0) <= t, s, NEG)
            return s
        m_i = m_scr[...]
        if causal:
            mx = functools.reduce(jnp.maximum, [jnp.max(piece(u), axis=0, keepdims=True) for u in range(sub)])
        else:
            mx = mx_buf[...]
        m_new = jnp.maximum(m_i, mx)
        for u in range(sub):
            p_buf[u * BLK:(u + 1) * BLK, :] = jnp.exp2(piece(u) - m_new).astype(BF16)
        m_scr[...] = m_new
        alpha_scr[...] = jnp.exp2(m_i - m_new)

    def stage(i, cur, nxt, p_cur, p_prev, far):
        add_values(jnp.maximum(i - 1, 0), p_prev)
        scores_into(i + 1, *nxt, far)
        softmax_into(i, *cur, p_cur, False)

    def finish(i, cur, p_cur, p_prev):
        add_values(jnp.maximum(i - 1, 0), p_prev)
        softmax_into(i, *cur, p_cur, True)
        add_values(i, p_cur)
        acc = acc_scr[...]
        o_t = acc[:HEAD_DIM] / acc[HEAD_DIM:HEAD_DIM + 1]
        for qb in range(SEL_QB):
            o_ref[0, qb * BLK:(qb + 1) * BLK, :] = jnp.concatenate(
                [_transpose_pad(o_t[:, qb * qlanes + h * BLK:qb * qlanes + (h + 1) * BLK], 0)[:, :HEAD_DIM]
                 for h in range(NSA_HEADS)], axis=1)

    s0, s1 = (s_scr.at[0], mx_scr.at[0]), (s_scr.at[1], mx_scr.at[1])
    p0, p1 = p_scr.at[0], p_scr.at[1]
    m_scr[...] = jnp.full((1, lanes), NEG, F32)
    alpha_scr[...] = jnp.ones((1, lanes), F32)
    acc_scr[...] = jnp.zeros(acc_scr.shape, F32)
    p1[...] = jnp.zeros((SEL_TILE, lanes), BF16)
    scores_into(0, *s0, False)
    last = (BLK * n + BLK - 1) // SEL_TILE
    n_far = jnp.maximum((n - (N_BAND - 2)) // sub, 0)
    far_pairs = jnp.minimum(jnp.maximum((n_far - 1) // 2, 0), last // 2)

    def pair(j, _, far):
        stage(2 * j, s0, s1, p0, p1, far)
        stage(2 * j + 1, s1, s0, p1, p0, far)
        return 0

    lax.fori_loop(0, far_pairs, functools.partial(pair, far=True), 0)
    lax.fori_loop(far_pairs, last // 2, functools.partial(pair, far=False), 0)

    @pl.when(last % 2 == 1)
    def _():
        stage(last - 1, s0, s1, p0, p1, False)
        finish(last, s1, p1, p0)

    @pl.when(last % 2 == 0)
    def _():
        finish(last, s0, p0, p1)


def _sel_attn(qt, ks, vst, pen, band_t):
    b, nb, _, qlanes = qt.shape
    s = nb * BLK
    ns = pen.shape[2]
    lanes = SEL_QB * qlanes
    per_cmp = CMP_QB // SEL_QB
    return pl.pallas_call(
        _sel_kernel,
        grid=(b, nb // SEL_QB),
        in_specs=[pl.BlockSpec((1, SEL_QB, HEAD_DIM, qlanes), lambda bi, i: (bi, i, 0, 0)),
                  pl.BlockSpec((1, s, 2 * HEAD_DIM), lambda bi, i: (bi, 0, 0)),
                  pl.BlockSpec((1, VT_ROWS, s), lambda bi, i: (bi, 0, 0)),
                  pl.BlockSpec((1, 1, ns, SEL_QB * BLK), lambda bi, i: (bi, i // per_cmp, 0, i % per_cmp)),
                  pl.BlockSpec((N_BAND, BLK, qlanes), lambda bi, i: (0, 0, 0))],
        out_specs=pl.BlockSpec((1, SEL_QB * BLK, NSA_HEADS * HEAD_DIM), lambda bi, i: (bi, i, 0)),
        out_shape=jax.ShapeDtypeStruct((b, s, NSA_HEADS * HEAD_DIM), F32),
        scratch_shapes=[pltpu.VMEM((2, SEL_TILE, lanes), F32),
                        pltpu.VMEM((2, SEL_TILE, lanes), BF16),
                        pltpu.VMEM((2, 1, lanes), F32),
                        pltpu.VMEM((1, lanes), F32),
                        pltpu.VMEM((1, lanes), F32),
                        pltpu.VMEM((VT_ROWS, lanes), F32)],
        compiler_params=_cparams(("parallel", "arbitrary")),
        name="sel_attn",
    )(qt, ks, vst, pen, band_t)


def _band_kernel(*refs, window, groups, use_sinks, base2, n_parts):
    qt_ref = refs[0]
    k_refs = refs[1:1 + n_parts]
    vt_refs = refs[1 + n_parts:1 + 2 * n_parts]
    band_ref, sink_ref, o_ref = refs[1 + 2 * n_parts:]
    step = pl.program_id(1)
    n_prev = window // BLK
    nk = (n_prev + 1) * BLK
    lanes = 4 * BLK
    row = _iota((nk, lanes), 0)
    dist = n_prev * BLK + (_iota((nk, lanes), 1) & (BLK - 1)) - row
    in_window = (dist >= 0) & (dist < window)
    chains = [(g, qb) for g in range(groups) for qb in range(BAND_QB)]
    kwin = [jnp.concatenate([r[0, g] for r in k_refs], axis=0) for g in range(groups)]
    vtwin = [jnp.concatenate([r[0, g] for r in vt_refs], axis=1) for g in range(groups)]
    bias = [jnp.concatenate([band_ref[n_prev - u, :, g * lanes:(g + 1) * lanes] for u in range(n_prev + 1)], axis=0)
            for g in range(groups)]
    scores = [jnp.dot(kwin[g][qb * BLK:qb * BLK + nk], qt_ref[0, qb, :, g * lanes:(g + 1) * lanes],
                      preferred_element_type=F32) for g, qb in chains]
    probs, maxes = [], []
    for (g, qb), s in zip(chains, scores):
        s = jnp.where(in_window & (row >= (n_prev - BAND_QB * step - qb) * BLK), s + bias[g], NEG)
        mx = jnp.max(s, axis=0, keepdims=True)
        if use_sinks:
            mx = jnp.maximum(mx, sink_ref[g])
        probs.append((jnp.exp2(s - mx) if base2 else jnp.exp(s - mx)).astype(BF16))
        maxes.append(mx)
    accs = [jnp.dot(vtwin[g][:, qb * BLK:qb * BLK + nk], p, preferred_element_type=F32)
            for (g, qb), p in zip(chains, probs)]
    for (g, qb), acc, mx in zip(chains, accs, maxes):
        den = acc[HEAD_DIM:HEAD_DIM + 1]
        if use_sinks:
            den = den + jnp.exp(sink_ref[g] - mx)
        o_t = acc[:HEAD_DIM] / den
        o_ref[0, qb * BLK:(qb + 1) * BLK, 4 * g * HEAD_DIM:4 * (g + 1) * HEAD_DIM] = jnp.concatenate(
            [_transpose_pad(o_t[:, h * BLK:(h + 1) * BLK], 0)[:, :HEAD_DIM] for h in range(4)], axis=1)


def _band_attn(qt, k, vt, band_t, sinks, window, use_sinks, base2, name):
    b, nb, _, width = qt.shape
    g = width // (4 * BLK)
    s = nb * BLK
    n_prev = window // BLK
    tq = BAND_QB * BLK
    if n_prev * BLK % tq == 0:
        piece = tq
        starts = [lambda i, d=d: jnp.maximum(i - d, 0) for d in range(n_prev * BLK // tq, 0, -1)]
    else:
        assert n_prev == 1
        piece = BLK
        starts = [lambda i: jnp.maximum(BAND_QB * i - 1, 0)]
    k_specs = [pl.BlockSpec((1, g, piece, HEAD_DIM), lambda bi, i, f=f: (bi, 0, f(i), 0)) for f in starts]
    k_specs.append(pl.BlockSpec((1, g, tq, HEAD_DIM), lambda bi, i: (bi, 0, i, 0)))
    vt_specs = [pl.BlockSpec((1, g, VT_ROWS, piece), lambda bi, i, f=f: (bi, 0, 0, f(i))) for f in starts]
    vt_specs.append(pl.BlockSpec((1, g, VT_ROWS, tq), lambda bi, i: (bi, 0, 0, i)))
    n_parts = len(k_specs)
    return pl.pallas_call(
        functools.partial(_band_kernel, window=window, groups=g, use_sinks=use_sinks, base2=base2,
                          n_parts=n_parts),
        grid=(b, s // tq),
        in_specs=[pl.BlockSpec((1, BAND_QB, HEAD_DIM, width), lambda bi, i: (bi, i, 0, 0))] + k_specs + vt_specs
                 + [pl.BlockSpec((n_prev + 1, BLK, width), lambda bi, i: (0, 0, 0)),
                    pl.BlockSpec((g, 1, 4 * BLK), lambda bi, i: (0, 0, 0))],
        out_specs=pl.BlockSpec((1, tq, g * 4 * HEAD_DIM), lambda bi, i: (bi, i, 0)),
        out_shape=jax.ShapeDtypeStruct((b, s, g * 4 * HEAD_DIM), F32),
        compiler_params=_cparams(("parallel", "arbitrary")),
        name=name,
    )(qt, *([k] * n_parts), *([vt] * n_parts), band_t, sinks)


def _sb_kernel(qt_ref, k_ref, vt_ref, later_ref, o_ref):
    n0 = SB_QB * pl.program_id(1)
    lanes = SB_HEADS * BLK
    lane_q = _iota((BLK, lanes), 1) & (BLK - 1)
    row = _iota((BLK, lanes), 0)
    later = later_ref[...]
    zero = jnp.zeros((HEAD_DIM, BLK), BF16)
    blocks = range(SB_QB)
    q_bd = [jnp.concatenate(
        [jnp.concatenate([zero] * h + [qt_ref[0, qb, :, h * BLK:(h + 1) * BLK]] + [zero] * (SB_HEADS - 1 - h), axis=0)
         for h in range(SB_HEADS)], axis=1) for qb in blocks]

    def cond(c):
        it, carry, _ = c
        return (n0 + SB_QB - 1 - SB_GROUP * it >= 0) & (jnp.max(functools.reduce(jnp.maximum, carry)) > SB_LOG2_FLOOR)

    def body(c):
        it, carry, acc = c
        k_all, vt_all, before = [], [], []
        for qb in blocks:
            top = n0 + qb - SB_GROUP * it
            ks, vts, befores = [], [], []
            for j in range(SB_GROUP - 1, -1, -1):
                kb = top - j
                k0 = pl.multiple_of(jnp.maximum(kb, 0) * BLK, BLK)
                ks.append(k_ref[0, pl.ds(k0, BLK), :])
                vts.append(vt_ref[0, :, pl.ds(k0, BLK)])
                befores.append((kb >= 0) & (k0 + row < BLK * (n0 + qb) + lane_q))
            k_all.append(jnp.concatenate(ks, axis=0))
            vt_all.append(jnp.concatenate(vts, axis=1))
            before.append(jnp.concatenate(befores, axis=0))
        z = [jnp.dot(k_all[qb], q_bd[qb], preferred_element_type=F32) for qb in blocks]
        log_not, split = [], []
        for qb in blocks:
            softplus = jnp.maximum(z[qb], 0.0) + jnp.log2(1.0 + jnp.exp2(-jnp.abs(z[qb])))
            ln = jnp.where(before[qb], -softplus, 0.0)
            hi = ln.astype(BF16)
            lo = (ln - hi.astype(F32)).astype(BF16)
            log_not.append(ln)
            split.append(jnp.concatenate([hi, lo], axis=0))
        tail = [jnp.dot(later, split[qb], preferred_element_type=F32) for qb in blocks]
        a = [jnp.where(before[qb], jnp.exp2(z[qb] + log_not[qb] + tail[qb] + carry[qb]), 0.0).astype(BF16)
             for qb in blocks]
        pv = [jnp.dot(vt_all[qb], a[qb], preferred_element_type=F32) for qb in blocks]
        acc = tuple(tuple(acc[qb][h] + pv[qb][h * HEAD_DIM:(h + 1) * HEAD_DIM, h * BLK:(h + 1) * BLK]
                          for h in range(SB_HEADS)) for qb in blocks)
        carry = tuple(carry[qb] + jnp.sum(log_not[qb], axis=0, keepdims=True) for qb in blocks)
        return it + 1, carry, acc

    init = (0, tuple(jnp.zeros((1, lanes), F32) for _ in blocks),
            tuple(tuple(jnp.zeros((HEAD_DIM, BLK), F32) for _ in range(SB_HEADS)) for _ in blocks))
    _, _, acc = lax.while_loop(cond, body, init)
    for qb in blocks:
        o_ref[0, qb * BLK:(qb + 1) * BLK, :] = jnp.concatenate(
            [_transpose_pad(a, 0)[:, :HEAD_DIM] for a in acc[qb]], axis=1)


def _sb_attn(qt, k, vt):
    b, nb, _, lanes = qt.shape
    s = nb * BLK
    width = SB_HEADS * HEAD_DIM
    gk = SB_GROUP * BLK
    later = (jnp.arange(2 * gk)[None, :] % gk > jnp.arange(gk)[:, None]).astype(BF16)
    return pl.pallas_call(
        _sb_kernel,
        grid=(b, nb // SB_QB),
        in_specs=[pl.BlockSpec((1, SB_QB, HEAD_DIM, lanes), lambda bi, n: (bi, n, 0, 0)),
                  pl.BlockSpec((1, s, width), lambda bi, n: (bi, 0, 0)),
                  pl.BlockSpec((1, width, s), lambda bi, n: (bi, 0, 0)),
                  pl.BlockSpec((gk, 2 * gk), lambda bi, n: (0, 0))],
        out_specs=pl.BlockSpec((1, SB_QB * BLK, width), lambda bi, n: (bi, n, 0)),
        out_shape=jax.ShapeDtypeStruct((b, s, width), F32),
        compiler_params=_cparams(("parallel", "arbitrary")),
        name="sb_attn",
    )(qt, k, vt, later)


def _out_kernel(x_ref, oc_ref, os_ref, ow_ref, gt_ref, swa_ref, sb_ref, gg_ref, w_ref, gp_ref, o_ref):
    wa = NSA_HEADS * HEAD_DIM
    wb = wa + SWA_HEADS * HEAD_DIM
    gg = gg_ref[...]
    lane = _iota((OUT_ROWS, wa), 1)
    groups = [slice(r0, r0 + OUT_ROWS) for r0 in range(0, x_ref.shape[1], OUT_ROWS)]
    mixes = []
    for rows in groups:
        gates = jax.nn.sigmoid(gt_ref[0, rows])

        def spread(branch, gates=gates):
            cols = [jnp.broadcast_to(gates[:, branch * NSA_HEADS + i:branch * NSA_HEADS + i + 1], (OUT_ROWS, wa))
                    for i in range(NSA_HEADS)]
            out = cols[NSA_HEADS - 1]
            for i in range(NSA_HEADS - 2, -1, -1):
                out = jnp.where(lane < (i + 1) * HEAD_DIM, cols[i], out)
            return out

        o_nsa = spread(0) * oc_ref[0, rows] + spread(1) * os_ref[0, rows] + spread(2) * ow_ref[0, rows]
        mixes.append(jnp.concatenate([_rms(o_nsa, gg[:, :wa]), _rms(swa_ref[0, rows], gg[:, wa:wb]),
                                      _rms(sb_ref[0, rows], gg[:, wb:])], axis=1).astype(BF16))
    ys = [jnp.dot(mix, w_ref[...], preferred_element_type=F32) for mix in mixes]
    for rows, y in zip(groups, ys):
        o_ref[0, rows] = x_ref[0, rows] + _rms(y, gp_ref[...])


def _out_proj(x, o_cmp, o_sel, o_win, gates, o_swa, o_sb, gg, w, gp, tm):
    b, s, d = x.shape
    row = lambda n: pl.BlockSpec((1, tm, n), lambda bi, i: (bi, i, 0))
    const = lambda a, c: pl.BlockSpec((a, c), lambda bi, i: (0, 0))
    return pl.pallas_call(
        _out_kernel,
        grid=(b, s // tm),
        in_specs=[row(d), row(256), row(256), row(256), row(128), row(512), row(256),
                  const(1, d), const(d, d), const(1, d)],
        out_specs=row(d),
        out_shape=jax.ShapeDtypeStruct((b, s, d), F32),
        compiler_params=_cparams(("parallel", "parallel")),
        name="out_proj",
    )(x, o_cmp, o_sel, o_win, gates, o_swa, o_sb, gg, w, gp)


def _ffn_kernel(x_ref, g1_ref, wg_ref, wu_ref, wd_ref, g2_ref, o_ref, *, chunk):
    x = x_ref[...]
    h = _rms(x, g1_ref[...]).astype(BF16)
    dff = wg_ref.shape[1]
    f = jnp.zeros(x.shape, F32)
    for a0 in range(0, dff, chunk):
        a1 = min(a0 + chunk, dff)
        gate = jnp.dot(h, wg_ref[:, a0:a1], preferred_element_type=F32)
        up = jnp.dot(h, wu_ref[:, a0:a1], preferred_element_type=F32)
        a = (gate * jax.nn.sigmoid(gate) * up).astype(BF16)
        f = f + jnp.dot(a, wd_ref[a0:a1, :], preferred_element_type=F32)
    o_ref[...] = x + _rms(f, g2_ref[...])


def _ffn(x, g1, wg, wu, wd, g2, tm):
    t, d = x.shape
    dff = wg.shape[1]
    row = pl.BlockSpec((tm, d), lambda i: (i, 0))
    const = lambda a, c: pl.BlockSpec((a, c), lambda i: (0, 0))
    weight = lambda a, c: pl.BlockSpec((a, c), lambda i: (0, 0), pipeline_mode=pl.Buffered(1))
    return pl.pallas_call(
        functools.partial(_ffn_kernel, chunk=FFN_CHUNK),
        grid=(t // tm,),
        in_specs=[row, const(1, d), weight(d, dff), weight(d, dff), weight(dff, d), const(1, d)],
        out_specs=row,
        out_shape=jax.ShapeDtypeStruct((t, d), F32),
        compiler_params=_cparams(("parallel",)),
        name="ffn",
    )(x, g1, wg, wu, wd, g2)


def _permute_w_in(w):
    scale = 1.0 / math.sqrt(HEAD_DIM)
    nq, kc, vc, rest, ng = w[:, :256], w[:, 256:320], w[:, 320:384], w[:, 384:640], w[:, 640:652]
    sq, skv = w[:, 652:1164], w[:, 1164:1420]
    bq, bkv = w[:, 1420:1676], w[:, 1676:2188]
    pad = jnp.zeros((w.shape[0], 128 - ng.shape[1]), w.dtype)
    return jnp.concatenate([nq * scale, rest, sq * scale, skv, bq * scale, bkv, kc, vc, ng, pad], axis=1)


def _cmp_to_sel(nc_pad, nc, ns):
    c0 = jnp.arange(nc_pad)[:, None] * CMP_STRIDE
    s0 = jnp.arange(ns)[None, :] * SEL_BLK
    ov = jnp.minimum(c0 + CMP_LEN, s0 + SEL_BLK) - jnp.maximum(c0, s0)
    w = jnp.clip(ov, 0, None).astype(F32) / CMP_LEN
    return jnp.where(jnp.arange(nc_pad)[:, None] < nc, w, 0.0).astype(BF16)


def kernel(x, rel_bias, ln_attn_pre, w_in, nsa_cmp_pos, nsa_phi_w1, nsa_phi_b1, nsa_phi_w2, swa_sinks,
           grp_norm_g, w_out, ln_attn_post, ln_ffn_pre, ffn_w_gate, ffn_w_up, ffn_w_down, ln_ffn_post):
    b, s, d = x.shape
    depth = w_in.shape[0]
    nch = s // CMP_STRIDE
    ns = s // SEL_BLK
    t = b * s

    band_nsa_t = _band_table_t(rel_bias, N_BAND, NSA_HEADS, 0, True)
    band_swa_t = _band_table_t(rel_bias, SWA_WINDOW // BLK + 1, SWA_HEADS, NSA_HEADS, False)
    hct = _cmp_table(rel_bias, nch)
    wmapt = jnp.pad(_cmp_to_sel(nch, nch - 1, ns).T, ((0, IMP_PAD), (0, 0)))
    no_sinks = jnp.zeros((1, 1, 4 * BLK), F32)

    for l in range(depth):
        outs = _proj(x, ln_attn_pre[l][None], _permute_w_in(w_in[l]).astype(BF16), 512)
        qt, ks, vs, kw, vw, sqt, sk, sv, bq, bk, bv, kc, vc, gates = outs
        ckv = jnp.stack([kc.reshape(b, nch, CMP_STRIDE * HEAD_DIM), vc.reshape(b, nch, CMP_STRIDE * HEAD_DIM)])
        kcv, kcv_t = _compress(ckv, nsa_cmp_pos[l].reshape(2, 2, CMP_STRIDE * HEAD_DIM),
                               nsa_phi_w1[l].reshape(2, 2, CMP_STRIDE * HEAD_DIM, CMP_HIDDEN).astype(BF16),
                               nsa_phi_b1[l][:, None, :], nsa_phi_w2[l].astype(BF16))
        o_cmp, selmask = _cmp_attn(qt, kcv, kcv_t, hct, wmapt)
        o_sel = _sel_attn(qt, ks, vs, selmask, band_nsa_t)
        o_win = _band_attn(qt, kw[:, None], vw[:, None], band_nsa_t, no_sinks, NSA_WINDOW, False, True,
                           "win_attn")
        sinks = jnp.broadcast_to(swa_sinks[l].reshape(SWA_KV_HEADS, 1, 4, 1),
                                 (SWA_KV_HEADS, 1, 4, BLK)).reshape(SWA_KV_HEADS, 1, 4 * BLK)
        o_swa = _band_attn(sqt, sk, sv, band_swa_t, sinks, SWA_WINDOW, True, False, "swa_attn")
        o_sb = _sb_attn(bq, bk, bv)
        x = _out_proj(x, o_cmp, o_sel, o_win, gates, o_swa, o_sb, grp_norm_g[l][None], w_out[l].astype(BF16),
                      ln_attn_post[l][None], 512)
        x = _ffn(x.reshape(t, d), ln_ffn_pre[l][None], ffn_w_gate[l].astype(BF16), ffn_w_up[l].astype(BF16),
                 ffn_w_down[l].astype(BF16), ln_ffn_post[l][None], 512).reshape(b, s, d)
    return x
```

```python
import functools
import math

import jax
import jax.numpy as jnp
from jax import lax
from jax.experimental import pallas as pl
from jax.experimental.pallas import tpu as pltpu

F32 = jnp.float32
BF16 = jnp.bfloat16

HEAD_DIM = 64
BLK = 128
NSA_HEADS = 4
CMP_LEN = 32
CMP_STRIDE = 16
CMP_HIDDEN = 256
SEL_BLK = 64
N_SEL = 16
N_LOCAL = 2
NSA_WINDOW = 512
SWA_HEADS = 8
SWA_KV_HEADS = 2
SWA_WINDOW = 128
SB_HEADS = 4
N_BUCKETS = 32
T5_MAX_DISTANCE = 4096
RMS_EPS = 1e-6
FORCE_SCORE = 1e6
NEG = -1e30
LOG2E = 1.0 / math.log(2.0)
SEL_TILE = 512
SB_GROUP = 3
SB_QB = 4
SB_LOG2_FLOOR = -104.0 * LOG2E
PICKED = -3e38
CMP_QB = 4
VT_ROWS = HEAD_DIM + 16
OUT_ROWS = 128
SEL_QB = 2
FFN_CHUNK = 1024
BAND_QB = 4
CMP_CHUNK = 128
IMP_PAD = 16
IMP_ROWS = CMP_CHUNK // 4 + IMP_PAD
N_BAND = 25
VMEM_LIMIT = 56 * 1024 * 1024


def _cparams(sem):
    return pltpu.CompilerParams(dimension_semantics=sem, vmem_limit_bytes=VMEM_LIMIT)


def _rms(x, g):
    ms = jnp.mean(x * x, axis=-1, keepdims=True)
    return x * lax.rsqrt(ms + RMS_EPS) * g


def _iota(shape, axis):
    return lax.broadcasted_iota(jnp.int32, shape, axis)


def _bias_of_dist(dist, tab_ref, head):
    n = jnp.maximum(dist, 0)
    nf = jnp.maximum(n, 1).astype(F32)
    exact = N_BUCKETS // 2
    large = exact + (jnp.log(nf / exact) / math.log(T5_MAX_DISTANCE / exact)
                     * (N_BUCKETS - exact)).astype(jnp.int32)
    large = jnp.minimum(large, N_BUCKETS - 1)
    bucket = jnp.where(n < exact, n, large)
    val = jnp.full(dist.shape, tab_ref[0, head], F32)
    for k in range(1, N_BUCKETS):
        val = jnp.where(bucket == k, tab_ref[k, head], val)
    return val


def _band_table_t_kernel(tab_ref, o_ref, *, head_off, shift):
    h = pl.program_id(0) + head_off

    def tile(m, _):
        dist = BLK * m + _iota((BLK, BLK), 1) - _iota((BLK, BLK), 0)
        bias = _bias_of_dist(dist, tab_ref, h)
        o_ref[m] = (bias - tab_ref[N_BUCKETS - 1, h]) * LOG2E if shift else bias
        return 0

    lax.fori_loop(0, o_ref.shape[0], tile, 0)


def _band_table_t(rel_bias, n_band, n_heads, head_off, shift):
    return pl.pallas_call(
        functools.partial(_band_table_t_kernel, head_off=head_off, shift=shift),
        grid=(n_heads,),
        in_specs=[pl.BlockSpec(memory_space=pltpu.SMEM)],
        out_specs=pl.BlockSpec((n_band, BLK, BLK), lambda h: (0, 0, h)),
        out_shape=jax.ShapeDtypeStruct((n_band, BLK, n_heads * BLK), F32),
        compiler_params=_cparams(("arbitrary",)),
        name="band_table_t",
    )(rel_bias)


def _cmp_table_kernel(tab_ref, o_ref, *, ncp):
    h = pl.program_id(0)
    rel = _iota((2 * ncp, BLK), 0) - (ncp - 8)
    dist = _iota((2 * ncp, BLK), 1) - CMP_STRIDE * rel - (CMP_LEN - 1)
    o_ref[...] = _bias_of_dist(dist, tab_ref, h) * LOG2E


def _cmp_table(rel_bias, ncp):
    return pl.pallas_call(
        functools.partial(_cmp_table_kernel, ncp=ncp),
        grid=(NSA_HEADS,),
        in_specs=[pl.BlockSpec(memory_space=pltpu.SMEM)],
        out_specs=pl.BlockSpec((2 * ncp, BLK), lambda h: (0, h)),
        out_shape=jax.ShapeDtypeStruct((2 * ncp, NSA_HEADS * BLK), F32),
        compiler_params=_cparams(("arbitrary",)),
        name="cmp_table",
    )(rel_bias)


def _proj_kernel(x_ref, g_ref, w_ref, qt_ref, ks_ref, vs_ref, kw_ref, vw_ref, sqt_ref, sk_ref, sv_ref,
                 bq_ref, bk_ref, bv_ref, kc_ref, vc_ref, gt_ref):
    h = _rms(x_ref[0], g_ref[...]).astype(BF16)
    tm = x_ref.shape[1]

    def seg(a, b):
        return jnp.dot(h, w_ref[:, a:b], preferred_element_type=F32)

    def heads(ref, a, n):
        y = seg(a, a + n * HEAD_DIM).astype(BF16)
        for i in range(n):
            ref[0, i] = y[:, i * HEAD_DIM:(i + 1) * HEAD_DIM]

    def queries_t(ref, a, n, scale=None):
        yt = seg(a, a + n * HEAD_DIM).T
        if scale is not None:
            yt = yt * scale
        for j in range(tm // BLK):
            for i in range(n):
                ref[0, j, :, i * BLK:(i + 1) * BLK] = (
                    yt[i * HEAD_DIM:(i + 1) * HEAD_DIM, j * BLK:(j + 1) * BLK].astype(BF16))

    ones_rows = jnp.where(_iota((VT_ROWS - HEAD_DIM, tm), 0) == 0, 1.0, 0.0)

    def values_t(yt):
        return jnp.concatenate([yt, ones_rows], axis=0).astype(BF16)

    queries_t(qt_ref, 0, NSA_HEADS, LOG2E)
    y = seg(256, 512)
    vs_ref[0] = values_t(y[:, 0:128].T[64:128])
    vw_ref[0] = values_t(y[:, 128:256].T[64:128])
    y = y.astype(BF16)
    blk_in_tile = ((pl.program_id(1) * tm + _iota((tm, HEAD_DIM), 0)) & (SEL_TILE - 1)) // SEL_BLK
    onehot = jnp.where(_iota((tm, HEAD_DIM), 1) == blk_in_tile, 1.0, 0.0).astype(BF16)
    ks_ref[0] = jnp.concatenate([y[:, 0:64], onehot], axis=1)
    kw_ref[0] = y[:, 128:192]
    queries_t(sqt_ref, 512, SWA_HEADS)
    heads(sk_ref, 1024, SWA_KV_HEADS)
    yt = seg(1152, 1280).T
    for i in range(SWA_KV_HEADS):
        sv_ref[0, i] = values_t(yt[i * HEAD_DIM:(i + 1) * HEAD_DIM])
    queries_t(bq_ref, 1280, SB_HEADS, LOG2E)
    bk_ref[0] = seg(1536, 1792).astype(BF16)
    bv_ref[0] = seg(1792, 2048).T.astype(BF16)
    y = seg(2048, 2304)
    kc_ref[0] = y[:, 0:64]
    vc_ref[0] = y[:, 64:128]
    gt_ref[0] = y[:, 128:256]


def _proj(x, g, w, tm):
    b, s, d = x.shape
    nw = w.shape[1]
    hd = lambda n: pl.BlockSpec((1, n, tm, HEAD_DIM), lambda bi, i: (bi, 0, i, 0))
    flat = lambda n: pl.BlockSpec((1, tm, n), lambda bi, i: (bi, i, 0))
    hshape = lambda n: jax.ShapeDtypeStruct((b, n, s, HEAD_DIM), BF16)
    fshape = lambda n, dt: jax.ShapeDtypeStruct((b, s, n), dt)
    qt_spec = lambda n: pl.BlockSpec((1, tm // BLK, HEAD_DIM, n * BLK), lambda bi, i: (bi, i, 0, 0))
    qt_shape = lambda n: jax.ShapeDtypeStruct((b, s // BLK, HEAD_DIM, n * BLK), BF16)
    vt_spec = pl.BlockSpec((1, VT_ROWS, tm), lambda bi, i: (bi, 0, i))
    vt_shape = jax.ShapeDtypeStruct((b, VT_ROWS, s), BF16)
    return pl.pallas_call(
        _proj_kernel,
        grid=(b, s // tm),
        in_specs=[pl.BlockSpec((1, tm, d), lambda bi, i: (bi, i, 0)),
                  pl.BlockSpec((1, d), lambda bi, i: (0, 0)),
                  pl.BlockSpec((d, nw), lambda bi, i: (0, 0))],
        out_specs=[qt_spec(NSA_HEADS), flat(128), vt_spec, flat(64), vt_spec,
                   qt_spec(SWA_HEADS), hd(SWA_KV_HEADS),
                   pl.BlockSpec((1, SWA_KV_HEADS, VT_ROWS, tm), lambda bi, i: (bi, 0, 0, i)),
                   qt_spec(SB_HEADS), flat(256),
                   pl.BlockSpec((1, SB_HEADS * HEAD_DIM, tm), lambda bi, i: (bi, 0, i)),
                   flat(64), flat(64), flat(128)],
        out_shape=[qt_shape(NSA_HEADS), fshape(128, BF16), vt_shape, fshape(64, BF16), vt_shape,
                   qt_shape(SWA_HEADS), hshape(SWA_KV_HEADS),
                   jax.ShapeDtypeStruct((b, SWA_KV_HEADS, VT_ROWS, s), BF16),
                   qt_shape(SB_HEADS), fshape(256, BF16),
                   jax.ShapeDtypeStruct((b, SB_HEADS * HEAD_DIM, s), BF16),
                   fshape(64, F32), fshape(64, F32), fshape(128, F32)],
        compiler_params=_cparams(("parallel", "parallel")),
        name="proj",
    )(x, g, w)


def _compress_kernel(c_ref, pos_ref, w1_ref, b1_ref, w2_ref, o_ref, ot_ref):
    c = c_ref[0, 0]
    nch = c.shape[0]
    xa = (c + pos_ref[0, 0:1]).astype(BF16)
    xb = (c + pos_ref[0, 1:2]).astype(BF16)
    p = jnp.dot(xa, w1_ref[0, 0], preferred_element_type=F32)
    q = jnp.dot(xb, w1_ref[0, 1], preferred_element_type=F32)
    hid = p + pltpu.roll(q, nch - 1, 0) + b1_ref[0]
    hid = hid * jax.nn.sigmoid(hid)
    out = jnp.dot(hid.astype(BF16), w2_ref[0], preferred_element_type=F32)
    row = _iota(out.shape, 0)
    out = jnp.where(row < nch - 1, out, 0.0)
    o_ref[0, 0] = out.astype(BF16)
    ot_ref[0, 0] = jnp.concatenate([out, jnp.zeros_like(out)], axis=1).T[:HEAD_DIM].astype(BF16)


def _compress(ckv, pos, w1, b1, w2):
    _, b, nch, cw = ckv.shape
    return pl.pallas_call(
        _compress_kernel,
        grid=(2, b),
        in_specs=[pl.BlockSpec((1, 1, nch, cw), lambda j, bi: (j, bi, 0, 0)),
                  pl.BlockSpec((1, 2, cw), lambda j, bi: (j, 0, 0)),
                  pl.BlockSpec((1, 2, cw, CMP_HIDDEN), lambda j, bi: (j, 0, 0, 0)),
                  pl.BlockSpec((1, 1, CMP_HIDDEN), lambda j, bi: (j, 0, 0)),
                  pl.BlockSpec((1, CMP_HIDDEN, HEAD_DIM), lambda j, bi: (j, 0, 0))],
        out_specs=[pl.BlockSpec((1, 1, nch, HEAD_DIM), lambda j, bi: (j, bi, 0, 0)),
                   pl.BlockSpec((1, 1, HEAD_DIM, nch), lambda j, bi: (j, bi, 0, 0))],
        out_shape=[jax.ShapeDtypeStruct((2, b, nch, HEAD_DIM), BF16),
                   jax.ShapeDtypeStruct((2, b, HEAD_DIM, nch), BF16)],
        compiler_params=_cparams(("parallel", "parallel")),
        name="compress",
    )(ckv, pos, w1, b1, w2)


def _transpose_pad(x, axis):
    pad = jnp.zeros(x.shape, x.dtype)
    return jnp.concatenate([x, pad], axis=axis).T


def _cmp_kernel(qt_ref, kc_ref, vct_ref, hct_ref, wmapt_ref, o_ref, pen_ref, s_scr, oacc_scr, iacc_scr):
    step = pl.program_id(1)
    ncp = kc_ref.shape[2]
    ns = wmapt_ref.shape[0] - IMP_PAD
    lanes = NSA_HEADS * BLK
    row_c = _iota((CMP_CHUNK, lanes), 0)
    lane_q = _iota((CMP_CHUNK, lanes), 1) & (BLK - 1)
    n_chunks = (CMP_QB * step) // (CMP_CHUNK // 8) + 1

    def score_chunk(ch, mxs):
        c0 = pl.multiple_of(ch * CMP_CHUNK, CMP_CHUNK)
        kc = kc_ref[0, 0, pl.ds(c0, CMP_CHUNK), :]
        raw = [jnp.dot(kc, qt_ref[0, qb], preferred_element_type=F32) for qb in range(CMP_QB)]
        out = []
        for qb in range(CMP_QB):
            n = CMP_QB * step + qb
            bias = hct_ref[pl.ds(pl.multiple_of(ncp - 8 - 8 * n + c0, 8), CMP_CHUNK), :]
            visible = (BLK * n + lane_q - CMP_STRIDE * (c0 + row_c) - (CMP_LEN - 1)) >= 0
            s = jnp.where(visible, raw[qb] + bias, NEG)
            s_scr[qb, ch] = s
            out.append(jnp.maximum(mxs[qb], jnp.max(s, axis=0, keepdims=True)))
        return tuple(out)

    mxs = lax.fori_loop(0, n_chunks, score_chunk, tuple(jnp.full((1, lanes), NEG, F32) for _ in range(CMP_QB)))
    ms = [jnp.where(mx <= 0.5 * NEG, 0.0, mx) for mx in mxs]
    oacc_scr[...] = jnp.zeros(oacc_scr.shape, F32)
    iacc_scr[...] = jnp.zeros(iacc_scr.shape, F32)

    def prob_chunk(ch, ls):
        c0 = pl.multiple_of(ch * CMP_CHUNK, CMP_CHUNK)
        vct = vct_ref[0, 0, :, pl.ds(c0, CMP_CHUNK)]
        r0 = pl.multiple_of(ch * (CMP_CHUNK // 4), 16)
        wmt = wmapt_ref[pl.ds(r0, IMP_ROWS), pl.ds(c0, CMP_CHUNK)]
        both = jnp.concatenate([vct, wmt], axis=0)
        ps = [jnp.exp2(s_scr[qb, ch] - ms[qb]) for qb in range(CMP_QB)]
        prods = [jnp.dot(both, p.astype(BF16), preferred_element_type=F32) for p in ps]
        for qb in range(CMP_QB):
            oacc_scr[qb] += prods[qb][:HEAD_DIM]
            iacc_scr[qb, pl.ds(r0, IMP_ROWS), :] += prods[qb][HEAD_DIM:]
        return tuple(ls[qb] + jnp.sum(ps[qb], axis=0, keepdims=True) for qb in range(CMP_QB))

    ls = lax.fori_loop(0, n_chunks, prob_chunk, tuple(jnp.zeros((1, lanes), F32) for _ in range(CMP_QB)))
    imps = []
    for qb in range(CMP_QB):
        inv = 1.0 / jnp.maximum(ls[qb], 1e-30)
        o_t = oacc_scr[qb] * inv
        o_ref[0, qb * BLK:(qb + 1) * BLK, :] = jnp.concatenate(
            [_transpose_pad(o_t[:, h * BLK:(h + 1) * BLK], 0)[:, :HEAD_DIM] for h in range(NSA_HEADS)], axis=1)
        w = iacc_scr[qb, 0:ns, :] * inv
        imps.append(functools.reduce(jnp.add, [w[:, h * BLK:(h + 1) * BLK] for h in range(NSA_HEADS)]))

    imp = jnp.concatenate(imps, axis=1)
    shape = (ns, CMP_QB * BLK)
    j = _iota(shape, 0)
    cur = (CMP_QB * BLK * step + _iota(shape, 1)) // SEL_BLK
    valid = j <= cur
    forced = valid & ((j == 0) | (j > cur - N_LOCAL))
    score = jnp.where(forced, PICKED, jnp.where(valid, imp, NEG))
    jf = j.astype(F32)
    for _ in range(min(N_SEL, ns) - (N_LOCAL + 1)):
        best = jnp.max(score, axis=0, keepdims=True)
        first = jnp.min(jnp.where(score == best, jf, float(ns)), axis=0, keepdims=True)
        score = jnp.where(jf == first, PICKED, score)
    pen_ref[0, 0] = jnp.where(valid & (score == PICKED), 0.0, NEG)


def _cmp_attn(qt, kcv, kcv_t, hct, wmapt):
    b, nb, _, lanes = qt.shape
    ncp = kcv.shape[2]
    ns = wmapt.shape[0] - IMP_PAD
    s = nb * BLK
    return pl.pallas_call(
        _cmp_kernel,
        grid=(b, nb // CMP_QB),
        in_specs=[pl.BlockSpec((1, CMP_QB, HEAD_DIM, lanes), lambda bi, i: (bi, i, 0, 0)),
                  pl.BlockSpec((1, 1, ncp, HEAD_DIM), lambda bi, i: (0, bi, 0, 0)),
                  pl.BlockSpec((1, 1, HEAD_DIM, ncp), lambda bi, i: (1, bi, 0, 0)),
                  pl.BlockSpec((2 * ncp, lanes), lambda bi, i: (0, 0)),
                  pl.BlockSpec((ns + IMP_PAD, ncp), lambda bi, i: (0, 0))],
        out_specs=[pl.BlockSpec((1, CMP_QB * BLK, NSA_HEADS * HEAD_DIM), lambda bi, i: (bi, i, 0)),
                   pl.BlockSpec((1, 1, ns, CMP_QB * BLK), lambda bi, i: (bi, i, 0, 0))],
        out_shape=[jax.ShapeDtypeStruct((b, s, NSA_HEADS * HEAD_DIM), F32),
                   jax.ShapeDtypeStruct((b, nb // CMP_QB, ns, CMP_QB * BLK), F32)],
        scratch_shapes=[pltpu.VMEM((CMP_QB, ncp // CMP_CHUNK, CMP_CHUNK, lanes), F32),
                        pltpu.VMEM((CMP_QB, HEAD_DIM, lanes), F32),
                        pltpu.VMEM((CMP_QB, ns + IMP_PAD, lanes), F32)],
        compiler_params=_cparams(("parallel", "arbitrary")),
        name="cmp_attn",
    )(qt, kcv, kcv_t, hct, wmapt)


def _sel_kernel(qt_ref, k_ref, vt_ref, pen_ref, band_ref, o_ref, s_scr, p_scr, mx_scr, m_scr, alpha_scr, acc_scr):
    n = SEL_QB * pl.program_id(1)
    qlanes = NSA_HEADS * BLK
    lanes = SEL_QB * qlanes
    sub = SEL_TILE // BLK
    per = BLK // SEL_BLK
    qt = jnp.concatenate([qt_ref[0, qb] for qb in range(SEL_QB)], axis=1)
    lane = _iota((BLK, lanes), 1)
    t = BLK * (n + lane // qlanes) + (lane & (BLK - 1))

    def scores_into(kt, s_buf, mx_buf, far):
        rows = sub * per
        pen = pen_ref[0, 0, pl.ds(pl.multiple_of(kt * rows, rows), rows), :]
        pen = jnp.concatenate([pen[:, qb * BLK:(qb + 1) * BLK] for qb in range(SEL_QB) for _ in range(NSA_HEADS)],
                              axis=1)
        tail = jnp.concatenate([pen, jnp.zeros((HEAD_DIM - rows, lanes), F32)], axis=0).astype(BF16)
        q_aug = jnp.concatenate([qt, tail], axis=0)
        k0 = pl.multiple_of(kt * SEL_TILE, SEL_TILE)
        s = jnp.dot(k_ref[0, pl.ds(k0, SEL_TILE), :], q_aug, preferred_element_type=F32)
        if far:
            s_buf[...] = s
            mx_buf[...] = jnp.max(s, axis=0, keepdims=True)
        else:
            mxs = []
            for qb in range(SEL_QB):
                cols = slice(qb * qlanes, (qb + 1) * qlanes)
                parts = [s[u * BLK:(u + 1) * BLK, cols] + band_ref[jnp.clip(n + qb - sub * kt - u, 0, N_BAND - 1)]
                         for u in range(sub)]
                for u in range(sub):
                    s_buf[u * BLK:(u + 1) * BLK, cols] = parts[u]
                mxs.append(functools.reduce(jnp.maximum, [jnp.max(v, axis=0, keepdims=True) for v in parts]))
            mx_buf[...] = jnp.concatenate(mxs, axis=1)

    def add_values(kt, p_buf):
        k0 = pl.multiple_of(kt * SEL_TILE, SEL_TILE)
        pv = jnp.dot(vt_ref[0, :, pl.ds(k0, SEL_TILE)], p_buf[...], preferred_element_type=F32)
        acc_scr[...] = alpha_scr[...] * acc_scr[...] + pv

    def softmax_into(kt, s_buf, mx_buf, p_buf, causal):
        def piece(u):
            s = s_buf[u * BLK:(u + 1) * BLK, :]
            if causal:
                s = jnp.where(kt * SEL_TILE + u * BLK + _iota((BLK, lanes), 0) <= t, s, NEG)
            return s
        m_i = m_scr[...]
        if causal:
            mx = functools.reduce(jnp.maximum, [jnp.max(piece(u), axis=0, keepdims=True) for u in range(sub)])
        else:
            mx = mx_buf[...]
        m_new = jnp.maximum(m_i, mx)
        for u in range(sub):
            p_buf[u * BLK:(u + 1) * BLK, :] = jnp.exp2((piece(u) - m_new).astype(BF16))
        m_scr[...] = m_new
        alpha_scr[...] = jnp.exp2(m_i - m_new)

    def stage(i, cur, nxt, p_cur, p_prev, far):
        add_values(jnp.maximum(i - 1, 0), p_prev)
        scores_into(i + 1, *nxt, far)
        softmax_into(i, *cur, p_cur, False)

    def finish(i, cur, p_cur, p_prev):
        add_values(jnp.maximum(i - 1, 0), p_prev)
        softmax_into(i, *cur, p_cur, True)
        add_values(i, p_cur)
        acc = acc_scr[...]
        o_t = acc[:HEAD_DIM] / acc[HEAD_DIM:HEAD_DIM + 1]
        for qb in range(SEL_QB):
            o_ref[0, qb * BLK:(qb + 1) * BLK, :] = jnp.concatenate(
                [_transpose_pad(o_t[:, qb * qlanes + h * BLK:qb * qlanes + (h + 1) * BLK], 0)[:, :HEAD_DIM]
                 for h in range(NSA_HEADS)], axis=1)

    s0, s1 = (s_scr.at[0], mx_scr.at[0]), (s_scr.at[1], mx_scr.at[1])
    p0, p1 = p_scr.at[0], p_scr.at[1]
    m_scr[...] = jnp.full((1, lanes), NEG, F32)
    alpha_scr[...] = jnp.ones((1, lanes), F32)
    acc_scr[...] = jnp.zeros(acc_scr.shape, F32)
    p1[...] = jnp.zeros((SEL_TILE, lanes), BF16)
    scores_into(0, *s0, False)
    last = (BLK * n + BLK - 1) // SEL_TILE
    n_far = jnp.maximum((n - (N_BAND - 2)) // sub, 0)
    far_pairs = jnp.minimum(jnp.maximum((n_far - 1) // 2, 0), last // 2)

    def pair(j, _, far):
        stage(2 * j, s0, s1, p0, p1, far)
        stage(2 * j + 1, s1, s0, p1, p0, far)
        return 0

    lax.fori_loop(0, far_pairs, functools.partial(pair, far=True), 0)
    lax.fori_loop(far_pairs, last // 2, functools.partial(pair, far=False), 0)

    @pl.when(last % 2 == 1)
    def _():
        stage(last - 1, s0, s1, p0, p1, False)
        finish(last, s1, p1, p0)

    @pl.when(last % 2 == 0)
    def _():
        finish(last, s0, p0, p1)


def _sel_attn(qt, ks, vst, pen, band_t):
    b, nb, _, qlanes = qt.shape
    s = nb * BLK
    ns = pen.shape[2]
    lanes = SEL_QB * qlanes
    per_cmp = CMP_QB // SEL_QB
    return pl.pallas_call(
        _sel_kernel,
        grid=(b, nb // SEL_QB),
        in_specs=[pl.BlockSpec((1, SEL_QB, HEAD_DIM, qlanes), lambda bi, i: (bi, i, 0, 0)),
                  pl.BlockSpec((1, s, 2 * HEAD_DIM), lambda bi, i: (bi, 0, 0)),
                  pl.BlockSpec((1, VT_ROWS, s), lambda bi, i: (bi, 0, 0)),
                  pl.BlockSpec((1, 1, ns, SEL_QB * BLK), lambda bi, i: (bi, i // per_cmp, 0, i % per_cmp)),
                  pl.BlockSpec((N_BAND, BLK, qlanes), lambda bi, i: (0, 0, 0))],
        out_specs=pl.BlockSpec((1, SEL_QB * BLK, NSA_HEADS * HEAD_DIM), lambda bi, i: (bi, i, 0)),
        out_shape=jax.ShapeDtypeStruct((b, s, NSA_HEADS * HEAD_DIM), F32),
        scratch_shapes=[pltpu.VMEM((2, SEL_TILE, lanes), F32),
                        pltpu.VMEM((2, SEL_TILE, lanes), BF16),
                        pltpu.VMEM((2, 1, lanes), F32),
                        pltpu.VMEM((1, lanes), F32),
                        pltpu.VMEM((1, lanes), F32),
                        pltpu.VMEM((VT_ROWS, lanes), F32)],
        compiler_params=_cparams(("parallel", "arbitrary")),
        name="sel_attn",
    )(qt, ks, vst, pen, band_t)


def _band_kernel(*refs, window, groups, use_sinks, base2, n_parts):
    qt_ref = refs[0]
    k_refs = refs[1:1 + n_parts]
    vt_refs = refs[1 + n_parts:1 + 2 * n_parts]
    band_ref, sink_ref, o_ref = refs[1 + 2 * n_parts:]
    step = pl.program_id(1)
    n_prev = window // BLK
    nk = (n_prev + 1) * BLK
    lanes = 4 * BLK
    row = _iota((nk, lanes), 0)
    dist = n_prev * BLK + (_iota((nk, lanes), 1) & (BLK - 1)) - row
    in_window = (dist >= 0) & (dist < window)
    chains = [(g, qb) for g in range(groups) for qb in range(BAND_QB)]
    kwin = [jnp.concatenate([r[0, g] for r in k_refs], axis=0) for g in range(groups)]
    vtwin = [jnp.concatenate([r[0, g] for r in vt_refs], axis=1) for g in range(groups)]
    bias = [jnp.concatenate([band_ref[n_prev - u, :, g * lanes:(g + 1) * lanes] for u in range(n_prev + 1)], axis=0)
            for g in range(groups)]
    scores = [jnp.dot(kwin[g][qb * BLK:qb * BLK + nk], qt_ref[0, qb, :, g * lanes:(g + 1) * lanes],
                      preferred_element_type=F32) for g, qb in chains]
    probs, maxes = [], []
    for (g, qb), s in zip(chains, scores):
        s = jnp.where(in_window & (row >= (n_prev - BAND_QB * step - qb) * BLK), s + bias[g], NEG)
        mx = jnp.max(s, axis=0, keepdims=True)
        if use_sinks:
            mx = jnp.maximum(mx, sink_ref[g])
        probs.append((jnp.exp2(s - mx) if base2 else jnp.exp(s - mx)).astype(BF16))
        maxes.append(mx)
    accs = [jnp.dot(vtwin[g][:, qb * BLK:qb * BLK + nk], p, preferred_element_type=F32)
            for (g, qb), p in zip(chains, probs)]
    for (g, qb), acc, mx in zip(chains, accs, maxes):
        den = acc[HEAD_DIM:HEAD_DIM + 1]
        if use_sinks:
            den = den + jnp.exp(sink_ref[g] - mx)
        o_t = acc[:HEAD_DIM] / den
        o_ref[0, qb * BLK:(qb + 1) * BLK, 4 * g * HEAD_DIM:4 * (g + 1) * HEAD_DIM] = jnp.concatenate(
            [_transpose_pad(o_t[:, h * BLK:(h + 1) * BLK], 0)[:, :HEAD_DIM] for h in range(4)], axis=1)


def _band_attn(qt, k, vt, band_t, sinks, window, use_sinks, base2, name):
    b, nb, _, width = qt.shape
    g = width // (4 * BLK)
    s = nb * BLK
    n_prev = window // BLK
    tq = BAND_QB * BLK
    if n_prev * BLK % tq == 0:
        piece = tq
        starts = [lambda i, d=d: jnp.maximum(i - d, 0) for d in range(n_prev * BLK // tq, 0, -1)]
    else:
        assert n_prev == 1
        piece = BLK
        starts = [lambda i: jnp.maximum(BAND_QB * i - 1, 0)]
    k_specs = [pl.BlockSpec((1, g, piece, HEAD_DIM), lambda bi, i, f=f: (bi, 0, f(i), 0)) for f in starts]
    k_specs.append(pl.BlockSpec((1, g, tq, HEAD_DIM), lambda bi, i: (bi, 0, i, 0)))
    vt_specs = [pl.BlockSpec((1, g, VT_ROWS, piece), lambda bi, i, f=f: (bi, 0, 0, f(i))) for f in starts]
    vt_specs.append(pl.BlockSpec((1, g, VT_ROWS, tq), lambda bi, i: (bi, 0, 0, i)))
    n_parts = len(k_specs)
    return pl.pallas_call(
        functools.partial(_band_kernel, window=window, groups=g, use_sinks=use_sinks, base2=base2,
                          n_parts=n_parts),
        grid=(b, s // tq),
        in_specs=[pl.BlockSpec((1, BAND_QB, HEAD_DIM, width), lambda bi, i: (bi, i, 0, 0))] + k_specs + vt_specs
                 + [pl.BlockSpec((n_prev + 1, BLK, width), lambda bi, i: (0, 0, 0)),
                    pl.BlockSpec((g, 1, 4 * BLK), lambda bi, i: (0, 0, 0))],
        out_specs=pl.BlockSpec((1, tq, g * 4 * HEAD_DIM), lambda bi, i: (bi, i, 0)),
        out_shape=jax.ShapeDtypeStruct((b, s, g * 4 * HEAD_DIM), F32),
        compiler_params=_cparams(("parallel", "arbitrary")),
        name=name,
    )(qt, *([k] * n_parts), *([vt] * n_parts), band_t, sinks)


def _sb_kernel(qt_ref, k_ref, vt_ref, later_ref, o_ref):
    n0 = SB_QB * pl.program_id(1)
    lanes = SB_HEADS * BLK
    lane_q = _iota((BLK, lanes), 1) & (BLK - 1)
    row = _iota((BLK, lanes), 0)
    later = later_ref[...]
    zero = jnp.zeros((HEAD_DIM, BLK), BF16)
    blocks = range(SB_QB)
    q_bd = [jnp.concatenate(
        [jnp.concatenate([zero] * h + [qt_ref[0, qb, :, h * BLK:(h + 1) * BLK]] + [zero] * (SB_HEADS - 1 - h), axis=0)
         for h in range(SB_HEADS)], axis=1) for qb in blocks]

    def cond(c):
        it, carry, _ = c
        return (n0 + SB_QB - 1 - SB_GROUP * it >= 0) & (jnp.max(functools.reduce(jnp.maximum, carry)) > SB_LOG2_FLOOR)

    def body(c):
        it, carry, acc = c
        k_all, vt_all, before = [], [], []
        for qb in blocks:
            top = n0 + qb - SB_GROUP * it
            ks, vts, befores = [], [], []
            for j in range(SB_GROUP - 1, -1, -1):
                kb = top - j
                k0 = pl.multiple_of(jnp.maximum(kb, 0) * BLK, BLK)
                ks.append(k_ref[0, pl.ds(k0, BLK), :])
                vts.append(vt_ref[0, :, pl.ds(k0, BLK)])
                befores.append((kb >= 0) & (k0 + row < BLK * (n0 + qb) + lane_q))
            k_all.append(jnp.concatenate(ks, axis=0))
            vt_all.append(jnp.concatenate(vts, axis=1))
            before.append(jnp.concatenate(befores, axis=0))
        z = [jnp.dot(k_all[qb], q_bd[qb], preferred_element_type=F32) for qb in blocks]
        log_not, split = [], []
        for qb in blocks:
            softplus = jnp.maximum(z[qb], 0.0) + jnp.log2(1.0 + jnp.exp2(-jnp.abs(z[qb])))
            ln = jnp.where(before[qb], -softplus, 0.0)
            hi = ln.astype(BF16)
            lo = (ln - hi.astype(F32)).astype(BF16)
            log_not.append(ln)
            split.append(jnp.concatenate([hi, lo], axis=0))
        tail = [jnp.dot(later, split[qb], preferred_element_type=F32) for qb in blocks]
        a = [jnp.where(before[qb], jnp.exp2(z[qb] + log_not[qb] + tail[qb] + carry[qb]), 0.0).astype(BF16)
             for qb in blocks]
        pv = [jnp.dot(vt_all[qb], a[qb], preferred_element_type=F32) for qb in blocks]
        acc = tuple(tuple(acc[qb][h] + pv[qb][h * HEAD_DIM:(h + 1) * HEAD_DIM, h * BLK:(h + 1) * BLK]
                          for h in range(SB_HEADS)) for qb in blocks)
        carry = tuple(carry[qb] + jnp.sum(log_not[qb], axis=0, keepdims=True) for qb in blocks)
        return it + 1, carry, acc

    init = (0, tuple(jnp.zeros((1, lanes), F32) for _ in blocks),
            tuple(tuple(jnp.zeros((HEAD_DIM, BLK), F32) for _ in range(SB_HEADS)) for _ in blocks))
    _, _, acc = lax.while_loop(cond, body, init)
    for qb in blocks:
        o_ref[0, qb * BLK:(qb + 1) * BLK, :] = jnp.concatenate(
            [_transpose_pad(a, 0)[:, :HEAD_DIM] for a in acc[qb]], axis=1)


def _sb_attn(qt, k, vt):
    b, nb, _, lanes = qt.shape
    s = nb * BLK
    width = SB_HEADS * HEAD_DIM
    gk = SB_GROUP * BLK
    later = (jnp.arange(2 * gk)[None, :] % gk > jnp.arange(gk)[:, None]).astype(BF16)
    return pl.pallas_call(
        _sb_kernel,
        grid=(b, nb // SB_QB),
        in_specs=[pl.BlockSpec((1, SB_QB, HEAD_DIM, lanes), lambda bi, n: (bi, n, 0, 0)),
                  pl.BlockSpec((1, s, width), lambda bi, n: (bi, 0, 0)),
                  pl.BlockSpec((1, width, s), lambda bi, n: (bi, 0, 0)),
                  pl.BlockSpec((gk, 2 * gk), lambda bi, n: (0, 0))],
        out_specs=pl.BlockSpec((1, SB_QB * BLK, width), lambda bi, n: (bi, n, 0)),
        out_shape=jax.ShapeDtypeStruct((b, s, width), F32),
        compiler_params=_cparams(("parallel", "arbitrary")),
        name="sb_attn",
    )(qt, k, vt, later)


def _out_kernel(x_ref, oc_ref, os_ref, ow_ref, gt_ref, swa_ref, sb_ref, gg_ref, w_ref, gp_ref, o_ref):
    wa = NSA_HEADS * HEAD_DIM
    wb = wa + SWA_HEADS * HEAD_DIM
    gg = gg_ref[...]
    lane = _iota((OUT_ROWS, wa), 1)
    groups = [slice(r0, r0 + OUT_ROWS) for r0 in range(0, x_ref.shape[1], OUT_ROWS)]
    mixes = []
    for rows in groups:
        gates = jax.nn.sigmoid(gt_ref[0, rows])

        def spread(branch, gates=gates):
            cols = [jnp.broadcast_to(gates[:, branch * NSA_HEADS + i:branch * NSA_HEADS + i + 1], (OUT_ROWS, wa))
                    for i in range(NSA_HEADS)]
            out = cols[NSA_HEADS - 1]
            for i in range(NSA_HEADS - 2, -1, -1):
                out = jnp.where(lane < (i + 1) * HEAD_DIM, cols[i], out)
            return out

        o_nsa = spread(0) * oc_ref[0, rows] + spread(1) * os_ref[0, rows] + spread(2) * ow_ref[0, rows]
        mixes.append(jnp.concatenate([_rms(o_nsa, gg[:, :wa]), _rms(swa_ref[0, rows], gg[:, wa:wb]),
                                      _rms(sb_ref[0, rows], gg[:, wb:])], axis=1).astype(BF16))
    ys = [jnp.dot(mix, w_ref[...], preferred_element_type=F32) for mix in mixes]
    for rows, y in zip(groups, ys):
        o_ref[0, rows] = x_ref[0, rows] + _rms(y, gp_ref[...])


def _out_proj(x, o_cmp, o_sel, o_win, gates, o_swa, o_sb, gg, w, gp, tm):
    b, s, d = x.shape
    row = lambda n: pl.BlockSpec((1, tm, n), lambda bi, i: (bi, i, 0))
    const = lambda a, c: pl.BlockSpec((a, c), lambda bi, i: (0, 0))
    return pl.pallas_call(
        _out_kernel,
        grid=(b, s // tm),
        in_specs=[row(d), row(256), row(256), row(256), row(128), row(512), row(256),
                  const(1, d), const(d, d), const(1, d)],
        out_specs=row(d),
        out_shape=jax.ShapeDtypeStruct((b, s, d), F32),
        compiler_params=_cparams(("parallel", "parallel")),
        name="out_proj",
    )(x, o_cmp, o_sel, o_win, gates, o_swa, o_sb, gg, w, gp)


def _ffn_kernel(x_ref, g1_ref, wg_ref, wu_ref, wd_ref, g2_ref, o_ref, *, chunk):
    x = x_ref[...]
    h = _rms(x, g1_ref[...]).astype(BF16)
    dff = wg_ref.shape[1]
    f = jnp.zeros(x.shape, F32)
    for a0 in range(0, dff, chunk):
        a1 = min(a0 + chunk, dff)
        gate = jnp.dot(h, wg_ref[:, a0:a1], preferred_element_type=F32)
        up = jnp.dot(h, wu_ref[:, a0:a1], preferred_element_type=F32)
        a = (gate * jax.nn.sigmoid(gate) * up).astype(BF16)
        f = f + jnp.dot(a, wd_ref[a0:a1, :], preferred_element_type=F32)
    o_ref[...] = x + _rms(f, g2_ref[...])


def _ffn(x, g1, wg, wu, wd, g2, tm):
    t, d = x.shape
    dff = wg.shape[1]
    row = pl.BlockSpec((tm, d), lambda i: (i, 0))
    const = lambda a, c: pl.BlockSpec((a, c), lambda i: (0, 0))
    weight = lambda a, c: pl.BlockSpec((a, c), lambda i: (0, 0), pipeline_mode=pl.Buffered(1))
    return pl.pallas_call(
        functools.partial(_ffn_kernel, chunk=FFN_CHUNK),
        grid=(t // tm,),
        in_specs=[row, const(1, d), weight(d, dff), weight(d, dff), weight(dff, d), const(1, d)],
        out_specs=row,
        out_shape=jax.ShapeDtypeStruct((t, d), F32),
        compiler_params=_cparams(("parallel",)),
        name="ffn",
    )(x, g1, wg, wu, wd, g2)


def _permute_w_in(w):
    scale = 1.0 / math.sqrt(HEAD_DIM)
    nq, kc, vc, rest, ng = w[:, :256], w[:, 256:320], w[:, 320:384], w[:, 384:640], w[:, 640:652]
    sq, skv = w[:, 652:1164], w[:, 1164:1420]
    bq, bkv = w[:, 1420:1676], w[:, 1676:2188]
    pad = jnp.zeros((w.shape[0], 128 - ng.shape[1]), w.dtype)
    return jnp.concatenate([nq * scale, rest, sq * scale, skv, bq * scale, bkv, kc, vc, ng, pad], axis=1)


def _cmp_to_sel(nc_pad, nc, ns):
    c0 = jnp.arange(nc_pad)[:, None] * CMP_STRIDE
    s0 = jnp.arange(ns)[None, :] * SEL_BLK
    ov = jnp.minimum(c0 + CMP_LEN, s0 + SEL_BLK) - jnp.maximum(c0, s0)
    w = jnp.clip(ov, 0, None).astype(F32) / CMP_LEN
    return jnp.where(jnp.arange(nc_pad)[:, None] < nc, w, 0.0).astype(BF16)


def kernel(x, rel_bias, ln_attn_pre, w_in, nsa_cmp_pos, nsa_phi_w1, nsa_phi_b1, nsa_phi_w2, swa_sinks,
           grp_norm_g, w_out, ln_attn_post, ln_ffn_pre, ffn_w_gate, ffn_w_up, ffn_w_down, ln_ffn_post):
    b, s, d = x.shape
    depth = w_in.shape[0]
    nch = s // CMP_STRIDE
    ns = s // SEL_BLK
    t = b * s

    band_nsa_t = _band_table_t(rel_bias, N_BAND, NSA_HEADS, 0, True)
    band_swa_t = _band_table_t(rel_bias, SWA_WINDOW // BLK + 1, SWA_HEADS, NSA_HEADS, False)
    hct = _cmp_table(rel_bias, nch)
    wmapt = jnp.pad(_cmp_to_sel(nch, nch - 1, ns).T, ((0, IMP_PAD), (0, 0)))
    no_sinks = jnp.zeros((1, 1, 4 * BLK), F32)

    for l in range(depth):
        outs = _proj(x, ln_attn_pre[l][None], _permute_w_in(w_in[l]).astype(BF16), 512)
        qt, ks, vs, kw, vw, sqt, sk, sv, bq, bk, bv, kc, vc, gates = outs
        ckv = jnp.stack([kc.reshape(b, nch, CMP_STRIDE * HEAD_DIM), vc.reshape(b, nch, CMP_STRIDE * HEAD_DIM)])
        kcv, kcv_t = _compress(ckv, nsa_cmp_pos[l].reshape(2, 2, CMP_STRIDE * HEAD_DIM),
                               nsa_phi_w1[l].reshape(2, 2, CMP_STRIDE * HEAD_DIM, CMP_HIDDEN).astype(BF16),
                               nsa_phi_b1[l][:, None, :], nsa_phi_w2[l].astype(BF16))
        o_cmp, selmask = _cmp_attn(qt, kcv, kcv_t, hct, wmapt)
        o_sel = _sel_attn(qt, ks, vs, selmask, band_nsa_t)
        o_win = _band_attn(qt, kw[:, None], vw[:, None], band_nsa_t, no_sinks, NSA_WINDOW, False, True,
                           "win_attn")
        sinks = jnp.broadcast_to(swa_sinks[l].reshape(SWA_KV_HEADS, 1, 4, 1),
                                 (SWA_KV_HEADS, 1, 4, BLK)).reshape(SWA_KV_HEADS, 1, 4 * BLK)
        o_swa = _band_attn(sqt, sk, sv, band_swa_t, sinks, SWA_WINDOW, True, False, "swa_attn")
        o_sb = _sb_attn(bq, bk, bv)
        x = _out_proj(x, o_cmp, o_sel, o_win, gates, o_swa, o_sb, grp_norm_g[l][None], w_out[l].astype(BF16),
                      ln_attn_post[l][None], 512)
        x = _ffn(x.reshape(t, d), ln_ffn_pre[l][None], ffn_w_gate[l].astype(BF16), ffn_w_up[l].astype(BF16),
                 ffn_w_down[l].astype(BF16), ln_ffn_post[l][None], 512).reshape(b, s, d)
    return x
```

```python
import functools
import math

import jax
import jax.numpy as jnp
from jax import lax
from jax.experimental import pallas as pl
from jax.experimental.pallas import tpu as pltpu

F32 = jnp.float32
BF16 = jnp.bfloat16

HEAD_DIM = 64
BLK = 128
NSA_HEADS = 4
CMP_LEN = 32
CMP_STRIDE = 16
CMP_HIDDEN = 256
SEL_BLK = 64
N_SEL = 16
N_LOCAL = 2
NSA_WINDOW = 512
SWA_HEADS = 8
SWA_KV_HEADS = 2
SWA_WINDOW = 128
SB_HEADS = 4
N_BUCKETS = 32
T5_MAX_DISTANCE = 4096
RMS_EPS = 1e-6
FORCE_SCORE = 1e6
NEG = -1e30
LOG2E = 1.0 / math.log(2.0)
SEL_TILE = 512
SB_GROUP = 3
SB_QB = 4
SB_LOG2_FLOOR = -104.0 * LOG2E
PICKED = -3e38
CMP_QB = 4
VT_ROWS = HEAD_DIM + 16
OUT_ROWS = 128
SEL_QB = 2
FFN_CHUNK = 1024
BAND_QB = 8
CMP_CHUNK = 128
IMP_PAD = 16
IMP_ROWS = CMP_CHUNK // 4 + IMP_PAD
N_BAND = 25
VMEM_LIMIT = 56 * 1024 * 1024


def _cparams(sem):
    return pltpu.CompilerParams(dimension_semantics=sem, vmem_limit_bytes=VMEM_LIMIT)


def _rms(x, g):
    ms = jnp.mean(x * x, axis=-1, keepdims=True)
    return x * lax.rsqrt(ms + RMS_EPS) * g


def _iota(shape, axis):
    return lax.broadcasted_iota(jnp.int32, shape, axis)


def _bias_of_dist(dist, tab_ref, head):
    n = jnp.maximum(dist, 0)
    nf = jnp.maximum(n, 1).astype(F32)
    exact = N_BUCKETS // 2
    large = exact + (jnp.log(nf / exact) / math.log(T5_MAX_DISTANCE / exact)
                     * (N_BUCKETS - exact)).astype(jnp.int32)
    large = jnp.minimum(large, N_BUCKETS - 1)
    bucket = jnp.where(n < exact, n, large)
    val = jnp.full(dist.shape, tab_ref[0, head], F32)
    for k in range(1, N_BUCKETS):
        val = jnp.where(bucket == k, tab_ref[k, head], val)
    return val


def _band_table_t_kernel(tab_ref, o_ref, *, head_off, shift):
    h = pl.program_id(0) + head_off

    def tile(m, _):
        dist = BLK * m + _iota((BLK, BLK), 1) - _iota((BLK, BLK), 0)
        bias = _bias_of_dist(dist, tab_ref, h)
        o_ref[m] = (bias - tab_ref[N_BUCKETS - 1, h]) * LOG2E if shift else bias
        return 0

    lax.fori_loop(0, o_ref.shape[0], tile, 0)


def _band_table_t(rel_bias, n_band, n_heads, head_off, shift):
    return pl.pallas_call(
        functools.partial(_band_table_t_kernel, head_off=head_off, shift=shift),
        grid=(n_heads,),
        in_specs=[pl.BlockSpec(memory_space=pltpu.SMEM)],
        out_specs=pl.BlockSpec((n_band, BLK, BLK), lambda h: (0, 0, h)),
        out_shape=jax.ShapeDtypeStruct((n_band, BLK, n_heads * BLK), F32),
        compiler_params=_cparams(("arbitrary",)),
        name="band_table_t",
    )(rel_bias)


def _cmp_table_kernel(tab_ref, o_ref, *, ncp):
    h = pl.program_id(0)
    rel = _iota((2 * ncp, BLK), 0) - (ncp - 8)
    dist = _iota((2 * ncp, BLK), 1) - CMP_STRIDE * rel - (CMP_LEN - 1)
    o_ref[...] = _bias_of_dist(dist, tab_ref, h) * LOG2E


def _cmp_table(rel_bias, ncp):
    return pl.pallas_call(
        functools.partial(_cmp_table_kernel, ncp=ncp),
        grid=(NSA_HEADS,),
        in_specs=[pl.BlockSpec(memory_space=pltpu.SMEM)],
        out_specs=pl.BlockSpec((2 * ncp, BLK), lambda h: (0, h)),
        out_shape=jax.ShapeDtypeStruct((2 * ncp, NSA_HEADS * BLK), F32),
        compiler_params=_cparams(("arbitrary",)),
        name="cmp_table",
    )(rel_bias)


def _proj_kernel(x_ref, g_ref, w_ref, qt_ref, ks_ref, vs_ref, kw_ref, vw_ref, sqt_ref, sk_ref, sv_ref,
                 bq_ref, bk_ref, bv_ref, ckv_ref, gt_ref):
    h = _rms(x_ref[0], g_ref[...]).astype(BF16)
    tm = x_ref.shape[1]

    def seg(a, b):
        return jnp.dot(h, w_ref[:, a:b], preferred_element_type=F32)

    def heads(ref, a, n):
        y = seg(a, a + n * HEAD_DIM).astype(BF16)
        for i in range(n):
            ref[0, i] = y[:, i * HEAD_DIM:(i + 1) * HEAD_DIM]

    def queries_t(ref, a, n, scale=None):
        yt = seg(a, a + n * HEAD_DIM).T
        if scale is not None:
            yt = yt * scale
        for j in range(tm // BLK):
            for i in range(n):
                ref[0, j, :, i * BLK:(i + 1) * BLK] = (
                    yt[i * HEAD_DIM:(i + 1) * HEAD_DIM, j * BLK:(j + 1) * BLK].astype(BF16))

    ones_rows = jnp.where(_iota((VT_ROWS - HEAD_DIM, tm), 0) == 0, 1.0, 0.0)

    def values_t(yt):
        return jnp.concatenate([yt, ones_rows], axis=0).astype(BF16)

    queries_t(qt_ref, 0, NSA_HEADS, LOG2E)
    y = seg(256, 512)
    vs_ref[0] = values_t(y[:, 0:128].T[64:128])
    vw_ref[0] = values_t(y[:, 128:256].T[64:128])
    y = y.astype(BF16)
    blk_in_tile = ((pl.program_id(1) * tm + _iota((tm, HEAD_DIM), 0)) & (SEL_TILE - 1)) // SEL_BLK
    onehot = jnp.where(_iota((tm, HEAD_DIM), 1) == blk_in_tile, 1.0, 0.0).astype(BF16)
    ks_ref[0] = jnp.concatenate([y[:, 0:64], onehot], axis=1)
    kw_ref[0] = y[:, 128:192]
    queries_t(sqt_ref, 512, SWA_HEADS)
    heads(sk_ref, 1024, SWA_KV_HEADS)
    yt = seg(1152, 1280).T
    for i in range(SWA_KV_HEADS):
        sv_ref[0, i] = values_t(yt[i * HEAD_DIM:(i + 1) * HEAD_DIM])
    queries_t(bq_ref, 1280, SB_HEADS, LOG2E)
    bk_ref[0] = seg(1536, 1792).astype(BF16)
    bv_ref[0] = seg(1792, 2048).T.astype(BF16)
    y = seg(2048, 2304)
    ckv_ref[0, 0] = y[:, 0:64]
    ckv_ref[1, 0] = y[:, 64:128]
    gt_ref[0] = y[:, 128:256]


def _proj(x, g, w, tm):
    b, s, d = x.shape
    nw = w.shape[1]
    hd = lambda n: pl.BlockSpec((1, n, tm, HEAD_DIM), lambda bi, i: (bi, 0, i, 0))
    flat = lambda n: pl.BlockSpec((1, tm, n), lambda bi, i: (bi, i, 0))
    hshape = lambda n: jax.ShapeDtypeStruct((b, n, s, HEAD_DIM), BF16)
    fshape = lambda n, dt: jax.ShapeDtypeStruct((b, s, n), dt)
    qt_spec = lambda n: pl.BlockSpec((1, tm // BLK, HEAD_DIM, n * BLK), lambda bi, i: (bi, i, 0, 0))
    qt_shape = lambda n: jax.ShapeDtypeStruct((b, s // BLK, HEAD_DIM, n * BLK), BF16)
    vt_spec = pl.BlockSpec((1, VT_ROWS, tm), lambda bi, i: (bi, 0, i))
    vt_shape = jax.ShapeDtypeStruct((b, VT_ROWS, s), BF16)
    return pl.pallas_call(
        _proj_kernel,
        grid=(b, s // tm),
        in_specs=[pl.BlockSpec((1, tm, d), lambda bi, i: (bi, i, 0)),
                  pl.BlockSpec((1, d), lambda bi, i: (0, 0)),
                  pl.BlockSpec((d, nw), lambda bi, i: (0, 0))],
        out_specs=[qt_spec(NSA_HEADS), flat(128), vt_spec, flat(64), vt_spec,
                   qt_spec(SWA_HEADS), hd(SWA_KV_HEADS),
                   pl.BlockSpec((1, SWA_KV_HEADS, VT_ROWS, tm), lambda bi, i: (bi, 0, 0, i)),
                   qt_spec(SB_HEADS), flat(256),
                   pl.BlockSpec((1, SB_HEADS * HEAD_DIM, tm), lambda bi, i: (bi, 0, i)),
                   pl.BlockSpec((2, 1, tm, HEAD_DIM), lambda bi, i: (0, bi, i, 0)), flat(128)],
        out_shape=[qt_shape(NSA_HEADS), fshape(128, BF16), vt_shape, fshape(64, BF16), vt_shape,
                   qt_shape(SWA_HEADS), hshape(SWA_KV_HEADS),
                   jax.ShapeDtypeStruct((b, SWA_KV_HEADS, VT_ROWS, s), BF16),
                   qt_shape(SB_HEADS), fshape(256, BF16),
                   jax.ShapeDtypeStruct((b, SB_HEADS * HEAD_DIM, s), BF16),
                   jax.ShapeDtypeStruct((2, b, s, HEAD_DIM), F32), fshape(128, F32)],
        compiler_params=_cparams(("parallel", "parallel")),
        name="proj",
    )(x, g, w)


def _compress_kernel(c_ref, pos_ref, w1_ref, b1_ref, w2_ref, o_ref, ot_ref):
    c = c_ref[0, 0]
    nch = c.shape[0]
    xa = (c + pos_ref[0, 0:1]).astype(BF16)
    xb = (c + pos_ref[0, 1:2]).astype(BF16)
    p = jnp.dot(xa, w1_ref[0, 0], preferred_element_type=F32)
    q = jnp.dot(xb, w1_ref[0, 1], preferred_element_type=F32)
    hid = p + pltpu.roll(q, nch - 1, 0) + b1_ref[0]
    hid = hid * jax.nn.sigmoid(hid)
    out = jnp.dot(hid.astype(BF16), w2_ref[0], preferred_element_type=F32)
    row = _iota(out.shape, 0)
    out = jnp.where(row < nch - 1, out, 0.0)
    o_ref[0, 0] = out.astype(BF16)
    ot_ref[0, 0] = jnp.concatenate([out, jnp.zeros_like(out)], axis=1).T[:HEAD_DIM].astype(BF16)


def _compress(ckv, pos, w1, b1, w2):
    _, b, nch, cw = ckv.shape
    return pl.pallas_call(
        _compress_kernel,
        grid=(2, b),
        in_specs=[pl.BlockSpec((1, 1, nch, cw), lambda j, bi: (j, bi, 0, 0)),
                  pl.BlockSpec((1, 2, cw), lambda j, bi: (j, 0, 0)),
                  pl.BlockSpec((1, 2, cw, CMP_HIDDEN), lambda j, bi: (j, 0, 0, 0)),
                  pl.BlockSpec((1, 1, CMP_HIDDEN), lambda j, bi: (j, 0, 0)),
                  pl.BlockSpec((1, CMP_HIDDEN, HEAD_DIM), lambda j, bi: (j, 0, 0))],
        out_specs=[pl.BlockSpec((1, 1, nch, HEAD_DIM), lambda j, bi: (j, bi, 0, 0)),
                   pl.BlockSpec((1, 1, HEAD_DIM, nch), lambda j, bi: (j, bi, 0, 0))],
        out_shape=[jax.ShapeDtypeStruct((2, b, nch, HEAD_DIM), BF16),
                   jax.ShapeDtypeStruct((2, b, HEAD_DIM, nch), BF16)],
        compiler_params=_cparams(("parallel", "parallel")),
        name="compress",
    )(ckv, pos, w1, b1, w2)


def _transpose_pad(x, axis):
    pad = jnp.zeros(x.shape, x.dtype)
    return jnp.concatenate([x, pad], axis=axis).T


def _cmp_kernel(qt_ref, kc_ref, vct_ref, hct_ref, wmapt_ref, o_ref, pen_ref, s_scr, oacc_scr, iacc_scr):
    step = pl.program_id(1)
    ncp = kc_ref.shape[2]
    ns = wmapt_ref.shape[0] - IMP_PAD
    lanes = NSA_HEADS * BLK
    row_c = _iota((CMP_CHUNK, lanes), 0)
    lane_q = _iota((CMP_CHUNK, lanes), 1) & (BLK - 1)
    n_chunks = (CMP_QB * step) // (CMP_CHUNK // 8) + 1

    def score_chunk(ch, mxs):
        c0 = pl.multiple_of(ch * CMP_CHUNK, CMP_CHUNK)
        kc = kc_ref[0, 0, pl.ds(c0, CMP_CHUNK), :]
        raw = [jnp.dot(kc, qt_ref[0, qb], preferred_element_type=F32) for qb in range(CMP_QB)]
        out = []
        for qb in range(CMP_QB):
            n = CMP_QB * step + qb
            bias = hct_ref[pl.ds(pl.multiple_of(ncp - 8 - 8 * n + c0, 8), CMP_CHUNK), :]
            visible = (BLK * n + lane_q - CMP_STRIDE * (c0 + row_c) - (CMP_LEN - 1)) >= 0
            s = jnp.where(visible, raw[qb] + bias, NEG)
            s_scr[qb, ch] = s
            out.append(jnp.maximum(mxs[qb], jnp.max(s, axis=0, keepdims=True)))
        return tuple(out)

    mxs = lax.fori_loop(0, n_chunks, score_chunk, tuple(jnp.full((1, lanes), NEG, F32) for _ in range(CMP_QB)))
    ms = [jnp.where(mx <= 0.5 * NEG, 0.0, mx) for mx in mxs]
    oacc_scr[...] = jnp.zeros(oacc_scr.shape, F32)
    iacc_scr[...] = jnp.zeros(iacc_scr.shape, F32)

    def prob_chunk(ch, ls):
        c0 = pl.multiple_of(ch * CMP_CHUNK, CMP_CHUNK)
        vct = vct_ref[0, 0, :, pl.ds(c0, CMP_CHUNK)]
        r0 = pl.multiple_of(ch * (CMP_CHUNK // 4), 16)
        wmt = wmapt_ref[pl.ds(r0, IMP_ROWS), pl.ds(c0, CMP_CHUNK)]
        both = jnp.concatenate([vct, wmt], axis=0)
        ps = [jnp.exp2(s_scr[qb, ch] - ms[qb]) for qb in range(CMP_QB)]
        prods = [jnp.dot(both, p.astype(BF16), preferred_element_type=F32) for p in ps]
        for qb in range(CMP_QB):
            oacc_scr[qb] += prods[qb][:HEAD_DIM]
            iacc_scr[qb, pl.ds(r0, IMP_ROWS), :] += prods[qb][HEAD_DIM:]
        return tuple(ls[qb] + jnp.sum(ps[qb], axis=0, keepdims=True) for qb in range(CMP_QB))

    ls = lax.fori_loop(0, n_chunks, prob_chunk, tuple(jnp.zeros((1, lanes), F32) for _ in range(CMP_QB)))
    imps = []
    for qb in range(CMP_QB):
        inv = 1.0 / jnp.maximum(ls[qb], 1e-30)
        o_t = oacc_scr[qb] * inv
        o_ref[0, qb * BLK:(qb + 1) * BLK, :] = jnp.concatenate(
            [_transpose_pad(o_t[:, h * BLK:(h + 1) * BLK], 0)[:, :HEAD_DIM] for h in range(NSA_HEADS)], axis=1)
        w = iacc_scr[qb, 0:ns, :] * inv
        imps.append(functools.reduce(jnp.add, [w[:, h * BLK:(h + 1) * BLK] for h in range(NSA_HEADS)]))

    imp = jnp.concatenate(imps, axis=1)
    shape = (ns, CMP_QB * BLK)
    j = _iota(shape, 0)
    cur = (CMP_QB * BLK * step + _iota(shape, 1)) // SEL_BLK
    valid = j <= cur
    forced = valid & ((j == 0) | (j > cur - N_LOCAL))
    score = jnp.where(forced, PICKED, jnp.where(valid, imp, NEG))
    jf = j.astype(F32)
    for _ in range(min(N_SEL, ns) - (N_LOCAL + 1)):
        best = jnp.max(score, axis=0, keepdims=True)
        first = jnp.min(jnp.where(score == best, jf, float(ns)), axis=0, keepdims=True)
        score = jnp.where(jf == first, PICKED, score)
    pen_ref[0, 0] = jnp.where(valid & (score == PICKED), 0.0, NEG)


def _cmp_attn(qt, kcv, kcv_t, hct, wmapt):
    b, nb, _, lanes = qt.shape
    ncp = kcv.shape[2]
    ns = wmapt.shape[0] - IMP_PAD
    s = nb * BLK
    return pl.pallas_call(
        _cmp_kernel,
        grid=(b, nb // CMP_QB),
        in_specs=[pl.BlockSpec((1, CMP_QB, HEAD_DIM, lanes), lambda bi, i: (bi, i, 0, 0)),
                  pl.BlockSpec((1, 1, ncp, HEAD_DIM), lambda bi, i: (0, bi, 0, 0)),
                  pl.BlockSpec((1, 1, HEAD_DIM, ncp), lambda bi, i: (1, bi, 0, 0)),
                  pl.BlockSpec((2 * ncp, lanes), lambda bi, i: (0, 0)),
                  pl.BlockSpec((ns + IMP_PAD, ncp), lambda bi, i: (0, 0))],
        out_specs=[pl.BlockSpec((1, CMP_QB * BLK, NSA_HEADS * HEAD_DIM), lambda bi, i: (bi, i, 0)),
                   pl.BlockSpec((1, 1, ns, CMP_QB * BLK), lambda bi, i: (bi, i, 0, 0))],
        out_shape=[jax.ShapeDtypeStruct((b, s, NSA_HEADS * HEAD_DIM), F32),
                   jax.ShapeDtypeStruct((b, nb // CMP_QB, ns, CMP_QB * BLK), F32)],
        scratch_shapes=[pltpu.VMEM((CMP_QB, ncp // CMP_CHUNK, CMP_CHUNK, lanes), F32),
                        pltpu.VMEM((CMP_QB, HEAD_DIM, lanes), F32),
                        pltpu.VMEM((CMP_QB, ns + IMP_PAD, lanes), F32)],
        compiler_params=_cparams(("parallel", "arbitrary")),
        name="cmp_attn",
    )(qt, kcv, kcv_t, hct, wmapt)


def _sel_kernel(qt_ref, k_ref, vt_ref, pen_ref, band_ref, o_ref, s_scr, p_scr, mx_scr, m_scr, alpha_scr, acc_scr):
    n = SEL_QB * pl.program_id(1)
    qlanes = NSA_HEADS * BLK
    lanes = SEL_QB * qlanes
    sub = SEL_TILE // BLK
    per = BLK // SEL_BLK
    qt = jnp.concatenate([qt_ref[0, qb] for qb in range(SEL_QB)], axis=1)
    lane = _iota((BLK, lanes), 1)
    t = BLK * (n + lane // qlanes) + (lane & (BLK - 1))

    def scores_into(kt, s_buf, mx_buf, far):
        rows = sub * per
        pen = pen_ref[0, 0, pl.ds(pl.multiple_of(kt * rows, rows), rows), :]
        pen = jnp.concatenate([pen[:, qb * BLK:(qb + 1) * BLK] for qb in range(SEL_QB) for _ in range(NSA_HEADS)],
                              axis=1)
        tail = jnp.concatenate([pen, jnp.zeros((HEAD_DIM - rows, lanes), F32)], axis=0).astype(BF16)
        q_aug = jnp.concatenate([qt, tail], axis=0)
        k0 = pl.multiple_of(kt * SEL_TILE, SEL_TILE)
        s = jnp.dot(k_ref[0, pl.ds(k0, SEL_TILE), :], q_aug, preferred_element_type=F32)
        if far:
            s_buf[...] = s
            mx_buf[...] = jnp.max(s, axis=0, keepdims=True)
        else:
            mxs = []
            for qb in range(SEL_QB):
                cols = slice(qb * qlanes, (qb + 1) * qlanes)
                parts = [s[u * BLK:(u + 1) * BLK, cols] + band_ref[jnp.clip(n + qb - sub * kt - u, 0, N_BAND - 1)]
                         for u in range(sub)]
                for u in range(sub):
                    s_buf[u * BLK:(u + 1) * BLK, cols] = parts[u]
                mxs.append(functools.reduce(jnp.maximum, [jnp.max(v, axis=0, keepdims=True) for v in parts]))
            mx_buf[...] = jnp.concatenate(mxs, axis=1)

    def add_values(kt, p_buf):
        k0 = pl.multiple_of(kt * SEL_TILE, SEL_TILE)
        pv = jnp.dot(vt_ref[0, :, pl.ds(k0, SEL_TILE)], p_buf[...], preferred_element_type=F32)
        acc_scr[...] = alpha_scr[...] * acc_scr[...] + pv

    def softmax_into(kt, s_buf, mx_buf, p_buf, causal):
        def piece(u):
            s = s_buf[u * BLK:(u + 1) * BLK, :]
            if causal:
                s = jnp.where(kt * SEL_TILE + u * BLK + _iota((BLK, lanes), 0) <= t, s, NEG)
            return s
        m_i = m_scr[...]
        if causal:
            mx = functools.reduce(jnp.maximum, [jnp.max(piece(u), axis=0, keepdims=True) for u in range(sub)])
        else:
            mx = mx_buf[...]
        m_new = jnp.maximum(m_i, mx)
        for u in range(sub):
            p_buf[u * BLK:(u + 1) * BLK, :] = jnp.exp2(piece(u) - m_new).astype(BF16)
        m_scr[...] = m_new
        alpha_scr[...] = jnp.exp2(m_i - m_new)

    def stage(i, cur, nxt, p_cur, p_prev, far):
        add_values(jnp.maximum(i - 1, 0), p_prev)
        scores_into(i + 1, *nxt, far)
        softmax_into(i, *cur, p_cur, False)

    def finish(i, cur, p_cur, p_prev):
        add_values(jnp.maximum(i - 1, 0), p_prev)
        softmax_into(i, *cur, p_cur, True)
        add_values(i, p_cur)
        acc = acc_scr[...]
        o_t = acc[:HEAD_DIM] / acc[HEAD_DIM:HEAD_DIM + 1]
        for qb in range(SEL_QB):
            o_ref[0, qb * BLK:(qb + 1) * BLK, :] = jnp.concatenate(
                [_transpose_pad(o_t[:, qb * qlanes + h * BLK:qb * qlanes + (h + 1) * BLK], 0)[:, :HEAD_DIM]
                 for h in range(NSA_HEADS)], axis=1)

    s0, s1 = (s_scr.at[0], mx_scr.at[0]), (s_scr.at[1], mx_scr.at[1])
    p0, p1 = p_scr.at[0], p_scr.at[1]
    m_scr[...] = jnp.full((1, lanes), NEG, F32)
    alpha_scr[...] = jnp.ones((1, lanes), F32)
    acc_scr[...] = jnp.zeros(acc_scr.shape, F32)
    p1[...] = jnp.zeros((SEL_TILE, lanes), BF16)
    scores_into(0, *s0, False)
    last = (BLK * n + BLK - 1) // SEL_TILE
    n_far = jnp.maximum((n - (N_BAND - 2)) // sub, 0)
    far_pairs = jnp.minimum(jnp.maximum((n_far - 1) // 2, 0), last // 2)

    def pair(j, _, far):
        stage(2 * j, s0, s1, p0, p1, far)
        stage(2 * j + 1, s1, s0, p1, p0, far)
        return 0

    lax.fori_loop(0, far_pairs, functools.partial(pair, far=True), 0)
    lax.fori_loop(far_pairs, last // 2, functools.partial(pair, far=False), 0)

    @pl.when(last % 2 == 1)
    def _():
        stage(last - 1, s0, s1, p0, p1, False)
        finish(last, s1, p1, p0)

    @pl.when(last % 2 == 0)
    def _():
        finish(last, s0, p0, p1)


def _sel_attn(qt, ks, vst, pen, band_t):
    b, nb, _, qlanes = qt.shape
    s = nb * BLK
    ns = pen.shape[2]
    lanes = SEL_QB * qlanes
    per_cmp = CMP_QB // SEL_QB
    return pl.pallas_call(
        _sel_kernel,
        grid=(b, nb // SEL_QB),
        in_specs=[pl.BlockSpec((1, SEL_QB, HEAD_DIM, qlanes), lambda bi, i: (bi, i, 0, 0)),
                  pl.BlockSpec((1, s, 2 * HEAD_DIM), lambda bi, i: (bi, 0, 0)),
                  pl.BlockSpec((1, VT_ROWS, s), lambda bi, i: (bi, 0, 0)),
                  pl.BlockSpec((1, 1, ns, SEL_QB * BLK), lambda bi, i: (bi, i // per_cmp, 0, i % per_cmp)),
                  pl.BlockSpec((N_BAND, BLK, qlanes), lambda bi, i: (0, 0, 0))],
        out_specs=pl.BlockSpec((1, SEL_QB * BLK, NSA_HEADS * HEAD_DIM), lambda bi, i: (bi, i, 0)),
        out_shape=jax.ShapeDtypeStruct((b, s, NSA_HEADS * HEAD_DIM), F32),
        scratch_shapes=[pltpu.VMEM((2, SEL_TILE, lanes), F32),
                        pltpu.VMEM((2, SEL_TILE, lanes), BF16),
                        pltpu.VMEM((2, 1, lanes), F32),
                        pltpu.VMEM((1, lanes), F32),
                        pltpu.VMEM((1, lanes), F32),
                        pltpu.VMEM((VT_ROWS, lanes), F32)],
        compiler_params=_cparams(("parallel", "arbitrary")),
        name="sel_attn",
    )(qt, ks, vst, pen, band_t)


def _band_kernel(*refs, window, groups, use_sinks, base2, n_parts):
    qt_ref = refs[0]
    k_refs = refs[1:1 + n_parts]
    vt_refs = refs[1 + n_parts:1 + 2 * n_parts]
    band_ref, sink_ref, o_ref = refs[1 + 2 * n_parts:]
    step = pl.program_id(1)
    n_prev = window // BLK
    nk = (n_prev + 1) * BLK
    lanes = 4 * BLK
    row = _iota((nk, lanes), 0)
    dist = n_prev * BLK + (_iota((nk, lanes), 1) & (BLK - 1)) - row
    in_window = (dist >= 0) & (dist < window)
    chains = [(g, qb) for g in range(groups) for qb in range(BAND_QB)]
    kwin = [jnp.concatenate([r[0, g] for r in k_refs], axis=0) for g in range(groups)]
    vtwin = [jnp.concatenate([r[0, g] for r in vt_refs], axis=1) for g in range(groups)]
    bias = [jnp.concatenate([band_ref[n_prev - u, :, g * lanes:(g + 1) * lanes] for u in range(n_prev + 1)], axis=0)
            for g in range(groups)]
    scores = [jnp.dot(kwin[g][qb * BLK:qb * BLK + nk], qt_ref[0, qb, :, g * lanes:(g + 1) * lanes],
                      preferred_element_type=F32) for g, qb in chains]
    probs, maxes = [], []
    for (g, qb), s in zip(chains, scores):
        s = jnp.where(in_window & (row >= (n_prev - BAND_QB * step - qb) * BLK), s + bias[g], NEG)
        mx = jnp.max(s, axis=0, keepdims=True)
        if use_sinks:
            mx = jnp.maximum(mx, sink_ref[g])
        probs.append((jnp.exp2(s - mx) if base2 else jnp.exp(s - mx)).astype(BF16))
        maxes.append(mx)
    accs = [jnp.dot(vtwin[g][:, qb * BLK:qb * BLK + nk], p, preferred_element_type=F32)
            for (g, qb), p in zip(chains, probs)]
    for (g, qb), acc, mx in zip(chains, accs, maxes):
        den = acc[HEAD_DIM:HEAD_DIM + 1]
        if use_sinks:
            den = den + jnp.exp(sink_ref[g] - mx)
        o_t = acc[:HEAD_DIM] / den
        o_ref[0, qb * BLK:(qb + 1) * BLK, 4 * g * HEAD_DIM:4 * (g + 1) * HEAD_DIM] = jnp.concatenate(
            [_transpose_pad(o_t[:, h * BLK:(h + 1) * BLK], 0)[:, :HEAD_DIM] for h in range(4)], axis=1)


def _band_attn(qt, k, vt, band_t, sinks, window, use_sinks, base2, name):
    b, nb, _, width = qt.shape
    g = width // (4 * BLK)
    s = nb * BLK
    n_prev = window // BLK
    tq = BAND_QB * BLK
    back = n_prev * BLK
    if back % tq == 0:
        piece = tq
        starts = [lambda i, d=d: jnp.maximum(i - d, 0) for d in range(back // tq, 0, -1)]
    else:
        assert tq % back == 0
        piece = back
        starts = [lambda i: jnp.maximum(i * (tq // back) - 1, 0)]
    k_specs = [pl.BlockSpec((1, g, piece, HEAD_DIM), lambda bi, i, f=f: (bi, 0, f(i), 0)) for f in starts]
    k_specs.append(pl.BlockSpec((1, g, tq, HEAD_DIM), lambda bi, i: (bi, 0, i, 0)))
    vt_specs = [pl.BlockSpec((1, g, VT_ROWS, piece), lambda bi, i, f=f: (bi, 0, 0, f(i))) for f in starts]
    vt_specs.append(pl.BlockSpec((1, g, VT_ROWS, tq), lambda bi, i: (bi, 0, 0, i)))
    n_parts = len(k_specs)
    return pl.pallas_call(
        functools.partial(_band_kernel, window=window, groups=g, use_sinks=use_sinks, base2=base2,
                          n_parts=n_parts),
        grid=(b, s // tq),
        in_specs=[pl.BlockSpec((1, BAND_QB, HEAD_DIM, width), lambda bi, i: (bi, i, 0, 0))] + k_specs + vt_specs
                 + [pl.BlockSpec((n_prev + 1, BLK, width), lambda bi, i: (0, 0, 0)),
                    pl.BlockSpec((g, 1, 4 * BLK), lambda bi, i: (0, 0, 0))],
        out_specs=pl.BlockSpec((1, tq, g * 4 * HEAD_DIM), lambda bi, i: (bi, i, 0)),
        out_shape=jax.ShapeDtypeStruct((b, s, g * 4 * HEAD_DIM), F32),
        compiler_params=_cparams(("parallel", "arbitrary")),
        name=name,
    )(qt, *([k] * n_parts), *([vt] * n_parts), band_t, sinks)


def _sb_kernel(qt_ref, k_ref, vt_ref, later_ref, o_ref):
    n0 = SB_QB * pl.program_id(1)
    lanes = SB_HEADS * BLK
    lane_q = _iota((BLK, lanes), 1) & (BLK - 1)
    row = _iota((BLK, lanes), 0)
    later = later_ref[...]
    zero = jnp.zeros((HEAD_DIM, BLK), BF16)
    blocks = range(SB_QB)
    q_bd = [jnp.concatenate(
        [jnp.concatenate([zero] * h + [qt_ref[0, qb, :, h * BLK:(h + 1) * BLK]] + [zero] * (SB_HEADS - 1 - h), axis=0)
         for h in range(SB_HEADS)], axis=1) for qb in blocks]

    def cond(c):
        it, carry, _ = c
        return (n0 + SB_QB - 1 - SB_GROUP * it >= 0) & (jnp.max(functools.reduce(jnp.maximum, carry)) > SB_LOG2_FLOOR)

    def body(c):
        it, carry, acc = c
        k_all, vt_all, before = [], [], []
        for qb in blocks:
            top = n0 + qb - SB_GROUP * it
            ks, vts, befores = [], [], []
            for j in range(SB_GROUP - 1, -1, -1):
                kb = top - j
                k0 = pl.multiple_of(jnp.maximum(kb, 0) * BLK, BLK)
                ks.append(k_ref[0, pl.ds(k0, BLK), :])
                vts.append(vt_ref[0, :, pl.ds(k0, BLK)])
                befores.append((kb >= 0) & (k0 + row < BLK * (n0 + qb) + lane_q))
            k_all.append(jnp.concatenate(ks, axis=0))
            vt_all.append(jnp.concatenate(vts, axis=1))
            before.append(jnp.concatenate(befores, axis=0))
        z = [jnp.dot(k_all[qb], q_bd[qb], preferred_element_type=F32) for qb in blocks]
        log_not, split = [], []
        for qb in blocks:
            softplus = jnp.maximum(z[qb], 0.0) + jnp.log2(1.0 + jnp.exp2(-jnp.abs(z[qb])))
            ln = jnp.where(before[qb], -softplus, 0.0)
            hi = ln.astype(BF16)
            lo = (ln - hi.astype(F32)).astype(BF16)
            log_not.append(ln)
            split.append(jnp.concatenate([hi, lo], axis=0))
        tail = [jnp.dot(later, split[qb], preferred_element_type=F32) for qb in blocks]
        a = [jnp.where(before[qb], jnp.exp2(z[qb] + log_not[qb] + tail[qb] + carry[qb]), 0.0).astype(BF16)
             for qb in blocks]
        pv = [jnp.dot(vt_all[qb], a[qb], preferred_element_type=F32) for qb in blocks]
        acc = tuple(tuple(acc[qb][h] + pv[qb][h * HEAD_DIM:(h + 1) * HEAD_DIM, h * BLK:(h + 1) * BLK]
                          for h in range(SB_HEADS)) for qb in blocks)
        carry = tuple(carry[qb] + jnp.sum(log_not[qb], axis=0, keepdims=True) for qb in blocks)
        return it + 1, carry, acc

    init = (0, tuple(jnp.zeros((1, lanes), F32) for _ in blocks),
            tuple(tuple(jnp.zeros((HEAD_DIM, BLK), F32) for _ in range(SB_HEADS)) for _ in blocks))
    _, _, acc = lax.while_loop(cond, body, init)
    for qb in blocks:
        o_ref[0, qb * BLK:(qb + 1) * BLK, :] = jnp.concatenate(
            [_transpose_pad(a, 0)[:, :HEAD_DIM] for a in acc[qb]], axis=1)


def _sb_attn(qt, k, vt):
    b, nb, _, lanes = qt.shape
    s = nb * BLK
    width = SB_HEADS * HEAD_DIM
    gk = SB_GROUP * BLK
    later = (jnp.arange(2 * gk)[None, :] % gk > jnp.arange(gk)[:, None]).astype(BF16)
    return pl.pallas_call(
        _sb_kernel,
        grid=(b, nb // SB_QB),
        in_specs=[pl.BlockSpec((1, SB_QB, HEAD_DIM, lanes), lambda bi, n: (bi, n, 0, 0)),
                  pl.BlockSpec((1, s, width), lambda bi, n: (bi, 0, 0)),
                  pl.BlockSpec((1, width, s), lambda bi, n: (bi, 0, 0)),
                  pl.BlockSpec((gk, 2 * gk), lambda bi, n: (0, 0))],
        out_specs=pl.BlockSpec((1, SB_QB * BLK, width), lambda bi, n: (bi, n, 0)),
        out_shape=jax.ShapeDtypeStruct((b, s, width), F32),
        compiler_params=_cparams(("parallel", "arbitrary")),
        name="sb_attn",
    )(qt, k, vt, later)


def _out_kernel(x_ref, oc_ref, os_ref, ow_ref, gt_ref, swa_ref, sb_ref, gg_ref, w_ref, gp_ref, o_ref):
    wa = NSA_HEADS * HEAD_DIM
    wb = wa + SWA_HEADS * HEAD_DIM
    gg = gg_ref[...]
    lane = _iota((OUT_ROWS, wa), 1)
    groups = [slice(r0, r0 + OUT_ROWS) for r0 in range(0, x_ref.shape[1], OUT_ROWS)]
    mixes = []
    for rows in groups:
        gates = jax.nn.sigmoid(gt_ref[0, rows])

        def spread(branch, gates=gates):
            cols = [jnp.broadcast_to(gates[:, branch * NSA_HEADS + i:branch * NSA_HEADS + i + 1], (OUT_ROWS, wa))
                    for i in range(NSA_HEADS)]
            out = cols[NSA_HEADS - 1]
            for i in range(NSA_HEADS - 2, -1, -1):
                out = jnp.where(lane < (i + 1) * HEAD_DIM, cols[i], out)
            return out

        o_nsa = spread(0) * oc_ref[0, rows] + spread(1) * os_ref[0, rows] + spread(2) * ow_ref[0, rows]
        mixes.append(jnp.concatenate([_rms(o_nsa, gg[:, :wa]), _rms(swa_ref[0, rows], gg[:, wa:wb]),
                                      _rms(sb_ref[0, rows], gg[:, wb:])], axis=1).astype(BF16))
    ys = [jnp.dot(mix, w_ref[...], preferred_element_type=F32) for mix in mixes]
    for rows, y in zip(groups, ys):
        o_ref[0, rows] = x_ref[0, rows] + _rms(y, gp_ref[...])


def _out_proj(x, o_cmp, o_sel, o_win, gates, o_swa, o_sb, gg, w, gp, tm):
    b, s, d = x.shape
    row = lambda n: pl.BlockSpec((1, tm, n), lambda bi, i: (bi, i, 0))
    const = lambda a, c: pl.BlockSpec((a, c), lambda bi, i: (0, 0))
    return pl.pallas_call(
        _out_kernel,
        grid=(b, s // tm),
        in_specs=[row(d), row(256), row(256), row(256), row(128), row(512), row(256),
                  const(1, d), const(d, d), const(1, d)],
        out_specs=row(d),
        out_shape=jax.ShapeDtypeStruct((b, s, d), F32),
        compiler_params=_cparams(("parallel", "parallel")),
        name="out_proj",
    )(x, o_cmp, o_sel, o_win, gates, o_swa, o_sb, gg, w, gp)


def _ffn_kernel(x_ref, g1_ref, wg_ref, wu_ref, wd_ref, g2_ref, o_ref, *, chunk):
    x = x_ref[...]
    h = _rms(x, g1_ref[...]).astype(BF16)
    dff = wg_ref.shape[1]
    f = jnp.zeros(x.shape, F32)
    for a0 in range(0, dff, chunk):
        a1 = min(a0 + chunk, dff)
        gate = jnp.dot(h, wg_ref[:, a0:a1], preferred_element_type=F32)
        up = jnp.dot(h, wu_ref[:, a0:a1], preferred_element_type=F32)
        a = (gate * jax.nn.sigmoid(gate) * up).astype(BF16)
        f = f + jnp.dot(a, wd_ref[a0:a1, :], preferred_element_type=F32)
    o_ref[...] = x + _rms(f, g2_ref[...])


def _ffn(x, g1, wg, wu, wd, g2, tm):
    t, d = x.shape
    dff = wg.shape[1]
    row = pl.BlockSpec((tm, d), lambda i: (i, 0))
    const = lambda a, c: pl.BlockSpec((a, c), lambda i: (0, 0))
    weight = lambda a, c: pl.BlockSpec((a, c), lambda i: (0, 0), pipeline_mode=pl.Buffered(1))
    return pl.pallas_call(
        functools.partial(_ffn_kernel, chunk=FFN_CHUNK),
        grid=(t // tm,),
        in_specs=[row, const(1, d), weight(d, dff), weight(d, dff), weight(dff, d), const(1, d)],
        out_specs=row,
        out_shape=jax.ShapeDtypeStruct((t, d), F32),
        compiler_params=_cparams(("parallel",)),
        name="ffn",
    )(x, g1, wg, wu, wd, g2)


def _permute_w_in(w):
    scale = 1.0 / math.sqrt(HEAD_DIM)
    nq, kc, vc, rest, ng = w[:, :256], w[:, 256:320], w[:, 320:384], w[:, 384:640], w[:, 640:652]
    sq, skv = w[:, 652:1164], w[:, 1164:1420]
    bq, bkv = w[:, 1420:1676], w[:, 1676:2188]
    pad = jnp.zeros((w.shape[0], 128 - ng.shape[1]), w.dtype)
    return jnp.concatenate([nq * scale, rest, sq * scale, skv, bq * scale, bkv, kc, vc, ng, pad], axis=1)


def _cmp_to_sel(nc_pad, nc, ns):
    c0 = jnp.arange(nc_pad)[:, None] * CMP_STRIDE
    s0 = jnp.arange(ns)[None, :] * SEL_BLK
    ov = jnp.minimum(c0 + CMP_LEN, s0 + SEL_BLK) - jnp.maximum(c0, s0)
    w = jnp.clip(ov, 0, None).astype(F32) / CMP_LEN
    return jnp.where(jnp.arange(nc_pad)[:, None] < nc, w, 0.0).astype(BF16)


def kernel(x, rel_bias, ln_attn_pre, w_in, nsa_cmp_pos, nsa_phi_w1, nsa_phi_b1, nsa_phi_w2, swa_sinks,
           grp_norm_g, w_out, ln_attn_post, ln_ffn_pre, ffn_w_gate, ffn_w_up, ffn_w_down, ln_ffn_post):
    b, s, d = x.shape
    depth = w_in.shape[0]
    nch = s // CMP_STRIDE
    ns = s // SEL_BLK
    t = b * s

    band_nsa_t = _band_table_t(rel_bias, N_BAND, NSA_HEADS, 0, True)
    band_swa_t = _band_table_t(rel_bias, SWA_WINDOW // BLK + 1, SWA_HEADS, NSA_HEADS, False)
    hct = _cmp_table(rel_bias, nch)
    wmapt = jnp.pad(_cmp_to_sel(nch, nch - 1, ns).T, ((0, IMP_PAD), (0, 0)))
    no_sinks = jnp.zeros((1, 1, 4 * BLK), F32)

    for l in range(depth):
        outs = _proj(x, ln_attn_pre[l][None], _permute_w_in(w_in[l]).astype(BF16), 512)
        qt, ks, vs, kw, vw, sqt, sk, sv, bq, bk, bv, ckv, gates = outs
        kcv, kcv_t = _compress(ckv.reshape(2, b, nch, CMP_STRIDE * HEAD_DIM),
                               nsa_cmp_pos[l].reshape(2, 2, CMP_STRIDE * HEAD_DIM),
                               nsa_phi_w1[l].reshape(2, 2, CMP_STRIDE * HEAD_DIM, CMP_HIDDEN).astype(BF16),
                               nsa_phi_b1[l][:, None, :], nsa_phi_w2[l].astype(BF16))
        o_cmp, selmask = _cmp_attn(qt, kcv, kcv_t, hct, wmapt)
        o_sel = _sel_attn(qt, ks, vs, selmask, band_nsa_t)
        o_win = _band_attn(qt, kw[:, None], vw[:, None], band_nsa_t, no_sinks, NSA_WINDOW, False, True,
                           "win_attn")
        sinks = jnp.broadcast_to(swa_sinks[l].reshape(SWA_KV_HEADS, 1, 4, 1),
                                 (SWA_KV_HEADS, 1, 4, BLK)).reshape(SWA_KV_HEADS, 1, 4 * BLK)
        o_swa = _band_attn(sqt, sk, sv, band_swa_t, sinks, SWA_WINDOW, True, False, "swa_attn")
        o_sb = _sb_attn(bq, bk, bv)
        x = _out_proj(x, o_cmp, o_sel, o_win, gates, o_swa, o_sb, grp_norm_g[l][None], w_out[l].astype(BF16),
                      ln_attn_post[l][None], 512)
        x = _ffn(x.reshape(t, d), ln_ffn_pre[l][None], ffn_w_gate[l].astype(BF16), ffn_w_up[l].astype(BF16),
                 ffn_w_down[l].astype(BF16), ln_ffn_post[l][None], 512).reshape(b, s, d)
    return x
```

```python
import functools
import math

import jax
import jax.numpy as jnp
from jax import lax
from jax.experimental import pallas as pl
from jax.experimental.pallas import tpu as pltpu

F32 = jnp.float32
BF16 = jnp.bfloat16

HEAD_DIM = 64
BLK = 128
NSA_HEADS = 4
CMP_LEN = 32
CMP_STRIDE = 16
CMP_HIDDEN = 256
SEL_BLK = 64
N_SEL = 16
N_LOCAL = 2
NSA_WINDOW = 512
SWA_HEADS = 8
SWA_KV_HEADS = 2
SWA_WINDOW = 128
SB_HEADS = 4
N_BUCKETS = 32
T5_MAX_DISTANCE = 4096
RMS_EPS = 1e-6
FORCE_SCORE = 1e6
NEG = -1e30
LOG2E = 1.0 / math.log(2.0)
SEL_TILE = 512
SB_GROUP = 3
SB_QB = 4
SB_LOG2_FLOOR = -104.0 * LOG2E
PICKED = -3e38
CMP_QB = 4
VT_ROWS = HEAD_DIM + 16
OUT_ROWS = 128
SEL_QB = 2
FFN_CHUNK = 1024
BAND_QB = 8
CMP_CHUNK = 128
IMP_PAD = 16
IMP_ROWS = CMP_CHUNK // 4 + IMP_PAD
N_BAND = 25
VMEM_LIMIT = 56 * 1024 * 1024


def _cparams(sem):
    return pltpu.CompilerParams(dimension_semantics=sem, vmem_limit_bytes=VMEM_LIMIT)


def _rms(x, g):
    ms = jnp.mean(x * x, axis=-1, keepdims=True)
    return x * lax.rsqrt(ms + RMS_EPS) * g


def _iota(shape, axis):
    return lax.broadcasted_iota(jnp.int32, shape, axis)


def _bias_of_dist(dist, tab_ref, head):
    n = jnp.maximum(dist, 0)
    nf = jnp.maximum(n, 1).astype(F32)
    exact = N_BUCKETS // 2
    large = exact + (jnp.log(nf / exact) / math.log(T5_MAX_DISTANCE / exact)
                     * (N_BUCKETS - exact)).astype(jnp.int32)
    large = jnp.minimum(large, N_BUCKETS - 1)
    bucket = jnp.where(n < exact, n, large)
    val = jnp.full(dist.shape, tab_ref[0, head], F32)
    for k in range(1, N_BUCKETS):
        val = jnp.where(bucket == k, tab_ref[k, head], val)
    return val


def _band_table_t_kernel(tab_ref, o_ref, *, head_off, shift):
    h = pl.program_id(0) + head_off

    def tile(m, _):
        dist = BLK * m + _iota((BLK, BLK), 1) - _iota((BLK, BLK), 0)
        bias = _bias_of_dist(dist, tab_ref, h)
        o_ref[m] = (bias - tab_ref[N_BUCKETS - 1, h]) * LOG2E if shift else bias
        return 0

    lax.fori_loop(0, o_ref.shape[0], tile, 0)


def _band_table_t(rel_bias, n_band, n_heads, head_off, shift):
    return pl.pallas_call(
        functools.partial(_band_table_t_kernel, head_off=head_off, shift=shift),
        grid=(n_heads,),
        in_specs=[pl.BlockSpec(memory_space=pltpu.SMEM)],
        out_specs=pl.BlockSpec((n_band, BLK, BLK), lambda h: (0, 0, h)),
        out_shape=jax.ShapeDtypeStruct((n_band, BLK, n_heads * BLK), F32),
        compiler_params=_cparams(("arbitrary",)),
        name="band_table_t",
    )(rel_bias)


def _cmp_table_kernel(tab_ref, o_ref, *, ncp):
    h = pl.program_id(0)
    rel = _iota((2 * ncp, BLK), 0) - (ncp - 8)
    dist = _iota((2 * ncp, BLK), 1) - CMP_STRIDE * rel - (CMP_LEN - 1)
    o_ref[...] = _bias_of_dist(dist, tab_ref, h) * LOG2E


def _cmp_table(rel_bias, ncp):
    return pl.pallas_call(
        functools.partial(_cmp_table_kernel, ncp=ncp),
        grid=(NSA_HEADS,),
        in_specs=[pl.BlockSpec(memory_space=pltpu.SMEM)],
        out_specs=pl.BlockSpec((2 * ncp, BLK), lambda h: (0, h)),
        out_shape=jax.ShapeDtypeStruct((2 * ncp, NSA_HEADS * BLK), F32),
        compiler_params=_cparams(("arbitrary",)),
        name="cmp_table",
    )(rel_bias)


def _proj_kernel(x_ref, g_ref, w_ref, qt_ref, ks_ref, vs_ref, kw_ref, vw_ref, sqt_ref, sk_ref, sv_ref,
                 bq_ref, bk_ref, bv_ref, ckv_ref, gt_ref):
    h = _rms(x_ref[0], g_ref[...]).astype(BF16)
    tm = x_ref.shape[1]

    def seg(a, b):
        return jnp.dot(h, w_ref[:, a:b], preferred_element_type=F32)

    def heads(ref, a, n):
        y = seg(a, a + n * HEAD_DIM).astype(BF16)
        for i in range(n):
            ref[0, i] = y[:, i * HEAD_DIM:(i + 1) * HEAD_DIM]

    def queries_t(ref, a, n, scale=None):
        yt = seg(a, a + n * HEAD_DIM).T
        if scale is not None:
            yt = yt * scale
        for j in range(tm // BLK):
            for i in range(n):
                ref[0, j, :, i * BLK:(i + 1) * BLK] = (
                    yt[i * HEAD_DIM:(i + 1) * HEAD_DIM, j * BLK:(j + 1) * BLK].astype(BF16))

    ones_rows = jnp.where(_iota((VT_ROWS - HEAD_DIM, tm), 0) == 0, 1.0, 0.0)

    def values_t(yt):
        return jnp.concatenate([yt, ones_rows], axis=0).astype(BF16)

    queries_t(qt_ref, 0, NSA_HEADS, LOG2E)
    y = seg(256, 512)
    vs_ref[0] = values_t(y[:, 0:128].T[64:128])
    vw_ref[0] = values_t(y[:, 128:256].T[64:128])
    y = y.astype(BF16)
    blk_in_tile = ((pl.program_id(1) * tm + _iota((tm, HEAD_DIM), 0)) & (SEL_TILE - 1)) // SEL_BLK
    onehot = jnp.where(_iota((tm, HEAD_DIM), 1) == blk_in_tile, 1.0, 0.0).astype(BF16)
    ks_ref[0] = jnp.concatenate([y[:, 0:64], onehot], axis=1)
    kw_ref[0] = y[:, 128:192]
    queries_t(sqt_ref, 512, SWA_HEADS)
    heads(sk_ref, 1024, SWA_KV_HEADS)
    yt = seg(1152, 1280).T
    for i in range(SWA_KV_HEADS):
        sv_ref[0, i] = values_t(yt[i * HEAD_DIM:(i + 1) * HEAD_DIM])
    queries_t(bq_ref, 1280, SB_HEADS, LOG2E)
    bk_ref[0] = seg(1536, 1792).astype(BF16)
    bv_ref[0] = seg(1792, 2048).T.astype(BF16)
    y = seg(2048, 2304)
    ckv_ref[0, 0] = y[:, 0:64]
    ckv_ref[1, 0] = y[:, 64:128]
    gt_ref[0] = y[:, 128:256]


def _proj(x, g, w, tm):
    b, s, d = x.shape
    nw = w.shape[1]
    hd = lambda n: pl.BlockSpec((1, n, tm, HEAD_DIM), lambda bi, i: (bi, 0, i, 0))
    flat = lambda n: pl.BlockSpec((1, tm, n), lambda bi, i: (bi, i, 0))
    hshape = lambda n: jax.ShapeDtypeStruct((b, n, s, HEAD_DIM), BF16)
    fshape = lambda n, dt: jax.ShapeDtypeStruct((b, s, n), dt)
    qt_spec = lambda n: pl.BlockSpec((1, tm // BLK, HEAD_DIM, n * BLK), lambda bi, i: (bi, i, 0, 0))
    qt_shape = lambda n: jax.ShapeDtypeStruct((b, s // BLK, HEAD_DIM, n * BLK), BF16)
    vt_spec = pl.BlockSpec((1, VT_ROWS, tm), lambda bi, i: (bi, 0, i))
    vt_shape = jax.ShapeDtypeStruct((b, VT_ROWS, s), BF16)
    return pl.pallas_call(
        _proj_kernel,
        grid=(b, s // tm),
        in_specs=[pl.BlockSpec((1, tm, d), lambda bi, i: (bi, i, 0)),
                  pl.BlockSpec((1, d), lambda bi, i: (0, 0)),
                  pl.BlockSpec((d, nw), lambda bi, i: (0, 0))],
        out_specs=[qt_spec(NSA_HEADS), flat(128), vt_spec, flat(64), vt_spec,
                   qt_spec(SWA_HEADS), hd(SWA_KV_HEADS),
                   pl.BlockSpec((1, SWA_KV_HEADS, VT_ROWS, tm), lambda bi, i: (bi, 0, 0, i)),
                   qt_spec(SB_HEADS), flat(256),
                   pl.BlockSpec((1, SB_HEADS * HEAD_DIM, tm), lambda bi, i: (bi, 0, i)),
                   pl.BlockSpec((2, 1, tm, HEAD_DIM), lambda bi, i: (0, bi, i, 0)), flat(128)],
        out_shape=[qt_shape(NSA_HEADS), fshape(128, BF16), vt_shape, fshape(64, BF16), vt_shape,
                   qt_shape(SWA_HEADS), hshape(SWA_KV_HEADS),
                   jax.ShapeDtypeStruct((b, SWA_KV_HEADS, VT_ROWS, s), BF16),
                   qt_shape(SB_HEADS), fshape(256, BF16),
                   jax.ShapeDtypeStruct((b, SB_HEADS * HEAD_DIM, s), BF16),
                   jax.ShapeDtypeStruct((2, b, s, HEAD_DIM), F32), fshape(128, F32)],
        compiler_params=_cparams(("parallel", "parallel")),
        name="proj",
    )(x, g, w)


def _compress_kernel(c_ref, pos_ref, w1_ref, b1_ref, w2_ref, o_ref, ot_ref):
    c = c_ref[0, 0]
    nch = c.shape[0]
    xa = (c + pos_ref[0, 0:1]).astype(BF16)
    xb = (c + pos_ref[0, 1:2]).astype(BF16)
    p = jnp.dot(xa, w1_ref[0, 0], preferred_element_type=F32)
    q = jnp.dot(xb, w1_ref[0, 1], preferred_element_type=F32)
    hid = p + pltpu.roll(q, nch - 1, 0) + b1_ref[0]
    hid = hid * jax.nn.sigmoid(hid)
    out = jnp.dot(hid.astype(BF16), w2_ref[0], preferred_element_type=F32)
    row = _iota(out.shape, 0)
    out = jnp.where(row < nch - 1, out, 0.0)
    o_ref[0, 0] = out.astype(BF16)
    ot_ref[0, 0] = jnp.concatenate([out, jnp.zeros_like(out)], axis=1).T[:HEAD_DIM].astype(BF16)


def _compress(ckv, pos, w1, b1, w2):
    _, b, nch, cw = ckv.shape
    return pl.pallas_call(
        _compress_kernel,
        grid=(2, b),
        in_specs=[pl.BlockSpec((1, 1, nch, cw), lambda j, bi: (j, bi, 0, 0)),
                  pl.BlockSpec((1, 2, cw), lambda j, bi: (j, 0, 0)),
                  pl.BlockSpec((1, 2, cw, CMP_HIDDEN), lambda j, bi: (j, 0, 0, 0)),
                  pl.BlockSpec((1, 1, CMP_HIDDEN), lambda j, bi: (j, 0, 0)),
                  pl.BlockSpec((1, CMP_HIDDEN, HEAD_DIM), lambda j, bi: (j, 0, 0))],
        out_specs=[pl.BlockSpec((1, 1, nch, HEAD_DIM), lambda j, bi: (j, bi, 0, 0)),
                   pl.BlockSpec((1, 1, HEAD_DIM, nch), lambda j, bi: (j, bi, 0, 0))],
        out_shape=[jax.ShapeDtypeStruct((2, b, nch, HEAD_DIM), BF16),
                   jax.ShapeDtypeStruct((2, b, HEAD_DIM, nch), BF16)],
        compiler_params=_cparams(("parallel", "parallel")),
        name="compress",
    )(ckv, pos, w1, b1, w2)


def _transpose_pad(x, axis):
    pad = jnp.zeros(x.shape, x.dtype)
    return jnp.concatenate([x, pad], axis=axis).T


def _cmp_kernel(qt_ref, kc_ref, vct_ref, hct_ref, wmapt_ref, o_ref, pen_ref, s_scr, oacc_scr, iacc_scr):
    step = pl.program_id(1)
    ncp = kc_ref.shape[2]
    ns = wmapt_ref.shape[0] - IMP_PAD
    lanes = NSA_HEADS * BLK
    row_c = _iota((CMP_CHUNK, lanes), 0)
    lane_q = _iota((CMP_CHUNK, lanes), 1) & (BLK - 1)
    n_chunks = (CMP_QB * step) // (CMP_CHUNK // 8) + 1

    def score_chunk(ch, mxs):
        c0 = pl.multiple_of(ch * CMP_CHUNK, CMP_CHUNK)
        kc = kc_ref[0, 0, pl.ds(c0, CMP_CHUNK), :]
        raw = [jnp.dot(kc, qt_ref[0, qb], preferred_element_type=F32) for qb in range(CMP_QB)]
        out = []
        for qb in range(CMP_QB):
            n = CMP_QB * step + qb
            bias = hct_ref[pl.ds(pl.multiple_of(ncp - 8 - 8 * n + c0, 8), CMP_CHUNK), :]
            visible = (BLK * n + lane_q - CMP_STRIDE * (c0 + row_c) - (CMP_LEN - 1)) >= 0
            s = jnp.where(visible, raw[qb] + bias, NEG)
            s_scr[qb, ch] = s
            out.append(jnp.maximum(mxs[qb], jnp.max(s, axis=0, keepdims=True)))
        return tuple(out)

    mxs = lax.fori_loop(0, n_chunks, score_chunk, tuple(jnp.full((1, lanes), NEG, F32) for _ in range(CMP_QB)))
    ms = [jnp.where(mx <= 0.5 * NEG, 0.0, mx) for mx in mxs]
    oacc_scr[...] = jnp.zeros(oacc_scr.shape, F32)
    iacc_scr[...] = jnp.zeros(iacc_scr.shape, F32)

    def prob_chunk(ch, ls):
        c0 = pl.multiple_of(ch * CMP_CHUNK, CMP_CHUNK)
        vct = vct_ref[0, 0, :, pl.ds(c0, CMP_CHUNK)]
        r0 = pl.multiple_of(ch * (CMP_CHUNK // 4), 16)
        wmt = wmapt_ref[pl.ds(r0, IMP_ROWS), pl.ds(c0, CMP_CHUNK)]
        both = jnp.concatenate([vct, wmt], axis=0)
        ps = [jnp.exp2(s_scr[qb, ch] - ms[qb]) for qb in range(CMP_QB)]
        prods = [jnp.dot(both, p.astype(BF16), preferred_element_type=F32) for p in ps]
        for qb in range(CMP_QB):
            oacc_scr[qb] += prods[qb][:HEAD_DIM]
            iacc_scr[qb, pl.ds(r0, IMP_ROWS), :] += prods[qb][HEAD_DIM:]
        return tuple(ls[qb] + jnp.sum(ps[qb], axis=0, keepdims=True) for qb in range(CMP_QB))

    ls = lax.fori_loop(0, n_chunks, prob_chunk, tuple(jnp.zeros((1, lanes), F32) for _ in range(CMP_QB)))
    imps = []
    for qb in range(CMP_QB):
        inv = 1.0 / jnp.maximum(ls[qb], 1e-30)
        o_t = oacc_scr[qb] * inv
        o_ref[0, qb * BLK:(qb + 1) * BLK, :] = jnp.concatenate(
            [_transpose_pad(o_t[:, h * BLK:(h + 1) * BLK], 0)[:, :HEAD_DIM] for h in range(NSA_HEADS)], axis=1)
        w = iacc_scr[qb, 0:ns, :] * inv
        imps.append(functools.reduce(jnp.add, [w[:, h * BLK:(h + 1) * BLK] for h in range(NSA_HEADS)]))

    imp = jnp.concatenate(imps, axis=1)
    shape = (ns, CMP_QB * BLK)
    j = _iota(shape, 0)
    cur = (CMP_QB * BLK * step + _iota(shape, 1)) // SEL_BLK
    valid = j <= cur
    forced = valid & ((j == 0) | (j > cur - N_LOCAL))
    score = jnp.where(forced, PICKED, jnp.where(valid, imp, NEG))
    jf = j.astype(F32)
    for _ in range(min(N_SEL, ns) - (N_LOCAL + 1)):
        best = jnp.max(score, axis=0, keepdims=True)
        first = jnp.min(jnp.where(score == best, jf, float(ns)), axis=0, keepdims=True)
        score = jnp.where(jf == first, PICKED, score)
    pen_ref[0, 0] = jnp.where(valid & (score == PICKED), 0.0, NEG)


def _cmp_attn(qt, kcv, kcv_t, hct, wmapt):
    b, nb, _, lanes = qt.shape
    ncp = kcv.shape[2]
    ns = wmapt.shape[0] - IMP_PAD
    s = nb * BLK
    return pl.pallas_call(
        _cmp_kernel,
        grid=(b, nb // CMP_QB),
        in_specs=[pl.BlockSpec((1, CMP_QB, HEAD_DIM, lanes), lambda bi, i: (bi, i, 0, 0)),
                  pl.BlockSpec((1, 1, ncp, HEAD_DIM), lambda bi, i: (0, bi, 0, 0)),
                  pl.BlockSpec((1, 1, HEAD_DIM, ncp), lambda bi, i: (1, bi, 0, 0)),
                  pl.BlockSpec((2 * ncp, lanes), lambda bi, i: (0, 0)),
                  pl.BlockSpec((ns + IMP_PAD, ncp), lambda bi, i: (0, 0))],
        out_specs=[pl.BlockSpec((1, CMP_QB * BLK, NSA_HEADS * HEAD_DIM), lambda bi, i: (bi, i, 0)),
                   pl.BlockSpec((1, 1, ns, CMP_QB * BLK), lambda bi, i: (bi, i, 0, 0))],
        out_shape=[jax.ShapeDtypeStruct((b, s, NSA_HEADS * HEAD_DIM), F32),
                   jax.ShapeDtypeStruct((b, nb // CMP_QB, ns, CMP_QB * BLK), F32)],
        scratch_shapes=[pltpu.VMEM((CMP_QB, ncp // CMP_CHUNK, CMP_CHUNK, lanes), F32),
                        pltpu.VMEM((CMP_QB, HEAD_DIM, lanes), F32),
                        pltpu.VMEM((CMP_QB, ns + IMP_PAD, lanes), F32)],
        compiler_params=_cparams(("parallel", "arbitrary")),
        name="cmp_attn",
    )(qt, kcv, kcv_t, hct, wmapt)


def _sel_kernel(qt_ref, k_ref, vt_ref, pen_ref, band_ref, o_ref, s_scr, p_scr, mx_scr, m_scr, alpha_scr, acc_scr):
    n = SEL_QB * pl.program_id(1)
    qlanes = NSA_HEADS * BLK
    lanes = SEL_QB * qlanes
    sub = SEL_TILE // BLK
    per = BLK // SEL_BLK
    qt = jnp.concatenate([qt_ref[0, qb] for qb in range(SEL_QB)], axis=1)
    lane = _iota((BLK, lanes), 1)
    t = BLK * (n + lane // qlanes) + (lane & (BLK - 1))

    def scores_into(kt, s_buf, mx_buf, far):
        rows = sub * per
        pen = pen_ref[0, 0, pl.ds(pl.multiple_of(kt * rows, rows), rows), :]
        pen = jnp.concatenate([pen[:, qb * BLK:(qb + 1) * BLK] for qb in range(SEL_QB) for _ in range(NSA_HEADS)],
                              axis=1)
        tail = jnp.concatenate([pen, jnp.zeros((HEAD_DIM - rows, lanes), F32)], axis=0).astype(BF16)
        q_aug = jnp.concatenate([qt, tail], axis=0)
        k0 = pl.multiple_of(kt * SEL_TILE, SEL_TILE)
        s = jnp.dot(k_ref[0, pl.ds(k0, SEL_TILE), :], q_aug, preferred_element_type=F32)
        if far:
            s_buf[...] = s
            mx_buf[...] = jnp.max(s, axis=0, keepdims=True)
        else:
            mxs = []
            for qb in range(SEL_QB):
                cols = slice(qb * qlanes, (qb + 1) * qlanes)
                parts = [s[u * BLK:(u + 1) * BLK, cols] + band_ref[jnp.clip(n + qb - sub * kt - u, 0, N_BAND - 1)]
                         for u in range(sub)]
                for u in range(sub):
                    s_buf[u * BLK:(u + 1) * BLK, cols] = parts[u]
                mxs.append(functools.reduce(jnp.maximum, [jnp.max(v, axis=0, keepdims=True) for v in parts]))
            mx_buf[...] = jnp.concatenate(mxs, axis=1)

    def add_values(kt, p_buf):
        k0 = pl.multiple_of(kt * SEL_TILE, SEL_TILE)
        pv = jnp.dot(vt_ref[0, :, pl.ds(k0, SEL_TILE)], p_buf[...], preferred_element_type=F32)
        acc_scr[...] = alpha_scr[...] * acc_scr[...] + pv

    def softmax_into(kt, s_buf, mx_buf, p_buf, causal):
        def piece(u):
            s = s_buf[u * BLK:(u + 1) * BLK, :]
            if causal:
                s = jnp.where(kt * SEL_TILE + u * BLK + _iota((BLK, lanes), 0) <= t, s, NEG)
            return s
        m_i = m_scr[...]
        if causal:
            mx = functools.reduce(jnp.maximum, [jnp.max(piece(u), axis=0, keepdims=True) for u in range(sub)])
        else:
            mx = mx_buf[...]
        m_new = jnp.maximum(m_i, mx)
        for u in range(sub):
            p_buf[u * BLK:(u + 1) * BLK, :] = jnp.exp2(piece(u) - m_new).astype(BF16)
        m_scr[...] = m_new
        alpha_scr[...] = jnp.exp2(m_i - m_new)

    def stage(i, cur, nxt, p_cur, p_prev, far):
        add_values(jnp.maximum(i - 1, 0), p_prev)
        scores_into(i + 1, *nxt, far)
        softmax_into(i, *cur, p_cur, False)

    def finish(i, cur, p_cur, p_prev):
        add_values(jnp.maximum(i - 1, 0), p_prev)
        softmax_into(i, *cur, p_cur, True)
        add_values(i, p_cur)
        acc = acc_scr[...]
        o_t = acc[:HEAD_DIM] / acc[HEAD_DIM:HEAD_DIM + 1]
        for qb in range(SEL_QB):
            o_ref[0, qb * BLK:(qb + 1) * BLK, :] = jnp.concatenate(
                [_transpose_pad(o_t[:, qb * qlanes + h * BLK:qb * qlanes + (h + 1) * BLK], 0)[:, :HEAD_DIM]
                 for h in range(NSA_HEADS)], axis=1)

    s0, s1 = (s_scr.at[0], mx_scr.at[0]), (s_scr.at[1], mx_scr.at[1])
    p0, p1 = p_scr.at[0], p_scr.at[1]
    m_scr[...] = jnp.full((1, lanes), NEG, F32)
    alpha_scr[...] = jnp.ones((1, lanes), F32)
    acc_scr[...] = jnp.zeros(acc_scr.shape, F32)
    p1[...] = jnp.zeros((SEL_TILE, lanes), BF16)
    scores_into(0, *s0, False)
    last = (BLK * n + BLK - 1) // SEL_TILE
    n_far = jnp.maximum((n - (N_BAND - 2)) // sub, 0)
    far_pairs = jnp.minimum(jnp.maximum((n_far - 1) // 2, 0), last // 2)

    def pair(j, _, far):
        stage(2 * j, s0, s1, p0, p1, far)
        stage(2 * j + 1, s1, s0, p1, p0, far)
        return 0

    lax.fori_loop(0, far_pairs, functools.partial(pair, far=True), 0)
    lax.fori_loop(far_pairs, last // 2, functools.partial(pair, far=False), 0)

    @pl.when(last % 2 == 1)
    def _():
        stage(last - 1, s0, s1, p0, p1, False)
        finish(last, s1, p1, p0)

    @pl.when(last % 2 == 0)
    def _():
        finish(last, s0, p0, p1)


def _sel_attn(qt, ks, vst, pen, band_t):
    b, nb, _, qlanes = qt.shape
    s = nb * BLK
    ns = pen.shape[2]
    lanes = SEL_QB * qlanes
    per_cmp = CMP_QB // SEL_QB
    return pl.pallas_call(
        _sel_kernel,
        grid=(b, nb // SEL_QB),
        in_specs=[pl.BlockSpec((1, SEL_QB, HEAD_DIM, qlanes), lambda bi, i: (bi, i, 0, 0)),
                  pl.BlockSpec((1, s, 2 * HEAD_DIM), lambda bi, i: (bi, 0, 0)),
                  pl.BlockSpec((1, VT_ROWS, s), lambda bi, i: (bi, 0, 0)),
                  pl.BlockSpec((1, 1, ns, SEL_QB * BLK), lambda bi, i: (bi, i // per_cmp, 0, i % per_cmp)),
                  pl.BlockSpec((N_BAND, BLK, qlanes), lambda bi, i: (0, 0, 0))],
        out_specs=pl.BlockSpec((1, SEL_QB * BLK, NSA_HEADS * HEAD_DIM), lambda bi, i: (bi, i, 0)),
        out_shape=jax.ShapeDtypeStruct((b, s, NSA_HEADS * HEAD_DIM), F32),
        scratch_shapes=[pltpu.VMEM((2, SEL_TILE, lanes), F32),
                        pltpu.VMEM((2, SEL_TILE, lanes), BF16),
                        pltpu.VMEM((2, 1, lanes), F32),
                        pltpu.VMEM((1, lanes), F32),
                        pltpu.VMEM((1, lanes), F32),
                        pltpu.VMEM((VT_ROWS, lanes), F32)],
        compiler_params=_cparams(("parallel", "arbitrary")),
        name="sel_attn",
    )(qt, ks, vst, pen, band_t)


def _band_kernel(*refs, window, groups, use_sinks, base2, n_parts):
    qt_ref = refs[0]
    k_refs = refs[1:1 + n_parts]
    vt_refs = refs[1 + n_parts:1 + 2 * n_parts]
    band_ref, sink_ref, o_ref = refs[1 + 2 * n_parts:]
    step = pl.program_id(1)
    n_prev = window // BLK
    nk = (n_prev + 1) * BLK
    lanes = 4 * BLK
    row = _iota((nk, lanes), 0)
    dist = n_prev * BLK + (_iota((nk, lanes), 1) & (BLK - 1)) - row
    in_window = (dist >= 0) & (dist < window)
    chains = [(g, qb) for g in range(groups) for qb in range(BAND_QB)]
    kwin = [jnp.concatenate([r[0, g] for r in k_refs], axis=0) for g in range(groups)]
    vtwin = [jnp.concatenate([r[0, g] for r in vt_refs], axis=1) for g in range(groups)]
    bias = [jnp.concatenate([band_ref[n_prev - u, :, g * lanes:(g + 1) * lanes] for u in range(n_prev + 1)], axis=0)
            for g in range(groups)]
    scores = [jnp.dot(kwin[g][qb * BLK:qb * BLK + nk], qt_ref[0, qb, :, g * lanes:(g + 1) * lanes],
                      preferred_element_type=F32) for g, qb in chains]
    probs, maxes = [], []
    for (g, qb), s in zip(chains, scores):
        s = jnp.where(in_window & (row >= (n_prev - BAND_QB * step - qb) * BLK), s + bias[g], NEG)
        mx = jnp.max(s, axis=0, keepdims=True)
        if use_sinks:
            mx = jnp.maximum(mx, sink_ref[g])
        probs.append((jnp.exp2(s - mx) if base2 else jnp.exp(s - mx)).astype(BF16))
        maxes.append(mx)
    accs = [jnp.dot(vtwin[g][:, qb * BLK:qb * BLK + nk], p, preferred_element_type=F32)
            for (g, qb), p in zip(chains, probs)]
    for (g, qb), acc, mx in zip(chains, accs, maxes):
        den = acc[HEAD_DIM:HEAD_DIM + 1]
        if use_sinks:
            den = den + jnp.exp(sink_ref[g] - mx)
        o_t = acc[:HEAD_DIM] / den
        o_ref[0, qb * BLK:(qb + 1) * BLK, 4 * g * HEAD_DIM:4 * (g + 1) * HEAD_DIM] = jnp.concatenate(
            [_transpose_pad(o_t[:, h * BLK:(h + 1) * BLK], 0)[:, :HEAD_DIM] for h in range(4)], axis=1)


def _band_attn(qt, k, vt, band_t, sinks, window, use_sinks, base2, name):
    b, nb, _, width = qt.shape
    g = width // (4 * BLK)
    s = nb * BLK
    n_prev = window // BLK
    tq = BAND_QB * BLK
    back = n_prev * BLK
    if back % tq == 0:
        piece = tq
        starts = [lambda i, d=d: jnp.maximum(i - d, 0) for d in range(back // tq, 0, -1)]
    else:
        assert tq % back == 0
        piece = back
        starts = [lambda i: jnp.maximum(i * (tq // back) - 1, 0)]
    k_specs = [pl.BlockSpec((1, g, piece, HEAD_DIM), lambda bi, i, f=f: (bi, 0, f(i), 0)) for f in starts]
    k_specs.append(pl.BlockSpec((1, g, tq, HEAD_DIM), lambda bi, i: (bi, 0, i, 0)))
    vt_specs = [pl.BlockSpec((1, g, VT_ROWS, piece), lambda bi, i, f=f: (bi, 0, 0, f(i))) for f in starts]
    vt_specs.append(pl.BlockSpec((1, g, VT_ROWS, tq), lambda bi, i: (bi, 0, 0, i)))
    n_parts = len(k_specs)
    return pl.pallas_call(
        functools.partial(_band_kernel, window=window, groups=g, use_sinks=use_sinks, base2=base2,
                          n_parts=n_parts),
        grid=(b, s // tq),
        in_specs=[pl.BlockSpec((1, BAND_QB, HEAD_DIM, width), lambda bi, i: (bi, i, 0, 0))] + k_specs + vt_specs
                 + [pl.BlockSpec((n_prev + 1, BLK, width), lambda bi, i: (0, 0, 0)),
                    pl.BlockSpec((g, 1, 4 * BLK), lambda bi, i: (0, 0, 0))],
        out_specs=pl.BlockSpec((1, tq, g * 4 * HEAD_DIM), lambda bi, i: (bi, i, 0)),
        out_shape=jax.ShapeDtypeStruct((b, s, g * 4 * HEAD_DIM), F32),
        compiler_params=_cparams(("parallel", "arbitrary")),
        name=name,
    )(qt, *([k] * n_parts), *([vt] * n_parts), band_t, sinks)


def _sb_kernel(qt_ref, k_ref, vt_ref, later_ref, o_ref):
    n0 = SB_QB * pl.program_id(1)
    lanes = SB_HEADS * BLK
    lane_q = _iota((BLK, lanes), 1) & (BLK - 1)
    row = _iota((BLK, lanes), 0)
    later = later_ref[...]
    zero = jnp.zeros((HEAD_DIM, BLK), BF16)
    blocks = range(SB_QB)
    q_bd = [jnp.concatenate(
        [jnp.concatenate([zero] * h + [qt_ref[0, qb, :, h * BLK:(h + 1) * BLK]] + [zero] * (SB_HEADS - 1 - h), axis=0)
         for h in range(SB_HEADS)], axis=1) for qb in blocks]

    def cond(c):
        it, carry, _ = c
        return (n0 + SB_QB - 1 - SB_GROUP * it >= 0) & (jnp.max(functools.reduce(jnp.maximum, carry)) > SB_LOG2_FLOOR)

    def body(c):
        it, carry, acc = c
        k_all, vt_all, before = [], [], []
        for qb in blocks:
            top = n0 + qb - SB_GROUP * it
            ks, vts, befores = [], [], []
            for j in range(SB_GROUP - 1, -1, -1):
                kb = top - j
                k0 = pl.multiple_of(jnp.maximum(kb, 0) * BLK, BLK)
                ks.append(k_ref[0, pl.ds(k0, BLK), :])
                vts.append(vt_ref[0, :, pl.ds(k0, BLK)])
                befores.append((kb >= 0) & (k0 + row < BLK * (n0 + qb) + lane_q))
            k_all.append(jnp.concatenate(ks, axis=0))
            vt_all.append(jnp.concatenate(vts, axis=1))
            before.append(jnp.concatenate(befores, axis=0))
        z = [jnp.dot(k_all[qb], q_bd[qb], preferred_element_type=F32) for qb in blocks]
        log_not, split = [], []
        for qb in blocks:
            softplus = jnp.maximum(z[qb], 0.0) + jnp.log2(1.0 + jnp.exp2(-jnp.abs(z[qb])))
            ln = jnp.where(before[qb], -softplus, 0.0)
            hi = ln.astype(BF16)
            lo = (ln - hi.astype(F32)).astype(BF16)
            log_not.append(ln)
            split.append(jnp.concatenate([hi, lo], axis=0))
        tail = [jnp.dot(later, split[qb], preferred_element_type=F32) for qb in blocks]
        a = [jnp.where(before[qb], jnp.exp2(z[qb] + log_not[qb] + tail[qb] + carry[qb]), 0.0).astype(BF16)
             for qb in blocks]
        pv = [jnp.dot(vt_all[qb], a[qb], preferred_element_type=F32) for qb in blocks]
        acc = tuple(tuple(acc[qb][h] + pv[qb][h * HEAD_DIM:(h + 1) * HEAD_DIM, h * BLK:(h + 1) * BLK]
                          for h in range(SB_HEADS)) for qb in blocks)
        carry = tuple(carry[qb] + jnp.sum(log_not[qb], axis=0, keepdims=True) for qb in blocks)
        return it + 1, carry, acc

    init = (0, tuple(jnp.zeros((1, lanes), F32) for _ in blocks),
            tuple(tuple(jnp.zeros((HEAD_DIM, BLK), F32) for _ in range(SB_HEADS)) for _ in blocks))
    _, _, acc = lax.while_loop(cond, body, init)
    for qb in blocks:
        o_ref[0, qb * BLK:(qb + 1) * BLK, :] = jnp.concatenate(
            [_transpose_pad(a, 0)[:, :HEAD_DIM] for a in acc[qb]], axis=1)


def _sb_attn(qt, k, vt):
    b, nb, _, lanes = qt.shape
    s = nb * BLK
    width = SB_HEADS * HEAD_DIM
    gk = SB_GROUP * BLK
    later = (jnp.arange(2 * gk)[None, :] % gk > jnp.arange(gk)[:, None]).astype(BF16)
    return pl.pallas_call(
        _sb_kernel,
        grid=(b, nb // SB_QB),
        in_specs=[pl.BlockSpec((1, SB_QB, HEAD_DIM, lanes), lambda bi, n: (bi, n, 0, 0)),
                  pl.BlockSpec((1, s, width), lambda bi, n: (bi, 0, 0)),
                  pl.BlockSpec((1, width, s), lambda bi, n: (bi, 0, 0)),
                  pl.BlockSpec((gk, 2 * gk), lambda bi, n: (0, 0))],
        out_specs=pl.BlockSpec((1, SB_QB * BLK, width), lambda bi, n: (bi, n, 0)),
        out_shape=jax.ShapeDtypeStruct((b, s, width), F32),
        compiler_params=_cparams(("parallel", "arbitrary")),
        name="sb_attn",
    )(qt, k, vt, later)


def _out_kernel(x_ref, oc_ref, os_ref, ow_ref, gt_ref, swa_ref, sb_ref, gg_ref, w_ref, gp_ref, o_ref):
    wa = NSA_HEADS * HEAD_DIM
    wb = wa + SWA_HEADS * HEAD_DIM
    gg = gg_ref[...]
    lane = _iota((OUT_ROWS, wa), 1)
    groups = [slice(r0, r0 + OUT_ROWS) for r0 in range(0, x_ref.shape[1], OUT_ROWS)]
    mixes = []
    for rows in groups:
        gates = jax.nn.sigmoid(gt_ref[0, rows])

        def spread(branch, gates=gates):
            cols = [jnp.broadcast_to(gates[:, branch * NSA_HEADS + i:branch * NSA_HEADS + i + 1], (OUT_ROWS, wa))
                    for i in range(NSA_HEADS)]
            out = cols[NSA_HEADS - 1]
            for i in range(NSA_HEADS - 2, -1, -1):
                out = jnp.where(lane < (i + 1) * HEAD_DIM, cols[i], out)
            return out

        o_nsa = spread(0) * oc_ref[0, rows] + spread(1) * os_ref[0, rows] + spread(2) * ow_ref[0, rows]
        mixes.append(jnp.concatenate([_rms(o_nsa, gg[:, :wa]), _rms(swa_ref[0, rows], gg[:, wa:wb]),
                                      _rms(sb_ref[0, rows], gg[:, wb:])], axis=1).astype(BF16))
    ys = [jnp.dot(mix, w_ref[...], preferred_element_type=F32) for mix in mixes]
    for rows, y in zip(groups, ys):
        o_ref[0, rows] = x_ref[0, rows] + _rms(y, gp_ref[...])


def _out_proj(x, o_cmp, o_sel, o_win, gates, o_swa, o_sb, gg, w, gp, tm):
    b, s, d = x.shape
    row = lambda n: pl.BlockSpec((1, tm, n), lambda bi, i: (bi, i, 0))
    const = lambda a, c: pl.BlockSpec((a, c), lambda bi, i: (0, 0))
    return pl.pallas_call(
        _out_kernel,
        grid=(b, s // tm),
        in_specs=[row(d), row(256), row(256), row(256), row(128), row(512), row(256),
                  const(1, d), const(d, d), const(1, d)],
        out_specs=row(d),
        out_shape=jax.ShapeDtypeStruct((b, s, d), F32),
        compiler_params=_cparams(("parallel", "parallel")),
        name="out_proj",
    )(x, o_cmp, o_sel, o_win, gates, o_swa, o_sb, gg, w, gp)


def _ffn_kernel(x_ref, g1_ref, wg_ref, wu_ref, wd_ref, g2_ref, o_ref, *, chunk):
    x = x_ref[...]
    h = _rms(x, g1_ref[...]).astype(BF16)
    dff = wg_ref.shape[1]
    f = jnp.zeros(x.shape, F32)
    for a0 in range(0, dff, chunk):
        a1 = min(a0 + chunk, dff)
        gate = jnp.dot(h, wg_ref[:, a0:a1], preferred_element_type=F32)
        up = jnp.dot(h, wu_ref[:, a0:a1], preferred_element_type=F32)
        a = (gate * jax.nn.sigmoid(gate) * up).astype(BF16)
        f = f + jnp.dot(a, wd_ref[a0:a1, :], preferred_element_type=F32)
    o_ref[...] = x + _rms(f, g2_ref[...])


def _ffn(x, g1, wg, wu, wd, g2, tm):
    t, d = x.shape
    dff = wg.shape[1]
    row = pl.BlockSpec((tm, d), lambda i: (i, 0))
    const = lambda a, c: pl.BlockSpec((a, c), lambda i: (0, 0))
    weight = lambda a, c: pl.BlockSpec((a, c), lambda i: (0, 0), pipeline_mode=pl.Buffered(1))
    return pl.pallas_call(
        functools.partial(_ffn_kernel, chunk=FFN_CHUNK),
        grid=(t // tm,),
        in_specs=[row, const(1, d), weight(d, dff), weight(d, dff), weight(dff, d), const(1, d)],
        out_specs=row,
        out_shape=jax.ShapeDtypeStruct((t, d), F32),
        compiler_params=_cparams(("parallel",)),
        name="ffn",
    )(x, g1, wg, wu, wd, g2)


def _permute_w_in(w):
    scale = 1.0 / math.sqrt(HEAD_DIM)
    nq, kc, vc, rest, ng = w[:, :256], w[:, 256:320], w[:, 320:384], w[:, 384:640], w[:, 640:652]
    sq, skv = w[:, 652:1164], w[:, 1164:1420]
    bq, bkv = w[:, 1420:1676], w[:, 1676:2188]
    pad = jnp.zeros((w.shape[0], 128 - ng.shape[1]), w.dtype)
    return jnp.concatenate([nq * scale, rest, sq * scale, skv, bq * scale, bkv, kc, vc, ng, pad], axis=1)


def _cmp_to_sel(nc_pad, nc, ns):
    c0 = jnp.arange(nc_pad)[:, None] * CMP_STRIDE
    s0 = jnp.arange(ns)[None, :] * SEL_BLK
    ov = jnp.minimum(c0 + CMP_LEN, s0 + SEL_BLK) - jnp.maximum(c0, s0)
    w = jnp.clip(ov, 0, None).astype(F32) / CMP_LEN
    return jnp.where(jnp.arange(nc_pad)[:, None] < nc, w, 0.0).astype(BF16)


def kernel(x, rel_bias, ln_attn_pre, w_in, nsa_cmp_pos, nsa_phi_w1, nsa_phi_b1, nsa_phi_w2, swa_sinks,
           grp_norm_g, w_out, ln_attn_post, ln_ffn_pre, ffn_w_gate, ffn_w_up, ffn_w_down, ln_ffn_post):
    b, s, d = x.shape
    depth = w_in.shape[0]
    nch = s // CMP_STRIDE
    ns = s // SEL_BLK
    t = b * s

    band_nsa_t = _band_table_t(rel_bias, N_BAND, NSA_HEADS, 0, True)
    band_swa_t = _band_table_t(rel_bias, SWA_WINDOW // BLK + 1, SWA_HEADS, NSA_HEADS, False)
    hct = _cmp_table(rel_bias, nch)
    wmapt = jnp.pad(_cmp_to_sel(nch, nch - 1, ns).T, ((0, IMP_PAD), (0, 0)))
    no_sinks = jnp.zeros((1, 1, 4 * BLK), F32)

    for l in range(depth):
        outs = _proj(x, ln_attn_pre[l][None], _permute_w_in(w_in[l]).astype(BF16), 512)
        qt, ks, vs, kw, vw, sqt, sk, sv, bq, bk, bv, ckv, gates = outs
        kcv, kcv_t = _compress(ckv.reshape(2, b, nch, CMP_STRIDE * HEAD_DIM),
                               nsa_cmp_pos[l].reshape(2, 2, CMP_STRIDE * HEAD_DIM),
                               nsa_phi_w1[l].reshape(2, 2, CMP_STRIDE * HEAD_DIM, CMP_HIDDEN).astype(BF16),
                               nsa_phi_b1[l][:, None, :], nsa_phi_w2[l].astype(BF16))
        o_cmp, selmask = _cmp_attn(qt, kcv, kcv_t, hct, wmapt)
        o_sel = _sel_attn(qt, ks, vs, selmask, band_nsa_t)
        o_win = _band_attn(qt, kw[:, None], vw[:, None], band_nsa_t, no_sinks, NSA_WINDOW, False, True,
                           "win_attn")
        sinks = jnp.broadcast_to(swa_sinks[l].reshape(SWA_KV_HEADS, 1, 4, 1),
                                 (SWA_KV_HEADS, 1, 4, BLK)).reshape(SWA_KV_HEADS, 1, 4 * BLK)
        o_swa = _band_attn(sqt, sk, sv, band_swa_t, sinks, SWA_WINDOW, True, False, "swa_attn")
        o_sb = _sb_attn(bq, bk, bv)
        x = _out_proj(x, o_cmp, o_sel, o_win, gates, o_swa, o_sb, grp_norm_g[l][None], w_out[l].astype(BF16),
                      ln_attn_post[l][None], 1024)
        x = _ffn(x.reshape(t, d), ln_ffn_pre[l][None], ffn_w_gate[l].astype(BF16), ffn_w_up[l].astype(BF16),
                 ffn_w_down[l].astype(BF16), ln_ffn_post[l][None], 512).reshape(b, s, d)
    return x
```

```python
import functools
import math

import jax
import jax.numpy as jnp
from jax import lax
from jax.experimental import pallas as pl
from jax.experimental.pallas import tpu as pltpu

F32 = jnp.float32
BF16 = jnp.bfloat16

HEAD_DIM = 64
BLK = 128
NSA_HEADS = 4
CMP_LEN = 32
CMP_STRIDE = 16
CMP_HIDDEN = 256
SEL_BLK = 64
N_SEL = 16
N_LOCAL = 2
NSA_WINDOW = 512
SWA_HEADS = 8
SWA_KV_HEADS = 2
SWA_WINDOW = 128
SB_HEADS = 4
N_BUCKETS = 32
T5_MAX_DISTANCE = 4096
RMS_EPS = 1e-6
NEG = -1e30
PICKED = -3e38
LOG2E = 1.0 / math.log(2.0)
SB_LOG2_FLOOR = -104.0 * LOG2E

SEL_TILE = 512
SEL_QB = 2
CMP_QB = 4
CMP_CHUNK = 128
IMP_PAD = 16
IMP_ROWS = CMP_CHUNK // 4 + IMP_PAD
BAND_QB = 8
SB_GROUP = 3
SB_QB = 4
VT_ROWS = HEAD_DIM + 16
OUT_ROWS = 128
FFN_CHUNK = 1024
N_BAND = 25
VMEM_LIMIT = 56 * 1024 * 1024


def _cparams(sem):
    return pltpu.CompilerParams(dimension_semantics=sem, vmem_limit_bytes=VMEM_LIMIT)


def _rms(x, g):
    ms = jnp.mean(x * x, axis=-1, keepdims=True)
    return x * lax.rsqrt(ms + RMS_EPS) * g


def _iota(shape, axis):
    return lax.broadcasted_iota(jnp.int32, shape, axis)


def _bias_of_dist(dist, tab_ref, head):
    n = jnp.maximum(dist, 0)
    nf = jnp.maximum(n, 1).astype(F32)
    exact = N_BUCKETS // 2
    large = exact + (jnp.log(nf / exact) / math.log(T5_MAX_DISTANCE / exact)
                     * (N_BUCKETS - exact)).astype(jnp.int32)
    large = jnp.minimum(large, N_BUCKETS - 1)
    bucket = jnp.where(n < exact, n, large)
    val = jnp.full(dist.shape, tab_ref[0, head], F32)
    for k in range(1, N_BUCKETS):
        val = jnp.where(bucket == k, tab_ref[k, head], val)
    return val


def _band_table_t_kernel(tab_ref, o_ref, *, head_off, shift):
    h = pl.program_id(0) + head_off

    def tile(m, _):
        dist = BLK * m + _iota((BLK, BLK), 1) - _iota((BLK, BLK), 0)
        bias = _bias_of_dist(dist, tab_ref, h)
        o_ref[m] = (bias - tab_ref[N_BUCKETS - 1, h]) * LOG2E if shift else bias
        return 0

    lax.fori_loop(0, o_ref.shape[0], tile, 0)


def _band_table_t(rel_bias, n_band, n_heads, head_off, shift):
    return pl.pallas_call(
        functools.partial(_band_table_t_kernel, head_off=head_off, shift=shift),
        grid=(n_heads,),
        in_specs=[pl.BlockSpec(memory_space=pltpu.SMEM)],
        out_specs=pl.BlockSpec((n_band, BLK, BLK), lambda h: (0, 0, h)),
        out_shape=jax.ShapeDtypeStruct((n_band, BLK, n_heads * BLK), F32),
        compiler_params=_cparams(("arbitrary",)),
        name="band_table_t",
    )(rel_bias)


def _cmp_table_kernel(tab_ref, o_ref, *, ncp):
    h = pl.program_id(0)
    rel = _iota((2 * ncp, BLK), 0) - (ncp - 8)
    dist = _iota((2 * ncp, BLK), 1) - CMP_STRIDE * rel - (CMP_LEN - 1)
    o_ref[...] = _bias_of_dist(dist, tab_ref, h) * LOG2E


def _cmp_table(rel_bias, ncp):
    return pl.pallas_call(
        functools.partial(_cmp_table_kernel, ncp=ncp),
        grid=(NSA_HEADS,),
        in_specs=[pl.BlockSpec(memory_space=pltpu.SMEM)],
        out_specs=pl.BlockSpec((2 * ncp, BLK), lambda h: (0, h)),
        out_shape=jax.ShapeDtypeStruct((2 * ncp, NSA_HEADS * BLK), F32),
        compiler_params=_cparams(("arbitrary",)),
        name="cmp_table",
    )(rel_bias)


def _proj_kernel(x_ref, g_ref, w_ref, qt_ref, ks_ref, vs_ref, kw_ref, vw_ref, sqt_ref, sk_ref, sv_ref,
                 bq_ref, bk_ref, bv_ref, ckv_ref, gt_ref):
    h = _rms(x_ref[0], g_ref[...]).astype(BF16)
    tm = x_ref.shape[1]

    def seg(a, b):
        return jnp.dot(h, w_ref[:, a:b], preferred_element_type=F32)

    def heads(ref, a, n):
        y = seg(a, a + n * HEAD_DIM).astype(BF16)
        for i in range(n):
            ref[0, i] = y[:, i * HEAD_DIM:(i + 1) * HEAD_DIM]

    def queries_t(ref, a, n, scale=None):
        yt = seg(a, a + n * HEAD_DIM).T
        if scale is not None:
            yt = yt * scale
        for j in range(tm // BLK):
            for i in range(n):
                ref[0, j, :, i * BLK:(i + 1) * BLK] = (
                    yt[i * HEAD_DIM:(i + 1) * HEAD_DIM, j * BLK:(j + 1) * BLK].astype(BF16))

    ones_rows = jnp.where(_iota((VT_ROWS - HEAD_DIM, tm), 0) == 0, 1.0, 0.0)

    def values_t(yt):
        return jnp.concatenate([yt, ones_rows], axis=0).astype(BF16)

    queries_t(qt_ref, 0, NSA_HEADS, LOG2E)
    y = seg(256, 512)
    vs_ref[0] = values_t(y[:, 0:128].T[64:128])
    vw_ref[0] = values_t(y[:, 128:256].T[64:128])
    y = y.astype(BF16)
    blk_in_tile = ((pl.program_id(1) * tm + _iota((tm, HEAD_DIM), 0)) & (SEL_TILE - 1)) // SEL_BLK
    onehot = jnp.where(_iota((tm, HEAD_DIM), 1) == blk_in_tile, 1.0, 0.0).astype(BF16)
    ks_ref[0] = jnp.concatenate([y[:, 0:64], onehot], axis=1)
    kw_ref[0] = y[:, 128:192]
    queries_t(sqt_ref, 512, SWA_HEADS)
    heads(sk_ref, 1024, SWA_KV_HEADS)
    yt = seg(1152, 1280).T
    for i in range(SWA_KV_HEADS):
        sv_ref[0, i] = values_t(yt[i * HEAD_DIM:(i + 1) * HEAD_DIM])
    queries_t(bq_ref, 1280, SB_HEADS, LOG2E)
    bk_ref[0] = seg(1536, 1792).astype(BF16)
    bv_ref[0] = seg(1792, 2048).T.astype(BF16)
    y = seg(2048, 2304)
    ckv_ref[0, 0] = y[:, 0:64]
    ckv_ref[1, 0] = y[:, 64:128]
    gt_ref[0] = y[:, 128:256]


def _proj(x, g, w, tm):
    b, s, d = x.shape
    nw = w.shape[1]
    hd = lambda n: pl.BlockSpec((1, n, tm, HEAD_DIM), lambda bi, i: (bi, 0, i, 0))
    flat = lambda n: pl.BlockSpec((1, tm, n), lambda bi, i: (bi, i, 0))
    hshape = lambda n: jax.ShapeDtypeStruct((b, n, s, HEAD_DIM), BF16)
    fshape = lambda n, dt: jax.ShapeDtypeStruct((b, s, n), dt)
    qt_spec = lambda n: pl.BlockSpec((1, tm // BLK, HEAD_DIM, n * BLK), lambda bi, i: (bi, i, 0, 0))
    qt_shape = lambda n: jax.ShapeDtypeStruct((b, s // BLK, HEAD_DIM, n * BLK), BF16)
    vt_spec = pl.BlockSpec((1, VT_ROWS, tm), lambda bi, i: (bi, 0, i))
    vt_shape = jax.ShapeDtypeStruct((b, VT_ROWS, s), BF16)
    return pl.pallas_call(
        _proj_kernel,
        grid=(b, s // tm),
        in_specs=[pl.BlockSpec((1, tm, d), lambda bi, i: (bi, i, 0)),
                  pl.BlockSpec((1, d), lambda bi, i: (0, 0)),
                  pl.BlockSpec((d, nw), lambda bi, i: (0, 0))],
        out_specs=[qt_spec(NSA_HEADS), flat(128), vt_spec, flat(64), vt_spec,
                   qt_spec(SWA_HEADS), hd(SWA_KV_HEADS),
                   pl.BlockSpec((1, SWA_KV_HEADS, VT_ROWS, tm), lambda bi, i: (bi, 0, 0, i)),
                   qt_spec(SB_HEADS), flat(256),
                   pl.BlockSpec((1, SB_HEADS * HEAD_DIM, tm), lambda bi, i: (bi, 0, i)),
                   pl.BlockSpec((2, 1, tm, HEAD_DIM), lambda bi, i: (0, bi, i, 0)), flat(128)],
        out_shape=[qt_shape(NSA_HEADS), fshape(128, BF16), vt_shape, fshape(64, BF16), vt_shape,
                   qt_shape(SWA_HEADS), hshape(SWA_KV_HEADS),
                   jax.ShapeDtypeStruct((b, SWA_KV_HEADS, VT_ROWS, s), BF16),
                   qt_shape(SB_HEADS), fshape(256, BF16),
                   jax.ShapeDtypeStruct((b, SB_HEADS * HEAD_DIM, s), BF16),
                   jax.ShapeDtypeStruct((2, b, s, HEAD_DIM), F32), fshape(128, F32)],
        compiler_params=_cparams(("parallel", "parallel")),
        name="proj",
    )(x, g, w)


def _compress_kernel(c_ref, pos_ref, w1_ref, b1_ref, w2_ref, o_ref, ot_ref):
    c = c_ref[0, 0]
    nch = c.shape[0]
    xa = (c + pos_ref[0, 0:1]).astype(BF16)
    xb = (c + pos_ref[0, 1:2]).astype(BF16)
    p = jnp.dot(xa, w1_ref[0, 0], preferred_element_type=F32)
    q = jnp.dot(xb, w1_ref[0, 1], preferred_element_type=F32)
    hid = p + pltpu.roll(q, nch - 1, 0) + b1_ref[0]
    hid = hid * jax.nn.sigmoid(hid)
    out = jnp.dot(hid.astype(BF16), w2_ref[0], preferred_element_type=F32)
    row = _iota(out.shape, 0)
    out = jnp.where(row < nch - 1, out, 0.0)
    o_ref[0, 0] = out.astype(BF16)
    ot_ref[0, 0] = jnp.concatenate([out, jnp.zeros_like(out)], axis=1).T[:HEAD_DIM].astype(BF16)


def _compress(ckv, pos, w1, b1, w2):
    _, b, nch, cw = ckv.shape
    return pl.pallas_call(
        _compress_kernel,
        grid=(2, b),
        in_specs=[pl.BlockSpec((1, 1, nch, cw), lambda j, bi: (j, bi, 0, 0)),
                  pl.BlockSpec((1, 2, cw), lambda j, bi: (j, 0, 0)),
                  pl.BlockSpec((1, 2, cw, CMP_HIDDEN), lambda j, bi: (j, 0, 0, 0)),
                  pl.BlockSpec((1, 1, CMP_HIDDEN), lambda j, bi: (j, 0, 0)),
                  pl.BlockSpec((1, CMP_HIDDEN, HEAD_DIM), lambda j, bi: (j, 0, 0))],
        out_specs=[pl.BlockSpec((1, 1, nch, HEAD_DIM), lambda j, bi: (j, bi, 0, 0)),
                   pl.BlockSpec((1, 1, HEAD_DIM, nch), lambda j, bi: (j, bi, 0, 0))],
        out_shape=[jax.ShapeDtypeStruct((2, b, nch, HEAD_DIM), BF16),
                   jax.ShapeDtypeStruct((2, b, HEAD_DIM, nch), BF16)],
        compiler_params=_cparams(("parallel", "parallel")),
        name="compress",
    )(ckv, pos, w1, b1, w2)


def _transpose_pad(x, axis):
    pad = jnp.zeros(x.shape, x.dtype)
    return jnp.concatenate([x, pad], axis=axis).T


def _cmp_kernel(qt_ref, kc_ref, vct_ref, hct_ref, wmapt_ref, o_ref, pen_ref, s_scr, oacc_scr, iacc_scr):
    step = pl.program_id(1)
    ncp = kc_ref.shape[2]
    ns = wmapt_ref.shape[0] - IMP_PAD
    lanes = NSA_HEADS * BLK
    row_c = _iota((CMP_CHUNK, lanes), 0)
    lane_q = _iota((CMP_CHUNK, lanes), 1) & (BLK - 1)
    n_chunks = (CMP_QB * step) // (CMP_CHUNK // 8) + 1

    def score_chunk(ch, mxs):
        c0 = pl.multiple_of(ch * CMP_CHUNK, CMP_CHUNK)
        kc = kc_ref[0, 0, pl.ds(c0, CMP_CHUNK), :]
        raw = [jnp.dot(kc, qt_ref[0, qb], preferred_element_type=F32) for qb in range(CMP_QB)]
        out = []
        for qb in range(CMP_QB):
            n = CMP_QB * step + qb
            bias = hct_ref[pl.ds(pl.multiple_of(ncp - 8 - 8 * n + c0, 8), CMP_CHUNK), :]
            visible = (BLK * n + lane_q - CMP_STRIDE * (c0 + row_c) - (CMP_LEN - 1)) >= 0
            s = jnp.where(visible, raw[qb] + bias, NEG)
            s_scr[qb, ch] = s
            out.append(jnp.maximum(mxs[qb], jnp.max(s, axis=0, keepdims=True)))
        return tuple(out)

    mxs = lax.fori_loop(0, n_chunks, score_chunk, tuple(jnp.full((1, lanes), NEG, F32) for _ in range(CMP_QB)))
    ms = [jnp.where(mx <= 0.5 * NEG, 0.0, mx) for mx in mxs]
    oacc_scr[...] = jnp.zeros(oacc_scr.shape, F32)
    iacc_scr[...] = jnp.zeros(iacc_scr.shape, F32)

    def prob_chunk(ch, ls):
        c0 = pl.multiple_of(ch * CMP_CHUNK, CMP_CHUNK)
        vct = vct_ref[0, 0, :, pl.ds(c0, CMP_CHUNK)]
        r0 = pl.multiple_of(ch * (CMP_CHUNK // 4), 16)
        wmt = wmapt_ref[pl.ds(r0, IMP_ROWS), pl.ds(c0, CMP_CHUNK)]
        both = jnp.concatenate([vct, wmt], axis=0)
        ps = [jnp.exp2(s_scr[qb, ch] - ms[qb]) for qb in range(CMP_QB)]
        prods = [jnp.dot(both, p.astype(BF16), preferred_element_type=F32) for p in ps]
        for qb in range(CMP_QB):
            oacc_scr[qb] += prods[qb][:HEAD_DIM]
            iacc_scr[qb, pl.ds(r0, IMP_ROWS), :] += prods[qb][HEAD_DIM:]
        return tuple(ls[qb] + jnp.sum(ps[qb], axis=0, keepdims=True) for qb in range(CMP_QB))

    ls = lax.fori_loop(0, n_chunks, prob_chunk, tuple(jnp.zeros((1, lanes), F32) for _ in range(CMP_QB)))
    imps = []
    for qb in range(CMP_QB):
        inv = 1.0 / jnp.maximum(ls[qb], 1e-30)
        o_t = oacc_scr[qb] * inv
        o_ref[0, qb * BLK:(qb + 1) * BLK, :] = jnp.concatenate(
            [_transpose_pad(o_t[:, h * BLK:(h + 1) * BLK], 0)[:, :HEAD_DIM] for h in range(NSA_HEADS)], axis=1)
        w = iacc_scr[qb, 0:ns, :] * inv
        imps.append(functools.reduce(jnp.add, [w[:, h * BLK:(h + 1) * BLK] for h in range(NSA_HEADS)]))

    imp = jnp.concatenate(imps, axis=1)
    shape = (ns, CMP_QB * BLK)
    j = _iota(shape, 0)
    cur = (CMP_QB * BLK * step + _iota(shape, 1)) // SEL_BLK
    valid = j <= cur
    forced = valid & ((j == 0) | (j > cur - N_LOCAL))
    score = jnp.where(forced, PICKED, jnp.where(valid, imp, NEG))
    jf = j.astype(F32)
    for _ in range(min(N_SEL, ns) - (N_LOCAL + 1)):
        best = jnp.max(score, axis=0, keepdims=True)
        first = jnp.min(jnp.where(score == best, jf, float(ns)), axis=0, keepdims=True)
        score = jnp.where(jf == first, PICKED, score)
    pen_ref[0, 0] = jnp.where(valid & (score == PICKED), 0.0, NEG)


def _cmp_attn(qt, kcv, kcv_t, hct, wmapt):
    b, nb, _, lanes = qt.shape
    ncp = kcv.shape[2]
    ns = wmapt.shape[0] - IMP_PAD
    s = nb * BLK
    return pl.pallas_call(
        _cmp_kernel,
        grid=(b, nb // CMP_QB),
        in_specs=[pl.BlockSpec((1, CMP_QB, HEAD_DIM, lanes), lambda bi, i: (bi, i, 0, 0)),
                  pl.BlockSpec((1, 1, ncp, HEAD_DIM), lambda bi, i: (0, bi, 0, 0)),
                  pl.BlockSpec((1, 1, HEAD_DIM, ncp), lambda bi, i: (1, bi, 0, 0)),
                  pl.BlockSpec((2 * ncp, lanes), lambda bi, i: (0, 0)),
                  pl.BlockSpec((ns + IMP_PAD, ncp), lambda bi, i: (0, 0))],
        out_specs=[pl.BlockSpec((1, CMP_QB * BLK, NSA_HEADS * HEAD_DIM), lambda bi, i: (bi, i, 0)),
                   pl.BlockSpec((1, 1, ns, CMP_QB * BLK), lambda bi, i: (bi, i, 0, 0))],
        out_shape=[jax.ShapeDtypeStruct((b, s, NSA_HEADS * HEAD_DIM), F32),
                   jax.ShapeDtypeStruct((b, nb // CMP_QB, ns, CMP_QB * BLK), F32)],
        scratch_shapes=[pltpu.VMEM((CMP_QB, ncp // CMP_CHUNK, CMP_CHUNK, lanes), F32),
                        pltpu.VMEM((CMP_QB, HEAD_DIM, lanes), F32),
                        pltpu.VMEM((CMP_QB, ns + IMP_PAD, lanes), F32)],
        compiler_params=_cparams(("parallel", "arbitrary")),
        name="cmp_attn",
    )(qt, kcv, kcv_t, hct, wmapt)


def _sel_kernel(qt_ref, k_ref, vt_ref, pen_ref, band_ref, o_ref, s_scr, p_scr, mx_scr, m_scr, alpha_scr, acc_scr):
    n = SEL_QB * pl.program_id(1)
    qlanes = NSA_HEADS * BLK
    lanes = SEL_QB * qlanes
    sub = SEL_TILE // BLK
    per = BLK // SEL_BLK
    qt = jnp.concatenate([qt_ref[0, qb] for qb in range(SEL_QB)], axis=1)
    lane = _iota((BLK, lanes), 1)
    t = BLK * (n + lane // qlanes) + (lane & (BLK - 1))

    def scores_into(kt, s_buf, mx_buf, far):
        rows = sub * per
        pen = pen_ref[0, 0, pl.ds(pl.multiple_of(kt * rows, rows), rows), :]
        pen = jnp.concatenate([pen[:, qb * BLK:(qb + 1) * BLK] for qb in range(SEL_QB) for _ in range(NSA_HEADS)],
                              axis=1)
        tail = jnp.concatenate([pen, jnp.zeros((HEAD_DIM - rows, lanes), F32)], axis=0).astype(BF16)
        q_aug = jnp.concatenate([qt, tail], axis=0)
        k0 = pl.multiple_of(kt * SEL_TILE, SEL_TILE)
        s = jnp.dot(k_ref[0, pl.ds(k0, SEL_TILE), :], q_aug, preferred_element_type=F32)
        if far:
            s_buf[...] = s
            mx_buf[...] = jnp.max(s, axis=0, keepdims=True)
        else:
            mxs = []
            for qb in range(SEL_QB):
                cols = slice(qb * qlanes, (qb + 1) * qlanes)
                parts = [s[u * BLK:(u + 1) * BLK, cols] + band_ref[jnp.clip(n + qb - sub * kt - u, 0, N_BAND - 1)]
                         for u in range(sub)]
                for u in range(sub):
                    s_buf[u * BLK:(u + 1) * BLK, cols] = parts[u]
                mxs.append(functools.reduce(jnp.maximum, [jnp.max(v, axis=0, keepdims=True) for v in parts]))
            mx_buf[...] = jnp.concatenate(mxs, axis=1)

    def add_values(kt, p_buf):
        k0 = pl.multiple_of(kt * SEL_TILE, SEL_TILE)
        pv = jnp.dot(vt_ref[0, :, pl.ds(k0, SEL_TILE)], p_buf[...], preferred_element_type=F32)
        acc_scr[...] = alpha_scr[...] * acc_scr[...] + pv

    def softmax_into(kt, s_buf, mx_buf, p_buf, causal):
        def piece(u):
            s = s_buf[u * BLK:(u + 1) * BLK, :]
            if causal:
                s = jnp.where(kt * SEL_TILE + u * BLK + _iota((BLK, lanes), 0) <= t, s, NEG)
            return s
        m_i = m_scr[...]
        if causal:
            mx = functools.reduce(jnp.maximum, [jnp.max(piece(u), axis=0, keepdims=True) for u in range(sub)])
        else:
            mx = mx_buf[...]
        m_new = jnp.maximum(m_i, mx)
        for u in range(sub):
            p_buf[u * BLK:(u + 1) * BLK, :] = jnp.exp2(piece(u) - m_new).astype(BF16)
        m_scr[...] = m_new
        alpha_scr[...] = jnp.exp2(m_i - m_new)

    def stage(i, cur, nxt, p_cur, p_prev, far):
        add_values(jnp.maximum(i - 1, 0), p_prev)
        scores_into(i + 1, *nxt, far)
        softmax_into(i, *cur, p_cur, False)

    def finish(i, cur, p_cur, p_prev):
        add_values(jnp.maximum(i - 1, 0), p_prev)
        softmax_into(i, *cur, p_cur, True)
        add_values(i, p_cur)
        acc = acc_scr[...]
        o_t = acc[:HEAD_DIM] / acc[HEAD_DIM:HEAD_DIM + 1]
        for qb in range(SEL_QB):
            o_ref[0, qb * BLK:(qb + 1) * BLK, :] = jnp.concatenate(
                [_transpose_pad(o_t[:, qb * qlanes + h * BLK:qb * qlanes + (h + 1) * BLK], 0)[:, :HEAD_DIM]
                 for h in range(NSA_HEADS)], axis=1)

    s0, s1 = (s_scr.at[0], mx_scr.at[0]), (s_scr.at[1], mx_scr.at[1])
    p0, p1 = p_scr.at[0], p_scr.at[1]
    m_scr[...] = jnp.full((1, lanes), NEG, F32)
    alpha_scr[...] = jnp.ones((1, lanes), F32)
    acc_scr[...] = jnp.zeros(acc_scr.shape, F32)
    p1[...] = jnp.zeros((SEL_TILE, lanes), BF16)
    scores_into(0, *s0, False)
    last = (BLK * n + BLK - 1) // SEL_TILE
    n_far = jnp.maximum((n - (N_BAND - 2)) // sub, 0)
    far_pairs = jnp.minimum(jnp.maximum((n_far - 1) // 2, 0), last // 2)

    def pair(j, _, far):
        stage(2 * j, s0, s1, p0, p1, far)
        stage(2 * j + 1, s1, s0, p1, p0, far)
        return 0

    lax.fori_loop(0, far_pairs, functools.partial(pair, far=True), 0)
    lax.fori_loop(far_pairs, last // 2, functools.partial(pair, far=False), 0)

    @pl.when(last % 2 == 1)
    def _():
        stage(last - 1, s0, s1, p0, p1, False)
        finish(last, s1, p1, p0)

    @pl.when(last % 2 == 0)
    def _():
        finish(last, s0, p0, p1)


def _sel_attn(qt, ks, vst, pen, band_t):
    b, nb, _, qlanes = qt.shape
    s = nb * BLK
    ns = pen.shape[2]
    lanes = SEL_QB * qlanes
    per_cmp = CMP_QB // SEL_QB
    return pl.pallas_call(
        _sel_kernel,
        grid=(b, nb // SEL_QB),
        in_specs=[pl.BlockSpec((1, SEL_QB, HEAD_DIM, qlanes), lambda bi, i: (bi, i, 0, 0)),
                  pl.BlockSpec((1, s, 2 * HEAD_DIM), lambda bi, i: (bi, 0, 0)),
                  pl.BlockSpec((1, VT_ROWS, s), lambda bi, i: (bi, 0, 0)),
                  pl.BlockSpec((1, 1, ns, SEL_QB * BLK), lambda bi, i: (bi, i // per_cmp, 0, i % per_cmp)),
                  pl.BlockSpec((N_BAND, BLK, qlanes), lambda bi, i: (0, 0, 0))],
        out_specs=pl.BlockSpec((1, SEL_QB * BLK, NSA_HEADS * HEAD_DIM), lambda bi, i: (bi, i, 0)),
        out_shape=jax.ShapeDtypeStruct((b, s, NSA_HEADS * HEAD_DIM), F32),
        scratch_shapes=[pltpu.VMEM((2, SEL_TILE, lanes), F32),
                        pltpu.VMEM((2, SEL_TILE, lanes), BF16),
                        pltpu.VMEM((2, 1, lanes), F32),
                        pltpu.VMEM((1, lanes), F32),
                        pltpu.VMEM((1, lanes), F32),
                        pltpu.VMEM((VT_ROWS, lanes), F32)],
        compiler_params=_cparams(("parallel", "arbitrary")),
        name="sel_attn",
    )(qt, ks, vst, pen, band_t)


def _band_kernel(*refs, window, groups, use_sinks, base2, n_parts):
    qt_ref = refs[0]
    k_refs = refs[1:1 + n_parts]
    vt_refs = refs[1 + n_parts:1 + 2 * n_parts]
    band_ref, sink_ref, o_ref = refs[1 + 2 * n_parts:]
    step = pl.program_id(1)
    n_prev = window // BLK
    nk = (n_prev + 1) * BLK
    lanes = 4 * BLK
    row = _iota((nk, lanes), 0)
    dist = n_prev * BLK + (_iota((nk, lanes), 1) & (BLK - 1)) - row
    in_window = (dist >= 0) & (dist < window)
    chains = [(g, qb) for g in range(groups) for qb in range(BAND_QB)]
    kwin = [jnp.concatenate([r[0, g] for r in k_refs], axis=0) for g in range(groups)]
    vtwin = [jnp.concatenate([r[0, g] for r in vt_refs], axis=1) for g in range(groups)]
    bias = [jnp.concatenate([band_ref[n_prev - u, :, g * lanes:(g + 1) * lanes] for u in range(n_prev + 1)], axis=0)
            for g in range(groups)]
    scores = [jnp.dot(kwin[g][qb * BLK:qb * BLK + nk], qt_ref[0, qb, :, g * lanes:(g + 1) * lanes],
                      preferred_element_type=F32) for g, qb in chains]
    probs, maxes = [], []
    for (g, qb), s in zip(chains, scores):
        s = jnp.where(in_window & (row >= (n_prev - BAND_QB * step - qb) * BLK), s + bias[g], NEG)
        mx = jnp.max(s, axis=0, keepdims=True)
        if use_sinks:
            mx = jnp.maximum(mx, sink_ref[g])
        probs.append((jnp.exp2(s - mx) if base2 else jnp.exp(s - mx)).astype(BF16))
        maxes.append(mx)
    accs = [jnp.dot(vtwin[g][:, qb * BLK:qb * BLK + nk], p, preferred_element_type=F32)
            for (g, qb), p in zip(chains, probs)]
    for (g, qb), acc, mx in zip(chains, accs, maxes):
        den = acc[HEAD_DIM:HEAD_DIM + 1]
        if use_sinks:
            den = den + jnp.exp(sink_ref[g] - mx)
        o_t = acc[:HEAD_DIM] / den
        o_ref[0, qb * BLK:(qb + 1) * BLK, 4 * g * HEAD_DIM:4 * (g + 1) * HEAD_DIM] = jnp.concatenate(
            [_transpose_pad(o_t[:, h * BLK:(h + 1) * BLK], 0)[:, :HEAD_DIM] for h in range(4)], axis=1)


def _band_attn(qt, k, vt, band_t, sinks, window, use_sinks, base2, name):
    b, nb, _, width = qt.shape
    g = width // (4 * BLK)
    s = nb * BLK
    n_prev = window // BLK
    tq = BAND_QB * BLK
    back = n_prev * BLK
    if back % tq == 0:
        piece = tq
        starts = [lambda i, d=d: jnp.maximum(i - d, 0) for d in range(back // tq, 0, -1)]
    else:
        assert tq % back == 0
        piece = back
        starts = [lambda i: jnp.maximum(i * (tq // back) - 1, 0)]
    k_specs = [pl.BlockSpec((1, g, piece, HEAD_DIM), lambda bi, i, f=f: (bi, 0, f(i), 0)) for f in starts]
    k_specs.append(pl.BlockSpec((1, g, tq, HEAD_DIM), lambda bi, i: (bi, 0, i, 0)))
    vt_specs = [pl.BlockSpec((1, g, VT_ROWS, piece), lambda bi, i, f=f: (bi, 0, 0, f(i))) for f in starts]
    vt_specs.append(pl.BlockSpec((1, g, VT_ROWS, tq), lambda bi, i: (bi, 0, 0, i)))
    n_parts = len(k_specs)
    return pl.pallas_call(
        functools.partial(_band_kernel, window=window, groups=g, use_sinks=use_sinks, base2=base2,
                          n_parts=n_parts),
        grid=(b, s // tq),
        in_specs=[pl.BlockSpec((1, BAND_QB, HEAD_DIM, width), lambda bi, i: (bi, i, 0, 0))] + k_specs + vt_specs
                 + [pl.BlockSpec((n_prev + 1, BLK, width), lambda bi, i: (0, 0, 0)),
                    pl.BlockSpec((g, 1, 4 * BLK), lambda bi, i: (0, 0, 0))],
        out_specs=pl.BlockSpec((1, tq, g * 4 * HEAD_DIM), lambda bi, i: (bi, i, 0)),
        out_shape=jax.ShapeDtypeStruct((b, s, g * 4 * HEAD_DIM), F32),
        compiler_params=_cparams(("parallel", "arbitrary")),
        name=name,
    )(qt, *([k] * n_parts), *([vt] * n_parts), band_t, sinks)


def _sb_kernel(qt_ref, k_ref, vt_ref, later_ref, o_ref):
    n0 = SB_QB * pl.program_id(1)
    lanes = SB_HEADS * BLK
    lane_q = _iota((BLK, lanes), 1) & (BLK - 1)
    row = _iota((BLK, lanes), 0)
    later = later_ref[...]
    zero = jnp.zeros((HEAD_DIM, BLK), BF16)
    blocks = range(SB_QB)
    q_bd = [jnp.concatenate(
        [jnp.concatenate([zero] * h + [qt_ref[0, qb, :, h * BLK:(h + 1) * BLK]] + [zero] * (SB_HEADS - 1 - h), axis=0)
         for h in range(SB_HEADS)], axis=1) for qb in blocks]

    def cond(c):
        it, carry, _ = c
        return (n0 + SB_QB - 1 - SB_GROUP * it >= 0) & (jnp.max(functools.reduce(jnp.maximum, carry)) > SB_LOG2_FLOOR)

    def body(c):
        it, carry, acc = c
        k_all, vt_all, before = [], [], []
        for qb in blocks:
            top = n0 + qb - SB_GROUP * it
            ks, vts, befores = [], [], []
            for j in range(SB_GROUP - 1, -1, -1):
                kb = top - j
                k0 = pl.multiple_of(jnp.maximum(kb, 0) * BLK, BLK)
                ks.append(k_ref[0, pl.ds(k0, BLK), :])
                vts.append(vt_ref[0, :, pl.ds(k0, BLK)])
                befores.append((kb >= 0) & (k0 + row < BLK * (n0 + qb) + lane_q))
            k_all.append(jnp.concatenate(ks, axis=0))
            vt_all.append(jnp.concatenate(vts, axis=1))
            before.append(jnp.concatenate(befores, axis=0))
        z = [jnp.dot(k_all[qb], q_bd[qb], preferred_element_type=F32) for qb in blocks]
        log_not, split = [], []
        for qb in blocks:
            softplus = jnp.maximum(z[qb], 0.0) + jnp.log2(1.0 + jnp.exp2(-jnp.abs(z[qb])))
            ln = jnp.where(before[qb], -softplus, 0.0)
            hi = ln.astype(BF16)
            lo = (ln - hi.astype(F32)).astype(BF16)
            log_not.append(ln)
            split.append(jnp.concatenate([hi, lo], axis=0))
        tail = [jnp.dot(later, split[qb], preferred_element_type=F32) for qb in blocks]
        a = [jnp.where(before[qb], jnp.exp2(z[qb] + log_not[qb] + tail[qb] + carry[qb]), 0.0).astype(BF16)
             for qb in blocks]
        pv = [jnp.dot(vt_all[qb], a[qb], preferred_element_type=F32) for qb in blocks]
        acc = tuple(tuple(acc[qb][h] + pv[qb][h * HEAD_DIM:(h + 1) * HEAD_DIM, h * BLK:(h + 1) * BLK]
                          for h in range(SB_HEADS)) for qb in blocks)
        carry = tuple(carry[qb] + jnp.sum(log_not[qb], axis=0, keepdims=True) for qb in blocks)
        return it + 1, carry, acc

    init = (0, tuple(jnp.zeros((1, lanes), F32) for _ in blocks),
            tuple(tuple(jnp.zeros((HEAD_DIM, BLK), F32) for _ in range(SB_HEADS)) for _ in blocks))
    _, _, acc = lax.while_loop(cond, body, init)
    for qb in blocks:
        o_ref[0, qb * BLK:(qb + 1) * BLK, :] = jnp.concatenate(
            [_transpose_pad(a, 0)[:, :HEAD_DIM] for a in acc[qb]], axis=1)


def _sb_attn(qt, k, vt):
    b, nb, _, lanes = qt.shape
    s = nb * BLK
    width = SB_HEADS * HEAD_DIM
    gk = SB_GROUP * BLK
    later = (jnp.arange(2 * gk)[None, :] % gk > jnp.arange(gk)[:, None]).astype(BF16)
    return pl.pallas_call(
        _sb_kernel,
        grid=(b, nb // SB_QB),
        in_specs=[pl.BlockSpec((1, SB_QB, HEAD_DIM, lanes), lambda bi, n: (bi, n, 0, 0)),
                  pl.BlockSpec((1, s, width), lambda bi, n: (bi, 0, 0)),
                  pl.BlockSpec((1, width, s), lambda bi, n: (bi, 0, 0)),
                  pl.BlockSpec((gk, 2 * gk), lambda bi, n: (0, 0))],
        out_specs=pl.BlockSpec((1, SB_QB * BLK, width), lambda bi, n: (bi, n, 0)),
        out_shape=jax.ShapeDtypeStruct((b, s, width), F32),
        compiler_params=_cparams(("parallel", "arbitrary")),
        name="sb_attn",
    )(qt, k, vt, later)


def _out_kernel(x_ref, oc_ref, os_ref, ow_ref, gt_ref, swa_ref, sb_ref, gg_ref, w_ref, gp_ref, o_ref):
    wa = NSA_HEADS * HEAD_DIM
    wb = wa + SWA_HEADS * HEAD_DIM
    gg = gg_ref[...]
    lane = _iota((OUT_ROWS, wa), 1)
    groups = [slice(r0, r0 + OUT_ROWS) for r0 in range(0, x_ref.shape[1], OUT_ROWS)]
    mixes = []
    for rows in groups:
        gates = jax.nn.sigmoid(gt_ref[0, rows])

        def spread(branch, gates=gates):
            cols = [jnp.broadcast_to(gates[:, branch * NSA_HEADS + i:branch * NSA_HEADS + i + 1], (OUT_ROWS, wa))
                    for i in range(NSA_HEADS)]
            out = cols[NSA_HEADS - 1]
            for i in range(NSA_HEADS - 2, -1, -1):
                out = jnp.where(lane < (i + 1) * HEAD_DIM, cols[i], out)
            return out

        o_nsa = spread(0) * oc_ref[0, rows] + spread(1) * os_ref[0, rows] + spread(2) * ow_ref[0, rows]
        mixes.append(jnp.concatenate([_rms(o_nsa, gg[:, :wa]), _rms(swa_ref[0, rows], gg[:, wa:wb]),
                                      _rms(sb_ref[0, rows], gg[:, wb:])], axis=1).astype(BF16))
    ys = [jnp.dot(mix, w_ref[...], preferred_element_type=F32) for mix in mixes]
    for rows, y in zip(groups, ys):
        o_ref[0, rows] = x_ref[0, rows] + _rms(y, gp_ref[...])


def _out_proj(x, o_cmp, o_sel, o_win, gates, o_swa, o_sb, gg, w, gp, tm):
    b, s, d = x.shape
    row = lambda n: pl.BlockSpec((1, tm, n), lambda bi, i: (bi, i, 0))
    const = lambda a, c: pl.BlockSpec((a, c), lambda bi, i: (0, 0))
    return pl.pallas_call(
        _out_kernel,
        grid=(b, s // tm),
        in_specs=[row(d), row(256), row(256), row(256), row(128), row(512), row(256),
                  const(1, d), const(d, d), const(1, d)],
        out_specs=row(d),
        out_shape=jax.ShapeDtypeStruct((b, s, d), F32),
        compiler_params=_cparams(("parallel", "parallel")),
        name="out_proj",
    )(x, o_cmp, o_sel, o_win, gates, o_swa, o_sb, gg, w, gp)


def _ffn_kernel(x_ref, g1_ref, wg_ref, wu_ref, wd_ref, g2_ref, o_ref, *, chunk):
    x = x_ref[...]
    h = _rms(x, g1_ref[...]).astype(BF16)
    dff = wg_ref.shape[1]
    f = jnp.zeros(x.shape, F32)
    for a0 in range(0, dff, chunk):
        a1 = min(a0 + chunk, dff)
        gate = jnp.dot(h, wg_ref[:, a0:a1], preferred_element_type=F32)
        up = jnp.dot(h, wu_ref[:, a0:a1], preferred_element_type=F32)
        a = (gate * jax.nn.sigmoid(gate) * up).astype(BF16)
        f = f + jnp.dot(a, wd_ref[a0:a1, :], preferred_element_type=F32)
    o_ref[...] = x + _rms(f, g2_ref[...])


def _ffn(x, g1, wg, wu, wd, g2, tm):
    t, d = x.shape
    dff = wg.shape[1]
    row = pl.BlockSpec((tm, d), lambda i: (i, 0))
    const = lambda a, c: pl.BlockSpec((a, c), lambda i: (0, 0))
    weight = lambda a, c: pl.BlockSpec((a, c), lambda i: (0, 0), pipeline_mode=pl.Buffered(1))
    return pl.pallas_call(
        functools.partial(_ffn_kernel, chunk=FFN_CHUNK),
        grid=(t // tm,),
        in_specs=[row, const(1, d), weight(d, dff), weight(d, dff), weight(dff, d), const(1, d)],
        out_specs=row,
        out_shape=jax.ShapeDtypeStruct((t, d), F32),
        compiler_params=_cparams(("parallel",)),
        name="ffn",
    )(x, g1, wg, wu, wd, g2)


def _permute_w_in(w):
    scale = 1.0 / math.sqrt(HEAD_DIM)
    nq, kc, vc, rest, ng = w[:, :256], w[:, 256:320], w[:, 320:384], w[:, 384:640], w[:, 640:652]
    sq, skv = w[:, 652:1164], w[:, 1164:1420]
    bq, bkv = w[:, 1420:1676], w[:, 1676:2188]
    pad = jnp.zeros((w.shape[0], 128 - ng.shape[1]), w.dtype)
    return jnp.concatenate([nq * scale, rest, sq * scale, skv, bq * scale, bkv, kc, vc, ng, pad], axis=1)


def _cmp_to_sel(nc_pad, nc, ns):
    c0 = jnp.arange(nc_pad)[:, None] * CMP_STRIDE
    s0 = jnp.arange(ns)[None, :] * SEL_BLK
    ov = jnp.minimum(c0 + CMP_LEN, s0 + SEL_BLK) - jnp.maximum(c0, s0)
    w = jnp.clip(ov, 0, None).astype(F32) / CMP_LEN
    return jnp.where(jnp.arange(nc_pad)[:, None] < nc, w, 0.0).astype(BF16)


def kernel(x, rel_bias, ln_attn_pre, w_in, nsa_cmp_pos, nsa_phi_w1, nsa_phi_b1, nsa_phi_w2, swa_sinks,
           grp_norm_g, w_out, ln_attn_post, ln_ffn_pre, ffn_w_gate, ffn_w_up, ffn_w_down, ln_ffn_post):
    b, s, d = x.shape
    depth = w_in.shape[0]
    nch = s // CMP_STRIDE
    ns = s // SEL_BLK
    t = b * s

    band_nsa_t = _band_table_t(rel_bias, N_BAND, NSA_HEADS, 0, True)
    band_swa_t = _band_table_t(rel_bias, SWA_WINDOW // BLK + 1, SWA_HEADS, NSA_HEADS, False)
    hct = _cmp_table(rel_bias, nch)
    wmapt = jnp.pad(_cmp_to_sel(nch, nch - 1, ns).T, ((0, IMP_PAD), (0, 0)))
    no_sinks = jnp.zeros((1, 1, 4 * BLK), F32)

    for l in range(depth):
        outs = _proj(x, ln_attn_pre[l][None], _permute_w_in(w_in[l]).astype(BF16), 512)
        qt, ks, vs, kw, vw, sqt, sk, sv, bq, bk, bv, ckv, gates = outs
        kcv, kcv_t = _compress(ckv.reshape(2, b, nch, CMP_STRIDE * HEAD_DIM),
                               nsa_cmp_pos[l].reshape(2, 2, CMP_STRIDE * HEAD_DIM),
                               nsa_phi_w1[l].reshape(2, 2, CMP_STRIDE * HEAD_DIM, CMP_HIDDEN).astype(BF16),
                               nsa_phi_b1[l][:, None, :], nsa_phi_w2[l].astype(BF16))
        o_cmp, selmask = _cmp_attn(qt, kcv, kcv_t, hct, wmapt)
        o_sel = _sel_attn(qt, ks, vs, selmask, band_nsa_t)
        o_win = _band_attn(qt, kw[:, None], vw[:, None], band_nsa_t, no_sinks, NSA_WINDOW, False, True,
                           "win_attn")
        sinks = jnp.broadcast_to(swa_sinks[l].reshape(SWA_KV_HEADS, 1, 4, 1),
                                 (SWA_KV_HEADS, 1, 4, BLK)).reshape(SWA_KV_HEADS, 1, 4 * BLK)
        o_swa = _band_attn(sqt, sk, sv, band_swa_t, sinks, SWA_WINDOW, True, False, "swa_attn")
        o_sb = _sb_attn(bq, bk, bv)
        x = _out_proj(x, o_cmp, o_sel, o_win, gates, o_swa, o_sb, grp_norm_g[l][None], w_out[l].astype(BF16),
                      ln_attn_post[l][None], 1024)
        x = _ffn(x.reshape(t, d), ln_ffn_pre[l][None], ffn_w_gate[l].astype(BF16), ffn_w_up[l].astype(BF16),
                 ffn_w_down[l].astype(BF16), ln_ffn_post[l][None], 512).reshape(b, s, d)
    return x
```

```python
import functools
import math

import jax
import jax.numpy as jnp
from jax import lax
from jax.experimental import pallas as pl
from jax.experimental.pallas import tpu as pltpu

F32 = jnp.float32
BF16 = jnp.bfloat16

HEAD_DIM = 64
BLK = 128
NSA_HEADS = 4
CMP_LEN = 32
CMP_STRIDE = 16
CMP_HIDDEN = 256
SEL_BLK = 64
N_SEL = 16
N_LOCAL = 2
NSA_WINDOW = 512
SWA_HEADS = 8
SWA_KV_HEADS = 2
SWA_WINDOW = 128
SB_HEADS = 4
N_BUCKETS = 32
T5_MAX_DISTANCE = 4096
RMS_EPS = 1e-6
NEG = -1e30
PICKED = -3e38
LOG2E = 1.0 / math.log(2.0)
SB_LOG2_FLOOR = -104.0 * LOG2E

SEL_TILE = 512
SEL_QB = 2
CMP_QB = 8
CMP_CHUNK = 128
IMP_PAD = 16
IMP_ROWS = CMP_CHUNK // 4 + IMP_PAD
BAND_QB = 8
SB_GROUP = 3
SB_QB = 8
VT_ROWS = HEAD_DIM + 16
OUT_ROWS = 128
FFN_CHUNK = 1024
N_BAND = 25
VMEM_LIMIT = 56 * 1024 * 1024


def _cparams(sem):
    return pltpu.CompilerParams(dimension_semantics=sem, vmem_limit_bytes=VMEM_LIMIT)


def _rms(x, g):
    ms = jnp.mean(x * x, axis=-1, keepdims=True)
    return x * lax.rsqrt(ms + RMS_EPS) * g


def _iota(shape, axis):
    return lax.broadcasted_iota(jnp.int32, shape, axis)


def _bias_of_dist(dist, tab_ref, head):
    n = jnp.maximum(dist, 0)
    nf = jnp.maximum(n, 1).astype(F32)
    exact = N_BUCKETS // 2
    large = exact + (jnp.log(nf / exact) / math.log(T5_MAX_DISTANCE / exact)
                     * (N_BUCKETS - exact)).astype(jnp.int32)
    large = jnp.minimum(large, N_BUCKETS - 1)
    bucket = jnp.where(n < exact, n, large)
    val = jnp.full(dist.shape, tab_ref[0, head], F32)
    for k in range(1, N_BUCKETS):
        val = jnp.where(bucket == k, tab_ref[k, head], val)
    return val


def _band_table_t_kernel(tab_ref, o_ref, *, head_off, shift):
    h = pl.program_id(0) + head_off

    def tile(m, _):
        dist = BLK * m + _iota((BLK, BLK), 1) - _iota((BLK, BLK), 0)
        bias = _bias_of_dist(dist, tab_ref, h)
        o_ref[m] = (bias - tab_ref[N_BUCKETS - 1, h]) * LOG2E if shift else bias
        return 0

    lax.fori_loop(0, o_ref.shape[0], tile, 0)


def _band_table_t(rel_bias, n_band, n_heads, head_off, shift):
    return pl.pallas_call(
        functools.partial(_band_table_t_kernel, head_off=head_off, shift=shift),
        grid=(n_heads,),
        in_specs=[pl.BlockSpec(memory_space=pltpu.SMEM)],
        out_specs=pl.BlockSpec((n_band, BLK, BLK), lambda h: (0, 0, h)),
        out_shape=jax.ShapeDtypeStruct((n_band, BLK, n_heads * BLK), F32),
        compiler_params=_cparams(("arbitrary",)),
        name="band_table_t",
    )(rel_bias)


def _cmp_table_kernel(tab_ref, o_ref, *, ncp):
    h = pl.program_id(0)
    rel = _iota((2 * ncp, BLK), 0) - (ncp - 8)
    dist = _iota((2 * ncp, BLK), 1) - CMP_STRIDE * rel - (CMP_LEN - 1)
    o_ref[...] = _bias_of_dist(dist, tab_ref, h) * LOG2E


def _cmp_table(rel_bias, ncp):
    return pl.pallas_call(
        functools.partial(_cmp_table_kernel, ncp=ncp),
        grid=(NSA_HEADS,),
        in_specs=[pl.BlockSpec(memory_space=pltpu.SMEM)],
        out_specs=pl.BlockSpec((2 * ncp, BLK), lambda h: (0, h)),
        out_shape=jax.ShapeDtypeStruct((2 * ncp, NSA_HEADS * BLK), F32),
        compiler_params=_cparams(("arbitrary",)),
        name="cmp_table",
    )(rel_bias)


def _proj_kernel(x_ref, g_ref, w_ref, qt_ref, ks_ref, vs_ref, kw_ref, vw_ref, sqt_ref, sk_ref, sv_ref,
                 bq_ref, bk_ref, bv_ref, ckv_ref, gt_ref):
    h = _rms(x_ref[0], g_ref[...]).astype(BF16)
    tm = x_ref.shape[1]

    def seg(a, b):
        return jnp.dot(h, w_ref[:, a:b], preferred_element_type=F32)

    def heads(ref, a, n):
        y = seg(a, a + n * HEAD_DIM).astype(BF16)
        for i in range(n):
            ref[0, i] = y[:, i * HEAD_DIM:(i + 1) * HEAD_DIM]

    def queries_t(ref, a, n, scale=None):
        yt = seg(a, a + n * HEAD_DIM).T
        if scale is not None:
            yt = yt * scale
        for j in range(tm // BLK):
            for i in range(n):
                ref[0, j, :, i * BLK:(i + 1) * BLK] = (
                    yt[i * HEAD_DIM:(i + 1) * HEAD_DIM, j * BLK:(j + 1) * BLK].astype(BF16))

    ones_rows = jnp.where(_iota((VT_ROWS - HEAD_DIM, tm), 0) == 0, 1.0, 0.0)

    def values_t(yt):
        return jnp.concatenate([yt, ones_rows], axis=0).astype(BF16)

    queries_t(qt_ref, 0, NSA_HEADS, LOG2E)
    y = seg(256, 512)
    vs_ref[0] = values_t(y[:, 0:128].T[64:128])
    vw_ref[0] = values_t(y[:, 128:256].T[64:128])
    y = y.astype(BF16)
    blk_in_tile = ((pl.program_id(1) * tm + _iota((tm, HEAD_DIM), 0)) & (SEL_TILE - 1)) // SEL_BLK
    onehot = jnp.where(_iota((tm, HEAD_DIM), 1) == blk_in_tile, 1.0, 0.0).astype(BF16)
    ks_ref[0] = jnp.concatenate([y[:, 0:64], onehot], axis=1)
    kw_ref[0] = y[:, 128:192]
    queries_t(sqt_ref, 512, SWA_HEADS)
    heads(sk_ref, 1024, SWA_KV_HEADS)
    yt = seg(1152, 1280).T
    for i in range(SWA_KV_HEADS):
        sv_ref[0, i] = values_t(yt[i * HEAD_DIM:(i + 1) * HEAD_DIM])
    queries_t(bq_ref, 1280, SB_HEADS, LOG2E)
    bk_ref[0] = seg(1536, 1792).astype(BF16)
    bv_ref[0] = seg(1792, 2048).T.astype(BF16)
    y = seg(2048, 2304)
    ckv_ref[0, 0] = y[:, 0:64]
    ckv_ref[1, 0] = y[:, 64:128]
    gt_ref[0] = y[:, 128:256]


def _proj(x, g, w, tm):
    b, s, d = x.shape
    nw = w.shape[1]
    hd = lambda n: pl.BlockSpec((1, n, tm, HEAD_DIM), lambda bi, i: (bi, 0, i, 0))
    flat = lambda n: pl.BlockSpec((1, tm, n), lambda bi, i: (bi, i, 0))
    hshape = lambda n: jax.ShapeDtypeStruct((b, n, s, HEAD_DIM), BF16)
    fshape = lambda n, dt: jax.ShapeDtypeStruct((b, s, n), dt)
    qt_spec = lambda n: pl.BlockSpec((1, tm // BLK, HEAD_DIM, n * BLK), lambda bi, i: (bi, i, 0, 0))
    qt_shape = lambda n: jax.ShapeDtypeStruct((b, s // BLK, HEAD_DIM, n * BLK), BF16)
    vt_spec = pl.BlockSpec((1, VT_ROWS, tm), lambda bi, i: (bi, 0, i))
    vt_shape = jax.ShapeDtypeStruct((b, VT_ROWS, s), BF16)
    return pl.pallas_call(
        _proj_kernel,
        grid=(b, s // tm),
        in_specs=[pl.BlockSpec((1, tm, d), lambda bi, i: (bi, i, 0)),
                  pl.BlockSpec((1, d), lambda bi, i: (0, 0)),
                  pl.BlockSpec((d, nw), lambda bi, i: (0, 0))],
        out_specs=[qt_spec(NSA_HEADS), flat(128), vt_spec, flat(64), vt_spec,
                   qt_spec(SWA_HEADS), hd(SWA_KV_HEADS),
                   pl.BlockSpec((1, SWA_KV_HEADS, VT_ROWS, tm), lambda bi, i: (bi, 0, 0, i)),
                   qt_spec(SB_HEADS), flat(256),
                   pl.BlockSpec((1, SB_HEADS * HEAD_DIM, tm), lambda bi, i: (bi, 0, i)),
                   pl.BlockSpec((2, 1, tm, HEAD_DIM), lambda bi, i: (0, bi, i, 0)), flat(128)],
        out_shape=[qt_shape(NSA_HEADS), fshape(128, BF16), vt_shape, fshape(64, BF16), vt_shape,
                   qt_shape(SWA_HEADS), hshape(SWA_KV_HEADS),
                   jax.ShapeDtypeStruct((b, SWA_KV_HEADS, VT_ROWS, s), BF16),
                   qt_shape(SB_HEADS), fshape(256, BF16),
                   jax.ShapeDtypeStruct((b, SB_HEADS * HEAD_DIM, s), BF16),
                   jax.ShapeDtypeStruct((2, b, s, HEAD_DIM), F32), fshape(128, F32)],
        compiler_params=_cparams(("parallel", "parallel")),
        name="proj",
    )(x, g, w)


def _compress_kernel(c_ref, pos_ref, w1_ref, b1_ref, w2_ref, o_ref, ot_ref):
    c = c_ref[0, 0]
    nch = c.shape[0]
    xa = (c + pos_ref[0, 0:1]).astype(BF16)
    xb = (c + pos_ref[0, 1:2]).astype(BF16)
    p = jnp.dot(xa, w1_ref[0, 0], preferred_element_type=F32)
    q = jnp.dot(xb, w1_ref[0, 1], preferred_element_type=F32)
    hid = p + pltpu.roll(q, nch - 1, 0) + b1_ref[0]
    hid = hid * jax.nn.sigmoid(hid)
    out = jnp.dot(hid.astype(BF16), w2_ref[0], preferred_element_type=F32)
    row = _iota(out.shape, 0)
    out = jnp.where(row < nch - 1, out, 0.0)
    o_ref[0, 0] = out.astype(BF16)
    ot_ref[0, 0] = jnp.concatenate([out, jnp.zeros_like(out)], axis=1).T[:HEAD_DIM].astype(BF16)


def _compress(ckv, pos, w1, b1, w2):
    _, b, nch, cw = ckv.shape
    return pl.pallas_call(
        _compress_kernel,
        grid=(2, b),
        in_specs=[pl.BlockSpec((1, 1, nch, cw), lambda j, bi: (j, bi, 0, 0)),
                  pl.BlockSpec((1, 2, cw), lambda j, bi: (j, 0, 0)),
                  pl.BlockSpec((1, 2, cw, CMP_HIDDEN), lambda j, bi: (j, 0, 0, 0)),
                  pl.BlockSpec((1, 1, CMP_HIDDEN), lambda j, bi: (j, 0, 0)),
                  pl.BlockSpec((1, CMP_HIDDEN, HEAD_DIM), lambda j, bi: (j, 0, 0))],
        out_specs=[pl.BlockSpec((1, 1, nch, HEAD_DIM), lambda j, bi: (j, bi, 0, 0)),
                   pl.BlockSpec((1, 1, HEAD_DIM, nch), lambda j, bi: (j, bi, 0, 0))],
        out_shape=[jax.ShapeDtypeStruct((2, b, nch, HEAD_DIM), BF16),
                   jax.ShapeDtypeStruct((2, b, HEAD_DIM, nch), BF16)],
        compiler_params=_cparams(("parallel", "parallel")),
        name="compress",
    )(ckv, pos, w1, b1, w2)


def _transpose_pad(x, axis):
    pad = jnp.zeros(x.shape, x.dtype)
    return jnp.concatenate([x, pad], axis=axis).T


def _cmp_kernel(qt_ref, kc_ref, vct_ref, hct_ref, wmapt_ref, o_ref, pen_ref, s_scr, oacc_scr, iacc_scr):
    step = pl.program_id(1)
    ncp = kc_ref.shape[2]
    ns = wmapt_ref.shape[0] - IMP_PAD
    lanes = NSA_HEADS * BLK
    row_c = _iota((CMP_CHUNK, lanes), 0)
    lane_q = _iota((CMP_CHUNK, lanes), 1) & (BLK - 1)
    n_chunks = (CMP_QB * step) // (CMP_CHUNK // 8) + 1

    def score_chunk(ch, mxs):
        c0 = pl.multiple_of(ch * CMP_CHUNK, CMP_CHUNK)
        kc = kc_ref[0, 0, pl.ds(c0, CMP_CHUNK), :]
        raw = [jnp.dot(kc, qt_ref[0, qb], preferred_element_type=F32) for qb in range(CMP_QB)]
        out = []
        for qb in range(CMP_QB):
            n = CMP_QB * step + qb
            bias = hct_ref[pl.ds(pl.multiple_of(ncp - 8 - 8 * n + c0, 8), CMP_CHUNK), :]
            visible = (BLK * n + lane_q - CMP_STRIDE * (c0 + row_c) - (CMP_LEN - 1)) >= 0
            s = jnp.where(visible, raw[qb] + bias, NEG)
            s_scr[qb, ch] = s
            out.append(jnp.maximum(mxs[qb], jnp.max(s, axis=0, keepdims=True)))
        return tuple(out)

    mxs = lax.fori_loop(0, n_chunks, score_chunk, tuple(jnp.full((1, lanes), NEG, F32) for _ in range(CMP_QB)))
    ms = [jnp.where(mx <= 0.5 * NEG, 0.0, mx) for mx in mxs]
    oacc_scr[...] = jnp.zeros(oacc_scr.shape, F32)
    iacc_scr[...] = jnp.zeros(iacc_scr.shape, F32)

    def prob_chunk(ch, ls):
        c0 = pl.multiple_of(ch * CMP_CHUNK, CMP_CHUNK)
        vct = vct_ref[0, 0, :, pl.ds(c0, CMP_CHUNK)]
        r0 = pl.multiple_of(ch * (CMP_CHUNK // 4), 16)
        wmt = wmapt_ref[pl.ds(r0, IMP_ROWS), pl.ds(c0, CMP_CHUNK)]
        both = jnp.concatenate([vct, wmt], axis=0)
        ps = [jnp.exp2(s_scr[qb, ch] - ms[qb]) for qb in range(CMP_QB)]
        prods = [jnp.dot(both, p.astype(BF16), preferred_element_type=F32) for p in ps]
        for qb in range(CMP_QB):
            oacc_scr[qb] += prods[qb][:HEAD_DIM]
            iacc_scr[qb, pl.ds(r0, IMP_ROWS), :] += prods[qb][HEAD_DIM:]
        return tuple(ls[qb] + jnp.sum(ps[qb], axis=0, keepdims=True) for qb in range(CMP_QB))

    ls = lax.fori_loop(0, n_chunks, prob_chunk, tuple(jnp.zeros((1, lanes), F32) for _ in range(CMP_QB)))
    imps = []
    for qb in range(CMP_QB):
        inv = 1.0 / jnp.maximum(ls[qb], 1e-30)
        o_t = oacc_scr[qb] * inv
        o_ref[0, qb * BLK:(qb + 1) * BLK, :] = jnp.concatenate(
            [_transpose_pad(o_t[:, h * BLK:(h + 1) * BLK], 0)[:, :HEAD_DIM] for h in range(NSA_HEADS)], axis=1)
        w = iacc_scr[qb, 0:ns, :] * inv
        imps.append(functools.reduce(jnp.add, [w[:, h * BLK:(h + 1) * BLK] for h in range(NSA_HEADS)]))

    imp = jnp.concatenate(imps, axis=1)
    shape = (ns, CMP_QB * BLK)
    j = _iota(shape, 0)
    cur = (CMP_QB * BLK * step + _iota(shape, 1)) // SEL_BLK
    valid = j <= cur
    forced = valid & ((j == 0) | (j > cur - N_LOCAL))
    score = jnp.where(forced, PICKED, jnp.where(valid, imp, NEG))
    jf = j.astype(F32)
    for _ in range(min(N_SEL, ns) - (N_LOCAL + 1)):
        best = jnp.max(score, axis=0, keepdims=True)
        first = jnp.min(jnp.where(score == best, jf, float(ns)), axis=0, keepdims=True)
        score = jnp.where(jf == first, PICKED, score)
    pen_ref[0, 0] = jnp.where(valid & (score == PICKED), 0.0, NEG)


def _cmp_attn(qt, kcv, kcv_t, hct, wmapt):
    b, nb, _, lanes = qt.shape
    ncp = kcv.shape[2]
    ns = wmapt.shape[0] - IMP_PAD
    s = nb * BLK
    return pl.pallas_call(
        _cmp_kernel,
        grid=(b, nb // CMP_QB),
        in_specs=[pl.BlockSpec((1, CMP_QB, HEAD_DIM, lanes), lambda bi, i: (bi, i, 0, 0)),
                  pl.BlockSpec((1, 1, ncp, HEAD_DIM), lambda bi, i: (0, bi, 0, 0)),
                  pl.BlockSpec((1, 1, HEAD_DIM, ncp), lambda bi, i: (1, bi, 0, 0)),
                  pl.BlockSpec((2 * ncp, lanes), lambda bi, i: (0, 0)),
                  pl.BlockSpec((ns + IMP_PAD, ncp), lambda bi, i: (0, 0))],
        out_specs=[pl.BlockSpec((1, CMP_QB * BLK, NSA_HEADS * HEAD_DIM), lambda bi, i: (bi, i, 0)),
                   pl.BlockSpec((1, 1, ns, CMP_QB * BLK), lambda bi, i: (bi, i, 0, 0))],
        out_shape=[jax.ShapeDtypeStruct((b, s, NSA_HEADS * HEAD_DIM), F32),
                   jax.ShapeDtypeStruct((b, nb // CMP_QB, ns, CMP_QB * BLK), F32)],
        scratch_shapes=[pltpu.VMEM((CMP_QB, ncp // CMP_CHUNK, CMP_CHUNK, lanes), F32),
                        pltpu.VMEM((CMP_QB, HEAD_DIM, lanes), F32),
                        pltpu.VMEM((CMP_QB, ns + IMP_PAD, lanes), F32)],
        compiler_params=_cparams(("parallel", "arbitrary")),
        name="cmp_attn",
    )(qt, kcv, kcv_t, hct, wmapt)


def _sel_kernel(qt_ref, k_ref, vt_ref, pen_ref, band_ref, o_ref, s_scr, p_scr, mx_scr, m_scr, alpha_scr, acc_scr):
    n = SEL_QB * pl.program_id(1)
    qlanes = NSA_HEADS * BLK
    lanes = SEL_QB * qlanes
    sub = SEL_TILE // BLK
    per = BLK // SEL_BLK
    qt = jnp.concatenate([qt_ref[0, qb] for qb in range(SEL_QB)], axis=1)
    lane = _iota((BLK, lanes), 1)
    t = BLK * (n + lane // qlanes) + (lane & (BLK - 1))

    def scores_into(kt, s_buf, mx_buf, far):
        rows = sub * per
        pen = pen_ref[0, 0, pl.ds(pl.multiple_of(kt * rows, rows), rows), :]
        pen = jnp.concatenate([pen[:, qb * BLK:(qb + 1) * BLK] for qb in range(SEL_QB) for _ in range(NSA_HEADS)],
                              axis=1)
        tail = jnp.concatenate([pen, jnp.zeros((HEAD_DIM - rows, lanes), F32)], axis=0).astype(BF16)
        q_aug = jnp.concatenate([qt, tail], axis=0)
        k0 = pl.multiple_of(kt * SEL_TILE, SEL_TILE)
        s = jnp.dot(k_ref[0, pl.ds(k0, SEL_TILE), :], q_aug, preferred_element_type=F32)
        if far:
            s_buf[...] = s
            mx_buf[...] = jnp.max(s, axis=0, keepdims=True)
        else:
            mxs = []
            for qb in range(SEL_QB):
                cols = slice(qb * qlanes, (qb + 1) * qlanes)
                parts = [s[u * BLK:(u + 1) * BLK, cols] + band_ref[jnp.clip(n + qb - sub * kt - u, 0, N_BAND - 1)]
                         for u in range(sub)]
                for u in range(sub):
                    s_buf[u * BLK:(u + 1) * BLK, cols] = parts[u]
                mxs.append(functools.reduce(jnp.maximum, [jnp.max(v, axis=0, keepdims=True) for v in parts]))
            mx_buf[...] = jnp.concatenate(mxs, axis=1)

    def add_values(kt, p_buf):
        k0 = pl.multiple_of(kt * SEL_TILE, SEL_TILE)
        pv = jnp.dot(vt_ref[0, :, pl.ds(k0, SEL_TILE)], p_buf[...], preferred_element_type=F32)
        acc_scr[...] = alpha_scr[...] * acc_scr[...] + pv

    def softmax_into(kt, s_buf, mx_buf, p_buf, causal):
        def piece(u):
            s = s_buf[u * BLK:(u + 1) * BLK, :]
            if causal:
                s = jnp.where(kt * SEL_TILE + u * BLK + _iota((BLK, lanes), 0) <= t, s, NEG)
            return s
        m_i = m_scr[...]
        if causal:
            mx = functools.reduce(jnp.maximum, [jnp.max(piece(u), axis=0, keepdims=True) for u in range(sub)])
        else:
            mx = mx_buf[...]
        m_new = jnp.maximum(m_i, mx)
        for u in range(sub):
            p_buf[u * BLK:(u + 1) * BLK, :] = jnp.exp2(piece(u) - m_new).astype(BF16)
        m_scr[...] = m_new
        alpha_scr[...] = jnp.exp2(m_i - m_new)

    def stage(i, cur, nxt, p_cur, p_prev, far):
        add_values(jnp.maximum(i - 1, 0), p_prev)
        scores_into(i + 1, *nxt, far)
        softmax_into(i, *cur, p_cur, False)

    def finish(i, cur, p_cur, p_prev):
        add_values(jnp.maximum(i - 1, 0), p_prev)
        softmax_into(i, *cur, p_cur, True)
        add_values(i, p_cur)
        acc = acc_scr[...]
        o_t = acc[:HEAD_DIM] / acc[HEAD_DIM:HEAD_DIM + 1]
        for qb in range(SEL_QB):
            o_ref[0, qb * BLK:(qb + 1) * BLK, :] = jnp.concatenate(
                [_transpose_pad(o_t[:, qb * qlanes + h * BLK:qb * qlanes + (h + 1) * BLK], 0)[:, :HEAD_DIM]
                 for h in range(NSA_HEADS)], axis=1)

    s0, s1 = (s_scr.at[0], mx_scr.at[0]), (s_scr.at[1], mx_scr.at[1])
    p0, p1 = p_scr.at[0], p_scr.at[1]
    m_scr[...] = jnp.full((1, lanes), NEG, F32)
    alpha_scr[...] = jnp.ones((1, lanes), F32)
    acc_scr[...] = jnp.zeros(acc_scr.shape, F32)
    p1[...] = jnp.zeros((SEL_TILE, lanes), BF16)
    scores_into(0, *s0, False)
    last = (BLK * n + BLK - 1) // SEL_TILE
    n_far = jnp.maximum((n - (N_BAND - 2)) // sub, 0)
    far_pairs = jnp.minimum(jnp.maximum((n_far - 1) // 2, 0), last // 2)

    def pair(j, _, far):
        stage(2 * j, s0, s1, p0, p1, far)
        stage(2 * j + 1, s1, s0, p1, p0, far)
        return 0

    lax.fori_loop(0, far_pairs, functools.partial(pair, far=True), 0)
    lax.fori_loop(far_pairs, last // 2, functools.partial(pair, far=False), 0)

    @pl.when(last % 2 == 1)
    def _():
        stage(last - 1, s0, s1, p0, p1, False)
        finish(last, s1, p1, p0)

    @pl.when(last % 2 == 0)
    def _():
        finish(last, s0, p0, p1)


def _sel_attn(qt, ks, vst, pen, band_t):
    b, nb, _, qlanes = qt.shape
    s = nb * BLK
    ns = pen.shape[2]
    lanes = SEL_QB * qlanes
    per_cmp = CMP_QB // SEL_QB
    return pl.pallas_call(
        _sel_kernel,
        grid=(b, nb // SEL_QB),
        in_specs=[pl.BlockSpec((1, SEL_QB, HEAD_DIM, qlanes), lambda bi, i: (bi, i, 0, 0)),
                  pl.BlockSpec((1, s, 2 * HEAD_DIM), lambda bi, i: (bi, 0, 0)),
                  pl.BlockSpec((1, VT_ROWS, s), lambda bi, i: (bi, 0, 0)),
                  pl.BlockSpec((1, 1, ns, SEL_QB * BLK), lambda bi, i: (bi, i // per_cmp, 0, i % per_cmp)),
                  pl.BlockSpec((N_BAND, BLK, qlanes), lambda bi, i: (0, 0, 0))],
        out_specs=pl.BlockSpec((1, SEL_QB * BLK, NSA_HEADS * HEAD_DIM), lambda bi, i: (bi, i, 0)),
        out_shape=jax.ShapeDtypeStruct((b, s, NSA_HEADS * HEAD_DIM), F32),
        scratch_shapes=[pltpu.VMEM((2, SEL_TILE, lanes), F32),
                        pltpu.VMEM((2, SEL_TILE, lanes), BF16),
                        pltpu.VMEM((2, 1, lanes), F32),
                        pltpu.VMEM((1, lanes), F32),
                        pltpu.VMEM((1, lanes), F32),
                        pltpu.VMEM((VT_ROWS, lanes), F32)],
        compiler_params=_cparams(("parallel", "arbitrary")),
        name="sel_attn",
    )(qt, ks, vst, pen, band_t)


def _band_kernel(*refs, window, groups, use_sinks, base2, n_parts):
    qt_ref = refs[0]
    k_refs = refs[1:1 + n_parts]
    vt_refs = refs[1 + n_parts:1 + 2 * n_parts]
    band_ref, sink_ref, o_ref = refs[1 + 2 * n_parts:]
    step = pl.program_id(1)
    n_prev = window // BLK
    nk = (n_prev + 1) * BLK
    lanes = 4 * BLK
    row = _iota((nk, lanes), 0)
    dist = n_prev * BLK + (_iota((nk, lanes), 1) & (BLK - 1)) - row
    in_window = (dist >= 0) & (dist < window)
    chains = [(g, qb) for g in range(groups) for qb in range(BAND_QB)]
    kwin = [jnp.concatenate([r[0, g] for r in k_refs], axis=0) for g in range(groups)]
    vtwin = [jnp.concatenate([r[0, g] for r in vt_refs], axis=1) for g in range(groups)]
    bias = [jnp.concatenate([band_ref[n_prev - u, :, g * lanes:(g + 1) * lanes] for u in range(n_prev + 1)], axis=0)
            for g in range(groups)]
    scores = [jnp.dot(kwin[g][qb * BLK:qb * BLK + nk], qt_ref[0, qb, :, g * lanes:(g + 1) * lanes],
                      preferred_element_type=F32) for g, qb in chains]
    probs, maxes = [], []
    for (g, qb), s in zip(chains, scores):
        s = jnp.where(in_window & (row >= (n_prev - BAND_QB * step - qb) * BLK), s + bias[g], NEG)
        mx = jnp.max(s, axis=0, keepdims=True)
        if use_sinks:
            mx = jnp.maximum(mx, sink_ref[g])
        probs.append((jnp.exp2(s - mx) if base2 else jnp.exp(s - mx)).astype(BF16))
        maxes.append(mx)
    accs = [jnp.dot(vtwin[g][:, qb * BLK:qb * BLK + nk], p, preferred_element_type=F32)
            for (g, qb), p in zip(chains, probs)]
    for (g, qb), acc, mx in zip(chains, accs, maxes):
        den = acc[HEAD_DIM:HEAD_DIM + 1]
        if use_sinks:
            den = den + jnp.exp(sink_ref[g] - mx)
        o_t = acc[:HEAD_DIM] / den
        o_ref[0, qb * BLK:(qb + 1) * BLK, 4 * g * HEAD_DIM:4 * (g + 1) * HEAD_DIM] = jnp.concatenate(
            [_transpose_pad(o_t[:, h * BLK:(h + 1) * BLK], 0)[:, :HEAD_DIM] for h in range(4)], axis=1)


def _band_attn(qt, k, vt, band_t, sinks, window, use_sinks, base2, name):
    b, nb, _, width = qt.shape
    g = width // (4 * BLK)
    s = nb * BLK
    n_prev = window // BLK
    tq = BAND_QB * BLK
    back = n_prev * BLK
    if back % tq == 0:
        piece = tq
        starts = [lambda i, d=d: jnp.maximum(i - d, 0) for d in range(back // tq, 0, -1)]
    else:
        assert tq % back == 0
        piece = back
        starts = [lambda i: jnp.maximum(i * (tq // back) - 1, 0)]
    k_specs = [pl.BlockSpec((1, g, piece, HEAD_DIM), lambda bi, i, f=f: (bi, 0, f(i), 0)) for f in starts]
    k_specs.append(pl.BlockSpec((1, g, tq, HEAD_DIM), lambda bi, i: (bi, 0, i, 0)))
    vt_specs = [pl.BlockSpec((1, g, VT_ROWS, piece), lambda bi, i, f=f: (bi, 0, 0, f(i))) for f in starts]
    vt_specs.append(pl.BlockSpec((1, g, VT_ROWS, tq), lambda bi, i: (bi, 0, 0, i)))
    n_parts = len(k_specs)
    return pl.pallas_call(
        functools.partial(_band_kernel, window=window, groups=g, use_sinks=use_sinks, base2=base2,
                          n_parts=n_parts),
        grid=(b, s // tq),
        in_specs=[pl.BlockSpec((1, BAND_QB, HEAD_DIM, width), lambda bi, i: (bi, i, 0, 0))] + k_specs + vt_specs
                 + [pl.BlockSpec((n_prev + 1, BLK, width), lambda bi, i: (0, 0, 0)),
                    pl.BlockSpec((g, 1, 4 * BLK), lambda bi, i: (0, 0, 0))],
        out_specs=pl.BlockSpec((1, tq, g * 4 * HEAD_DIM), lambda bi, i: (bi, i, 0)),
        out_shape=jax.ShapeDtypeStruct((b, s, g * 4 * HEAD_DIM), F32),
        compiler_params=_cparams(("parallel", "arbitrary")),
        name=name,
    )(qt, *([k] * n_parts), *([vt] * n_parts), band_t, sinks)


def _sb_kernel(qt_ref, k_ref, vt_ref, later_ref, o_ref):
    n0 = SB_QB * pl.program_id(1)
    lanes = SB_HEADS * BLK
    lane_q = _iota((BLK, lanes), 1) & (BLK - 1)
    row = _iota((BLK, lanes), 0)
    later = later_ref[...]
    zero = jnp.zeros((HEAD_DIM, BLK), BF16)
    blocks = range(SB_QB)
    q_bd = [jnp.concatenate(
        [jnp.concatenate([zero] * h + [qt_ref[0, qb, :, h * BLK:(h + 1) * BLK]] + [zero] * (SB_HEADS - 1 - h), axis=0)
         for h in range(SB_HEADS)], axis=1) for qb in blocks]

    def cond(c):
        it, carry, _ = c
        return (n0 + SB_QB - 1 - SB_GROUP * it >= 0) & (jnp.max(functools.reduce(jnp.maximum, carry)) > SB_LOG2_FLOOR)

    def body(c):
        it, carry, acc = c
        k_all, vt_all, before = [], [], []
        for qb in blocks:
            top = n0 + qb - SB_GROUP * it
            ks, vts, befores = [], [], []
            for j in range(SB_GROUP - 1, -1, -1):
                kb = top - j
                k0 = pl.multiple_of(jnp.maximum(kb, 0) * BLK, BLK)
                ks.append(k_ref[0, pl.ds(k0, BLK), :])
                vts.append(vt_ref[0, :, pl.ds(k0, BLK)])
                befores.append((kb >= 0) & (k0 + row < BLK * (n0 + qb) + lane_q))
            k_all.append(jnp.concatenate(ks, axis=0))
            vt_all.append(jnp.concatenate(vts, axis=1))
            before.append(jnp.concatenate(befores, axis=0))
        z = [jnp.dot(k_all[qb], q_bd[qb], preferred_element_type=F32) for qb in blocks]
        log_not, split = [], []
        for qb in blocks:
            softplus = jnp.maximum(z[qb], 0.0) + jnp.log2(1.0 + jnp.exp2(-jnp.abs(z[qb])))
            ln = jnp.where(before[qb], -softplus, 0.0)
            hi = ln.astype(BF16)
            lo = (ln - hi.astype(F32)).astype(BF16)
            log_not.append(ln)
            split.append(jnp.concatenate([hi, lo], axis=0))
        tail = [jnp.dot(later, split[qb], preferred_element_type=F32) for qb in blocks]
        a = [jnp.where(before[qb], jnp.exp2(z[qb] + log_not[qb] + tail[qb] + carry[qb]), 0.0).astype(BF16)
             for qb in blocks]
        pv = [jnp.dot(vt_all[qb], a[qb], preferred_element_type=F32) for qb in blocks]
        acc = tuple(tuple(acc[qb][h] + pv[qb][h * HEAD_DIM:(h + 1) * HEAD_DIM, h * BLK:(h + 1) * BLK]
                          for h in range(SB_HEADS)) for qb in blocks)
        carry = tuple(carry[qb] + jnp.sum(log_not[qb], axis=0, keepdims=True) for qb in blocks)
        return it + 1, carry, acc

    init = (0, tuple(jnp.zeros((1, lanes), F32) for _ in blocks),
            tuple(tuple(jnp.zeros((HEAD_DIM, BLK), F32) for _ in range(SB_HEADS)) for _ in blocks))
    _, _, acc = lax.while_loop(cond, body, init)
    for qb in blocks:
        o_ref[0, qb * BLK:(qb + 1) * BLK, :] = jnp.concatenate(
            [_transpose_pad(a, 0)[:, :HEAD_DIM] for a in acc[qb]], axis=1)


def _sb_attn(qt, k, vt):
    b, nb, _, lanes = qt.shape
    s = nb * BLK
    width = SB_HEADS * HEAD_DIM
    gk = SB_GROUP * BLK
    later = (jnp.arange(2 * gk)[None, :] % gk > jnp.arange(gk)[:, None]).astype(BF16)
    return pl.pallas_call(
        _sb_kernel,
        grid=(b, nb // SB_QB),
        in_specs=[pl.BlockSpec((1, SB_QB, HEAD_DIM, lanes), lambda bi, n: (bi, n, 0, 0)),
                  pl.BlockSpec((1, s, width), lambda bi, n: (bi, 0, 0)),
                  pl.BlockSpec((1, width, s), lambda bi, n: (bi, 0, 0)),
                  pl.BlockSpec((gk, 2 * gk), lambda bi, n: (0, 0))],
        out_specs=pl.BlockSpec((1, SB_QB * BLK, width), lambda bi, n: (bi, n, 0)),
        out_shape=jax.ShapeDtypeStruct((b, s, width), F32),
        compiler_params=_cparams(("parallel", "arbitrary")),
        name="sb_attn",
    )(qt, k, vt, later)


def _out_kernel(x_ref, oc_ref, os_ref, ow_ref, gt_ref, swa_ref, sb_ref, gg_ref, w_ref, gp_ref, o_ref):
    wa = NSA_HEADS * HEAD_DIM
    wb = wa + SWA_HEADS * HEAD_DIM
    gg = gg_ref[...]
    lane = _iota((OUT_ROWS, wa), 1)
    groups = [slice(r0, r0 + OUT_ROWS) for r0 in range(0, x_ref.shape[1], OUT_ROWS)]
    mixes = []
    for rows in groups:
        gates = jax.nn.sigmoid(gt_ref[0, rows])

        def spread(branch, gates=gates):
            cols = [jnp.broadcast_to(gates[:, branch * NSA_HEADS + i:branch * NSA_HEADS + i + 1], (OUT_ROWS, wa))
                    for i in range(NSA_HEADS)]
            out = cols[NSA_HEADS - 1]
            for i in range(NSA_HEADS - 2, -1, -1):
                out = jnp.where(lane < (i + 1) * HEAD_DIM, cols[i], out)
            return out

        o_nsa = spread(0) * oc_ref[0, rows] + spread(1) * os_ref[0, rows] + spread(2) * ow_ref[0, rows]
        mixes.append(jnp.concatenate([_rms(o_nsa, gg[:, :wa]), _rms(swa_ref[0, rows], gg[:, wa:wb]),
                                      _rms(sb_ref[0, rows], gg[:, wb:])], axis=1).astype(BF16))
    ys = [jnp.dot(mix, w_ref[...], preferred_element_type=F32) for mix in mixes]
    for rows, y in zip(groups, ys):
        o_ref[0, rows] = x_ref[0, rows] + _rms(y, gp_ref[...])


def _out_proj(x, o_cmp, o_sel, o_win, gates, o_swa, o_sb, gg, w, gp, tm):
    b, s, d = x.shape
    row = lambda n: pl.BlockSpec((1, tm, n), lambda bi, i: (bi, i, 0))
    const = lambda a, c: pl.BlockSpec((a, c), lambda bi, i: (0, 0))
    return pl.pallas_call(
        _out_kernel,
        grid=(b, s // tm),
        in_specs=[row(d), row(256), row(256), row(256), row(128), row(512), row(256),
                  const(1, d), const(d, d), const(1, d)],
        out_specs=row(d),
        out_shape=jax.ShapeDtypeStruct((b, s, d), F32),
        compiler_params=_cparams(("parallel", "parallel")),
        name="out_proj",
    )(x, o_cmp, o_sel, o_win, gates, o_swa, o_sb, gg, w, gp)


def _ffn_kernel(x_ref, g1_ref, wg_ref, wu_ref, wd_ref, g2_ref, o_ref, *, chunk):
    x = x_ref[...]
    h = _rms(x, g1_ref[...]).astype(BF16)
    dff = wg_ref.shape[1]
    f = jnp.zeros(x.shape, F32)
    for a0 in range(0, dff, chunk):
        a1 = min(a0 + chunk, dff)
        gate = jnp.dot(h, wg_ref[:, a0:a1], preferred_element_type=F32)
        up = jnp.dot(h, wu_ref[:, a0:a1], preferred_element_type=F32)
        a = (gate * jax.nn.sigmoid(gate) * up).astype(BF16)
        f = f + jnp.dot(a, wd_ref[a0:a1, :], preferred_element_type=F32)
    o_ref[...] = x + _rms(f, g2_ref[...])


def _ffn(x, g1, wg, wu, wd, g2, tm):
    t, d = x.shape
    dff = wg.shape[1]
    row = pl.BlockSpec((tm, d), lambda i: (i, 0))
    const = lambda a, c: pl.BlockSpec((a, c), lambda i: (0, 0))
    weight = lambda a, c: pl.BlockSpec((a, c), lambda i: (0, 0), pipeline_mode=pl.Buffered(1))
    return pl.pallas_call(
        functools.partial(_ffn_kernel, chunk=FFN_CHUNK),
        grid=(t // tm,),
        in_specs=[row, const(1, d), weight(d, dff), weight(d, dff), weight(dff, d), const(1, d)],
        out_specs=row,
        out_shape=jax.ShapeDtypeStruct((t, d), F32),
        compiler_params=_cparams(("parallel",)),
        name="ffn",
    )(x, g1, wg, wu, wd, g2)


def _permute_w_in(w):
    scale = 1.0 / math.sqrt(HEAD_DIM)
    nq, kc, vc, rest, ng = w[:, :256], w[:, 256:320], w[:, 320:384], w[:, 384:640], w[:, 640:652]
    sq, skv = w[:, 652:1164], w[:, 1164:1420]
    bq, bkv = w[:, 1420:1676], w[:, 1676:2188]
    pad = jnp.zeros((w.shape[0], 128 - ng.shape[1]), w.dtype)
    return jnp.concatenate([nq * scale, rest, sq * scale, skv, bq * scale, bkv, kc, vc, ng, pad], axis=1)


def _cmp_to_sel(nc_pad, nc, ns):
    c0 = jnp.arange(nc_pad)[:, None] * CMP_STRIDE
    s0 = jnp.arange(ns)[None, :] * SEL_BLK
    ov = jnp.minimum(c0 + CMP_LEN, s0 + SEL_BLK) - jnp.maximum(c0, s0)
    w = jnp.clip(ov, 0, None).astype(F32) / CMP_LEN
    return jnp.where(jnp.arange(nc_pad)[:, None] < nc, w, 0.0).astype(BF16)


def kernel(x, rel_bias, ln_attn_pre, w_in, nsa_cmp_pos, nsa_phi_w1, nsa_phi_b1, nsa_phi_w2, swa_sinks,
           grp_norm_g, w_out, ln_attn_post, ln_ffn_pre, ffn_w_gate, ffn_w_up, ffn_w_down, ln_ffn_post):
    b, s, d = x.shape
    depth = w_in.shape[0]
    nch = s // CMP_STRIDE
    ns = s // SEL_BLK
    t = b * s

    band_nsa_t = _band_table_t(rel_bias, N_BAND, NSA_HEADS, 0, True)
    band_swa_t = _band_table_t(rel_bias, SWA_WINDOW // BLK + 1, SWA_HEADS, NSA_HEADS, False)
    hct = _cmp_table(rel_bias, nch)
    wmapt = jnp.pad(_cmp_to_sel(nch, nch - 1, ns).T, ((0, IMP_PAD), (0, 0)))
    no_sinks = jnp.zeros((1, 1, 4 * BLK), F32)

    for l in range(depth):
        outs = _proj(x, ln_attn_pre[l][None], _permute_w_in(w_in[l]).astype(BF16), 512)
        qt, ks, vs, kw, vw, sqt, sk, sv, bq, bk, bv, ckv, gates = outs
        kcv, kcv_t = _compress(ckv.reshape(2, b, nch, CMP_STRIDE * HEAD_DIM),
                               nsa_cmp_pos[l].reshape(2, 2, CMP_STRIDE * HEAD_DIM),
                               nsa_phi_w1[l].reshape(2, 2, CMP_STRIDE * HEAD_DIM, CMP_HIDDEN).astype(BF16),
                               nsa_phi_b1[l][:, None, :], nsa_phi_w2[l].astype(BF16))
        o_cmp, selmask = _cmp_attn(qt, kcv, kcv_t, hct, wmapt)
        o_sel = _sel_attn(qt, ks, vs, selmask, band_nsa_t)
        o_win = _band_attn(qt, kw[:, None], vw[:, None], band_nsa_t, no_sinks, NSA_WINDOW, False, True,
                           "win_attn")
        sinks = jnp.broadcast_to(swa_sinks[l].reshape(SWA_KV_HEADS, 1, 4, 1),
                                 (SWA_KV_HEADS, 1, 4, BLK)).reshape(SWA_KV_HEADS, 1, 4 * BLK)
        o_swa = _band_attn(sqt, sk, sv, band_swa_t, sinks, SWA_WINDOW, True, False, "swa_attn")
        o_sb = _sb_attn(bq, bk, bv)
        x = _out_proj(x, o_cmp, o_sel, o_win, gates, o_swa, o_sb, grp_norm_g[l][None], w_out[l].astype(BF16),
                      ln_attn_post[l][None], 1024)
        x = _ffn(x.reshape(t, d), ln_ffn_pre[l][None], ffn_w_gate[l].astype(BF16), ffn_w_up[l].astype(BF16),
                 ffn_w_down[l].astype(BF16), ln_ffn_post[l][None], 512).reshape(b, s, d)
    return x
```

```python
import functools
import math

import jax
import jax.numpy as jnp
from jax import lax
from jax.experimental import pallas as pl
from jax.experimental.pallas import tpu as pltpu

F32 = jnp.float32
BF16 = jnp.bfloat16

HEAD_DIM = 64
BLK = 128
NSA_HEADS = 4
CMP_LEN = 32
CMP_STRIDE = 16
CMP_HIDDEN = 256
SEL_BLK = 64
N_SEL = 16
N_LOCAL = 2
NSA_WINDOW = 512
SWA_HEADS = 8
SWA_KV_HEADS = 2
SWA_WINDOW = 128
SB_HEADS = 4
N_BUCKETS = 32
T5_MAX_DISTANCE = 4096
RMS_EPS = 1e-6
NEG = -1e30
PICKED = -3e38
LOG2E = 1.0 / math.log(2.0)
SB_LOG2_FLOOR = -104.0 * LOG2E

SEL_TILE = 512
SEL_QB = 2
CMP_QB = 8
CMP_CHUNK = 128
IMP_PAD = 16
IMP_ROWS = CMP_CHUNK // 4 + IMP_PAD
BAND_QB = 8
SB_GROUP = 3
SB_QB = 4
VT_ROWS = HEAD_DIM + 16
OUT_ROWS = 128
FFN_CHUNK = 1024
N_BAND = 25
VMEM_LIMIT = 56 * 1024 * 1024


def _cparams(sem):
    return pltpu.CompilerParams(dimension_semantics=sem, vmem_limit_bytes=VMEM_LIMIT)


def _rms(x, g):
    ms = jnp.mean(x * x, axis=-1, keepdims=True)
    return x * lax.rsqrt(ms + RMS_EPS) * g


def _iota(shape, axis):
    return lax.broadcasted_iota(jnp.int32, shape, axis)


def _bias_of_dist(dist, tab_ref, head):
    n = jnp.maximum(dist, 0)
    nf = jnp.maximum(n, 1).astype(F32)
    exact = N_BUCKETS // 2
    large = exact + (jnp.log(nf / exact) / math.log(T5_MAX_DISTANCE / exact)
                     * (N_BUCKETS - exact)).astype(jnp.int32)
    large = jnp.minimum(large, N_BUCKETS - 1)
    bucket = jnp.where(n < exact, n, large)
    val = jnp.full(dist.shape, tab_ref[0, head], F32)
    for k in range(1, N_BUCKETS):
        val = jnp.where(bucket == k, tab_ref[k, head], val)
    return val


def _band_table_t_kernel(tab_ref, o_ref, *, head_off, shift):
    h = pl.program_id(0) + head_off

    def tile(m, _):
        dist = BLK * m + _iota((BLK, BLK), 1) - _iota((BLK, BLK), 0)
        bias = _bias_of_dist(dist, tab_ref, h)
        o_ref[m] = (bias - tab_ref[N_BUCKETS - 1, h]) * LOG2E if shift else bias
        return 0

    lax.fori_loop(0, o_ref.shape[0], tile, 0)


def _band_table_t(rel_bias, n_band, n_heads, head_off, shift):
    return pl.pallas_call(
        functools.partial(_band_table_t_kernel, head_off=head_off, shift=shift),
        grid=(n_heads,),
        in_specs=[pl.BlockSpec(memory_space=pltpu.SMEM)],
        out_specs=pl.BlockSpec((n_band, BLK, BLK), lambda h: (0, 0, h)),
        out_shape=jax.ShapeDtypeStruct((n_band, BLK, n_heads * BLK), F32),
        compiler_params=_cparams(("arbitrary",)),
        name="band_table_t",
    )(rel_bias)


def _cmp_table_kernel(tab_ref, o_ref, *, ncp):
    h = pl.program_id(0)
    rel = _iota((2 * ncp, BLK), 0) - (ncp - 8)
    dist = _iota((2 * ncp, BLK), 1) - CMP_STRIDE * rel - (CMP_LEN - 1)
    o_ref[...] = _bias_of_dist(dist, tab_ref, h) * LOG2E


def _cmp_table(rel_bias, ncp):
    return pl.pallas_call(
        functools.partial(_cmp_table_kernel, ncp=ncp),
        grid=(NSA_HEADS,),
        in_specs=[pl.BlockSpec(memory_space=pltpu.SMEM)],
        out_specs=pl.BlockSpec((2 * ncp, BLK), lambda h: (0, h)),
        out_shape=jax.ShapeDtypeStruct((2 * ncp, NSA_HEADS * BLK), F32),
        compiler_params=_cparams(("arbitrary",)),
        name="cmp_table",
    )(rel_bias)


def _proj_kernel(x_ref, g_ref, w_ref, qt_ref, ks_ref, vs_ref, kw_ref, vw_ref, sqt_ref, sk_ref, sv_ref,
                 bq_ref, bk_ref, bv_ref, ckv_ref, gt_ref):
    h = _rms(x_ref[0], g_ref[...]).astype(BF16)
    tm = x_ref.shape[1]

    def seg(a, b):
        return jnp.dot(h, w_ref[:, a:b], preferred_element_type=F32)

    def heads(ref, a, n):
        y = seg(a, a + n * HEAD_DIM).astype(BF16)
        for i in range(n):
            ref[0, i] = y[:, i * HEAD_DIM:(i + 1) * HEAD_DIM]

    def queries_t(ref, a, n, scale=None):
        yt = seg(a, a + n * HEAD_DIM).T
        if scale is not None:
            yt = yt * scale
        for j in range(tm // BLK):
            for i in range(n):
                ref[0, j, :, i * BLK:(i + 1) * BLK] = (
                    yt[i * HEAD_DIM:(i + 1) * HEAD_DIM, j * BLK:(j + 1) * BLK].astype(BF16))

    ones_rows = jnp.where(_iota((VT_ROWS - HEAD_DIM, tm), 0) == 0, 1.0, 0.0)

    def values_t(yt):
        return jnp.concatenate([yt, ones_rows], axis=0).astype(BF16)

    queries_t(qt_ref, 0, NSA_HEADS, LOG2E)
    y = seg(256, 512)
    vs_ref[0] = values_t(y[:, 0:128].T[64:128])
    vw_ref[0] = values_t(y[:, 128:256].T[64:128])
    y = y.astype(BF16)
    blk_in_tile = ((pl.program_id(1) * tm + _iota((tm, HEAD_DIM), 0)) & (SEL_TILE - 1)) // SEL_BLK
    onehot = jnp.where(_iota((tm, HEAD_DIM), 1) == blk_in_tile, 1.0, 0.0).astype(BF16)
    ks_ref[0] = jnp.concatenate([y[:, 0:64], onehot], axis=1)
    kw_ref[0] = y[:, 128:192]
    queries_t(sqt_ref, 512, SWA_HEADS)
    heads(sk_ref, 1024, SWA_KV_HEADS)
    yt = seg(1152, 1280).T
    for i in range(SWA_KV_HEADS):
        sv_ref[0, i] = values_t(yt[i * HEAD_DIM:(i + 1) * HEAD_DIM])
    queries_t(bq_ref, 1280, SB_HEADS, LOG2E)
    bk_ref[0] = seg(1536, 1792).astype(BF16)
    bv_ref[0] = seg(1792, 2048).T.astype(BF16)
    y = seg(2048, 2304)
    ckv_ref[0, 0] = y[:, 0:64]
    ckv_ref[1, 0] = y[:, 64:128]
    gt_ref[0] = y[:, 128:256]


def _proj(x, g, w, tm):
    b, s, d = x.shape
    nw = w.shape[1]
    hd = lambda n: pl.BlockSpec((1, n, tm, HEAD_DIM), lambda bi, i: (bi, 0, i, 0))
    flat = lambda n: pl.BlockSpec((1, tm, n), lambda bi, i: (bi, i, 0))
    hshape = lambda n: jax.ShapeDtypeStruct((b, n, s, HEAD_DIM), BF16)
    fshape = lambda n, dt: jax.ShapeDtypeStruct((b, s, n), dt)
    qt_spec = lambda n: pl.BlockSpec((1, tm // BLK, HEAD_DIM, n * BLK), lambda bi, i: (bi, i, 0, 0))
    qt_shape = lambda n: jax.ShapeDtypeStruct((b, s // BLK, HEAD_DIM, n * BLK), BF16)
    vt_spec = pl.BlockSpec((1, VT_ROWS, tm), lambda bi, i: (bi, 0, i))
    vt_shape = jax.ShapeDtypeStruct((b, VT_ROWS, s), BF16)
    return pl.pallas_call(
        _proj_kernel,
        grid=(b, s // tm),
        in_specs=[pl.BlockSpec((1, tm, d), lambda bi, i: (bi, i, 0)),
                  pl.BlockSpec((1, d), lambda bi, i: (0, 0)),
                  pl.BlockSpec((d, nw), lambda bi, i: (0, 0))],
        out_specs=[qt_spec(NSA_HEADS), flat(128), vt_spec, flat(64), vt_spec,
                   qt_spec(SWA_HEADS), hd(SWA_KV_HEADS),
                   pl.BlockSpec((1, SWA_KV_HEADS, VT_ROWS, tm), lambda bi, i: (bi, 0, 0, i)),
                   qt_spec(SB_HEADS), flat(256),
                   pl.BlockSpec((1, SB_HEADS * HEAD_DIM, tm), lambda bi, i: (bi, 0, i)),
                   pl.BlockSpec((2, 1, tm, HEAD_DIM), lambda bi, i: (0, bi, i, 0)), flat(128)],
        out_shape=[qt_shape(NSA_HEADS), fshape(128, BF16), vt_shape, fshape(64, BF16), vt_shape,
                   qt_shape(SWA_HEADS), hshape(SWA_KV_HEADS),
                   jax.ShapeDtypeStruct((b, SWA_KV_HEADS, VT_ROWS, s), BF16),
                   qt_shape(SB_HEADS), fshape(256, BF16),
                   jax.ShapeDtypeStruct((b, SB_HEADS * HEAD_DIM, s), BF16),
                   jax.ShapeDtypeStruct((2, b, s, HEAD_DIM), F32), fshape(128, F32)],
        compiler_params=_cparams(("parallel", "parallel")),
        name="proj",
    )(x, g, w)


def _compress_kernel(c_ref, pos_ref, w1_ref, b1_ref, w2_ref, o_ref, ot_ref):
    c = c_ref[0, 0]
    nch = c.shape[0]
    xa = (c + pos_ref[0, 0:1]).astype(BF16)
    xb = (c + pos_ref[0, 1:2]).astype(BF16)
    p = jnp.dot(xa, w1_ref[0, 0], preferred_element_type=F32)
    q = jnp.dot(xb, w1_ref[0, 1], preferred_element_type=F32)
    hid = p + pltpu.roll(q, nch - 1, 0) + b1_ref[0]
    hid = hid * jax.nn.sigmoid(hid)
    out = jnp.dot(hid.astype(BF16), w2_ref[0], preferred_element_type=F32)
    row = _iota(out.shape, 0)
    out = jnp.where(row < nch - 1, out, 0.0)
    o_ref[0, 0] = out.astype(BF16)
    ot_ref[0, 0] = jnp.concatenate([out, jnp.zeros_like(out)], axis=1).T[:HEAD_DIM].astype(BF16)


def _compress(ckv, pos, w1, b1, w2):
    _, b, nch, cw = ckv.shape
    return pl.pallas_call(
        _compress_kernel,
        grid=(2, b),
        in_specs=[pl.BlockSpec((1, 1, nch, cw), lambda j, bi: (j, bi, 0, 0)),
                  pl.BlockSpec((1, 2, cw), lambda j, bi: (j, 0, 0)),
                  pl.BlockSpec((1, 2, cw, CMP_HIDDEN), lambda j, bi: (j, 0, 0, 0)),
                  pl.BlockSpec((1, 1, CMP_HIDDEN), lambda j, bi: (j, 0, 0)),
                  pl.BlockSpec((1, CMP_HIDDEN, HEAD_DIM), lambda j, bi: (j, 0, 0))],
        out_specs=[pl.BlockSpec((1, 1, nch, HEAD_DIM), lambda j, bi: (j, bi, 0, 0)),
                   pl.BlockSpec((1, 1, HEAD_DIM, nch), lambda j, bi: (j, bi, 0, 0))],
        out_shape=[jax.ShapeDtypeStruct((2, b, nch, HEAD_DIM), BF16),
                   jax.ShapeDtypeStruct((2, b, HEAD_DIM, nch), BF16)],
        compiler_params=_cparams(("parallel", "parallel")),
        name="compress",
    )(ckv, pos, w1, b1, w2)


def _transpose_pad(x, axis):
    pad = jnp.zeros(x.shape, x.dtype)
    return jnp.concatenate([x, pad], axis=axis).T


def _cmp_kernel(qt_ref, kc_ref, vct_ref, hct_ref, wmapt_ref, o_ref, pen_ref, s_scr, oacc_scr, iacc_scr):
    step = pl.program_id(1)
    ncp = kc_ref.shape[2]
    ns = wmapt_ref.shape[0] - IMP_PAD
    lanes = NSA_HEADS * BLK
    row_c = _iota((CMP_CHUNK, lanes), 0)
    lane_q = _iota((CMP_CHUNK, lanes), 1) & (BLK - 1)
    n_chunks = (CMP_QB * step) // (CMP_CHUNK // 8) + 1

    def score_chunk(ch, mxs):
        c0 = pl.multiple_of(ch * CMP_CHUNK, CMP_CHUNK)
        kc = kc_ref[0, 0, pl.ds(c0, CMP_CHUNK), :]
        raw = [jnp.dot(kc, qt_ref[0, qb], preferred_element_type=F32) for qb in range(CMP_QB)]
        out = []
        for qb in range(CMP_QB):
            n = CMP_QB * step + qb
            bias = hct_ref[pl.ds(pl.multiple_of(ncp - 8 - 8 * n + c0, 8), CMP_CHUNK), :]
            visible = (BLK * n + lane_q - CMP_STRIDE * (c0 + row_c) - (CMP_LEN - 1)) >= 0
            s = jnp.where(visible, raw[qb] + bias, NEG)
            s_scr[qb, ch] = s
            out.append(jnp.maximum(mxs[qb], jnp.max(s, axis=0, keepdims=True)))
        return tuple(out)

    mxs = lax.fori_loop(0, n_chunks, score_chunk, tuple(jnp.full((1, lanes), NEG, F32) for _ in range(CMP_QB)))
    ms = [jnp.where(mx <= 0.5 * NEG, 0.0, mx) for mx in mxs]
    oacc_scr[...] = jnp.zeros(oacc_scr.shape, F32)
    iacc_scr[...] = jnp.zeros(iacc_scr.shape, F32)

    def prob_chunk(ch, ls):
        c0 = pl.multiple_of(ch * CMP_CHUNK, CMP_CHUNK)
        vct = vct_ref[0, 0, :, pl.ds(c0, CMP_CHUNK)]
        r0 = pl.multiple_of(ch * (CMP_CHUNK // 4), 16)
        wmt = wmapt_ref[pl.ds(r0, IMP_ROWS), pl.ds(c0, CMP_CHUNK)]
        both = jnp.concatenate([vct, wmt], axis=0)
        ps = [jnp.exp2(s_scr[qb, ch] - ms[qb]) for qb in range(CMP_QB)]
        prods = [jnp.dot(both, p.astype(BF16), preferred_element_type=F32) for p in ps]
        for qb in range(CMP_QB):
            oacc_scr[qb] += prods[qb][:HEAD_DIM]
            iacc_scr[qb, pl.ds(r0, IMP_ROWS), :] += prods[qb][HEAD_DIM:]
        return tuple(ls[qb] + jnp.sum(ps[qb], axis=0, keepdims=True) for qb in range(CMP_QB))

    ls = lax.fori_loop(0, n_chunks, prob_chunk, tuple(jnp.zeros((1, lanes), F32) for _ in range(CMP_QB)))
    imps = []
    for qb in range(CMP_QB):
        inv = 1.0 / jnp.maximum(ls[qb], 1e-30)
        o_t = oacc_scr[qb] * inv
        o_ref[0, qb * BLK:(qb + 1) * BLK, :] = jnp.concatenate(
            [_transpose_pad(o_t[:, h * BLK:(h + 1) * BLK], 0)[:, :HEAD_DIM] for h in range(NSA_HEADS)], axis=1)
        w = iacc_scr[qb, 0:ns, :] * inv
        imps.append(functools.reduce(jnp.add, [w[:, h * BLK:(h + 1) * BLK] for h in range(NSA_HEADS)]))

    imp = jnp.concatenate(imps, axis=1)
    shape = (ns, CMP_QB * BLK)
    j = _iota(shape, 0)
    cur = (CMP_QB * BLK * step + _iota(shape, 1)) // SEL_BLK
    valid = j <= cur
    forced = valid & ((j == 0) | (j > cur - N_LOCAL))
    score = jnp.where(forced, PICKED, jnp.where(valid, imp, NEG))
    jf = j.astype(F32)
    for _ in range(min(N_SEL, ns) - (N_LOCAL + 1)):
        best = jnp.max(score, axis=0, keepdims=True)
        first = jnp.min(jnp.where(score == best, jf, float(ns)), axis=0, keepdims=True)
        score = jnp.where(jf == first, PICKED, score)
    pen_ref[0, 0] = jnp.where(valid & (score == PICKED), 0.0, NEG)


def _cmp_attn(qt, kcv, kcv_t, hct, wmapt):
    b, nb, _, lanes = qt.shape
    ncp = kcv.shape[2]
    ns = wmapt.shape[0] - IMP_PAD
    s = nb * BLK
    return pl.pallas_call(
        _cmp_kernel,
        grid=(b, nb // CMP_QB),
        in_specs=[pl.BlockSpec((1, CMP_QB, HEAD_DIM, lanes), lambda bi, i: (bi, i, 0, 0)),
                  pl.BlockSpec((1, 1, ncp, HEAD_DIM), lambda bi, i: (0, bi, 0, 0)),
                  pl.BlockSpec((1, 1, HEAD_DIM, ncp), lambda bi, i: (1, bi, 0, 0)),
                  pl.BlockSpec((2 * ncp, lanes), lambda bi, i: (0, 0)),
                  pl.BlockSpec((ns + IMP_PAD, ncp), lambda bi, i: (0, 0))],
        out_specs=[pl.BlockSpec((1, CMP_QB * BLK, NSA_HEADS * HEAD_DIM), lambda bi, i: (bi, i, 0)),
                   pl.BlockSpec((1, 1, ns, CMP_QB * BLK), lambda bi, i: (bi, i, 0, 0))],
        out_shape=[jax.ShapeDtypeStruct((b, s, NSA_HEADS * HEAD_DIM), F32),
                   jax.ShapeDtypeStruct((b, nb // CMP_QB, ns, CMP_QB * BLK), F32)],
        scratch_shapes=[pltpu.VMEM((CMP_QB, ncp // CMP_CHUNK, CMP_CHUNK, lanes), F32),
                        pltpu.VMEM((CMP_QB, HEAD_DIM, lanes), F32),
                        pltpu.VMEM((CMP_QB, ns + IMP_PAD, lanes), F32)],
        compiler_params=_cparams(("parallel", "arbitrary")),
        name="cmp_attn",
    )(qt, kcv, kcv_t, hct, wmapt)


def _sel_kernel(qt_ref, k_ref, vt_ref, pen_ref, band_ref, o_ref, s_scr, p_scr, mx_scr, m_scr, alpha_scr, acc_scr):
    n = SEL_QB * pl.program_id(1)
    qlanes = NSA_HEADS * BLK
    lanes = SEL_QB * qlanes
    sub = SEL_TILE // BLK
    per = BLK // SEL_BLK
    qt = jnp.concatenate([qt_ref[0, qb] for qb in range(SEL_QB)], axis=1)
    lane = _iota((BLK, lanes), 1)
    t = BLK * (n + lane // qlanes) + (lane & (BLK - 1))

    def scores_into(kt, s_buf, mx_buf, far):
        rows = sub * per
        pen = pen_ref[0, 0, pl.ds(pl.multiple_of(kt * rows, rows), rows), :]
        pen = jnp.concatenate([pen[:, qb * BLK:(qb + 1) * BLK] for qb in range(SEL_QB) for _ in range(NSA_HEADS)],
                              axis=1)
        tail = jnp.concatenate([pen, jnp.zeros((HEAD_DIM - rows, lanes), F32)], axis=0).astype(BF16)
        q_aug = jnp.concatenate([qt, tail], axis=0)
        k0 = pl.multiple_of(kt * SEL_TILE, SEL_TILE)
        s = jnp.dot(k_ref[0, pl.ds(k0, SEL_TILE), :], q_aug, preferred_element_type=F32)
        if far:
            s_buf[...] = s
            mx_buf[...] = jnp.max(s, axis=0, keepdims=True)
        else:
            mxs = []
            for qb in range(SEL_QB):
                cols = slice(qb * qlanes, (qb + 1) * qlanes)
                parts = [s[u * BLK:(u + 1) * BLK, cols] + band_ref[jnp.clip(n + qb - sub * kt - u, 0, N_BAND - 1)]
                         for u in range(sub)]
                for u in range(sub):
                    s_buf[u * BLK:(u + 1) * BLK, cols] = parts[u]
                mxs.append(functools.reduce(jnp.maximum, [jnp.max(v, axis=0, keepdims=True) for v in parts]))
            mx_buf[...] = jnp.concatenate(mxs, axis=1)

    def add_values(kt, p_buf):
        k0 = pl.multiple_of(kt * SEL_TILE, SEL_TILE)
        pv = jnp.dot(vt_ref[0, :, pl.ds(k0, SEL_TILE)], p_buf[...], preferred_element_type=F32)
        acc_scr[...] = alpha_scr[...] * acc_scr[...] + pv

    def softmax_into(kt, s_buf, mx_buf, p_buf, causal):
        def piece(u):
            s = s_buf[u * BLK:(u + 1) * BLK, :]
            if causal:
                s = jnp.where(kt * SEL_TILE + u * BLK + _iota((BLK, lanes), 0) <= t, s, NEG)
            return s
        m_i = m_scr[...]
        if causal:
            mx = functools.reduce(jnp.maximum, [jnp.max(piece(u), axis=0, keepdims=True) for u in range(sub)])
        else:
            mx = mx_buf[...]
        m_new = jnp.maximum(m_i, mx)
        for u in range(sub):
            p_buf[u * BLK:(u + 1) * BLK, :] = jnp.exp2(piece(u) - m_new).astype(BF16)
        m_scr[...] = m_new
        alpha_scr[...] = jnp.exp2(m_i - m_new)

    def stage(i, cur, nxt, p_cur, p_prev, far):
        add_values(jnp.maximum(i - 1, 0), p_prev)
        scores_into(i + 1, *nxt, far)
        softmax_into(i, *cur, p_cur, False)

    def finish(i, cur, p_cur, p_prev):
        add_values(jnp.maximum(i - 1, 0), p_prev)
        softmax_into(i, *cur, p_cur, True)
        add_values(i, p_cur)
        acc = acc_scr[...]
        o_t = acc[:HEAD_DIM] / acc[HEAD_DIM:HEAD_DIM + 1]
        for qb in range(SEL_QB):
            o_ref[0, qb * BLK:(qb + 1) * BLK, :] = jnp.concatenate(
                [_transpose_pad(o_t[:, qb * qlanes + h * BLK:qb * qlanes + (h + 1) * BLK], 0)[:, :HEAD_DIM]
                 for h in range(NSA_HEADS)], axis=1)

    s0, s1 = (s_scr.at[0], mx_scr.at[0]), (s_scr.at[1], mx_scr.at[1])
    p0, p1 = p_scr.at[0], p_scr.at[1]
    m_scr[...] = jnp.full((1, lanes), NEG, F32)
    alpha_scr[...] = jnp.ones((1, lanes), F32)
    acc_scr[...] = jnp.zeros(acc_scr.shape, F32)
    p1[...] = jnp.zeros((SEL_TILE, lanes), BF16)
    scores_into(0, *s0, False)
    last = (BLK * n + BLK - 1) // SEL_TILE
    n_far = jnp.maximum((n - (N_BAND - 2)) // sub, 0)
    far_pairs = jnp.minimum(jnp.maximum((n_far - 1) // 2, 0), last // 2)

    def pair(j, _, far):
        stage(2 * j, s0, s1, p0, p1, far)
        stage(2 * j + 1, s1, s0, p1, p0, far)
        return 0

    lax.fori_loop(0, far_pairs, functools.partial(pair, far=True), 0)
    lax.fori_loop(far_pairs, last // 2, functools.partial(pair, far=False), 0)

    @pl.when(last % 2 == 1)
    def _():
        stage(last - 1, s0, s1, p0, p1, False)
        finish(last, s1, p1, p0)

    @pl.when(last % 2 == 0)
    def _():
        finish(last, s0, p0, p1)


def _sel_attn(qt, ks, vst, pen, band_t):
    b, nb, _, qlanes = qt.shape
    s = nb * BLK
    ns = pen.shape[2]
    lanes = SEL_QB * qlanes
    per_cmp = CMP_QB // SEL_QB
    return pl.pallas_call(
        _sel_kernel,
        grid=(b, nb // SEL_QB),
        in_specs=[pl.BlockSpec((1, SEL_QB, HEAD_DIM, qlanes), lambda bi, i: (bi, i, 0, 0)),
                  pl.BlockSpec((1, s, 2 * HEAD_DIM), lambda bi, i: (bi, 0, 0)),
                  pl.BlockSpec((1, VT_ROWS, s), lambda bi, i: (bi, 0, 0)),
                  pl.BlockSpec((1, 1, ns, SEL_QB * BLK), lambda bi, i: (bi, i // per_cmp, 0, i % per_cmp)),
                  pl.BlockSpec((N_BAND, BLK, qlanes), lambda bi, i: (0, 0, 0))],
        out_specs=pl.BlockSpec((1, SEL_QB * BLK, NSA_HEADS * HEAD_DIM), lambda bi, i: (bi, i, 0)),
        out_shape=jax.ShapeDtypeStruct((b, s, NSA_HEADS * HEAD_DIM), F32),
        scratch_shapes=[pltpu.VMEM((2, SEL_TILE, lanes), F32),
                        pltpu.VMEM((2, SEL_TILE, lanes), BF16),
                        pltpu.VMEM((2, 1, lanes), F32),
                        pltpu.VMEM((1, lanes), F32),
                        pltpu.VMEM((1, lanes), F32),
                        pltpu.VMEM((VT_ROWS, lanes), F32)],
        compiler_params=_cparams(("parallel", "arbitrary")),
        name="sel_attn",
    )(qt, ks, vst, pen, band_t)


def _band_kernel(*refs, window, groups, use_sinks, base2, n_parts):
    qt_ref = refs[0]
    k_refs = refs[1:1 + n_parts]
    vt_refs = refs[1 + n_parts:1 + 2 * n_parts]
    band_ref, sink_ref, o_ref = refs[1 + 2 * n_parts:]
    step = pl.program_id(1)
    n_prev = window // BLK
    nk = (n_prev + 1) * BLK
    lanes = 4 * BLK
    row = _iota((nk, lanes), 0)
    dist = n_prev * BLK + (_iota((nk, lanes), 1) & (BLK - 1)) - row
    in_window = (dist >= 0) & (dist < window)
    chains = [(g, qb) for g in range(groups) for qb in range(BAND_QB)]
    kwin = [jnp.concatenate([r[0, g] for r in k_refs], axis=0) for g in range(groups)]
    vtwin = [jnp.concatenate([r[0, g] for r in vt_refs], axis=1) for g in range(groups)]
    bias = [jnp.concatenate([band_ref[n_prev - u, :, g * lanes:(g + 1) * lanes] for u in range(n_prev + 1)], axis=0)
            for g in range(groups)]
    scores = [jnp.dot(kwin[g][qb * BLK:qb * BLK + nk], qt_ref[0, qb, :, g * lanes:(g + 1) * lanes],
                      preferred_element_type=F32) for g, qb in chains]
    probs, maxes = [], []
    for (g, qb), s in zip(chains, scores):
        s = jnp.where(in_window & (row >= (n_prev - BAND_QB * step - qb) * BLK), s + bias[g], NEG)
        mx = jnp.max(s, axis=0, keepdims=True)
        if use_sinks:
            mx = jnp.maximum(mx, sink_ref[g])
        probs.append((jnp.exp2(s - mx) if base2 else jnp.exp(s - mx)).astype(BF16))
        maxes.append(mx)
    accs = [jnp.dot(vtwin[g][:, qb * BLK:qb * BLK + nk], p, preferred_element_type=F32)
            for (g, qb), p in zip(chains, probs)]
    for (g, qb), acc, mx in zip(chains, accs, maxes):
        den = acc[HEAD_DIM:HEAD_DIM + 1]
        if use_sinks:
            den = den + jnp.exp(sink_ref[g] - mx)
        o_t = acc[:HEAD_DIM] / den
        o_ref[0, qb * BLK:(qb + 1) * BLK, 4 * g * HEAD_DIM:4 * (g + 1) * HEAD_DIM] = jnp.concatenate(
            [_transpose_pad(o_t[:, h * BLK:(h + 1) * BLK], 0)[:, :HEAD_DIM] for h in range(4)], axis=1)


def _band_attn(qt, k, vt, band_t, sinks, window, use_sinks, base2, name):
    b, nb, _, width = qt.shape
    g = width // (4 * BLK)
    s = nb * BLK
    n_prev = window // BLK
    tq = BAND_QB * BLK
    back = n_prev * BLK
    if back % tq == 0:
        piece = tq
        starts = [lambda i, d=d: jnp.maximum(i - d, 0) for d in range(back // tq, 0, -1)]
    else:
        assert tq % back == 0
        piece = back
        starts = [lambda i: jnp.maximum(i * (tq // back) - 1, 0)]
    k_specs = [pl.BlockSpec((1, g, piece, HEAD_DIM), lambda bi, i, f=f: (bi, 0, f(i), 0)) for f in starts]
    k_specs.append(pl.BlockSpec((1, g, tq, HEAD_DIM), lambda bi, i: (bi, 0, i, 0)))
    vt_specs = [pl.BlockSpec((1, g, VT_ROWS, piece), lambda bi, i, f=f: (bi, 0, 0, f(i))) for f in starts]
    vt_specs.append(pl.BlockSpec((1, g, VT_ROWS, tq), lambda bi, i: (bi, 0, 0, i)))
    n_parts = len(k_specs)
    return pl.pallas_call(
        functools.partial(_band_kernel, window=window, groups=g, use_sinks=use_sinks, base2=base2,
                          n_parts=n_parts),
        grid=(b, s // tq),
        in_specs=[pl.BlockSpec((1, BAND_QB, HEAD_DIM, width), lambda bi, i: (bi, i, 0, 0))] + k_specs + vt_specs
                 + [pl.BlockSpec((n_prev + 1, BLK, width), lambda bi, i: (0, 0, 0)),
                    pl.BlockSpec((g, 1, 4 * BLK), lambda bi, i: (0, 0, 0))],
        out_specs=pl.BlockSpec((1, tq, g * 4 * HEAD_DIM), lambda bi, i: (bi, i, 0)),
        out_shape=jax.ShapeDtypeStruct((b, s, g * 4 * HEAD_DIM), F32),
        compiler_params=_cparams(("parallel", "arbitrary")),
        name=name,
    )(qt, *([k] * n_parts), *([vt] * n_parts), band_t, sinks)


def _sb_kernel(qt_ref, k_ref, vt_ref, later_ref, o_ref):
    n0 = SB_QB * pl.program_id(1)
    lanes = SB_HEADS * BLK
    lane_q = _iota((BLK, lanes), 1) & (BLK - 1)
    row = _iota((BLK, lanes), 0)
    later = later_ref[...]
    zero = jnp.zeros((HEAD_DIM, BLK), BF16)
    blocks = range(SB_QB)
    q_bd = [jnp.concatenate(
        [jnp.concatenate([zero] * h + [qt_ref[0, qb, :, h * BLK:(h + 1) * BLK]] + [zero] * (SB_HEADS - 1 - h), axis=0)
         for h in range(SB_HEADS)], axis=1) for qb in blocks]

    def cond(c):
        it, carry, _ = c
        return (n0 + SB_QB - 1 - SB_GROUP * it >= 0) & (jnp.max(functools.reduce(jnp.maximum, carry)) > SB_LOG2_FLOOR)

    def body(c):
        it, carry, acc = c
        k_all, vt_all, before = [], [], []
        for qb in blocks:
            top = n0 + qb - SB_GROUP * it
            ks, vts, befores = [], [], []
            for j in range(SB_GROUP - 1, -1, -1):
                kb = top - j
                k0 = pl.multiple_of(jnp.maximum(kb, 0) * BLK, BLK)
                ks.append(k_ref[0, pl.ds(k0, BLK), :])
                vts.append(vt_ref[0, :, pl.ds(k0, BLK)])
                befores.append((kb >= 0) & (k0 + row < BLK * (n0 + qb) + lane_q))
            k_all.append(jnp.concatenate(ks, axis=0))
            vt_all.append(jnp.concatenate(vts, axis=1))
            before.append(jnp.concatenate(befores, axis=0))
        z = [jnp.dot(k_all[qb], q_bd[qb], preferred_element_type=F32) for qb in blocks]
        log_not, split = [], []
        for qb in blocks:
            softplus = jnp.maximum(z[qb], 0.0) + jnp.log2(1.0 + jnp.exp2(-jnp.abs(z[qb])))
            ln = jnp.where(before[qb], -softplus, 0.0)
            hi = ln.astype(BF16)
            lo = (ln - hi.astype(F32)).astype(BF16)
            log_not.append(ln)
            split.append(jnp.concatenate([hi, lo], axis=0))
        tail = [jnp.dot(later, split[qb], preferred_element_type=F32) for qb in blocks]
        a = [jnp.where(before[qb], jnp.exp2(z[qb] + log_not[qb] + tail[qb] + carry[qb]), 0.0).astype(BF16)
             for qb in blocks]
        pv = [jnp.dot(vt_all[qb], a[qb], preferred_element_type=F32) for qb in blocks]
        acc = tuple(tuple(acc[qb][h] + pv[qb][h * HEAD_DIM:(h + 1) * HEAD_DIM, h * BLK:(h + 1) * BLK]
                          for h in range(SB_HEADS)) for qb in blocks)
        carry = tuple(carry[qb] + jnp.sum(log_not[qb], axis=0, keepdims=True) for qb in blocks)
        return it + 1, carry, acc

    init = (0, tuple(jnp.zeros((1, lanes), F32) for _ in blocks),
            tuple(tuple(jnp.zeros((HEAD_DIM, BLK), F32) for _ in range(SB_HEADS)) for _ in blocks))
    _, _, acc = lax.while_loop(cond, body, init)
    for qb in blocks:
        o_ref[0, qb * BLK:(qb + 1) * BLK, :] = jnp.concatenate(
            [_transpose_pad(a, 0)[:, :HEAD_DIM] for a in acc[qb]], axis=1)


def _sb_attn(qt, k, vt):
    b, nb, _, lanes = qt.shape
    s = nb * BLK
    width = SB_HEADS * HEAD_DIM
    gk = SB_GROUP * BLK
    later = (jnp.arange(2 * gk)[None, :] % gk > jnp.arange(gk)[:, None]).astype(BF16)
    return pl.pallas_call(
        _sb_kernel,
        grid=(b, nb // SB_QB),
        in_specs=[pl.BlockSpec((1, SB_QB, HEAD_DIM, lanes), lambda bi, n: (bi, n, 0, 0)),
                  pl.BlockSpec((1, s, width), lambda bi, n: (bi, 0, 0)),
                  pl.BlockSpec((1, width, s), lambda bi, n: (bi, 0, 0)),
                  pl.BlockSpec((gk, 2 * gk), lambda bi, n: (0, 0))],
        out_specs=pl.BlockSpec((1, SB_QB * BLK, width), lambda bi, n: (bi, n, 0)),
        out_shape=jax.ShapeDtypeStruct((b, s, width), F32),
        compiler_params=_cparams(("parallel", "arbitrary")),
        name="sb_attn",
    )(qt, k, vt, later)


def _out_kernel(x_ref, oc_ref, os_ref, ow_ref, gt_ref, swa_ref, sb_ref, gg_ref, w_ref, gp_ref, o_ref):
    wa = NSA_HEADS * HEAD_DIM
    wb = wa + SWA_HEADS * HEAD_DIM
    gg = gg_ref[...]
    lane = _iota((OUT_ROWS, wa), 1)
    groups = [slice(r0, r0 + OUT_ROWS) for r0 in range(0, x_ref.shape[1], OUT_ROWS)]
    mixes = []
    for rows in groups:
        gates = jax.nn.sigmoid(gt_ref[0, rows])

        def spread(branch, gates=gates):
            cols = [jnp.broadcast_to(gates[:, branch * NSA_HEADS + i:branch * NSA_HEADS + i + 1], (OUT_ROWS, wa))
                    for i in range(NSA_HEADS)]
            out = cols[NSA_HEADS - 1]
            for i in range(NSA_HEADS - 2, -1, -1):
                out = jnp.where(lane < (i + 1) * HEAD_DIM, cols[i], out)
            return out

        o_nsa = spread(0) * oc_ref[0, rows] + spread(1) * os_ref[0, rows] + spread(2) * ow_ref[0, rows]
        mixes.append(jnp.concatenate([_rms(o_nsa, gg[:, :wa]), _rms(swa_ref[0, rows], gg[:, wa:wb]),
                                      _rms(sb_ref[0, rows], gg[:, wb:])], axis=1).astype(BF16))
    ys = [jnp.dot(mix, w_ref[...], preferred_element_type=F32) for mix in mixes]
    for rows, y in zip(groups, ys):
        o_ref[0, rows] = x_ref[0, rows] + _rms(y, gp_ref[...])


def _out_proj(x, o_cmp, o_sel, o_win, gates, o_swa, o_sb, gg, w, gp, tm):
    b, s, d = x.shape
    row = lambda n: pl.BlockSpec((1, tm, n), lambda bi, i: (bi, i, 0))
    const = lambda a, c: pl.BlockSpec((a, c), lambda bi, i: (0, 0))
    return pl.pallas_call(
        _out_kernel,
        grid=(b, s // tm),
        in_specs=[row(d), row(256), row(256), row(256), row(128), row(512), row(256),
                  const(1, d), const(d, d), const(1, d)],
        out_specs=row(d),
        out_shape=jax.ShapeDtypeStruct((b, s, d), F32),
        compiler_params=_cparams(("parallel", "parallel")),
        name="out_proj",
    )(x, o_cmp, o_sel, o_win, gates, o_swa, o_sb, gg, w, gp)


def _ffn_kernel(x_ref, g1_ref, wg_ref, wu_ref, wd_ref, g2_ref, o_ref, *, chunk):
    x = x_ref[...]
    h = _rms(x, g1_ref[...]).astype(BF16)
    dff = wg_ref.shape[1]
    f = jnp.zeros(x.shape, F32)
    for a0 in range(0, dff, chunk):
        a1 = min(a0 + chunk, dff)
        gate = jnp.dot(h, wg_ref[:, a0:a1], preferred_element_type=F32)
        up = jnp.dot(h, wu_ref[:, a0:a1], preferred_element_type=F32)
        a = (gate * jax.nn.sigmoid(gate) * up).astype(BF16)
        f = f + jnp.dot(a, wd_ref[a0:a1, :], preferred_element_type=F32)
    o_ref[...] = x + _rms(f, g2_ref[...])


def _ffn(x, g1, wg, wu, wd, g2, tm):
    t, d = x.shape
    dff = wg.shape[1]
    row = pl.BlockSpec((tm, d), lambda i: (i, 0))
    const = lambda a, c: pl.BlockSpec((a, c), lambda i: (0, 0))
    weight = lambda a, c: pl.BlockSpec((a, c), lambda i: (0, 0), pipeline_mode=pl.Buffered(1))
    return pl.pallas_call(
        functools.partial(_ffn_kernel, chunk=FFN_CHUNK),
        grid=(t // tm,),
        in_specs=[row, const(1, d), weight(d, dff), weight(d, dff), weight(dff, d), const(1, d)],
        out_specs=row,
        out_shape=jax.ShapeDtypeStruct((t, d), F32),
        compiler_params=_cparams(("parallel",)),
        name="ffn",
    )(x, g1, wg, wu, wd, g2)


def _permute_w_in(w):
    scale = 1.0 / math.sqrt(HEAD_DIM)
    nq, kc, vc, rest, ng = w[:, :256], w[:, 256:320], w[:, 320:384], w[:, 384:640], w[:, 640:652]
    sq, skv = w[:, 652:1164], w[:, 1164:1420]
    bq, bkv = w[:, 1420:1676], w[:, 1676:2188]
    pad = jnp.zeros((w.shape[0], 128 - ng.shape[1]), w.dtype)
    return jnp.concatenate([nq * scale, rest, sq * scale, skv, bq * scale, bkv, kc, vc, ng, pad], axis=1)


def _cmp_to_sel(nc_pad, nc, ns):
    c0 = jnp.arange(nc_pad)[:, None] * CMP_STRIDE
    s0 = jnp.arange(ns)[None, :] * SEL_BLK
    ov = jnp.minimum(c0 + CMP_LEN, s0 + SEL_BLK) - jnp.maximum(c0, s0)
    w = jnp.clip(ov, 0, None).astype(F32) / CMP_LEN
    return jnp.where(jnp.arange(nc_pad)[:, None] < nc, w, 0.0).astype(BF16)


def kernel(x, rel_bias, ln_attn_pre, w_in, nsa_cmp_pos, nsa_phi_w1, nsa_phi_b1, nsa_phi_w2, swa_sinks,
           grp_norm_g, w_out, ln_attn_post, ln_ffn_pre, ffn_w_gate, ffn_w_up, ffn_w_down, ln_ffn_post):
    b, s, d = x.shape
    depth = w_in.shape[0]
    nch = s // CMP_STRIDE
    ns = s // SEL_BLK
    t = b * s

    band_nsa_t = _band_table_t(rel_bias, N_BAND, NSA_HEADS, 0, True)
    band_swa_t = _band_table_t(rel_bias, SWA_WINDOW // BLK + 1, SWA_HEADS, NSA_HEADS, False)
    hct = _cmp_table(rel_bias, nch)
    wmapt = jnp.pad(_cmp_to_sel(nch, nch - 1, ns).T, ((0, IMP_PAD), (0, 0)))
    no_sinks = jnp.zeros((1, 1, 4 * BLK), F32)

    for l in range(depth):
        outs = _proj(x, ln_attn_pre[l][None], _permute_w_in(w_in[l]).astype(BF16), 512)
        qt, ks, vs, kw, vw, sqt, sk, sv, bq, bk, bv, ckv, gates = outs
        kcv, kcv_t = _compress(ckv.reshape(2, b, nch, CMP_STRIDE * HEAD_DIM),
                               nsa_cmp_pos[l].reshape(2, 2, CMP_STRIDE * HEAD_DIM),
                               nsa_phi_w1[l].reshape(2, 2, CMP_STRIDE * HEAD_DIM, CMP_HIDDEN).astype(BF16),
                               nsa_phi_b1[l][:, None, :], nsa_phi_w2[l].astype(BF16))
        o_cmp, selmask = _cmp_attn(qt, kcv, kcv_t, hct, wmapt)
        o_sel = _sel_attn(qt, ks, vs, selmask, band_nsa_t)
        o_win = _band_attn(qt, kw[:, None], vw[:, None], band_nsa_t, no_sinks, NSA_WINDOW, False, True,
                           "win_attn")
        sinks = jnp.broadcast_to(swa_sinks[l].reshape(SWA_KV_HEADS, 1, 4, 1),
                                 (SWA_KV_HEADS, 1, 4, BLK)).reshape(SWA_KV_HEADS, 1, 4 * BLK)
        o_swa = _band_attn(sqt, sk, sv, band_swa_t, sinks, SWA_WINDOW, True, False, "swa_attn")
        o_sb = _sb_attn(bq, bk, bv)
        x = _out_proj(x, o_cmp, o_sel, o_win, gates, o_swa, o_sb, grp_norm_g[l][None], w_out[l].astype(BF16),
                      ln_attn_post[l][None], 1024)
        x = _ffn(x.reshape(t, d), ln_ffn_pre[l][None], ffn_w_gate[l].astype(BF16), ffn_w_up[l].astype(BF16),
                 ffn_w_down[l].astype(BF16), ln_ffn_post[l][None], 512).reshape(b, s, d)
    return x
```

```python
import functools
import math

import jax
import jax.numpy as jnp
from jax import lax
from jax.experimental import pallas as pl
from jax.experimental.pallas import tpu as pltpu

F32 = jnp.float32
BF16 = jnp.bfloat16

HEAD_DIM = 64
BLK = 128
NSA_HEADS = 4
CMP_LEN = 32
CMP_STRIDE = 16
CMP_HIDDEN = 256
SEL_BLK = 64
N_SEL = 16
N_LOCAL = 2
NSA_WINDOW = 512
SWA_HEADS = 8
SWA_KV_HEADS = 2
SWA_WINDOW = 128
SB_HEADS = 4
N_BUCKETS = 32
T5_MAX_DISTANCE = 4096
RMS_EPS = 1e-6
NEG = -1e30
PICKED = -3e38
LOG2E = 1.0 / math.log(2.0)
SB_LOG2_FLOOR = -104.0 * LOG2E

SEL_TILE = 512
SEL_QB = 2
CMP_QB = 8
CMP_CHUNK = 128
IMP_PAD = 16
IMP_ROWS = CMP_CHUNK // 4 + IMP_PAD
BAND_QB = 8
SB_GROUP = 3
SB_QB = 4
VT_ROWS = HEAD_DIM + 16
OUT_ROWS = 128
FFN_CHUNK = 1024
N_BAND = 25
VMEM_LIMIT = 56 * 1024 * 1024


def _cparams(sem):
    return pltpu.CompilerParams(dimension_semantics=sem, vmem_limit_bytes=VMEM_LIMIT)


def _rms(x, g):
    ms = jnp.mean(x * x, axis=-1, keepdims=True)
    return x * lax.rsqrt(ms + RMS_EPS) * g


def _iota(shape, axis):
    return lax.broadcasted_iota(jnp.int32, shape, axis)


def _bias_of_dist(dist, tab_ref, head):
    n = jnp.maximum(dist, 0)
    nf = jnp.maximum(n, 1).astype(F32)
    exact = N_BUCKETS // 2
    large = exact + (jnp.log(nf / exact) / math.log(T5_MAX_DISTANCE / exact)
                     * (N_BUCKETS - exact)).astype(jnp.int32)
    large = jnp.minimum(large, N_BUCKETS - 1)
    bucket = jnp.where(n < exact, n, large)
    val = jnp.full(dist.shape, tab_ref[0, head], F32)
    for k in range(1, N_BUCKETS):
        val = jnp.where(bucket == k, tab_ref[k, head], val)
    return val


def _band_table_t_kernel(tab_ref, o_ref, *, head_off, shift):
    h = pl.program_id(0) + head_off

    def tile(m, _):
        dist = BLK * m + _iota((BLK, BLK), 1) - _iota((BLK, BLK), 0)
        bias = _bias_of_dist(dist, tab_ref, h)
        o_ref[m] = (bias - tab_ref[N_BUCKETS - 1, h]) * LOG2E if shift else bias
        return 0

    lax.fori_loop(0, o_ref.shape[0], tile, 0)


def _band_table_t(rel_bias, n_band, n_heads, head_off, shift):
    return pl.pallas_call(
        functools.partial(_band_table_t_kernel, head_off=head_off, shift=shift),
        grid=(n_heads,),
        in_specs=[pl.BlockSpec(memory_space=pltpu.SMEM)],
        out_specs=pl.BlockSpec((n_band, BLK, BLK), lambda h: (0, 0, h)),
        out_shape=jax.ShapeDtypeStruct((n_band, BLK, n_heads * BLK), F32),
        compiler_params=_cparams(("arbitrary",)),
        name="band_table_t",
    )(rel_bias)


def _cmp_table_kernel(tab_ref, o_ref, *, ncp):
    h = pl.program_id(0)
    rel = _iota((2 * ncp, BLK), 0) - (ncp - 8)
    dist = _iota((2 * ncp, BLK), 1) - CMP_STRIDE * rel - (CMP_LEN - 1)
    o_ref[...] = _bias_of_dist(dist, tab_ref, h) * LOG2E


def _cmp_table(rel_bias, ncp):
    return pl.pallas_call(
        functools.partial(_cmp_table_kernel, ncp=ncp),
        grid=(NSA_HEADS,),
        in_specs=[pl.BlockSpec(memory_space=pltpu.SMEM)],
        out_specs=pl.BlockSpec((2 * ncp, BLK), lambda h: (0, h)),
        out_shape=jax.ShapeDtypeStruct((2 * ncp, NSA_HEADS * BLK), F32),
        compiler_params=_cparams(("arbitrary",)),
        name="cmp_table",
    )(rel_bias)


def _proj_kernel(x_ref, g_ref, w_ref, qt_ref, ks_ref, vs_ref, kw_ref, vw_ref, sqt_ref, sk_ref, sv_ref,
                 bq_ref, bk_ref, bv_ref, ckv_ref, gt_ref):
    h = _rms(x_ref[0], g_ref[...]).astype(BF16)
    tm = x_ref.shape[1]

    def seg(a, b):
        return jnp.dot(h, w_ref[:, a:b], preferred_element_type=F32)

    def heads(ref, a, n):
        y = seg(a, a + n * HEAD_DIM).astype(BF16)
        for i in range(n):
            ref[0, i] = y[:, i * HEAD_DIM:(i + 1) * HEAD_DIM]

    def queries_t(ref, a, n, scale=None):
        yt = seg(a, a + n * HEAD_DIM).T
        if scale is not None:
            yt = yt * scale
        for j in range(tm // BLK):
            for i in range(n):
                ref[0, j, :, i * BLK:(i + 1) * BLK] = (
                    yt[i * HEAD_DIM:(i + 1) * HEAD_DIM, j * BLK:(j + 1) * BLK].astype(BF16))

    ones_rows = jnp.where(_iota((VT_ROWS - HEAD_DIM, tm), 0) == 0, 1.0, 0.0)

    def values_t(yt):
        return jnp.concatenate([yt, ones_rows], axis=0).astype(BF16)

    queries_t(qt_ref, 0, NSA_HEADS, LOG2E)
    y = seg(256, 512)
    vs_ref[0] = values_t(y[:, 0:128].T[64:128])
    vw_ref[0] = values_t(y[:, 128:256].T[64:128])
    y = y.astype(BF16)
    blk_in_tile = ((pl.program_id(1) * tm + _iota((tm, HEAD_DIM), 0)) & (SEL_TILE - 1)) // SEL_BLK
    onehot = jnp.where(_iota((tm, HEAD_DIM), 1) == blk_in_tile, 1.0, 0.0).astype(BF16)
    ks_ref[0] = jnp.concatenate([y[:, 0:64], onehot], axis=1)
    kw_ref[0] = y[:, 128:192]
    queries_t(sqt_ref, 512, SWA_HEADS)
    heads(sk_ref, 1024, SWA_KV_HEADS)
    yt = seg(1152, 1280).T
    for i in range(SWA_KV_HEADS):
        sv_ref[0, i] = values_t(yt[i * HEAD_DIM:(i + 1) * HEAD_DIM])
    queries_t(bq_ref, 1280, SB_HEADS, LOG2E)
    bk_ref[0] = seg(1536, 1792).astype(BF16)
    bv_ref[0] = seg(1792, 2048).T.astype(BF16)
    y = seg(2048, 2304)
    ckv_ref[0, 0] = y[:, 0:64]
    ckv_ref[1, 0] = y[:, 64:128]
    gt_ref[0] = y[:, 128:256]


def _proj(x, g, w, tm):
    b, s, d = x.shape
    nw = w.shape[1]
    hd = lambda n: pl.BlockSpec((1, n, tm, HEAD_DIM), lambda bi, i: (bi, 0, i, 0))
    flat = lambda n: pl.BlockSpec((1, tm, n), lambda bi, i: (bi, i, 0))
    hshape = lambda n: jax.ShapeDtypeStruct((b, n, s, HEAD_DIM), BF16)
    fshape = lambda n, dt: jax.ShapeDtypeStruct((b, s, n), dt)
    qt_spec = lambda n: pl.BlockSpec((1, tm // BLK, HEAD_DIM, n * BLK), lambda bi, i: (bi, i, 0, 0))
    qt_shape = lambda n: jax.ShapeDtypeStruct((b, s // BLK, HEAD_DIM, n * BLK), BF16)
    vt_spec = pl.BlockSpec((1, VT_ROWS, tm), lambda bi, i: (bi, 0, i))
    vt_shape = jax.ShapeDtypeStruct((b, VT_ROWS, s), BF16)
    return pl.pallas_call(
        _proj_kernel,
        grid=(b, s // tm),
        in_specs=[pl.BlockSpec((1, tm, d), lambda bi, i: (bi, i, 0)),
                  pl.BlockSpec((1, d), lambda bi, i: (0, 0)),
                  pl.BlockSpec((d, nw), lambda bi, i: (0, 0))],
        out_specs=[qt_spec(NSA_HEADS), flat(128), vt_spec, flat(64), vt_spec,
                   qt_spec(SWA_HEADS), hd(SWA_KV_HEADS),
                   pl.BlockSpec((1, SWA_KV_HEADS, VT_ROWS, tm), lambda bi, i: (bi, 0, 0, i)),
                   qt_spec(SB_HEADS), flat(256),
                   pl.BlockSpec((1, SB_HEADS * HEAD_DIM, tm), lambda bi, i: (bi, 0, i)),
                   pl.BlockSpec((2, 1, tm, HEAD_DIM), lambda bi, i: (0, bi, i, 0)), flat(128)],
        out_shape=[qt_shape(NSA_HEADS), fshape(128, BF16), vt_shape, fshape(64, BF16), vt_shape,
                   qt_shape(SWA_HEADS), hshape(SWA_KV_HEADS),
                   jax.ShapeDtypeStruct((b, SWA_KV_HEADS, VT_ROWS, s), BF16),
                   qt_shape(SB_HEADS), fshape(256, BF16),
                   jax.ShapeDtypeStruct((b, SB_HEADS * HEAD_DIM, s), BF16),
                   jax.ShapeDtypeStruct((2, b, s, HEAD_DIM), F32), fshape(128, F32)],
        compiler_params=_cparams(("parallel", "parallel")),
        name="proj",
    )(x, g, w)


def _compress_kernel(c_ref, pos_ref, w1_ref, b1_ref, w2_ref, o_ref, ot_ref):
    c = c_ref[0, 0]
    nch = c.shape[0]
    xa = (c + pos_ref[0, 0:1]).astype(BF16)
    xb = (c + pos_ref[0, 1:2]).astype(BF16)
    p = jnp.dot(xa, w1_ref[0, 0], preferred_element_type=F32)
    q = jnp.dot(xb, w1_ref[0, 1], preferred_element_type=F32)
    hid = p + pltpu.roll(q, nch - 1, 0) + b1_ref[0]
    hid = hid * jax.nn.sigmoid(hid)
    out = jnp.dot(hid.astype(BF16), w2_ref[0], preferred_element_type=F32)
    row = _iota(out.shape, 0)
    out = jnp.where(row < nch - 1, out, 0.0)
    o_ref[0, 0] = out.astype(BF16)
    ot_ref[0, 0] = jnp.concatenate([out, jnp.zeros_like(out)], axis=1).T[:HEAD_DIM].astype(BF16)


def _compress(ckv, pos, w1, b1, w2):
    _, b, nch, cw = ckv.shape
    return pl.pallas_call(
        _compress_kernel,
        grid=(2, b),
        in_specs=[pl.BlockSpec((1, 1, nch, cw), lambda j, bi: (j, bi, 0, 0)),
                  pl.BlockSpec((1, 2, cw), lambda j, bi: (j, 0, 0)),
                  pl.BlockSpec((1, 2, cw, CMP_HIDDEN), lambda j, bi: (j, 0, 0, 0)),
                  pl.BlockSpec((1, 1, CMP_HIDDEN), lambda j, bi: (j, 0, 0)),
                  pl.BlockSpec((1, CMP_HIDDEN, HEAD_DIM), lambda j, bi: (j, 0, 0))],
        out_specs=[pl.BlockSpec((1, 1, nch, HEAD_DIM), lambda j, bi: (j, bi, 0, 0)),
                   pl.BlockSpec((1, 1, HEAD_DIM, nch), lambda j, bi: (j, bi, 0, 0))],
        out_shape=[jax.ShapeDtypeStruct((2, b, nch, HEAD_DIM), BF16),
                   jax.ShapeDtypeStruct((2, b, HEAD_DIM, nch), BF16)],
        compiler_params=_cparams(("parallel", "parallel")),
        name="compress",
    )(ckv, pos, w1, b1, w2)


def _transpose_pad(x, axis):
    pad = jnp.zeros(x.shape, x.dtype)
    return jnp.concatenate([x, pad], axis=axis).T


def _cmp_kernel(qt_ref, kc_ref, vct_ref, hct_ref, wmapt_ref, o_ref, pen_ref, s_scr, oacc_scr, iacc_scr):
    step = pl.program_id(1)
    ncp = kc_ref.shape[2]
    ns = wmapt_ref.shape[0] - IMP_PAD
    lanes = NSA_HEADS * BLK
    row_c = _iota((CMP_CHUNK, lanes), 0)
    lane_q = _iota((CMP_CHUNK, lanes), 1) & (BLK - 1)
    n_chunks = (CMP_QB * step) // (CMP_CHUNK // 8) + 1

    def score_chunk(ch, mxs):
        c0 = pl.multiple_of(ch * CMP_CHUNK, CMP_CHUNK)
        kc = kc_ref[0, 0, pl.ds(c0, CMP_CHUNK), :]
        raw = [jnp.dot(kc, qt_ref[0, qb], preferred_element_type=F32) for qb in range(CMP_QB)]
        out = []
        for qb in range(CMP_QB):
            n = CMP_QB * step + qb
            bias = hct_ref[pl.ds(pl.multiple_of(ncp - 8 - 8 * n + c0, 8), CMP_CHUNK), :]
            visible = (BLK * n + lane_q - CMP_STRIDE * (c0 + row_c) - (CMP_LEN - 1)) >= 0
            s = jnp.where(visible, raw[qb] + bias, NEG)
            s_scr[qb, ch] = s
            out.append(jnp.maximum(mxs[qb], jnp.max(s, axis=0, keepdims=True)))
        return tuple(out)

    mxs = lax.fori_loop(0, n_chunks, score_chunk, tuple(jnp.full((1, lanes), NEG, F32) for _ in range(CMP_QB)))
    ms = [jnp.where(mx <= 0.5 * NEG, 0.0, mx) for mx in mxs]
    oacc_scr[...] = jnp.zeros(oacc_scr.shape, F32)
    iacc_scr[...] = jnp.zeros(iacc_scr.shape, F32)

    def prob_chunk(ch, ls):
        c0 = pl.multiple_of(ch * CMP_CHUNK, CMP_CHUNK)
        vct = vct_ref[0, 0, :, pl.ds(c0, CMP_CHUNK)]
        r0 = pl.multiple_of(ch * (CMP_CHUNK // 4), 16)
        wmt = wmapt_ref[pl.ds(r0, IMP_ROWS), pl.ds(c0, CMP_CHUNK)]
        both = jnp.concatenate([vct, wmt], axis=0)
        ps = [jnp.exp2(s_scr[qb, ch] - ms[qb]) for qb in range(CMP_QB)]
        prods = [jnp.dot(both, p.astype(BF16), preferred_element_type=F32) for p in ps]
        for qb in range(CMP_QB):
            oacc_scr[qb] += prods[qb][:HEAD_DIM]
            iacc_scr[qb, pl.ds(r0, IMP_ROWS), :] += prods[qb][HEAD_DIM:]
        return tuple(ls[qb] + jnp.sum(ps[qb], axis=0, keepdims=True) for qb in range(CMP_QB))

    ls = lax.fori_loop(0, n_chunks, prob_chunk, tuple(jnp.zeros((1, lanes), F32) for _ in range(CMP_QB)))
    imps = []
    for qb in range(CMP_QB):
        inv = 1.0 / jnp.maximum(ls[qb], 1e-30)
        o_t = oacc_scr[qb] * inv
        o_ref[0, qb * BLK:(qb + 1) * BLK, :] = jnp.concatenate(
            [_transpose_pad(o_t[:, h * BLK:(h + 1) * BLK], 0)[:, :HEAD_DIM] for h in range(NSA_HEADS)], axis=1)
        w = iacc_scr[qb, 0:ns, :] * inv
        imps.append(functools.reduce(jnp.add, [w[:, h * BLK:(h + 1) * BLK] for h in range(NSA_HEADS)]))

    imp = jnp.concatenate(imps, axis=1)
    shape = (ns, CMP_QB * BLK)
    j = _iota(shape, 0)
    cur = (CMP_QB * BLK * step + _iota(shape, 1)) // SEL_BLK
    valid = j <= cur
    forced = valid & ((j == 0) | (j > cur - N_LOCAL))
    score = jnp.where(forced, PICKED, jnp.where(valid, imp, NEG))
    jf = j.astype(F32)
    for _ in range(min(N_SEL, ns) - (N_LOCAL + 1)):
        best = jnp.max(score, axis=0, keepdims=True)
        first = jnp.min(jnp.where(score == best, jf, float(ns)), axis=0, keepdims=True)
        score = jnp.where(jf == first, PICKED, score)
    pen_ref[0, 0] = jnp.where(valid & (score == PICKED), 0.0, NEG)


def _cmp_attn(qt, kcv, kcv_t, hct, wmapt):
    b, nb, _, lanes = qt.shape
    ncp = kcv.shape[2]
    ns = wmapt.shape[0] - IMP_PAD
    s = nb * BLK
    return pl.pallas_call(
        _cmp_kernel,
        grid=(b, nb // CMP_QB),
        in_specs=[pl.BlockSpec((1, CMP_QB, HEAD_DIM, lanes), lambda bi, i: (bi, i, 0, 0)),
                  pl.BlockSpec((1, 1, ncp, HEAD_DIM), lambda bi, i: (0, bi, 0, 0)),
                  pl.BlockSpec((1, 1, HEAD_DIM, ncp), lambda bi, i: (1, bi, 0, 0)),
                  pl.BlockSpec((2 * ncp, lanes), lambda bi, i: (0, 0)),
                  pl.BlockSpec((ns + IMP_PAD, ncp), lambda bi, i: (0, 0))],
        out_specs=[pl.BlockSpec((1, CMP_QB * BLK, NSA_HEADS * HEAD_DIM), lambda bi, i: (bi, i, 0)),
                   pl.BlockSpec((1, 1, ns, CMP_QB * BLK), lambda bi, i: (bi, i, 0, 0))],
        out_shape=[jax.ShapeDtypeStruct((b, s, NSA_HEADS * HEAD_DIM), F32),
                   jax.ShapeDtypeStruct((b, nb // CMP_QB, ns, CMP_QB * BLK), F32)],
        scratch_shapes=[pltpu.VMEM((CMP_QB, ncp // CMP_CHUNK, CMP_CHUNK, lanes), F32),
                        pltpu.VMEM((CMP_QB, HEAD_DIM, lanes), F32),
                        pltpu.VMEM((CMP_QB, ns + IMP_PAD, lanes), F32)],
        compiler_params=_cparams(("parallel", "arbitrary")),
        name="cmp_attn",
    )(qt, kcv, kcv_t, hct, wmapt)


def _sel_kernel(qt_ref, k_ref, vt_ref, pen_ref, band_ref, o_ref, s_scr, p_scr, mx_scr, m_scr, alpha_scr, acc_scr):
    n = SEL_QB * pl.program_id(1)
    qlanes = NSA_HEADS * BLK
    lanes = SEL_QB * qlanes
    sub = SEL_TILE // BLK
    per = BLK // SEL_BLK
    qt = jnp.concatenate([qt_ref[0, qb] for qb in range(SEL_QB)], axis=1)
    lane = _iota((BLK, lanes), 1)
    t = BLK * (n + lane // qlanes) + (lane & (BLK - 1))

    def scores_into(kt, s_buf, mx_buf, far):
        rows = sub * per
        pen = pen_ref[0, 0, pl.ds(pl.multiple_of(kt * rows, rows), rows), :]
        pen = jnp.concatenate([pen[:, qb * BLK:(qb + 1) * BLK] for qb in range(SEL_QB) for _ in range(NSA_HEADS)],
                              axis=1)
        tail = jnp.concatenate([pen, jnp.zeros((HEAD_DIM - rows, lanes), F32)], axis=0).astype(BF16)
        q_aug = jnp.concatenate([qt, tail], axis=0)
        k0 = pl.multiple_of(kt * SEL_TILE, SEL_TILE)
        s = jnp.dot(k_ref[0, pl.ds(k0, SEL_TILE), :], q_aug, preferred_element_type=F32)
        if far:
            s_buf[...] = s
            mx_buf[...] = jnp.max(s, axis=0, keepdims=True)
        else:
            mxs = []
            for qb in range(SEL_QB):
                cols = slice(qb * qlanes, (qb + 1) * qlanes)
                parts = [s[u * BLK:(u + 1) * BLK, cols] + band_ref[jnp.clip(n + qb - sub * kt - u, 0, N_BAND - 1)]
                         for u in range(sub)]
                for u in range(sub):
                    s_buf[u * BLK:(u + 1) * BLK, cols] = parts[u]
                mxs.append(functools.reduce(jnp.maximum, [jnp.max(v, axis=0, keepdims=True) for v in parts]))
            mx_buf[...] = jnp.concatenate(mxs, axis=1)

    def add_values(kt, p_buf):
        k0 = pl.multiple_of(kt * SEL_TILE, SEL_TILE)
        pv = jnp.dot(vt_ref[0, :, pl.ds(k0, SEL_TILE)], p_buf[...], preferred_element_type=F32)
        acc_scr[...] = alpha_scr[...] * acc_scr[...] + pv

    def softmax_into(kt, s_buf, mx_buf, p_buf, causal):
        def piece(u):
            s = s_buf[u * BLK:(u + 1) * BLK, :]
            if causal:
                s = jnp.where(kt * SEL_TILE + u * BLK + _iota((BLK, lanes), 0) <= t, s, NEG)
            return s
        m_i = m_scr[...]
        if causal:
            mx = functools.reduce(jnp.maximum, [jnp.max(piece(u), axis=0, keepdims=True) for u in range(sub)])
        else:
            mx = mx_buf[...]
        m_new = jnp.maximum(m_i, mx)
        for u in range(sub):
            p_buf[u * BLK:(u + 1) * BLK, :] = jnp.exp2(piece(u) - m_new).astype(BF16)
        m_scr[...] = m_new
        alpha_scr[...] = jnp.exp2(m_i - m_new)

    def stage(i, cur, nxt, p_cur, p_prev, far):
        add_values(jnp.maximum(i - 1, 0), p_prev)
        scores_into(i + 1, *nxt, far)
        softmax_into(i, *cur, p_cur, False)

    def finish(i, cur, p_cur, p_prev):
        add_values(jnp.maximum(i - 1, 0), p_prev)
        softmax_into(i, *cur, p_cur, True)
        add_values(i, p_cur)
        acc = acc_scr[...]
        o_t = acc[:HEAD_DIM] / acc[HEAD_DIM:HEAD_DIM + 1]
        for qb in range(SEL_QB):
            o_ref[0, qb * BLK:(qb + 1) * BLK, :] = jnp.concatenate(
                [_transpose_pad(o_t[:, qb * qlanes + h * BLK:qb * qlanes + (h + 1) * BLK], 0)[:, :HEAD_DIM]
                 for h in range(NSA_HEADS)], axis=1)

    s0, s1 = (s_scr.at[0], mx_scr.at[0]), (s_scr.at[1], mx_scr.at[1])
    p0, p1 = p_scr.at[0], p_scr.at[1]
    m_scr[...] = jnp.full((1, lanes), NEG, F32)
    alpha_scr[...] = jnp.ones((1, lanes), F32)
    acc_scr[...] = jnp.zeros(acc_scr.shape, F32)
    p1[...] = jnp.zeros((SEL_TILE, lanes), BF16)
    scores_into(0, *s0, False)
    last = (BLK * n + BLK - 1) // SEL_TILE
    n_far = jnp.maximum((n - (N_BAND - 2)) // sub, 0)
    far_pairs = jnp.minimum(jnp.maximum((n_far - 1) // 2, 0), last // 2)

    def pair(j, _, far):
        stage(2 * j, s0, s1, p0, p1, far)
        stage(2 * j + 1, s1, s0, p1, p0, far)
        return 0

    lax.fori_loop(0, far_pairs, functools.partial(pair, far=True), 0)
    lax.fori_loop(far_pairs, last // 2, functools.partial(pair, far=False), 0)

    @pl.when(last % 2 == 1)
    def _():
        stage(last - 1, s0, s1, p0, p1, False)
        finish(last, s1, p1, p0)

    @pl.when(last % 2 == 0)
    def _():
        finish(last, s0, p0, p1)


def _sel_attn(qt, ks, vst, pen, band_t):
    b, nb, _, qlanes = qt.shape
    s = nb * BLK
    ns = pen.shape[2]
    lanes = SEL_QB * qlanes
    per_cmp = CMP_QB // SEL_QB
    return pl.pallas_call(
        _sel_kernel,
        grid=(b, nb // SEL_QB),
        in_specs=[pl.BlockSpec((1, SEL_QB, HEAD_DIM, qlanes), lambda bi, i: (bi, i, 0, 0)),
                  pl.BlockSpec((1, s, 2 * HEAD_DIM), lambda bi, i: (bi, 0, 0)),
                  pl.BlockSpec((1, VT_ROWS, s), lambda bi, i: (bi, 0, 0)),
                  pl.BlockSpec((1, 1, ns, SEL_QB * BLK), lambda bi, i: (bi, i // per_cmp, 0, i % per_cmp)),
                  pl.BlockSpec((N_BAND, BLK, qlanes), lambda bi, i: (0, 0, 0))],
        out_specs=pl.BlockSpec((1, SEL_QB * BLK, NSA_HEADS * HEAD_DIM), lambda bi, i: (bi, i, 0)),
        out_shape=jax.ShapeDtypeStruct((b, s, NSA_HEADS * HEAD_DIM), F32),
        scratch_shapes=[pltpu.VMEM((2, SEL_TILE, lanes), F32),
                        pltpu.VMEM((2, SEL_TILE, lanes), BF16),
                        pltpu.VMEM((2, 1, lanes), F32),
                        pltpu.VMEM((1, lanes), F32),
                        pltpu.VMEM((1, lanes), F32),
                        pltpu.VMEM((VT_ROWS, lanes), F32)],
        compiler_params=_cparams(("parallel", "arbitrary")),
        name="sel_attn",
    )(qt, ks, vst, pen, band_t)


def _band_kernel(*refs, window, groups, use_sinks, base2, n_parts):
    qt_ref = refs[0]
    k_refs = refs[1:1 + n_parts]
    vt_refs = refs[1 + n_parts:1 + 2 * n_parts]
    band_ref, sink_ref, o_ref = refs[1 + 2 * n_parts:]
    step = pl.program_id(1)
    n_prev = window // BLK
    nk = (n_prev + 1) * BLK
    lanes = 4 * BLK
    row = _iota((nk, lanes), 0)
    dist = n_prev * BLK + (_iota((nk, lanes), 1) & (BLK - 1)) - row
    in_window = (dist >= 0) & (dist < window)
    chains = [(g, qb) for g in range(groups) for qb in range(BAND_QB)]
    kwin = [jnp.concatenate([r[0, g] for r in k_refs], axis=0) for g in range(groups)]
    vtwin = [jnp.concatenate([r[0, g] for r in vt_refs], axis=1) for g in range(groups)]
    bias = [jnp.concatenate([band_ref[n_prev - u, :, g * lanes:(g + 1) * lanes] for u in range(n_prev + 1)], axis=0)
            for g in range(groups)]
    scores = [jnp.dot(kwin[g][qb * BLK:qb * BLK + nk], qt_ref[0, qb, :, g * lanes:(g + 1) * lanes],
                      preferred_element_type=F32) for g, qb in chains]
    probs, maxes = [], []
    for (g, qb), s in zip(chains, scores):
        s = jnp.where(in_window & (row >= (n_prev - BAND_QB * step - qb) * BLK), s + bias[g], NEG)
        mx = jnp.max(s, axis=0, keepdims=True)
        if use_sinks:
            mx = jnp.maximum(mx, sink_ref[g])
        probs.append((jnp.exp2(s - mx) if base2 else jnp.exp(s - mx)).astype(BF16))
        maxes.append(mx)
    accs = [jnp.dot(vtwin[g][:, qb * BLK:qb * BLK + nk], p, preferred_element_type=F32)
            for (g, qb), p in zip(chains, probs)]
    for (g, qb), acc, mx in zip(chains, accs, maxes):
        den = acc[HEAD_DIM:HEAD_DIM + 1]
        if use_sinks:
            den = den + jnp.exp(sink_ref[g] - mx)
        o_t = acc[:HEAD_DIM] / den
        o_ref[0, qb * BLK:(qb + 1) * BLK, 4 * g * HEAD_DIM:4 * (g + 1) * HEAD_DIM] = jnp.concatenate(
            [_transpose_pad(o_t[:, h * BLK:(h + 1) * BLK], 0)[:, :HEAD_DIM] for h in range(4)], axis=1)


def _band_attn(qt, k, vt, band_t, sinks, window, use_sinks, base2, name):
    b, nb, _, width = qt.shape
    g = width // (4 * BLK)
    s = nb * BLK
    n_prev = window // BLK
    tq = BAND_QB * BLK
    back = n_prev * BLK
    if back % tq == 0:
        piece = tq
        starts = [lambda i, d=d: jnp.maximum(i - d, 0) for d in range(back // tq, 0, -1)]
    else:
        assert tq % back == 0
        piece = back
        starts = [lambda i: jnp.maximum(i * (tq // back) - 1, 0)]
    k_specs = [pl.BlockSpec((1, g, piece, HEAD_DIM), lambda bi, i, f=f: (bi, 0, f(i), 0)) for f in starts]
    k_specs.append(pl.BlockSpec((1, g, tq, HEAD_DIM), lambda bi, i: (bi, 0, i, 0)))
    vt_specs = [pl.BlockSpec((1, g, VT_ROWS, piece), lambda bi, i, f=f: (bi, 0, 0, f(i))) for f in starts]
    vt_specs.append(pl.BlockSpec((1, g, VT_ROWS, tq), lambda bi, i: (bi, 0, 0, i)))
    n_parts = len(k_specs)
    return pl.pallas_call(
        functools.partial(_band_kernel, window=window, groups=g, use_sinks=use_sinks, base2=base2,
                          n_parts=n_parts),
        grid=(b, s // tq),
        in_specs=[pl.BlockSpec((1, BAND_QB, HEAD_DIM, width), lambda bi, i: (bi, i, 0, 0))] + k_specs + vt_specs
                 + [pl.BlockSpec((n_prev + 1, BLK, width), lambda bi, i: (0, 0, 0)),
                    pl.BlockSpec((g, 1, 4 * BLK), lambda bi, i: (0, 0, 0))],
        out_specs=pl.BlockSpec((1, tq, g * 4 * HEAD_DIM), lambda bi, i: (bi, i, 0)),
        out_shape=jax.ShapeDtypeStruct((b, s, g * 4 * HEAD_DIM), F32),
        compiler_params=_cparams(("parallel", "arbitrary")),
        name=name,
    )(qt, *([k] * n_parts), *([vt] * n_parts), band_t, sinks)


def _sb_kernel(qt_ref, k_ref, vt_ref, later_ref, o_ref):
    n0 = SB_QB * pl.program_id(1)
    lanes = SB_HEADS * BLK
    lane_q = _iota((BLK, lanes), 1) & (BLK - 1)
    row = _iota((BLK, lanes), 0)
    later = later_ref[...]
    zero = jnp.zeros((HEAD_DIM, BLK), BF16)
    blocks = range(SB_QB)
    q_bd = [jnp.concatenate(
        [jnp.concatenate([zero] * h + [qt_ref[0, qb, :, h * BLK:(h + 1) * BLK]] + [zero] * (SB_HEADS - 1 - h), axis=0)
         for h in range(SB_HEADS)], axis=1) for qb in blocks]

    def cond(c):
        it, carry, _ = c
        return (n0 + SB_QB - 1 - SB_GROUP * it >= 0) & (jnp.max(functools.reduce(jnp.maximum, carry)) > SB_LOG2_FLOOR)

    def body(c):
        it, carry, acc = c
        k_all, vt_all, before = [], [], []
        for qb in blocks:
            top = n0 + qb - SB_GROUP * it
            ks, vts, befores = [], [], []
            for j in range(SB_GROUP - 1, -1, -1):
                kb = top - j
                k0 = pl.multiple_of(jnp.maximum(kb, 0) * BLK, BLK)
                ks.append(k_ref[0, pl.ds(k0, BLK), :])
                vts.append(vt_ref[0, :, pl.ds(k0, BLK)])
                befores.append((kb >= 0) & (k0 + row < BLK * (n0 + qb) + lane_q))
            k_all.append(jnp.concatenate(ks, axis=0))
            vt_all.append(jnp.concatenate(vts, axis=1))
            before.append(jnp.concatenate(befores, axis=0))
        z = [jnp.dot(k_all[qb], q_bd[qb], preferred_element_type=F32) for qb in blocks]
        log_not, split = [], []
        for qb in blocks:
            softplus = jnp.maximum(z[qb], 0.0) + jnp.log2(1.0 + jnp.exp2(-jnp.abs(z[qb])))
            ln = jnp.where(before[qb], -softplus, 0.0)
            hi = ln.astype(BF16)
            lo = (ln - hi.astype(F32)).astype(BF16)
            log_not.append(ln)
            split.append(jnp.concatenate([hi, lo], axis=0))
        tail = [jnp.dot(later, split[qb], preferred_element_type=F32) for qb in blocks]
        a = [jnp.where(before[qb], jnp.exp2(z[qb] + log_not[qb] + tail[qb] + carry[qb]), 0.0).astype(BF16)
             for qb in blocks]
        pv = [jnp.dot(vt_all[qb], a[qb], preferred_element_type=F32) for qb in blocks]
        acc = tuple(tuple(acc[qb][h] + pv[qb][h * HEAD_DIM:(h + 1) * HEAD_DIM, h * BLK:(h + 1) * BLK]
                          for h in range(SB_HEADS)) for qb in blocks)
        carry = tuple(carry[qb] + jnp.sum(log_not[qb], axis=0, keepdims=True) for qb in blocks)
        return it + 1, carry, acc

    init = (0, tuple(jnp.zeros((1, lanes), F32) for _ in blocks),
            tuple(tuple(jnp.zeros((HEAD_DIM, BLK), F32) for _ in range(SB_HEADS)) for _ in blocks))
    _, _, acc = lax.while_loop(cond, body, init)
    for qb in blocks:
        o_ref[0, qb * BLK:(qb + 1) * BLK, :] = jnp.concatenate(
            [_transpose_pad(a, 0)[:, :HEAD_DIM] for a in acc[qb]], axis=1)


def _sb_attn(qt, k, vt):
    b, nb, _, lanes = qt.shape
    s = nb * BLK
    width = SB_HEADS * HEAD_DIM
    gk = SB_GROUP * BLK
    later = (jnp.arange(2 * gk)[None, :] % gk > jnp.arange(gk)[:, None]).astype(BF16)
    return pl.pallas_call(
        _sb_kernel,
        grid=(b, nb // SB_QB),
        in_specs=[pl.BlockSpec((1, SB_QB, HEAD_DIM, lanes), lambda bi, n: (bi, n, 0, 0)),
                  pl.BlockSpec((1, s, width), lambda bi, n: (bi, 0, 0)),
                  pl.BlockSpec((1, width, s), lambda bi, n: (bi, 0, 0)),
                  pl.BlockSpec((gk, 2 * gk), lambda bi, n: (0, 0))],
        out_specs=pl.BlockSpec((1, SB_QB * BLK, width), lambda bi, n: (bi, n, 0)),
        out_shape=jax.ShapeDtypeStruct((b, s, width), F32),
        compiler_params=_cparams(("parallel", "arbitrary")),
        name="sb_attn",
    )(qt, k, vt, later)


def _out_kernel(x_ref, oc_ref, os_ref, ow_ref, gt_ref, swa_ref, sb_ref, gg_ref, w_ref, gp_ref, o_ref):
    wa = NSA_HEADS * HEAD_DIM
    wb = wa + SWA_HEADS * HEAD_DIM
    gg = gg_ref[...]
    lane = _iota((OUT_ROWS, wa), 1)
    groups = [slice(r0, r0 + OUT_ROWS) for r0 in range(0, x_ref.shape[1], OUT_ROWS)]
    mixes = []
    for rows in groups:
        gates = jax.nn.sigmoid(gt_ref[0, rows])

        def spread(branch, gates=gates):
            cols = [jnp.broadcast_to(gates[:, branch * NSA_HEADS + i:branch * NSA_HEADS + i + 1], (OUT_ROWS, wa))
                    for i in range(NSA_HEADS)]
            out = cols[NSA_HEADS - 1]
            for i in range(NSA_HEADS - 2, -1, -1):
                out = jnp.where(lane < (i + 1) * HEAD_DIM, cols[i], out)
            return out

        o_nsa = spread(0) * oc_ref[0, rows] + spread(1) * os_ref[0, rows] + spread(2) * ow_ref[0, rows]
        mixes.append(jnp.concatenate([_rms(o_nsa, gg[:, :wa]), _rms(swa_ref[0, rows], gg[:, wa:wb]),
                                      _rms(sb_ref[0, rows], gg[:, wb:])], axis=1).astype(BF16))
    ys = [jnp.dot(mix, w_ref[...], preferred_element_type=F32) for mix in mixes]
    for rows, y in zip(groups, ys):
        o_ref[0, rows] = x_ref[0, rows] + _rms(y, gp_ref[...])


def _out_proj(x, o_cmp, o_sel, o_win, gates, o_swa, o_sb, gg, w, gp, tm):
    b, s, d = x.shape
    row = lambda n: pl.BlockSpec((1, tm, n), lambda bi, i: (bi, i, 0))
    const = lambda a, c: pl.BlockSpec((a, c), lambda bi, i: (0, 0))
    return pl.pallas_call(
        _out_kernel,
        grid=(b, s // tm),
        in_specs=[row(d), row(256), row(256), row(256), row(128), row(512), row(256),
                  const(1, d), const(d, d), const(1, d)],
        out_specs=row(d),
        out_shape=jax.ShapeDtypeStruct((b, s, d), F32),
        compiler_params=_cparams(("parallel", "parallel")),
        name="out_proj",
    )(x, o_cmp, o_sel, o_win, gates, o_swa, o_sb, gg, w, gp)


def _ffn_kernel(x_ref, g1_ref, wg_ref, wu_ref, wd_ref, g2_ref, o_ref, *, chunk):
    x = x_ref[...]
    h = _rms(x, g1_ref[...]).astype(BF16)
    dff = wg_ref.shape[1]
    f = jnp.zeros(x.shape, F32)
    for a0 in range(0, dff, chunk):
        a1 = min(a0 + chunk, dff)
        gate = jnp.dot(h, wg_ref[:, a0:a1], preferred_element_type=F32)
        up = jnp.dot(h, wu_ref[:, a0:a1], preferred_element_type=F32)
        a = (gate * jax.nn.sigmoid(gate) * up).astype(BF16)
        f = f + jnp.dot(a, wd_ref[a0:a1, :], preferred_element_type=F32)
    o_ref[...] = x + _rms(f, g2_ref[...])


def _out_ffn_kernel(x_ref, oc_ref, os_ref, ow_ref, gt_ref, swa_ref, sb_ref, gg_ref, w_ref, gp_ref,
                    g1_ref, wg_ref, wu_ref, wd_ref, g2_ref, o_ref, x1_scr, *, chunk):
    _out_kernel(x_ref, oc_ref, os_ref, ow_ref, gt_ref, swa_ref, sb_ref, gg_ref, w_ref, gp_ref, x1_scr)
    _ffn_kernel(x1_scr.at[0], g1_ref, wg_ref, wu_ref, wd_ref, g2_ref, o_ref.at[0], chunk=chunk)


def _out_ffn(x, o_cmp, o_sel, o_win, gates, o_swa, o_sb, gg, w, gp, g1, wg, wu, wd, g2, tm):
    b, s, d = x.shape
    dff = wg.shape[1]
    row = lambda n: pl.BlockSpec((1, tm, n), lambda bi, i: (bi, i, 0))
    const = lambda a, c: pl.BlockSpec((a, c), lambda bi, i: (0, 0))
    weight = lambda a, c: pl.BlockSpec((a, c), lambda bi, i: (0, 0), pipeline_mode=pl.Buffered(1))
    return pl.pallas_call(
        functools.partial(_out_ffn_kernel, chunk=FFN_CHUNK),
        grid=(b, s // tm),
        in_specs=[row(d), row(256), row(256), row(256), row(128), row(512), row(256),
                  const(1, d), const(d, d), const(1, d),
                  const(1, d), weight(d, dff), weight(d, dff), weight(dff, d), const(1, d)],
        out_specs=row(d),
        out_shape=jax.ShapeDtypeStruct((b, s, d), F32),
        scratch_shapes=[pltpu.VMEM((1, tm, d), F32)],
        compiler_params=_cparams(("parallel", "parallel")),
        name="out_ffn",
    )(x, o_cmp, o_sel, o_win, gates, o_swa, o_sb, gg, w, gp, g1, wg, wu, wd, g2)


def _ffn(x, g1, wg, wu, wd, g2, tm):
    t, d = x.shape
    dff = wg.shape[1]
    row = pl.BlockSpec((tm, d), lambda i: (i, 0))
    const = lambda a, c: pl.BlockSpec((a, c), lambda i: (0, 0))
    weight = lambda a, c: pl.BlockSpec((a, c), lambda i: (0, 0), pipeline_mode=pl.Buffered(1))
    return pl.pallas_call(
        functools.partial(_ffn_kernel, chunk=FFN_CHUNK),
        grid=(t // tm,),
        in_specs=[row, const(1, d), weight(d, dff), weight(d, dff), weight(dff, d), const(1, d)],
        out_specs=row,
        out_shape=jax.ShapeDtypeStruct((t, d), F32),
        compiler_params=_cparams(("parallel",)),
        name="ffn",
    )(x, g1, wg, wu, wd, g2)


def _permute_w_in(w):
    scale = 1.0 / math.sqrt(HEAD_DIM)
    nq, kc, vc, rest, ng = w[:, :256], w[:, 256:320], w[:, 320:384], w[:, 384:640], w[:, 640:652]
    sq, skv = w[:, 652:1164], w[:, 1164:1420]
    bq, bkv = w[:, 1420:1676], w[:, 1676:2188]
    pad = jnp.zeros((w.shape[0], 128 - ng.shape[1]), w.dtype)
    return jnp.concatenate([nq * scale, rest, sq * scale, skv, bq * scale, bkv, kc, vc, ng, pad], axis=1)


def _cmp_to_sel(nc_pad, nc, ns):
    c0 = jnp.arange(nc_pad)[:, None] * CMP_STRIDE
    s0 = jnp.arange(ns)[None, :] * SEL_BLK
    ov = jnp.minimum(c0 + CMP_LEN, s0 + SEL_BLK) - jnp.maximum(c0, s0)
    w = jnp.clip(ov, 0, None).astype(F32) / CMP_LEN
    return jnp.where(jnp.arange(nc_pad)[:, None] < nc, w, 0.0).astype(BF16)


def kernel(x, rel_bias, ln_attn_pre, w_in, nsa_cmp_pos, nsa_phi_w1, nsa_phi_b1, nsa_phi_w2, swa_sinks,
           grp_norm_g, w_out, ln_attn_post, ln_ffn_pre, ffn_w_gate, ffn_w_up, ffn_w_down, ln_ffn_post):
    b, s, d = x.shape
    depth = w_in.shape[0]
    nch = s // CMP_STRIDE
    ns = s // SEL_BLK
    t = b * s

    band_nsa_t = _band_table_t(rel_bias, N_BAND, NSA_HEADS, 0, True)
    band_swa_t = _band_table_t(rel_bias, SWA_WINDOW // BLK + 1, SWA_HEADS, NSA_HEADS, False)
    hct = _cmp_table(rel_bias, nch)
    wmapt = jnp.pad(_cmp_to_sel(nch, nch - 1, ns).T, ((0, IMP_PAD), (0, 0)))
    no_sinks = jnp.zeros((1, 1, 4 * BLK), F32)

    for l in range(depth):
        outs = _proj(x, ln_attn_pre[l][None], _permute_w_in(w_in[l]).astype(BF16), 512)
        qt, ks, vs, kw, vw, sqt, sk, sv, bq, bk, bv, ckv, gates = outs
        kcv, kcv_t = _compress(ckv.reshape(2, b, nch, CMP_STRIDE * HEAD_DIM),
                               nsa_cmp_pos[l].reshape(2, 2, CMP_STRIDE * HEAD_DIM),
                               nsa_phi_w1[l].reshape(2, 2, CMP_STRIDE * HEAD_DIM, CMP_HIDDEN).astype(BF16),
                               nsa_phi_b1[l][:, None, :], nsa_phi_w2[l].astype(BF16))
        o_cmp, selmask = _cmp_attn(qt, kcv, kcv_t, hct, wmapt)
        o_sel = _sel_attn(qt, ks, vs, selmask, band_nsa_t)
        o_win = _band_attn(qt, kw[:, None], vw[:, None], band_nsa_t, no_sinks, NSA_WINDOW, False, True,
                           "win_attn")
        sinks = jnp.broadcast_to(swa_sinks[l].reshape(SWA_KV_HEADS, 1, 4, 1),
                                 (SWA_KV_HEADS, 1, 4, BLK)).reshape(SWA_KV_HEADS, 1, 4 * BLK)
        o_swa = _band_attn(sqt, sk, sv, band_swa_t, sinks, SWA_WINDOW, True, False, "swa_attn")
        o_sb = _sb_attn(bq, bk, bv)
        x = _out_ffn(x, o_cmp, o_sel, o_win, gates, o_swa, o_sb, grp_norm_g[l][None], w_out[l].astype(BF16),
                     ln_attn_post[l][None], ln_ffn_pre[l][None], ffn_w_gate[l].astype(BF16),
                     ffn_w_up[l].astype(BF16), ffn_w_down[l].astype(BF16), ln_ffn_post[l][None], 512)
    return x
```
